```python
import jax, jax.numpy as jnp
from jax import lax
import numpy as np

D_MODEL = 1024
BATCH = 8
SEQ = 4096
DEPTH = 4

D_FF = 2816
FFN_RESID = 0.5
EPS = 1e-6
GM_WIDTH = 512
GM_GROUPS = 4
GM_GROUP_DIM = GM_WIDTH // GM_GROUPS
GM_CHUNK = 128
MLA_HEADS = 8
MLA_Q_RANK = 384
MLA_KV_RANK = 256
MLA_NOPE = 64
MLA_ROPE = 32
MLA_QK_DIM = MLA_NOPE + MLA_ROPE
MLA_V = 64
MLA_WIDTH = MLA_HEADS * MLA_V
ROPE_THETA = 10000.0
Q_BLOCK = 128
SSD_HEADS = 8
SSD_HEAD_DIM = 64
SSD_INNER = SSD_HEADS * SSD_HEAD_DIM
SSD_GROUPS = 2
SSD_STATE = 128
SSD_CONV = 4
SSD_CHUNK = 128
SSD_CONV_DIM = SSD_INNER + 2 * SSD_GROUPS * SSD_STATE
N_BRANCH = 3
BRANCH_WIDTH = 512
IN_WIDTHS = (2 * GM_WIDTH, MLA_Q_RANK, MLA_KV_RANK, MLA_ROPE, SSD_INNER, SSD_CONV_DIM, SSD_HEADS, N_BRANCH * D_MODEL)
IN_COLS = sum(IN_WIDTHS)
IN_OFFSETS = tuple(int(v) for v in np.cumsum(IN_WIDTHS)[:-1])

kernel_name = 'hybrid_gmlp_mla_ssd_macaron'


def rms_norm(x, gain):
    xf = x.astype(jnp.float32)
    y = xf * lax.rsqrt(jnp.mean(xf * xf, axis=-1, keepdims=True) + EPS)
    return (y * gain.astype(jnp.float32)).astype(x.dtype)


def swiglu_ffn(h, w_in, w_out):
    gate, up = jnp.split(h @ w_in, 2, axis=-1)
    return (jax.nn.silu(gate) * up) @ w_out


def apply_rope(x, cos, sin):
    x1, x2 = jnp.split(x, 2, axis=-1)
    return jnp.concatenate([x1 * cos - x2 * sin, x2 * cos + x1 * sin], axis=-1)


def gmlp_mixer(uv, v_gain, w_s, b_s):
    bsz, s, _ = uv.shape
    u, v = jnp.split(jax.nn.gelu(uv, approximate=False), 2, axis=-1)
    v = rms_norm(v, v_gain).reshape(bsz, s // GM_CHUNK, GM_CHUNK, GM_GROUPS, GM_GROUP_DIM)
    causal = jnp.tril(jnp.ones((GM_CHUNK, GM_CHUNK), dtype=bool))
    w = jnp.where(causal[None], w_s, 0.0).astype(v.dtype)
    sp = jnp.einsum('gts,bcsgd->bctgd', w, v) + b_s.T[:, :, None].astype(v.dtype)
    return u * sp.reshape(bsz, s, GM_WIDTH)


def blocked_causal_attention(q, k, v):
    bsz, s, nh, dqk = q.shape
    dv = v.shape[-1]
    nb = s // Q_BLOCK
    scale = dqk ** -0.5
    qb = q.reshape(bsz, nb, Q_BLOCK, nh, dqk).transpose(1, 0, 3, 2, 4)
    kt = k.transpose(0, 2, 1, 3)
    vt = v.transpose(0, 2, 1, 3)
    kpos = jnp.arange(s)

    def one_block(args):
        qi, i = args
        sc = jnp.einsum('bhqd,bhkd->bhqk', qi, kt, preferred_element_type=jnp.float32) * scale
        qpos = i * Q_BLOCK + jnp.arange(Q_BLOCK)
        sc = jnp.where(kpos[None, :] <= qpos[:, None], sc, -jnp.inf)
        p = jax.nn.softmax(sc, axis=-1).astype(vt.dtype)
        return jnp.einsum('bhqk,bhkd->bhqd', p, vt)

    out = lax.map(one_block, (qb, jnp.arange(nb)))
    return out.transpose(1, 0, 3, 2, 4).reshape(bsz, s, nh * dv)


def mla_mixer(c_q, c_kv, k_rope, cos, sin, q_norm, kv_norm, w_uq, w_ukv, q_gain, k_gain):
    bsz, s, _ = c_q.shape
    q = (rms_norm(c_q, q_norm) @ w_uq).reshape(bsz, s, MLA_HEADS, MLA_QK_DIM)
    kv = (rms_norm(c_kv, kv_norm) @ w_ukv).reshape(bsz, s, MLA_HEADS, MLA_NOPE + MLA_V)
    k_nope, v = jnp.split(kv, [MLA_NOPE], axis=-1)
    k_pe = jnp.broadcast_to(k_rope[:, :, None, :], (bsz, s, MLA_HEADS, MLA_ROPE))
    k = jnp.concatenate([k_nope, k_pe], axis=-1)
    q = rms_norm(q, q_gain)
    k = rms_norm(k, k_gain)
    q = jnp.concatenate([q[..., :MLA_NOPE], apply_rope(q[..., MLA_NOPE:], cos, sin)], axis=-1)
    k = jnp.concatenate([k[..., :MLA_NOPE], apply_rope(k[..., MLA_NOPE:], cos, sin)], axis=-1)
    return blocked_causal_attention(q, k, v)


def segsum(a):
    t = a.shape[-1]
    idx = jnp.arange(t)
    ax = jnp.where(idx[:, None] > idx[None, :], a[..., :, None], 0.0)
    ss = jnp.cumsum(ax, axis=-2)
    return jnp.where(idx[:, None] >= idx[None, :], ss, -jnp.inf)


def ssd_scan(xs, dt, a_log, b_in, c_in):
    bsz, s, nh, hp = xs.shape
    ng, ns = b_in.shape[2], b_in.shape[3]
    nr = nh // ng
    nc = s // SSD_CHUNK
    a = -jnp.exp(a_log.astype(jnp.float32))
    da = (dt * a).reshape(bsz, nc, SSD_CHUNK, ng, nr).transpose(0, 3, 4, 1, 2)
    xdt = (xs * dt[..., None].astype(xs.dtype)).reshape(bsz, nc, SSD_CHUNK, ng, nr, hp)
    bc = b_in.reshape(bsz, nc, SSD_CHUNK, ng, ns)
    cc = c_in.reshape(bsz, nc, SSD_CHUNK, ng, ns)
    cs = jnp.cumsum(da, axis=-1)
    dt_ = xs.dtype
    lmat = jnp.exp(segsum(da)).astype(dt_)
    cb = jnp.einsum('bclgn,bcsgn->bgcls', cc, bc)
    y_diag = jnp.einsum('bgrcls,bcsgrp->bclgrp', cb[:, :, None] * lmat, xdt)
    decay_states = jnp.exp(cs[..., -1:] - cs).astype(dt_)
    states = jnp.einsum('bclgn,bgrcl,bclgrp->bcgrpn', bc, decay_states, xdt)
    chunk_tot = jnp.pad(cs[..., -1], ((0, 0), (0, 0), (0, 0), (1, 0)))
    decay_chunk = jnp.exp(segsum(chunk_tot)).astype(dt_)
    states0 = jnp.concatenate([jnp.zeros_like(states[:, :1]), states], axis=1)
    new_states = jnp.einsum('bgrzc,bcgrpn->bzgrpn', decay_chunk, states0)
    states_in = new_states[:, :-1]
    y_off = jnp.einsum('bclgn,bcgrpn,bgrcl->bclgrp', cc, states_in, jnp.exp(cs).astype(dt_))
    return (y_diag + y_off).reshape(bsz, s, nh, hp)


def ssd_mixer(z, xbc, dt_raw, conv_w, conv_b, dt_bias, a_log, d_skip, norm_gain):
    bsz, s, _ = xbc.shape
    xbc = lax.conv_general_dilated(xbc, conv_w[:, None, :].astype(xbc.dtype), (1,), [(SSD_CONV - 1, 0)],
                                   dimension_numbers=('NWC', 'WIO', 'NWC'), feature_group_count=SSD_CONV_DIM)
    xbc = jax.nn.silu(xbc + conv_b.astype(xbc.dtype))
    xs, b_in, c_in = jnp.split(xbc, [SSD_INNER, SSD_INNER + SSD_GROUPS * SSD_STATE], axis=-1)
    xs = xs.reshape(bsz, s, SSD_HEADS, SSD_HEAD_DIM)
    b_in = b_in.reshape(bsz, s, SSD_GROUPS, SSD_STATE)
    c_in = c_in.reshape(bsz, s, SSD_GROUPS, SSD_STATE)
    dt = jax.nn.softplus(dt_raw.astype(jnp.float32) + dt_bias.astype(jnp.float32))
    y = ssd_scan(xs, dt, a_log, b_in, c_in) + xs * d_skip[:, None].astype(xs.dtype)
    y = y.reshape(bsz, s, SSD_INNER) * jax.nn.silu(z)
    y = rms_norm(y.reshape(bsz, s, SSD_GROUPS, SSD_INNER // SSD_GROUPS), norm_gain.reshape(SSD_GROUPS, -1))
    return y.reshape(bsz, s, SSD_INNER)


def _fwd_setup_inputs(seed: int = 0) -> dict:
    key = jax.random.key(seed)
    ks = jax.random.split(key, 32)
    f32 = jnp.float32

    def nrm(k, shape, scale):
        return jax.random.normal(k, shape, f32) * scale

    def gain(k, shape):
        return 1.0 + 0.05 * jax.random.normal(k, shape, f32)

    L = DEPTH
    x = jax.random.normal(ks[0], (BATCH, SEQ, D_MODEL), f32)
    offsets = jax.random.randint(ks[1], (BATCH, 1), 0, SEQ, dtype=jnp.int32)
    positions = offsets + jnp.arange(SEQ, dtype=jnp.int32)[None, :]
    dt0 = jnp.exp(jax.random.uniform(ks[2], (L, SSD_HEADS), f32, np.log(1e-3), np.log(1e-1)))
    return {
        'x': x,
        'positions': positions,
        'ffn1_norm': gain(ks[3], (L, D_MODEL)),
        'ffn1_w_in': nrm(ks[4], (L, D_MODEL, 2 * D_FF), D_MODEL ** -0.5),
        'ffn1_w_out': nrm(ks[5], (L, D_FF, D_MODEL), D_FF ** -0.5),
        'mix_norm': gain(ks[6], (L, D_MODEL)),
        'w_in': nrm(ks[7], (L, D_MODEL, IN_COLS), D_MODEL ** -0.5),
        'gm_v_norm': gain(ks[8], (L, GM_WIDTH)),
        'gm_w_s': nrm(ks[9], (L, GM_GROUPS, GM_CHUNK, GM_CHUNK), 0.5 * GM_CHUNK ** -0.5),
        'gm_b_s': 1.0 + 0.1 * jax.random.normal(ks[10], (L, GM_GROUPS, GM_CHUNK), f32),
        'mla_q_norm': gain(ks[11], (L, MLA_Q_RANK)),
        'mla_kv_norm': gain(ks[12], (L, MLA_KV_RANK)),
        'mla_w_uq': nrm(ks[13], (L, MLA_Q_RANK, MLA_HEADS * MLA_QK_DIM), MLA_Q_RANK ** -0.5),
        'mla_w_ukv': nrm(ks[14], (L, MLA_KV_RANK, MLA_HEADS * (MLA_NOPE + MLA_V)), MLA_KV_RANK ** -0.5),
        'mla_q_gain': gain(ks[15], (L, MLA_QK_DIM)),
        'mla_k_gain': gain(ks[16], (L, MLA_QK_DIM)),
        'ssd_conv_w': nrm(ks[17], (L, SSD_CONV, SSD_CONV_DIM), SSD_CONV ** -0.5),
        'ssd_conv_b': nrm(ks[18], (L, SSD_CONV_DIM), 0.02),
        'ssd_dt_bias': dt0 + jnp.log(-jnp.expm1(-dt0)),
        'ssd_a_log': jnp.log(jax.random.uniform(ks[19], (L, SSD_HEADS), f32, 1.0, 16.0)),
        'ssd_d': 1.0 + 0.1 * jax.random.normal(ks[20], (L, SSD_HEADS), f32),
        'ssd_norm': gain(ks[21], (L, SSD_INNER)),
        'w_branch': nrm(ks[22], (L, N_BRANCH, BRANCH_WIDTH, D_MODEL), BRANCH_WIDTH ** -0.5),
        'w_out': nrm(ks[23], (L, D_MODEL, D_MODEL), D_MODEL ** -0.5),
        'ffn2_norm': gain(ks[24], (L, D_MODEL)),
        'ffn2_w_in': nrm(ks[25], (L, D_MODEL, 2 * D_FF), D_MODEL ** -0.5),
        'ffn2_w_out': nrm(ks[26], (L, D_FF, D_MODEL), D_FF ** -0.5),
    }


def _fwd_reference(x, positions, ffn1_norm, ffn1_w_in, ffn1_w_out, mix_norm, w_in, gm_v_norm, gm_w_s, gm_b_s,
              mla_q_norm, mla_kv_norm, mla_w_uq, mla_w_ukv, mla_q_gain, mla_k_gain,
              ssd_conv_w, ssd_conv_b, ssd_dt_bias, ssd_a_log, ssd_d, ssd_norm,
              w_branch, w_out, ffn2_norm, ffn2_w_in, ffn2_w_out):
    bsz, s, _ = x.shape
    inv_freq = 1.0 / (ROPE_THETA ** (jnp.arange(0, MLA_ROPE, 2, dtype=jnp.float32) / MLA_ROPE))
    ang = positions.astype(jnp.float32)[..., None] * inv_freq
    cos = jnp.cos(ang)[:, :, None, :].astype(x.dtype)
    sin = jnp.sin(ang)[:, :, None, :].astype(x.dtype)
    for l in range(DEPTH):
        x = x + FFN_RESID * swiglu_ffn(rms_norm(x, ffn1_norm[l]), ffn1_w_in[l], ffn1_w_out[l])
        h = rms_norm(x, mix_norm[l])
        uv, c_q, c_kv, k_rope, z, xbc, dt_raw, gates = jnp.split(h @ w_in[l], IN_OFFSETS, axis=-1)
        y_a = gmlp_mixer(uv, gm_v_norm[l], gm_w_s[l], gm_b_s[l])
        y_b = mla_mixer(c_q, c_kv, k_rope, cos, sin, mla_q_norm[l], mla_kv_norm[l], mla_w_uq[l], mla_w_ukv[l],
                        mla_q_gain[l], mla_k_gain[l])
        y_c = ssd_mixer(z, xbc, dt_raw, ssd_conv_w[l], ssd_conv_b[l], ssd_dt_bias[l], ssd_a_log[l], ssd_d[l],
                        ssd_norm[l])
        g = jax.nn.sigmoid(gates).reshape(bsz, s, N_BRANCH, D_MODEL)
        merged = (g[:, :, 0] * (y_a @ w_branch[l, 0])
                  + g[:, :, 1] * (y_b @ w_branch[l, 1])
                  + g[:, :, 2] * (y_c @ w_branch[l, 2]))
        x = x + merged @ w_out[l]
        x = x + FFN_RESID * swiglu_ffn(rms_norm(x, ffn2_norm[l]), ffn2_w_in[l], ffn2_w_out[l])
    return x


import jax as _jax
import jax.numpy as _jnp

TWIN_FORMAT = 'train_step'
FWD_PARAMS = ['x', 'positions', 'ffn1_norm', 'ffn1_w_in', 'ffn1_w_out', 'mix_norm', 'w_in', 'gm_v_norm', 'gm_w_s', 'gm_b_s', 'mla_q_norm', 'mla_kv_norm', 'mla_w_uq', 'mla_w_ukv', 'mla_q_gain', 'mla_k_gain', 'ssd_conv_w', 'ssd_conv_b', 'ssd_dt_bias', 'ssd_a_log', 'ssd_d', 'ssd_norm', 'w_branch', 'w_out', 'ffn2_norm', 'ffn2_w_in', 'ffn2_w_out']
TWIN_WEIGHTS = ['ffn1_norm', 'ffn1_w_in', 'ffn1_w_out', 'mix_norm', 'w_in', 'gm_v_norm', 'gm_w_s', 'gm_b_s', 'mla_q_norm', 'mla_kv_norm', 'mla_w_uq', 'mla_w_ukv', 'mla_q_gain', 'mla_k_gain', 'ssd_conv_w', 'ssd_conv_b', 'ssd_dt_bias', 'ssd_a_log', 'ssd_d', 'ssd_norm', 'w_branch', 'w_out', 'ffn2_norm', 'ffn2_w_in', 'ffn2_w_out']
TWIN_DIFF_INPUT = 'x'
TWIN_INPUTS = ['x', 'positions', 'ffn1_norm', 'ffn1_w_in', 'ffn1_w_out', 'mix_norm', 'w_in', 'gm_v_norm', 'gm_w_s', 'gm_b_s', 'mla_q_norm', 'mla_kv_norm', 'mla_w_uq', 'mla_w_ukv', 'mla_q_gain', 'mla_k_gain', 'ssd_conv_w', 'ssd_conv_b', 'ssd_dt_bias', 'ssd_a_log', 'ssd_d', 'ssd_norm', 'w_branch', 'w_out', 'ffn2_norm', 'ffn2_w_in', 'ffn2_w_out', 'loss_target', 'm_ffn1_norm', 'm_ffn1_w_in', 'm_ffn1_w_out', 'm_mix_norm', 'm_w_in', 'm_gm_v_norm', 'm_gm_w_s', 'm_gm_b_s', 'm_mla_q_norm', 'm_mla_kv_norm', 'm_mla_w_uq', 'm_mla_w_ukv', 'm_mla_q_gain', 'm_mla_k_gain', 'm_ssd_conv_w', 'm_ssd_conv_b', 'm_ssd_dt_bias', 'm_ssd_a_log', 'm_ssd_d', 'm_ssd_norm', 'm_w_branch', 'm_w_out', 'm_ffn2_norm', 'm_ffn2_w_in', 'm_ffn2_w_out', 'v_ffn1_norm', 'v_ffn1_w_in', 'v_ffn1_w_out', 'v_mix_norm', 'v_w_in', 'v_gm_v_norm', 'v_gm_w_s', 'v_gm_b_s', 'v_mla_q_norm', 'v_mla_kv_norm', 'v_mla_w_uq', 'v_mla_w_ukv', 'v_mla_q_gain', 'v_mla_k_gain', 'v_ssd_conv_w', 'v_ssd_conv_b', 'v_ssd_dt_bias', 'v_ssd_a_log', 'v_ssd_d', 'v_ssd_norm', 'v_w_branch', 'v_w_out', 'v_ffn2_norm', 'v_ffn2_w_in', 'v_ffn2_w_out']
TWIN_OUTPUTS = ['loss', 'grad_x', 'grad_ffn1_norm', 'grad_ffn1_w_in', 'grad_ffn1_w_out', 'grad_mix_norm', 'grad_w_in', 'grad_gm_v_norm', 'grad_gm_w_s', 'grad_gm_b_s', 'grad_mla_q_norm', 'grad_mla_kv_norm', 'grad_mla_w_uq', 'grad_mla_w_ukv', 'grad_mla_q_gain', 'grad_mla_k_gain', 'grad_ssd_conv_w', 'grad_ssd_conv_b', 'grad_ssd_dt_bias', 'grad_ssd_a_log', 'grad_ssd_d', 'grad_ssd_norm', 'grad_w_branch', 'grad_w_out', 'grad_ffn2_norm', 'grad_ffn2_w_in', 'grad_ffn2_w_out', 'delta_ffn1_norm', 'delta_ffn1_w_in', 'delta_ffn1_w_out', 'delta_mix_norm', 'delta_w_in', 'delta_gm_v_norm', 'delta_gm_w_s', 'delta_gm_b_s', 'delta_mla_q_norm', 'delta_mla_kv_norm', 'delta_mla_w_uq', 'delta_mla_w_ukv', 'delta_mla_q_gain', 'delta_mla_k_gain', 'delta_ssd_conv_w', 'delta_ssd_conv_b', 'delta_ssd_dt_bias', 'delta_ssd_a_log', 'delta_ssd_d', 'delta_ssd_norm', 'delta_w_branch', 'delta_w_out', 'delta_ffn2_norm', 'delta_ffn2_w_in', 'delta_ffn2_w_out', 'new_m_ffn1_norm', 'new_m_ffn1_w_in', 'new_m_ffn1_w_out', 'new_m_mix_norm', 'new_m_w_in', 'new_m_gm_v_norm', 'new_m_gm_w_s', 'new_m_gm_b_s', 'new_m_mla_q_norm', 'new_m_mla_kv_norm', 'new_m_mla_w_uq', 'new_m_mla_w_ukv', 'new_m_mla_q_gain', 'new_m_mla_k_gain', 'new_m_ssd_conv_w', 'new_m_ssd_conv_b', 'new_m_ssd_dt_bias', 'new_m_ssd_a_log', 'new_m_ssd_d', 'new_m_ssd_norm', 'new_m_w_branch', 'new_m_w_out', 'new_m_ffn2_norm', 'new_m_ffn2_w_in', 'new_m_ffn2_w_out', 'new_v_ffn1_norm', 'new_v_ffn1_w_in', 'new_v_ffn1_w_out', 'new_v_mix_norm', 'new_v_w_in', 'new_v_gm_v_norm', 'new_v_gm_w_s', 'new_v_gm_b_s', 'new_v_mla_q_norm', 'new_v_mla_kv_norm', 'new_v_mla_w_uq', 'new_v_mla_w_ukv', 'new_v_mla_q_gain', 'new_v_mla_k_gain', 'new_v_ssd_conv_w', 'new_v_ssd_conv_b', 'new_v_ssd_dt_bias', 'new_v_ssd_a_log', 'new_v_ssd_d', 'new_v_ssd_norm', 'new_v_w_branch', 'new_v_w_out', 'new_v_ffn2_norm', 'new_v_ffn2_w_in', 'new_v_ffn2_w_out']
TWIN_LEAF_KINDS = {'loss': 'loss', 'grad_x': 'grad_x', 'grad_ffn1_norm': 'grad_w', 'grad_ffn1_w_in': 'grad_w', 'grad_ffn1_w_out': 'grad_w', 'grad_mix_norm': 'grad_w', 'grad_w_in': 'grad_w', 'grad_gm_v_norm': 'grad_w', 'grad_gm_w_s': 'grad_w', 'grad_gm_b_s': 'grad_w', 'grad_mla_q_norm': 'grad_w', 'grad_mla_kv_norm': 'grad_w', 'grad_mla_w_uq': 'grad_w', 'grad_mla_w_ukv': 'grad_w', 'grad_mla_q_gain': 'grad_w', 'grad_mla_k_gain': 'grad_w', 'grad_ssd_conv_w': 'grad_w', 'grad_ssd_conv_b': 'grad_w', 'grad_ssd_dt_bias': 'grad_w', 'grad_ssd_a_log': 'grad_w', 'grad_ssd_d': 'grad_w', 'grad_ssd_norm': 'grad_w', 'grad_w_branch': 'grad_w', 'grad_w_out': 'grad_w', 'grad_ffn2_norm': 'grad_w', 'grad_ffn2_w_in': 'grad_w', 'grad_ffn2_w_out': 'grad_w', 'delta_ffn1_norm': 'delta_w', 'delta_ffn1_w_in': 'delta_w', 'delta_ffn1_w_out': 'delta_w', 'delta_mix_norm': 'delta_w', 'delta_w_in': 'delta_w', 'delta_gm_v_norm': 'delta_w', 'delta_gm_w_s': 'delta_w', 'delta_gm_b_s': 'delta_w', 'delta_mla_q_norm': 'delta_w', 'delta_mla_kv_norm': 'delta_w', 'delta_mla_w_uq': 'delta_w', 'delta_mla_w_ukv': 'delta_w', 'delta_mla_q_gain': 'delta_w', 'delta_mla_k_gain': 'delta_w', 'delta_ssd_conv_w': 'delta_w', 'delta_ssd_conv_b': 'delta_w', 'delta_ssd_dt_bias': 'delta_w', 'delta_ssd_a_log': 'delta_w', 'delta_ssd_d': 'delta_w', 'delta_ssd_norm': 'delta_w', 'delta_w_branch': 'delta_w', 'delta_w_out': 'delta_w', 'delta_ffn2_norm': 'delta_w', 'delta_ffn2_w_in': 'delta_w', 'delta_ffn2_w_out': 'delta_w', 'new_m_ffn1_norm': 'new_m', 'new_m_ffn1_w_in': 'new_m', 'new_m_ffn1_w_out': 'new_m', 'new_m_mix_norm': 'new_m', 'new_m_w_in': 'new_m', 'new_m_gm_v_norm': 'new_m', 'new_m_gm_w_s': 'new_m', 'new_m_gm_b_s': 'new_m', 'new_m_mla_q_norm': 'new_m', 'new_m_mla_kv_norm': 'new_m', 'new_m_mla_w_uq': 'new_m', 'new_m_mla_w_ukv': 'new_m', 'new_m_mla_q_gain': 'new_m', 'new_m_mla_k_gain': 'new_m', 'new_m_ssd_conv_w': 'new_m', 'new_m_ssd_conv_b': 'new_m', 'new_m_ssd_dt_bias': 'new_m', 'new_m_ssd_a_log': 'new_m', 'new_m_ssd_d': 'new_m', 'new_m_ssd_norm': 'new_m', 'new_m_w_branch': 'new_m', 'new_m_w_out': 'new_m', 'new_m_ffn2_norm': 'new_m', 'new_m_ffn2_w_in': 'new_m', 'new_m_ffn2_w_out': 'new_m', 'new_v_ffn1_norm': 'new_v', 'new_v_ffn1_w_in': 'new_v', 'new_v_ffn1_w_out': 'new_v', 'new_v_mix_norm': 'new_v', 'new_v_w_in': 'new_v', 'new_v_gm_v_norm': 'new_v', 'new_v_gm_w_s': 'new_v', 'new_v_gm_b_s': 'new_v', 'new_v_mla_q_norm': 'new_v', 'new_v_mla_kv_norm': 'new_v', 'new_v_mla_w_uq': 'new_v', 'new_v_mla_w_ukv': 'new_v', 'new_v_mla_q_gain': 'new_v', 'new_v_mla_k_gain': 'new_v', 'new_v_ssd_conv_w': 'new_v', 'new_v_ssd_conv_b': 'new_v', 'new_v_ssd_dt_bias': 'new_v', 'new_v_ssd_a_log': 'new_v', 'new_v_ssd_d': 'new_v', 'new_v_ssd_norm': 'new_v', 'new_v_w_branch': 'new_v', 'new_v_w_out': 'new_v', 'new_v_ffn2_norm': 'new_v', 'new_v_ffn2_w_in': 'new_v', 'new_v_ffn2_w_out': 'new_v'}


def _forward(args):
    return _fwd_reference(*[args[k] for k in FWD_PARAMS])


def _output_shape():
    out = _jax.eval_shape(lambda: _forward(_fwd_setup_inputs(0)))
    return out.shape, out.dtype

N_MICROBATCH = 1
ADAM_LR = 0.001
ADAM_B1 = 0.9
ADAM_B2 = 0.999
ADAM_EPS = 1e-08
ADAM_WD = 0.01
ADAM_STEP = 10
PER_EXAMPLE_BATCH_AXIS = {'x': 0, 'positions': 0, 'loss_target': 0}
SHARED_INPUTS = []
_WEIGHT_DTYPES = {'ffn1_norm': _jnp.float32, 'ffn1_w_in': _jnp.float32, 'ffn1_w_out': _jnp.float32, 'mix_norm': _jnp.float32, 'w_in': _jnp.float32, 'gm_v_norm': _jnp.float32, 'gm_w_s': _jnp.float32, 'gm_b_s': _jnp.float32, 'mla_q_norm': _jnp.float32, 'mla_kv_norm': _jnp.float32, 'mla_w_uq': _jnp.float32, 'mla_w_ukv': _jnp.float32, 'mla_q_gain': _jnp.float32, 'mla_k_gain': _jnp.float32, 'ssd_conv_w': _jnp.float32, 'ssd_conv_b': _jnp.float32, 'ssd_dt_bias': _jnp.float32, 'ssd_a_log': _jnp.float32, 'ssd_d': _jnp.float32, 'ssd_norm': _jnp.float32, 'w_branch': _jnp.float32, 'w_out': _jnp.float32, 'ffn2_norm': _jnp.float32, 'ffn2_w_in': _jnp.float32, 'ffn2_w_out': _jnp.float32}
MOMENT_SCALE = {'ffn1_norm': 5.871945e+00, 'ffn1_w_in': 1.956599e-01, 'ffn1_w_out': 3.398099e-01, 'mix_norm': 6.794620e+00, 'w_in': 6.220787e-01, 'gm_v_norm': 9.448952e-01, 'gm_w_s': 2.418447e+00, 'gm_b_s': 8.352885e+00, 'mla_q_norm': 1.115466e-01, 'mla_kv_norm': 2.313895e+00, 'mla_w_uq': 7.978172e-02, 'mla_w_ukv': 1.125529e+00, 'mla_q_gain': 5.162966e-01, 'mla_k_gain': 5.177247e-01, 'ssd_conv_w': 1.325423e+00, 'ssd_conv_b': 4.214931e+00, 'ssd_dt_bias': 3.080401e+00, 'ssd_a_log': 9.435152e+00, 'ssd_d': 1.202027e+01, 'ssd_norm': 2.678647e+01, 'w_branch': 1.988423e+00, 'w_out': 3.104206e+00, 'ffn2_norm': 6.136680e+00, 'ffn2_w_in': 1.875907e-01, 'ffn2_w_out': 3.372428e-01}


def _to_microbatches(a, axis):
    t = _jnp.moveaxis(a, axis, 0)
    t = t.reshape((N_MICROBATCH, t.shape[0] // N_MICROBATCH) + t.shape[1:])
    return _jnp.moveaxis(t, 1, axis + 1)


def setup_inputs(seed: int = 0) -> dict:
    inp = _fwd_setup_inputs(seed)
    key = _jax.random.fold_in(_jax.random.key(seed), 7919)
    shape, _ = _output_shape()
    out = dict(inp)
    out["loss_target"] = _jax.random.normal(_jax.random.fold_in(key, 0), shape, _jnp.float32)
    for i, name in enumerate(TWIN_WEIGHTS):
        w = inp[name].astype(_jnp.float32)
        if MOMENT_SCALE is None:
            s = _jnp.sqrt(_jnp.mean(_jnp.square(w)) + 1e-30)
        else:
            s = MOMENT_SCALE[name]
        km, kv = _jax.random.split(_jax.random.fold_in(key, i + 1))
        out[name] = w
        out["m_" + name] = s * _jax.random.normal(km, w.shape, _jnp.float32)
        out["v_" + name] = (s * s) * _jax.random.uniform(kv, w.shape, _jnp.float32, 0.5, 1.5)
    if N_MICROBATCH > 1:
        for name, axis in PER_EXAMPLE_BATCH_AXIS.items():
            out[name] = _to_microbatches(out[name], axis)
    return {'x': out['x'], 'positions': out['positions'], 'ffn1_norm': out['ffn1_norm'], 'ffn1_w_in': out['ffn1_w_in'], 'ffn1_w_out': out['ffn1_w_out'], 'mix_norm': out['mix_norm'], 'w_in': out['w_in'], 'gm_v_norm': out['gm_v_norm'], 'gm_w_s': out['gm_w_s'], 'gm_b_s': out['gm_b_s'], 'mla_q_norm': out['mla_q_norm'], 'mla_kv_norm': out['mla_kv_norm'], 'mla_w_uq': out['mla_w_uq'], 'mla_w_ukv': out['mla_w_ukv'], 'mla_q_gain': out['mla_q_gain'], 'mla_k_gain': out['mla_k_gain'], 'ssd_conv_w': out['ssd_conv_w'], 'ssd_conv_b': out['ssd_conv_b'], 'ssd_dt_bias': out['ssd_dt_bias'], 'ssd_a_log': out['ssd_a_log'], 'ssd_d': out['ssd_d'], 'ssd_norm': out['ssd_norm'], 'w_branch': out['w_branch'], 'w_out': out['w_out'], 'ffn2_norm': out['ffn2_norm'], 'ffn2_w_in': out['ffn2_w_in'], 'ffn2_w_out': out['ffn2_w_out'], 'loss_target': out['loss_target'], 'm_ffn1_norm': out['m_ffn1_norm'], 'm_ffn1_w_in': out['m_ffn1_w_in'], 'm_ffn1_w_out': out['m_ffn1_w_out'], 'm_mix_norm': out['m_mix_norm'], 'm_w_in': out['m_w_in'], 'm_gm_v_norm': out['m_gm_v_norm'], 'm_gm_w_s': out['m_gm_w_s'], 'm_gm_b_s': out['m_gm_b_s'], 'm_mla_q_norm': out['m_mla_q_norm'], 'm_mla_kv_norm': out['m_mla_kv_norm'], 'm_mla_w_uq': out['m_mla_w_uq'], 'm_mla_w_ukv': out['m_mla_w_ukv'], 'm_mla_q_gain': out['m_mla_q_gain'], 'm_mla_k_gain': out['m_mla_k_gain'], 'm_ssd_conv_w': out['m_ssd_conv_w'], 'm_ssd_conv_b': out['m_ssd_conv_b'], 'm_ssd_dt_bias': out['m_ssd_dt_bias'], 'm_ssd_a_log': out['m_ssd_a_log'], 'm_ssd_d': out['m_ssd_d'], 'm_ssd_norm': out['m_ssd_norm'], 'm_w_branch': out['m_w_branch'], 'm_w_out': out['m_w_out'], 'm_ffn2_norm': out['m_ffn2_norm'], 'm_ffn2_w_in': out['m_ffn2_w_in'], 'm_ffn2_w_out': out['m_ffn2_w_out'], 'v_ffn1_norm': out['v_ffn1_norm'], 'v_ffn1_w_in': out['v_ffn1_w_in'], 'v_ffn1_w_out': out['v_ffn1_w_out'], 'v_mix_norm': out['v_mix_norm'], 'v_w_in': out['v_w_in'], 'v_gm_v_norm': out['v_gm_v_norm'], 'v_gm_w_s': out['v_gm_w_s'], 'v_gm_b_s': out['v_gm_b_s'], 'v_mla_q_norm': out['v_mla_q_norm'], 'v_mla_kv_norm': out['v_mla_kv_norm'], 'v_mla_w_uq': out['v_mla_w_uq'], 'v_mla_w_ukv': out['v_mla_w_ukv'], 'v_mla_q_gain': out['v_mla_q_gain'], 'v_mla_k_gain': out['v_mla_k_gain'], 'v_ssd_conv_w': out['v_ssd_conv_w'], 'v_ssd_conv_b': out['v_ssd_conv_b'], 'v_ssd_dt_bias': out['v_ssd_dt_bias'], 'v_ssd_a_log': out['v_ssd_a_log'], 'v_ssd_d': out['v_ssd_d'], 'v_ssd_norm': out['v_ssd_norm'], 'v_w_branch': out['v_w_branch'], 'v_w_out': out['v_w_out'], 'v_ffn2_norm': out['v_ffn2_norm'], 'v_ffn2_w_in': out['v_ffn2_w_in'], 'v_ffn2_w_out': out['v_ffn2_w_out']}


def _loss(weights, diff, rest, loss_target):
    with _jax.named_scope("forward"):
        args = {**rest, TWIN_DIFF_INPUT: diff, **{k: w.astype(_WEIGHT_DTYPES[k]) for k, w in weights.items()}}
        y = _forward(args)
    with _jax.named_scope("loss_head"):
        err = _jnp.square(y.astype(_jnp.float32) - loss_target)
        return 0.5 * _jnp.sum(_jnp.mean(err, axis=-1)) if err.ndim else 0.5 * err


def _adamw(w, g, m, v):
    m = ADAM_B1 * m + (1.0 - ADAM_B1) * g
    v = ADAM_B2 * v + (1.0 - ADAM_B2) * _jnp.square(g)
    m_hat = m / (1.0 - ADAM_B1 ** ADAM_STEP)
    v_hat = v / (1.0 - ADAM_B2 ** ADAM_STEP)
    delta = -ADAM_LR * (m_hat / (_jnp.sqrt(v_hat) + ADAM_EPS) + ADAM_WD * w)
    return delta, m, v


def reference(x, positions, ffn1_norm, ffn1_w_in, ffn1_w_out, mix_norm, w_in, gm_v_norm, gm_w_s, gm_b_s, mla_q_norm, mla_kv_norm, mla_w_uq, mla_w_ukv, mla_q_gain, mla_k_gain, ssd_conv_w, ssd_conv_b, ssd_dt_bias, ssd_a_log, ssd_d, ssd_norm, w_branch, w_out, ffn2_norm, ffn2_w_in, ffn2_w_out, loss_target, m_ffn1_norm, m_ffn1_w_in, m_ffn1_w_out, m_mix_norm, m_w_in, m_gm_v_norm, m_gm_w_s, m_gm_b_s, m_mla_q_norm, m_mla_kv_norm, m_mla_w_uq, m_mla_w_ukv, m_mla_q_gain, m_mla_k_gain, m_ssd_conv_w, m_ssd_conv_b, m_ssd_dt_bias, m_ssd_a_log, m_ssd_d, m_ssd_norm, m_w_branch, m_w_out, m_ffn2_norm, m_ffn2_w_in, m_ffn2_w_out, v_ffn1_norm, v_ffn1_w_in, v_ffn1_w_out, v_mix_norm, v_w_in, v_gm_v_norm, v_gm_w_s, v_gm_b_s, v_mla_q_norm, v_mla_kv_norm, v_mla_w_uq, v_mla_w_ukv, v_mla_q_gain, v_mla_k_gain, v_ssd_conv_w, v_ssd_conv_b, v_ssd_dt_bias, v_ssd_a_log, v_ssd_d, v_ssd_norm, v_w_branch, v_w_out, v_ffn2_norm, v_ffn2_w_in, v_ffn2_w_out):
    given = dict(x=x, positions=positions, ffn1_norm=ffn1_norm, ffn1_w_in=ffn1_w_in, ffn1_w_out=ffn1_w_out, mix_norm=mix_norm, w_in=w_in, gm_v_norm=gm_v_norm, gm_w_s=gm_w_s, gm_b_s=gm_b_s, mla_q_norm=mla_q_norm, mla_kv_norm=mla_kv_norm, mla_w_uq=mla_w_uq, mla_w_ukv=mla_w_ukv, mla_q_gain=mla_q_gain, mla_k_gain=mla_k_gain, ssd_conv_w=ssd_conv_w, ssd_conv_b=ssd_conv_b, ssd_dt_bias=ssd_dt_bias, ssd_a_log=ssd_a_log, ssd_d=ssd_d, ssd_norm=ssd_norm, w_branch=w_branch, w_out=w_out, ffn2_norm=ffn2_norm, ffn2_w_in=ffn2_w_in, ffn2_w_out=ffn2_w_out, loss_target=loss_target, m_ffn1_norm=m_ffn1_norm, m_ffn1_w_in=m_ffn1_w_in, m_ffn1_w_out=m_ffn1_w_out, m_mix_norm=m_mix_norm, m_w_in=m_w_in, m_gm_v_norm=m_gm_v_norm, m_gm_w_s=m_gm_w_s, m_gm_b_s=m_gm_b_s, m_mla_q_norm=m_mla_q_norm, m_mla_kv_norm=m_mla_kv_norm, m_mla_w_uq=m_mla_w_uq, m_mla_w_ukv=m_mla_w_ukv, m_mla_q_gain=m_mla_q_gain, m_mla_k_gain=m_mla_k_gain, m_ssd_conv_w=m_ssd_conv_w, m_ssd_conv_b=m_ssd_conv_b, m_ssd_dt_bias=m_ssd_dt_bias, m_ssd_a_log=m_ssd_a_log, m_ssd_d=m_ssd_d, m_ssd_norm=m_ssd_norm, m_w_branch=m_w_branch, m_w_out=m_w_out, m_ffn2_norm=m_ffn2_norm, m_ffn2_w_in=m_ffn2_w_in, m_ffn2_w_out=m_ffn2_w_out, v_ffn1_norm=v_ffn1_norm, v_ffn1_w_in=v_ffn1_w_in, v_ffn1_w_out=v_ffn1_w_out, v_mix_norm=v_mix_norm, v_w_in=v_w_in, v_gm_v_norm=v_gm_v_norm, v_gm_w_s=v_gm_w_s, v_gm_b_s=v_gm_b_s, v_mla_q_norm=v_mla_q_norm, v_mla_kv_norm=v_mla_kv_norm, v_mla_w_uq=v_mla_w_uq, v_mla_w_ukv=v_mla_w_ukv, v_mla_q_gain=v_mla_q_gain, v_mla_k_gain=v_mla_k_gain, v_ssd_conv_w=v_ssd_conv_w, v_ssd_conv_b=v_ssd_conv_b, v_ssd_dt_bias=v_ssd_dt_bias, v_ssd_a_log=v_ssd_a_log, v_ssd_d=v_ssd_d, v_ssd_norm=v_ssd_norm, v_w_branch=v_w_branch, v_w_out=v_w_out, v_ffn2_norm=v_ffn2_norm, v_ffn2_w_in=v_ffn2_w_in, v_ffn2_w_out=v_ffn2_w_out)
    weights = {n: given[n] for n in TWIN_WEIGHTS}
    shared = {n: given[n] for n in SHARED_INPUTS}
    per_example = {n: given[n] for n in ['x', 'positions']}
    grad_fn = _jax.value_and_grad(_loss, argnums=(0, 1))

    def one_microbatch(ex, loss_target):
        ex = dict(ex)
        diff = ex.pop(TWIN_DIFF_INPUT)
        return grad_fn(weights, diff, {**shared, **ex}, loss_target)

    if N_MICROBATCH == 1:
        loss, (grad_w, grad_x) = one_microbatch(per_example, given["loss_target"])
    else:
        def body(carry, xs):
            loss_sum, grad_sum = carry
            l_k, (gw_k, gx_k) = one_microbatch(xs[0], xs[1])
            with _jax.named_scope("update"):
                return (loss_sum + l_k, _jax.tree.map(_jnp.add, grad_sum, gw_k)), gx_k

        init = (_jnp.zeros((), _jnp.float32), _jax.tree.map(_jnp.zeros_like, weights))
        (loss, grad_w), grad_x = _jax.lax.scan(body, init, (per_example, given["loss_target"]))
    with _jax.named_scope("update"):
        delta_w, new_m, new_v = {}, {}, {}
        for n in TWIN_WEIGHTS:
            delta_w[n], new_m[n], new_v[n] = _adamw(weights[n], grad_w[n], given["m_" + n], given["v_" + n])
    return (loss, grad_x, *[grad_w[n] for n in TWIN_WEIGHTS], *[delta_w[n] for n in TWIN_WEIGHTS],
            *[new_m[n] for n in TWIN_WEIGHTS], *[new_v[n] for n in TWIN_WEIGHTS])
```

```python
import functools

import jax
import jax.numpy as jnp
import numpy as np
from jax import lax
from jax.experimental import pallas as pl
from jax.experimental.pallas import tpu as pltpu

F32 = jnp.float32
BF16 = jnp.bfloat16

D_MODEL = 1024
DEPTH = 4
D_FF = 2816
FFN_RESID = 0.5
EPS = 1e-6
GM_WIDTH = 512
GM_GROUPS = 4
GM_CHUNK = 128
MLA_HEADS = 8
MLA_Q_RANK = 384
MLA_KV_RANK = 256
MLA_NOPE = 64
MLA_ROPE = 32
MLA_QK_DIM = 96
MLA_V = 64
ROPE_THETA = 10000.0
SSD_HEADS = 8
SSD_HEAD_DIM = 64
SSD_INNER = 512
SSD_GROUPS = 2
SSD_STATE = 128
SSD_CONV = 4
SSD_CHUNK = 128
SSD_CONV_DIM = 1024
N_BRANCH = 3
IN_WIDTHS = (1024, 384, 256, 32, 512, 1024, 8, 3072)
IN_OFFSETS = (0, 1024, 1408, 1664, 1696, 2208, 3232, 3240)
IN_COLS = 6312
LANES = 128
N_DEV = 8

ADAM_LR = 0.001
ADAM_B1 = 0.9
ADAM_B2 = 0.999
ADAM_EPS = 1e-08
ADAM_WD = 0.01
ADAM_STEP = 10

VMEM_LIMIT = 56 * 1024 * 1024

P_UV, P_XBC, P_Z, P_KR, P_CQ, P_GATES, P_CKV, P_DT = 0, 1024, 2048, 2560, 2688, 3072, 6144, 6400
P_COLS = 6528


def _cparams(sem):
    return pltpu.CompilerParams(dimension_semantics=sem, vmem_limit_bytes=VMEM_LIMIT)


def _bdot(a, b, dims):
    return lax.dot_general(a.astype(BF16), b.astype(BF16), (dims, ((), ())), preferred_element_type=F32)


@jax.custom_vjp
def _nn(a, b):
    return _bdot(a, b, ((1,), (0,)))


@jax.custom_vjp
def _nt(a, b):
    return _bdot(a, b, ((1,), (1,)))


@jax.custom_vjp
def _tn(a, b):
    return _bdot(a, b, ((0,), (0,)))


def _dot_fwd(dims):
    return lambda a, b: (_bdot(a, b, dims), (a, b))


_nn.defvjp(_dot_fwd(((1,), (0,))), lambda r, g: (_nt(g, r[1]).astype(r[0].dtype), _tn(r[0], g).astype(r[1].dtype)))
_nt.defvjp(_dot_fwd(((1,), (1,))), lambda r, g: (_nn(g, r[1]).astype(r[0].dtype), _tn(g, r[0]).astype(r[1].dtype)))
_tn.defvjp(_dot_fwd(((0,), (0,))), lambda r, g: (_nt(r[1], g).astype(r[0].dtype), _nn(r[0], g).astype(r[1].dtype)))


def _exact_nn(a, b):
    return lax.dot_general(a, b, (((1,), (0,)), ((), ())), precision=lax.Precision.HIGHEST, preferred_element_type=F32)


def _sigmoid(x):
    return 1.0 / (1.0 + jnp.exp(-x))


def _silu(x):
    return x * _sigmoid(x)


def _softplus(x):
    return jnp.maximum(x, 0.0) + jnp.log(1.0 + jnp.exp(-jnp.abs(x)))


def _gelu(x):
    return 0.5 * x * (1.0 + lax.erf(x * 0.7071067811865476))


def _pick(n, cands):
    for c in cands:
        if n % c == 0:
            return c
    return n


def _matmul(a, b, mode, out_dtype, name, alpha=1.0, tm=None, tn=None, tk=None):
    if mode == "nn":
        (m, k), (_, n) = a.shape, b.shape
    elif mode == "nt":
        (m, k), (n, _) = a.shape, b.shape
    else:
        (k, m), (_, n) = a.shape, b.shape
    tm = tm or _pick(m, (512, 384, 256, 128))
    tn = tn or _pick(n, (1024, 768, 512, 384, 256, 128))
    tk = tk or _pick(k, (1024, 512, 256, 128))
    nk = k // tk
    if mode == "nn":
        a_spec = pl.BlockSpec((tm, tk), lambda i, j, kk: (i, kk))
        b_spec = pl.BlockSpec((tk, tn), lambda i, j, kk: (kk, j))
        dot = _nn
    elif mode == "nt":
        a_spec = pl.BlockSpec((tm, tk), lambda i, j, kk: (i, kk))
        b_spec = pl.BlockSpec((tn, tk), lambda i, j, kk: (j, kk))
        dot = _nt
    else:
        a_spec = pl.BlockSpec((tk, tm), lambda i, j, kk: (kk, i))
        b_spec = pl.BlockSpec((tk, tn), lambda i, j, kk: (kk, j))
        dot = _tn

    def body(a_ref, b_ref, o_ref, acc_ref):
        kk = pl.program_id(2)

        @pl.when(kk == 0)
        def _():
            acc_ref[...] = jnp.zeros_like(acc_ref)

        acc_ref[...] += dot(a_ref[...], b_ref[...])

        @pl.when(kk == nk - 1)
        def _():
            o_ref[...] = (alpha * acc_ref[...]).astype(out_dtype)

    return pl.pallas_call(
        body,
        name=name,
        out_shape=jax.ShapeDtypeStruct((m, n), out_dtype),
        grid=(m // tm, n // tn, nk),
        in_specs=[a_spec, b_spec],
        out_specs=pl.BlockSpec((tm, tn), lambda i, j, kk: (i, j)),
        scratch_shapes=[pltpu.VMEM((tm, tn), F32)],
        compiler_params=_cparams(("parallel", "parallel", "arbitrary")),
    )(a, b)


def _rms_stats(x):
    r = lax.rsqrt(jnp.mean(x * x, axis=-1, keepdims=True) + EPS)
    return x * r, r


def _rms_bwd(xhat, r, gain, dy):
    dxhat = dy * gain
    return r * (dxhat - xhat * jnp.mean(dxhat * xhat, axis=-1, keepdims=True))


def _acc_rows(ref, val, first):
    s = jnp.sum(val, axis=0, keepdims=True)

    @pl.when(first)
    def _():
        ref[...] = s

    @pl.when(jnp.logical_not(first))
    def _():
        ref[...] += s


def _rmsnorm_fwd(x, gain, name, tm=512):
    s, d = x.shape

    def body(x_ref, g_ref, h_ref):
        xhat, _ = _rms_stats(x_ref[...])
        h_ref[...] = (xhat * g_ref[...]).astype(BF16)

    return pl.pallas_call(
        body,
        name=name,
        out_shape=jax.ShapeDtypeStruct((s, d), BF16),
        grid=(s // tm,),
        in_specs=[pl.BlockSpec((tm, d), lambda i: (i, 0)), pl.BlockSpec((1, d), lambda i: (0, 0))],
        out_specs=pl.BlockSpec((tm, d), lambda i: (i, 0)),
        compiler_params=_cparams(("parallel",)),
    )(x, gain)


def _rmsnorm_bwd(x, gain, dh, dres, name, tm=512):
    s, d = x.shape

    def body(x_ref, g_ref, dh_ref, dres_ref, dx_ref, dg_ref):
        xhat, r = _rms_stats(x_ref[...])
        dh = dh_ref[...].astype(F32)
        dx_ref[...] = dres_ref[...] + _rms_bwd(xhat, r, g_ref[...], dh)
        _acc_rows(dg_ref, dh * xhat, pl.program_id(0) == 0)

    return pl.pallas_call(
        body,
        name=name,
        out_shape=(jax.ShapeDtypeStruct((s, d), F32), jax.ShapeDtypeStruct((1, d), F32)),
        grid=(s // tm,),
        in_specs=[
            pl.BlockSpec((tm, d), lambda i: (i, 0)),
            pl.BlockSpec((1, d), lambda i: (0, 0)),
            pl.BlockSpec((tm, d), lambda i: (i, 0)),
            pl.BlockSpec((tm, d), lambda i: (i, 0)),
        ],
        out_specs=(pl.BlockSpec((tm, d), lambda i: (i, 0)), pl.BlockSpec((1, d), lambda i: (0, 0))),
        compiler_params=_cparams(("arbitrary",)),
    )(x, gain, dh, dres)


FF_BLOCK = 2 * D_FF // N_DEV
FF_BLOCKS = D_FF // FF_BLOCK
FF_ROWS = D_FF // N_DEV


def _ffn_weight_specs(layer):
    return [
        pl.BlockSpec((None, None, D_MODEL, FF_BLOCK), lambda i, j: (j, layer, 0, 0)),
        pl.BlockSpec((None, None, D_MODEL, FF_BLOCK), lambda i, j: (j + FF_BLOCKS, layer, 0, 0)),
        pl.BlockSpec((2, None, FF_ROWS, D_MODEL), lambda i, j: (j, layer, 0, 0)),
    ]


def _ffn_fwd(x, gain, w_in, w_out, layer, name, tm=512):
    s, d = x.shape

    def body(x_ref, gain_ref, wg_ref, wu_ref, wo_ref, y_ref, gu_ref, h_scr, acc_scr):
        j = pl.program_id(1)

        @pl.when(j == 0)
        def _():
            xhat, _ = _rms_stats(x_ref[...])
            h_scr[...] = (xhat * gain_ref[...]).astype(BF16)
            acc_scr[...] = jnp.zeros_like(acc_scr)

        h = h_scr[...]
        g = _nn(h, wg_ref[...])
        u = _nn(h, wu_ref[...])
        gu_ref[0] = g.astype(BF16)
        gu_ref[1] = u.astype(BF16)
        acc_scr[...] += _nn(_silu(g) * u, wo_ref[...].reshape(FF_BLOCK, d))

        @pl.when(j == FF_BLOCKS - 1)
        def _():
            y_ref[...] = x_ref[...] + FFN_RESID * acc_scr[...]

    return pl.pallas_call(
        body,
        name=name,
        out_shape=(
            jax.ShapeDtypeStruct((s, d), F32),
            jax.ShapeDtypeStruct((2, FF_BLOCKS, s, FF_BLOCK), BF16),
        ),
        grid=(s // tm, FF_BLOCKS),
        in_specs=[
            pl.BlockSpec((tm, d), lambda i, j: (i, 0)),
            pl.BlockSpec((1, d), lambda i, j: (0, 0)),
        ] + _ffn_weight_specs(layer),
        out_specs=(
            pl.BlockSpec((tm, d), lambda i, j: (i, 0)),
            pl.BlockSpec((2, None, tm, FF_BLOCK), lambda i, j: (0, j, i, 0)),
        ),
        scratch_shapes=[pltpu.VMEM((tm, d), BF16), pltpu.VMEM((tm, d), F32)],
        compiler_params=_cparams(("parallel", "arbitrary")),
    )(x, gain, w_in, w_in, w_out)


def _ffn_bwd(x, gain, w_in, w_out, layer, gu, dy, name, tm=512, tk=1024):
    s, d = x.shape
    tk = min(tk, s)

    def body(x_ref, gain_ref, wg_ref, wu_ref, wo_ref, gu_ref, dy_ref,
             dx_ref, dgain_ref, h_ref, a_ref, dgu_ref, dyb_scr, acc_scr):
        i = pl.program_id(0)
        j = pl.program_id(1)

        @pl.when(j == 0)
        def _():
            xhat, _ = _rms_stats(x_ref[...])
            h_ref[...] = (xhat * gain_ref[...]).astype(BF16)
            dyb_scr[...] = (FFN_RESID * dy_ref[...]).astype(BF16)
            acc_scr[...] = jnp.zeros_like(acc_scr)

        da = _nt(dyb_scr[...], wo_ref[...].reshape(FF_BLOCK, d))
        gv = gu_ref[0].astype(F32)
        uv = gu_ref[1].astype(F32)
        sg = _sigmoid(gv)
        sl = gv * sg
        a_ref[...] = (sl * uv).astype(BF16)
        du = (da * sl).astype(BF16)
        dg = (da * uv * (sg * (1.0 + gv * (1.0 - sg)))).astype(BF16)
        dgu_ref[0] = dg
        dgu_ref[1] = du
        acc_scr[...] += _nt(dg, wg_ref[...]) + _nt(du, wu_ref[...])

        @pl.when(j == FF_BLOCKS - 1)
        def _():
            xhat, r = _rms_stats(x_ref[...])
            dh = acc_scr[...]
            dx_ref[...] = dy_ref[...] + _rms_bwd(xhat, r, gain_ref[...], dh)
            _acc_rows(dgain_ref, dh * xhat, i == 0)

    gu_spec = pl.BlockSpec((2, None, tm, FF_BLOCK), lambda i, j: (0, j, i, 0))
    dx, dgain, h, a, dgu = pl.pallas_call(
        body,
        name=name,
        out_shape=(
            jax.ShapeDtypeStruct((s, d), F32),
            jax.ShapeDtypeStruct((1, d), F32),
            jax.ShapeDtypeStruct((s, d), BF16),
            jax.ShapeDtypeStruct((FF_BLOCKS, s, FF_BLOCK), BF16),
            jax.ShapeDtypeStruct((2, FF_BLOCKS, s, FF_BLOCK), BF16),
        ),
        grid=(s // tm, FF_BLOCKS),
        in_specs=[
            pl.BlockSpec((tm, d), lambda i, j: (i, 0)),
            pl.BlockSpec((1, d), lambda i, j: (0, 0)),
        ] + _ffn_weight_specs(layer) + [gu_spec, pl.BlockSpec((tm, d), lambda i, j: (i, 0))],
        out_specs=(
            pl.BlockSpec((tm, d), lambda i, j: (i, 0)),
            pl.BlockSpec((1, d), lambda i, j: (0, 0)),
            pl.BlockSpec((tm, d), lambda i, j: (i, 0)),
            pl.BlockSpec((None, tm, FF_BLOCK), lambda i, j: (j, i, 0)),
            gu_spec,
        ),
        scratch_shapes=[pltpu.VMEM((tm, d), BF16), pltpu.VMEM((tm, d), F32)],
        compiler_params=_cparams(("arbitrary", "arbitrary")),
    )(x, gain, w_in, w_in, w_out, gu, dy)
    nk = s // tk

    def acc_matmul(first, last, acc_ref, o_ref, val, alpha):
        @pl.when(first)
        def _():
            acc_ref[...] = jnp.zeros_like(acc_ref)

        acc_ref[...] += val

        @pl.when(last)
        def _():
            o_ref[...] = (alpha * acc_ref[...]).astype(BF16)

    def dwin_body(h_ref, dgu_ref, o_ref, acc_ref):
        kk = pl.program_id(2)
        acc_matmul(kk == 0, kk == nk - 1, acc_ref, o_ref, _tn(h_ref[...], dgu_ref[...]), 1.0)

    tmw = 512
    dw_in = pl.pallas_call(
        dwin_body,
        name=name + "_dwin",
        out_shape=jax.ShapeDtypeStruct((N_DEV, d, FF_BLOCK), BF16),
        grid=(N_DEV, d // tmw, nk),
        in_specs=[
            pl.BlockSpec((tk, tmw), lambda n, i, kk: (kk, i)),
            pl.BlockSpec((None, tk, FF_BLOCK), lambda n, i, kk: (n, kk, 0)),
        ],
        out_specs=pl.BlockSpec((None, tmw, FF_BLOCK), lambda n, i, kk: (n, i, 0)),
        scratch_shapes=[pltpu.VMEM((tmw, FF_BLOCK), F32)],
        compiler_params=_cparams(("parallel", "parallel", "arbitrary")),
    )(h, dgu.reshape(N_DEV, s, FF_BLOCK))

    def dwout_body(a_ref, dy_ref, o_ref, acc_ref):
        kk = pl.program_id(1)
        acc_matmul(kk == 0, kk == nk - 1, acc_ref, o_ref, _tn(a_ref[...], dy_ref[...]), FFN_RESID)

    dw_out = pl.pallas_call(
        dwout_body,
        name=name + "_dwout",
        out_shape=jax.ShapeDtypeStruct((FF_BLOCKS, FF_BLOCK, d), BF16),
        grid=(FF_BLOCKS, nk),
        in_specs=[
            pl.BlockSpec((None, tk, FF_BLOCK), lambda j, kk: (j, kk, 0)),
            pl.BlockSpec((tk, d), lambda j, kk: (kk, 0)),
        ],
        out_specs=pl.BlockSpec((None, FF_BLOCK, d), lambda j, kk: (j, 0, 0)),
        scratch_shapes=[pltpu.VMEM((FF_BLOCK, d), F32)],
        compiler_params=_cparams(("parallel", "arbitrary")),
    )(a, dy)
    return dx, dgain, dw_in, dw_out.reshape(N_DEV, FF_ROWS, d)


def _acc(ref, val, first):
    @pl.when(first)
    def _():
        ref[...] = val

    @pl.when(jnp.logical_not(first))
    def _():
        ref[...] += val


def _full(shape):
    nd = len(shape)
    return pl.BlockSpec(shape, lambda *_: (0,) * nd)


def _iota(shape, dim):
    return lax.broadcasted_iota(jnp.int32, shape, dim)


def _gmlp_chunk(u, v, gain, w_s, b_s):
    va = [_gelu(t) for t in v]
    ms = sum(jnp.sum(t * t, axis=-1, keepdims=True) for t in va) * (1.0 / GM_WIDTH)
    r = lax.rsqrt(ms + EPS)
    tri = _iota((GM_CHUNK, GM_CHUNK), 0) >= _iota((GM_CHUNK, GM_CHUNK), 1)
    out = []
    for g in range(GM_GROUPS):
        vn = va[g] * r * gain[g]
        sp = _nn(jnp.where(tri, w_s[g], 0.0), vn) + b_s[g]
        out.append(_gelu(u[g]) * sp)
    return out


def _gmlp_load(uv_ref, c):
    rows = pl.ds(c * GM_CHUNK, GM_CHUNK)
    u = [uv_ref[rows, pl.ds(g * LANES, LANES)] for g in range(GM_GROUPS)]
    v = [uv_ref[rows, pl.ds(GM_WIDTH + g * LANES, LANES)] for g in range(GM_GROUPS)]
    return u, v


def _gmlp_params(gain_ref, ws_ref, bs_ref):
    gain = [gain_ref[:, pl.ds(g * LANES, LANES)] for g in range(GM_GROUPS)]
    w_s = [ws_ref[g] for g in range(GM_GROUPS)]
    b_s = [bs_ref[g] for g in range(GM_GROUPS)]
    return gain, w_s, b_s


def _gmlp_fwd(proj, gain, w_s, b_s, name, tm=512):
    s = proj.shape[0]

    def body(uv_ref, gain_ref, ws_ref, bs_ref, y_ref):
        params = _gmlp_params(gain_ref, ws_ref, bs_ref)
        for c in range(tm // GM_CHUNK):
            u, v = _gmlp_load(uv_ref, c)
            y = _gmlp_chunk(u, v, *params)
            for g in range(GM_GROUPS):
                y_ref[pl.ds(c * GM_CHUNK, GM_CHUNK), pl.ds(g * LANES, LANES)] = y[g]

    return pl.pallas_call(
        body,
        name=name,
        out_shape=jax.ShapeDtypeStruct((s, GM_WIDTH), F32),
        grid=(s // tm,),
        in_specs=[
            pl.BlockSpec((tm, 2 * GM_WIDTH), lambda i: (i, P_UV // (2 * GM_WIDTH))),
            _full((1, GM_WIDTH)),
            _full((GM_GROUPS, GM_CHUNK, GM_CHUNK)),
            _full((GM_GROUPS, GM_CHUNK, 1)),
        ],
        out_specs=pl.BlockSpec((tm, GM_WIDTH), lambda i: (i, 0)),
        compiler_params=_cparams(("parallel",)),
    )(proj, gain, w_s, b_s)


def _gmlp_bwd(proj, gain, w_s, b_s, dy, name, tm=512):
    s = proj.shape[0]

    def body(uv_ref, gain_ref, ws_ref, bs_ref, dy_ref, duv_ref, dgain_ref, dws_ref, dbs_ref):
        params = _gmlp_params(gain_ref, ws_ref, bs_ref)
        dgain = dws = dbs = None
        for c in range(tm // GM_CHUNK):
            rows = pl.ds(c * GM_CHUNK, GM_CHUNK)
            u, v = _gmlp_load(uv_ref, c)
            _, vjp = jax.vjp(_gmlp_chunk, u, v, *params)
            du, dv, dg, dw, db = vjp([dy_ref[rows, pl.ds(g * LANES, LANES)] for g in range(GM_GROUPS)])
            for g in range(GM_GROUPS):
                duv_ref[rows, pl.ds(g * LANES, LANES)] = du[g].astype(BF16)
                duv_ref[rows, pl.ds(GM_WIDTH + g * LANES, LANES)] = dv[g].astype(BF16)
            if c == 0:
                dgain, dws, dbs = dg, dw, db
            else:
                dgain = [p + q for p, q in zip(dgain, dg)]
                dws = [p + q for p, q in zip(dws, dw)]
                dbs = [p + q for p, q in zip(dbs, db)]
        first = pl.program_id(0) == 0
        for g in range(GM_GROUPS):
            _acc(dgain_ref.at[:, pl.ds(g * LANES, LANES)], dgain[g], first)
            _acc(dws_ref.at[g], dws[g], first)
            _acc(dbs_ref.at[g], dbs[g], first)

    return pl.pallas_call(
        body,
        name=name,
        out_shape=(
            jax.ShapeDtypeStruct((s, 2 * GM_WIDTH), BF16),
            jax.ShapeDtypeStruct((1, GM_WIDTH), F32),
            jax.ShapeDtypeStruct((GM_GROUPS, GM_CHUNK, GM_CHUNK), F32),
            jax.ShapeDtypeStruct((GM_GROUPS, GM_CHUNK, 1), F32),
        ),
        grid=(s // tm,),
        in_specs=[
            pl.BlockSpec((tm, 2 * GM_WIDTH), lambda i: (i, P_UV // (2 * GM_WIDTH))),
            _full((1, GM_WIDTH)),
            _full((GM_GROUPS, GM_CHUNK, GM_CHUNK)),
            _full((GM_GROUPS, GM_CHUNK, 1)),
            pl.BlockSpec((tm, GM_WIDTH), lambda i: (i, 0)),
        ],
        out_specs=(
            pl.BlockSpec((tm, 2 * GM_WIDTH), lambda i: (i, 0)),
            _full((1, GM_WIDTH)),
            _full((GM_GROUPS, GM_CHUNK, GM_CHUNK)),
            _full((GM_GROUPS, GM_CHUNK, 1)),
        ),
        compiler_params=_cparams(("arbitrary",)),
    )(proj, gain, w_s, b_s, dy)


HEAD_BLOCK = MLA_HEADS * LANES


def _mla_pre(rope, cq, ckv, kr, qn_g, kvn_g, wq, wk, wv, qg, kg):
    cosf, sinf, rot = rope
    xq, _ = _rms_stats(cq)
    qn = xq * qn_g
    xk, _ = _rms_stats(ckv)
    kvn = xk * kvn_g

    def head_norm(t, gain):
        r = lax.rsqrt(jnp.sum(t * t, axis=-1, keepdims=True) * (1.0 / MLA_QK_DIM) + EPS)
        th = t * r * gain
        return th * cosf + _exact_nn(th, rot) * sinf

    q = [head_norm(_nn(qn, wq[h]), qg) for h in range(MLA_HEADS)]
    k = [head_norm(_nn(kvn, wk[h]) + kr, kg) for h in range(MLA_HEADS)]
    v = [_nn(kvn, wv[h]) for h in range(MLA_HEADS)]
    return q, k, v


def _mla_pre_specs(tm):
    row = lambda w, off: pl.BlockSpec((tm, w), lambda i: (i, off // w))
    return [
        row(MLA_Q_RANK, P_CQ),
        row(MLA_KV_RANK, P_CKV),
        row(LANES, P_KR),
        pl.BlockSpec((tm, LANES), lambda i: (i, 0)),
        pl.BlockSpec((tm, LANES), lambda i: (i, 0)),
        _full((LANES, LANES)),
        _full((1, MLA_Q_RANK)),
        _full((1, MLA_KV_RANK)),
        _full((MLA_HEADS, MLA_Q_RANK, LANES)),
        _full((MLA_HEADS, MLA_KV_RANK, LANES)),
        _full((MLA_HEADS, MLA_KV_RANK, LANES)),
        _full((1, LANES)),
        _full((1, LANES)),
    ]


def _mla_pre_args(cq_ref, ckv_ref, kr_ref, cos_ref, sin_ref, rot_ref, qn_ref, kvn_ref, wq_ref, wk_ref, wv_ref,
                  qg_ref, kg_ref):
    heads = lambda ref: [ref[h].astype(F32) for h in range(MLA_HEADS)]
    rope = (cos_ref[...], sin_ref[...], rot_ref[...])
    args = (cq_ref[...], ckv_ref[...], kr_ref[...], qn_ref[...], kvn_ref[...], heads(wq_ref), heads(wk_ref),
            heads(wv_ref), qg_ref[...], kg_ref[...])
    return rope, args


def _mla_pre_fwd(proj, cosf, sinf, rot, qn_g, kvn_g, wq, wk, wv, qg, kg, name, tm=256):
    s = proj.shape[0]

    def body(*refs):
        q_ref, k_ref, v_ref = refs[13:]
        rope, args = _mla_pre_args(*refs[:13])
        q, k, v = _mla_pre(rope, *args)
        for h in range(MLA_HEADS):
            cols = pl.ds(h * LANES, LANES)
            q_ref[:, cols] = q[h].astype(BF16)
            k_ref[:, cols] = k[h].astype(BF16)
            v_ref[:, cols] = v[h].astype(BF16)

    out = jax.ShapeDtypeStruct((s, HEAD_BLOCK), BF16)
    blk = pl.BlockSpec((tm, HEAD_BLOCK), lambda i: (i, 0))
    return pl.pallas_call(
        body,
        name=name,
        out_shape=(out, out, out),
        grid=(s // tm,),
        in_specs=_mla_pre_specs(tm),
        out_specs=(blk, blk, blk),
        compiler_params=_cparams(("parallel",)),
    )(proj, proj, proj, cosf, sinf, rot, qn_g, kvn_g, wq, wk, wv, qg, kg)


def _mla_pre_bwd(proj, cosf, sinf, rot, qn_g, kvn_g, wq, wk, wv, qg, kg, dq, dk, dv, name, tm=256):
    s = proj.shape[0]

    def body(*refs):
        dq_ref, dk_ref, dv_ref = refs[13:16]
        dcq_ref, dckv_ref, dkr_ref, dqn_ref, dkvn_ref, dwq_ref, dwk_ref, dwv_ref, dqg_ref, dkg_ref = refs[16:]
        rope, args = _mla_pre_args(*refs[:13])
        _, vjp = jax.vjp(functools.partial(_mla_pre, rope), *args)
        heads = lambda ref: [ref[:, pl.ds(h * LANES, LANES)] for h in range(MLA_HEADS)]
        dcq, dckv, dkr, dqn, dkvn, dwq, dwk, dwv, dqg, dkg = vjp((heads(dq_ref), heads(dk_ref), heads(dv_ref)))
        dcq_ref[...] = dcq.astype(BF16)
        dckv_ref[...] = dckv.astype(BF16)
        dkr_ref[...] = dkr.astype(BF16)
        first = pl.program_id(0) == 0
        _acc(dqn_ref, dqn, first)
        _acc(dkvn_ref, dkvn, first)
        _acc(dqg_ref, dqg, first)
        _acc(dkg_ref, dkg, first)
        for h in range(MLA_HEADS):
            _acc(dwq_ref.at[h], dwq[h], first)
            _acc(dwk_ref.at[h], dwk[h], first)
            _acc(dwv_ref.at[h], dwv[h], first)

    hb = pl.BlockSpec((tm, HEAD_BLOCK), lambda i: (i, 0))
    row = lambda w: pl.BlockSpec((tm, w), lambda i: (i, 0))
    sds = jax.ShapeDtypeStruct
    return pl.pallas_call(
        body,
        name=name,
        out_shape=(
            sds((s, MLA_Q_RANK), BF16), sds((s, MLA_KV_RANK), BF16), sds((s, LANES), BF16),
            sds((1, MLA_Q_RANK), F32), sds((1, MLA_KV_RANK), F32),
            sds((MLA_HEADS, MLA_Q_RANK, LANES), F32), sds((MLA_HEADS, MLA_KV_RANK, LANES), F32),
            sds((MLA_HEADS, MLA_KV_RANK, LANES), F32),
            sds((1, LANES), F32), sds((1, LANES), F32),
        ),
        grid=(s // tm,),
        in_specs=_mla_pre_specs(tm) + [hb, hb, hb],
        out_specs=(
            row(MLA_Q_RANK), row(MLA_KV_RANK), row(LANES),
            _full((1, MLA_Q_RANK)), _full((1, MLA_KV_RANK)),
            _full((MLA_HEADS, MLA_Q_RANK, LANES)), _full((MLA_HEADS, MLA_KV_RANK, LANES)),
            _full((MLA_HEADS, MLA_KV_RANK, LANES)),
            _full((1, LANES)), _full((1, LANES)),
        ),
        compiler_params=_cparams(("arbitrary",)),
    )(proj, proj, proj, cosf, sinf, rot, qn_g, kvn_g, wq, wk, wv, qg, kg, dq, dk, dv)


ATT_SCALE = MLA_QK_DIM ** -0.5
NEG_BIG = -1e30


def _att_scores(q, k, diagonal):
    s = _nt(q, k) * ATT_SCALE
    if diagonal:
        s = jnp.where(_iota(s.shape, 0) >= _iota(s.shape, 1), s, NEG_BIG)
    return s


def _key_loop(lo, hi, t, step):
    def body(i, carry):
        step(pl.ds(pl.multiple_of(i * t, t), t))
        return carry

    lax.fori_loop(lo, hi, body, 0)


def _attention_fwd(q, k, v, name, t=512):
    s = q.shape[0]
    n = s // t

    def body(q_ref, k_ref, v_ref, o_ref, lse_ref, m_scr, l_scr, acc_scr):
        qi = pl.program_id(1)
        m_scr[...] = jnp.full_like(m_scr, NEG_BIG)
        l_scr[...] = jnp.zeros_like(l_scr)
        acc_scr[...] = jnp.zeros_like(acc_scr)

        def step(rows, diagonal=False):
            sc = _att_scores(q_ref[...], k_ref[rows, :], diagonal)
            m_old = m_scr[...]
            m_new = jnp.maximum(m_old, jnp.max(sc, axis=-1, keepdims=True))
            p = jnp.exp(sc - m_new)
            alpha = jnp.exp(m_old - m_new)
            l_scr[...] = alpha * l_scr[...] + jnp.sum(p, axis=-1, keepdims=True)
            acc_scr[...] = alpha * acc_scr[...] + _nn(p, v_ref[rows, :])
            m_scr[...] = m_new

        _key_loop(0, qi, t, step)
        step(pl.ds(pl.multiple_of(qi * t, t), t), diagonal=True)
        o_ref[...] = acc_scr[...] / l_scr[...]
        lse_ref[...] = jnp.broadcast_to(m_scr[...] + jnp.log(l_scr[...]), (t, LANES))

    qspec = pl.BlockSpec((t, LANES), lambda h, qi: (qi, h))
    kspec = pl.BlockSpec((s, LANES), lambda h, qi: (0, h))
    out = jax.ShapeDtypeStruct((s, HEAD_BLOCK), F32)
    return pl.pallas_call(
        body,
        name=name,
        out_shape=(out, out),
        grid=(MLA_HEADS, n),
        in_specs=[qspec, kspec, kspec],
        out_specs=(qspec, qspec),
        scratch_shapes=[pltpu.VMEM((t, 1), F32), pltpu.VMEM((t, 1), F32), pltpu.VMEM((t, LANES), F32)],
        compiler_params=_cparams(("parallel", "parallel")),
    )(q, k, v)


def _attention_bwd(q, k, v, o, lse, do, name, t=512):
    s = q.shape[0]
    n = s // t

    def dq_body(q_ref, k_ref, v_ref, o_ref, lse_ref, do_ref, dq_ref, delta_ref, acc_scr):
        qi = pl.program_id(1)
        do = do_ref[...]
        delta = jnp.sum(do * o_ref[...], axis=-1, keepdims=True)
        delta_ref[...] = jnp.broadcast_to(delta, (t, LANES))
        acc_scr[...] = jnp.zeros_like(acc_scr)

        def step(rows, diagonal=False):
            kb = k_ref[rows, :]
            p = jnp.exp(_att_scores(q_ref[...], kb, diagonal) - lse_ref[:, 0:1])
            ds = p * (_nt(do, v_ref[rows, :]) - delta) * ATT_SCALE
            acc_scr[...] += _nn(ds, kb)

        _key_loop(0, qi, t, step)
        step(pl.ds(pl.multiple_of(qi * t, t), t), diagonal=True)
        dq_ref[...] = acc_scr[...]

    def dkv_body(q_ref, k_ref, v_ref, lse_ref, delta_ref, do_ref, dk_ref, dv_ref, dk_scr, dv_scr):
        ki = pl.program_id(1)
        dk_scr[...] = jnp.zeros_like(dk_scr)
        dv_scr[...] = jnp.zeros_like(dv_scr)

        def step(rows, diagonal=False):
            qb = q_ref[rows, :]
            dob = do_ref[rows, :]
            p = jnp.exp(_att_scores(qb, k_ref[...], diagonal) - lse_ref[rows, 0:1])
            dv_scr[...] += _tn(p, dob)
            ds = p * (_nt(dob, v_ref[...]) - delta_ref[rows, 0:1]) * ATT_SCALE
            dk_scr[...] += _tn(ds, qb)

        step(pl.ds(pl.multiple_of(ki * t, t), t), diagonal=True)
        _key_loop(ki + 1, n, t, step)
        dk_ref[...] = dk_scr[...]
        dv_ref[...] = dv_scr[...]

    out = jax.ShapeDtypeStruct((s, HEAD_BLOCK), F32)
    blk = pl.BlockSpec((t, LANES), lambda h, i: (i, h))
    head = pl.BlockSpec((s, LANES), lambda h, i: (0, h))
    dq, delta = pl.pallas_call(
        dq_body,
        name=name + "_dq",
        out_shape=(out, out),
        grid=(MLA_HEADS, n),
        in_specs=[blk, head, head, blk, blk, blk],
        out_specs=(blk, blk),
        scratch_shapes=[pltpu.VMEM((t, LANES), F32)],
        compiler_params=_cparams(("parallel", "parallel")),
    )(q, k, v, o, lse, do)
    dk, dv = pl.pallas_call(
        dkv_body,
        name=name + "_dkv",
        out_shape=(out, out),
        grid=(MLA_HEADS, n),
        in_specs=[head, blk, blk, head, head, head],
        out_specs=(blk, blk),
        scratch_shapes=[pltpu.VMEM((t, LANES), F32), pltpu.VMEM((t, LANES), F32)],
        compiler_params=_cparams(("parallel", "parallel")),
    )(q, k, v, lse, delta, do)
    return dq, dk, dv


HALO = 8


def _conv_fwd(proj, w, b, name, tm=512):
    s = proj.shape[0]
    cb = P_XBC // SSD_CONV_DIM

    def body(x_ref, halo_ref, w_ref, b_ref, y_ref, cat_scr):
        i = pl.program_id(0)
        cat_scr[pl.ds(0, HALO), :] = jnp.where(i > 0, halo_ref[...], 0.0)
        cat_scr[pl.ds(HALO, tm), :] = x_ref[...]
        pre = b_ref[...]
        for j in range(SSD_CONV):
            pre = pre + w_ref[pl.ds(SSD_CONV - 1 - j, 1), :] * cat_scr[pl.ds(HALO - j, tm), :]
        y_ref[...] = _silu(pre)

    return pl.pallas_call(
        body,
        name=name,
        out_shape=jax.ShapeDtypeStruct((s, SSD_CONV_DIM), F32),
        grid=(s // tm,),
        in_specs=[
            pl.BlockSpec((tm, SSD_CONV_DIM), lambda i: (i, cb)),
            pl.BlockSpec((HALO, SSD_CONV_DIM), lambda i: (jnp.maximum(i * (tm // HALO) - 1, 0), cb)),
            _full((SSD_CONV, SSD_CONV_DIM)),
            _full((1, SSD_CONV_DIM)),
        ],
        out_specs=pl.BlockSpec((tm, SSD_CONV_DIM), lambda i: (i, 0)),
        scratch_shapes=[pltpu.VMEM((tm + HALO, SSD_CONV_DIM), F32)],
        compiler_params=_cparams(("parallel",)),
    )(proj, proj, w, b)


def _conv_bwd(proj, w, b, dact, name, tm=512):
    s = proj.shape[0]
    cb = P_XBC // SSD_CONV_DIM
    n = s // tm

    def pre_body(x_ref, halo_ref, w_ref, b_ref, dact_ref, dpre_ref, dw_ref, db_ref, cat_scr):
        i = pl.program_id(0)
        cat_scr[pl.ds(0, HALO), :] = jnp.where(i > 0, halo_ref[...], 0.0)
        cat_scr[pl.ds(HALO, tm), :] = x_ref[...]
        pre = b_ref[...]
        for j in range(SSD_CONV):
            pre = pre + w_ref[pl.ds(SSD_CONV - 1 - j, 1), :] * cat_scr[pl.ds(HALO - j, tm), :]
        sg = _sigmoid(pre)
        dpre = dact_ref[...] * (sg * (1.0 + pre * (1.0 - sg)))
        dpre_ref[...] = dpre
        first = i == 0
        _acc_rows(db_ref, dpre, first)
        for j in range(SSD_CONV):
            _acc_rows(dw_ref.at[pl.ds(SSD_CONV - 1 - j, 1), :], dpre * cat_scr[pl.ds(HALO - j, tm), :], first)

    dpre, dw, db = pl.pallas_call(
        pre_body,
        name=name + "_pre",
        out_shape=(
            jax.ShapeDtypeStruct((s, SSD_CONV_DIM), F32),
            jax.ShapeDtypeStruct((SSD_CONV, SSD_CONV_DIM), F32),
            jax.ShapeDtypeStruct((1, SSD_CONV_DIM), F32),
        ),
        grid=(n,),
        in_specs=[
            pl.BlockSpec((tm, SSD_CONV_DIM), lambda i: (i, cb)),
            pl.BlockSpec((HALO, SSD_CONV_DIM), lambda i: (jnp.maximum(i * (tm // HALO) - 1, 0), cb)),
            _full((SSD_CONV, SSD_CONV_DIM)),
            _full((1, SSD_CONV_DIM)),
            pl.BlockSpec((tm, SSD_CONV_DIM), lambda i: (i, 0)),
        ],
        out_specs=(
            pl.BlockSpec((tm, SSD_CONV_DIM), lambda i: (i, 0)),
            _full((SSD_CONV, SSD_CONV_DIM)),
            _full((1, SSD_CONV_DIM)),
        ),
        scratch_shapes=[pltpu.VMEM((tm + HALO, SSD_CONV_DIM), F32)],
        compiler_params=_cparams(("arbitrary",)),
    )(proj, proj, w, b, dact)

    def dx_body(d_ref, halo_ref, w_ref, dx_ref, cat_scr):
        i = pl.program_id(0)
        cat_scr[pl.ds(0, tm), :] = d_ref[...]
        cat_scr[pl.ds(tm, HALO), :] = jnp.where(i < n - 1, halo_ref[...], 0.0)
        dx = jnp.zeros((tm, SSD_CONV_DIM), F32)
        for j in range(SSD_CONV):
            dx = dx + w_ref[pl.ds(SSD_CONV - 1 - j, 1), :] * cat_scr[pl.ds(j, tm), :]
        dx_ref[...] = dx.astype(BF16)

    dx = pl.pallas_call(
        dx_body,
        name=name + "_dx",
        out_shape=jax.ShapeDtypeStruct((s, SSD_CONV_DIM), BF16),
        grid=(n,),
        in_specs=[
            pl.BlockSpec((tm, SSD_CONV_DIM), lambda i: (i, 0)),
            pl.BlockSpec((HALO, SSD_CONV_DIM), lambda i: (jnp.minimum((i + 1) * (tm // HALO), s // HALO - 1), 0)),
            _full((SSD_CONV, SSD_CONV_DIM)),
        ],
        out_specs=pl.BlockSpec((tm, SSD_CONV_DIM), lambda i: (i, 0)),
        scratch_shapes=[pltpu.VMEM((tm + HALO, SSD_CONV_DIM), F32)],
        compiler_params=_cparams(("parallel",)),
    )(dpre, dpre, w)
    return dx, dw, db


N_PAIR = SSD_HEADS // 2


def _ssd_chunk(xs, bm, cm, z, dtp, state, dtb, alog, dskip, ng):
    t = SSD_CHUNK
    lane = _iota((1, LANES), 1)
    row = _iota((LANES, 1), 0)
    dt_all = jnp.where(lane < SSD_HEADS, _softplus(dtp + dtb), 0.0)
    da = dt_all * (-jnp.exp(alog))
    causal = _iota((t, t), 0) >= _iota((t, t), 1)
    cs = _exact_nn(causal.astype(F32), da)
    cs_t = cs.T
    tot = jnp.sum(da, axis=0, keepdims=True)
    col = lambda m, h: jnp.sum(jnp.where(lane == h, m, 0.0), axis=1, keepdims=True)
    rowv = lambda m, h: jnp.sum(jnp.where(row == h, m, 0.0), axis=0, keepdims=True)
    low = lane < SSD_HEAD_DIM
    cb = [_nt(cm[g], bm[g]) for g in range(SSD_GROUPS)]
    gated, new_state = [], []
    for j in range(N_PAIR):
        g = j // (N_PAIR // SSD_GROUPS)
        h0, h1 = 2 * j, 2 * j + 1
        y = jnp.zeros((t, LANES), F32)
        for h, mask in ((h0, low), (h1, jnp.logical_not(low))):
            lmat = jnp.exp(jnp.where(causal, col(cs, h) - rowv(cs_t, h), NEG_BIG))
            y = y + _nn(cb[g] * lmat, jnp.where(mask, xs[j] * col(dt_all, h), 0.0))
        cs_p = jnp.where(low, col(cs, h0), col(cs, h1))
        dt_p = jnp.where(low, col(dt_all, h0), col(dt_all, h1))
        tot_p = jnp.where(low, col(tot, h0), col(tot, h1))
        tot_c = jnp.where(row < SSD_HEAD_DIM, col(tot, h0), col(tot, h1))
        d_p = jnp.where(low, col(dskip, h0), col(dskip, h1))
        xdt = xs[j] * dt_p
        y = y + _nt(cm[g], state[j]) * jnp.exp(cs_p) + xs[j] * d_p
        new_state.append(state[j] * jnp.exp(tot_c) + _tn(xdt * jnp.exp(tot_p - cs_p), bm[g]))
        gated.append(y * _silu(z[j]))
    out = []
    per_group = N_PAIR // SSD_GROUPS
    for g in range(SSD_GROUPS):
        blocks = gated[g * per_group:(g + 1) * per_group]
        ms = sum(jnp.sum(v * v, axis=-1, keepdims=True) for v in blocks) * (1.0 / (per_group * LANES))
        r = lax.rsqrt(ms + EPS)
        out += [v * r * ng[g * per_group + i] for i, v in enumerate(blocks)]
    return out, new_state


def _ssd_specs(rev, nc):
    idx = (lambda c: nc - 1 - c) if rev else (lambda c: c)
    t = SSD_CHUNK
    return [
        pl.BlockSpec((t, SSD_CONV_DIM), lambda c: (idx(c), 0)),
        pl.BlockSpec((t, SSD_INNER), lambda c: (idx(c), P_Z // SSD_INNER)),
        pl.BlockSpec((t, LANES), lambda c: (idx(c), P_DT // LANES)),
        _full((1, LANES)), _full((1, LANES)), _full((1, LANES)), _full((1, SSD_INNER)),
    ]


def _ssd_args(act_ref, z_ref, dt_ref, dtb_ref, alog_ref, dskip_ref, ng_ref):
    blk = lambda ref, off, n: [ref[:, pl.ds(off + i * LANES, LANES)] for i in range(n)]
    xs = blk(act_ref, 0, N_PAIR)
    bm = blk(act_ref, SSD_INNER, SSD_GROUPS)
    cm = blk(act_ref, SSD_INNER + SSD_GROUPS * SSD_STATE, SSD_GROUPS)
    return xs, bm, cm, blk(z_ref, 0, N_PAIR), dt_ref[...], dtb_ref[...], alog_ref[...], dskip_ref[...], blk(ng_ref, 0, N_PAIR)


def _ssd_fwd(act, proj, dtb, alog, dskip, ng, name):
    s = act.shape[0]
    nc = s // SSD_CHUNK

    def body(act_ref, z_ref, dt_ref, dtb_ref, alog_ref, dskip_ref, ng_ref, y_ref, st_ref, st_scr):
        @pl.when(pl.program_id(0) == 0)
        def _():
            st_scr[...] = jnp.zeros_like(st_scr)

        xs, bm, cm, z, dtp, dtb_v, alog_v, dskip_v, ng_v = _ssd_args(act_ref, z_ref, dt_ref, dtb_ref, alog_ref, dskip_ref, ng_ref)
        state = [st_scr[j] for j in range(N_PAIR)]
        st_ref[0] = st_scr[...]
        y, new_state = _ssd_chunk(xs, bm, cm, z, dtp, state, dtb_v, alog_v, dskip_v, ng_v)
        for j in range(N_PAIR):
            y_ref[:, pl.ds(j * LANES, LANES)] = y[j]
            st_scr[j] = new_state[j]

    return pl.pallas_call(
        body,
        name=name,
        out_shape=(
            jax.ShapeDtypeStruct((s, SSD_INNER), F32),
            jax.ShapeDtypeStruct((nc, N_PAIR, LANES, SSD_STATE), F32),
        ),
        grid=(nc,),
        in_specs=_ssd_specs(False, nc),
        out_specs=(
            pl.BlockSpec((SSD_CHUNK, SSD_INNER), lambda c: (c, 0)),
            pl.BlockSpec((1, N_PAIR, LANES, SSD_STATE), lambda c: (c, 0, 0, 0)),
        ),
        scratch_shapes=[pltpu.VMEM((N_PAIR, LANES, SSD_STATE), F32)],
        compiler_params=_cparams(("arbitrary",)),
    )(act, proj, proj, dtb, alog, dskip, ng)


def _ssd_bwd(act, proj, dtb, alog, dskip, ng, states, dy, name):
    s = act.shape[0]
    nc = s // SSD_CHUNK

    def body(act_ref, z_ref, dt_ref, dtb_ref, alog_ref, dskip_ref, ng_ref, st_ref, dy_ref,
             dact_ref, dz_ref, ddt_ref, ddtb_ref, dalog_ref, ddskip_ref, dng_ref, dst_scr):
        first = pl.program_id(0) == 0

        @pl.when(first)
        def _():
            dst_scr[...] = jnp.zeros_like(dst_scr)

        xs, bm, cm, z, dtp, dtb_v, alog_v, dskip_v, ng_v = _ssd_args(act_ref, z_ref, dt_ref, dtb_ref, alog_ref, dskip_ref, ng_ref)
        state = [st_ref[0, j] for j in range(N_PAIR)]
        _, vjp = jax.vjp(_ssd_chunk, xs, bm, cm, z, dtp, state, dtb_v, alog_v, dskip_v, ng_v)
        dy_v = [dy_ref[:, pl.ds(j * LANES, LANES)] for j in range(N_PAIR)]
        dxs, dbm, dcm, dz, ddtp, dstate, ddtb, dalog, ddskip, dng = vjp((dy_v, [dst_scr[j] for j in range(N_PAIR)]))
        for i, v in enumerate(dxs + dbm + dcm):
            dact_ref[:, pl.ds(i * LANES, LANES)] = v
        for j in range(N_PAIR):
            dz_ref[:, pl.ds(j * LANES, LANES)] = dz[j].astype(BF16)
            dst_scr[j] = dstate[j]
            _acc(dng_ref.at[:, pl.ds(j * LANES, LANES)], dng[j], first)
        ddt_ref[...] = ddtp.astype(BF16)
        _acc(ddtb_ref, ddtb, first)
        _acc(dalog_ref, dalog, first)
        _acc(ddskip_ref, ddskip, first)

    rv = lambda c: nc - 1 - c
    sds = jax.ShapeDtypeStruct
    return pl.pallas_call(
        body,
        name=name,
        out_shape=(
            sds((s, SSD_CONV_DIM), F32), sds((s, SSD_INNER), BF16), sds((s, LANES), BF16),
            sds((1, LANES), F32), sds((1, LANES), F32), sds((1, LANES), F32), sds((1, SSD_INNER), F32),
        ),
        grid=(nc,),
        in_specs=_ssd_specs(True, nc) + [
            pl.BlockSpec((1, N_PAIR, LANES, SSD_STATE), lambda c: (rv(c), 0, 0, 0)),
            pl.BlockSpec((SSD_CHUNK, SSD_INNER), lambda c: (rv(c), 0)),
        ],
        out_specs=(
            pl.BlockSpec((SSD_CHUNK, SSD_CONV_DIM), lambda c: (rv(c), 0)),
            pl.BlockSpec((SSD_CHUNK, SSD_INNER), lambda c: (rv(c), 0)),
            pl.BlockSpec((SSD_CHUNK, LANES), lambda c: (rv(c), 0)),
            _full((1, LANES)), _full((1, LANES)), _full((1, LANES)), _full((1, SSD_INNER)),
        ),
        scratch_shapes=[pltpu.VMEM((N_PAIR, LANES, SSD_STATE), F32)],
        compiler_params=_cparams(("arbitrary",)),
    )(act, proj, proj, dtb, alog, dskip, ng, states, dy)


def _merge_specs(tm):
    row = lambda w: pl.BlockSpec((tm, w), lambda i: (i, 0))
    return [
        row(GM_WIDTH), row(HEAD_BLOCK), row(SSD_INNER),
        pl.BlockSpec((tm, N_BRANCH * D_MODEL), lambda i: (i, P_GATES // (N_BRANCH * D_MODEL))),
        row(D_MODEL),
        _full((GM_WIDTH, D_MODEL)), _full((HEAD_BLOCK, D_MODEL)), _full((SSD_INNER, D_MODEL)), _full((D_MODEL, D_MODEL)),
    ]


def _merge_fwd(ya, yb, yc, proj, x1, pa, pb, pc, wo, name, tm=256):
    s = x1.shape[0]

    def body(ya_ref, yb_ref, yc_ref, gates_ref, x1_ref, pa_ref, pb_ref, pc_ref, wo_ref, x2_ref, mg_ref):
        merged = jnp.zeros((tm, D_MODEL), F32)
        for i, (y_ref, p_ref) in enumerate(((ya_ref, pa_ref), (yb_ref, pb_ref), (yc_ref, pc_ref))):
            gate = _sigmoid(gates_ref[:, pl.ds(i * D_MODEL, D_MODEL)])
            merged = merged + gate * _nn(y_ref[...], p_ref[...])
        mg_ref[...] = merged.astype(BF16)
        x2_ref[...] = x1_ref[...] + _nn(merged, wo_ref[...])

    row = lambda w: pl.BlockSpec((tm, w), lambda i: (i, 0))
    return pl.pallas_call(
        body,
        name=name,
        out_shape=(jax.ShapeDtypeStruct((s, D_MODEL), F32), jax.ShapeDtypeStruct((s, D_MODEL), BF16)),
        grid=(s // tm,),
        in_specs=_merge_specs(tm),
        out_specs=(row(D_MODEL), row(D_MODEL)),
        compiler_params=_cparams(("parallel",)),
    )(ya, yb, yc, proj, x1, pa, pb, pc, wo)


def _merge_bwd(ya, yb, yc, proj, dx2, pa, pb, pc, wo, name, tm=256):
    s = dx2.shape[0]

    def body(ya_ref, yb_ref, yc_ref, gates_ref, dx2_ref, pa_ref, pb_ref, pc_ref, wo_ref,
             dya_ref, dyb_ref, dyc_ref, dgates_ref, ta_ref, tb_ref, tc_ref):
        dmerged = _nt(dx2_ref[...], wo_ref[...])
        branches = ((ya_ref, pa_ref, dya_ref, ta_ref), (yb_ref, pb_ref, dyb_ref, tb_ref), (yc_ref, pc_ref, dyc_ref, tc_ref))
        for i, (y_ref, p_ref, dy_ref, t_ref) in enumerate(branches):
            cols = pl.ds(i * D_MODEL, D_MODEL)
            gate = _sigmoid(gates_ref[:, cols])
            dgates_ref[:, cols] = (dmerged * _nn(y_ref[...], p_ref[...]) * gate * (1.0 - gate)).astype(BF16)
            dt = (dmerged * gate).astype(BF16)
            t_ref[...] = dt
            dy_ref[...] = _nt(dt, p_ref[...])

    row = lambda w: pl.BlockSpec((tm, w), lambda i: (i, 0))
    sds = jax.ShapeDtypeStruct
    return pl.pallas_call(
        body,
        name=name,
        out_shape=(
            sds((s, GM_WIDTH), F32), sds((s, HEAD_BLOCK), F32), sds((s, SSD_INNER), F32),
            sds((s, N_BRANCH * D_MODEL), BF16),
            sds((s, D_MODEL), BF16), sds((s, D_MODEL), BF16), sds((s, D_MODEL), BF16),
        ),
        grid=(s // tm,),
        in_specs=_merge_specs(tm),
        out_specs=(row(GM_WIDTH), row(HEAD_BLOCK), row(SSD_INNER), row(N_BRANCH * D_MODEL),
                   row(D_MODEL), row(D_MODEL), row(D_MODEL)),
        compiler_params=_cparams(("parallel",)),
    )(ya, yb, yc, proj, dx2, pa, pb, pc, wo)


def _loss_head(y, target, name, tm=512):
    s, d = y.shape

    def body(y_ref, t_ref, dy_ref, loss_ref):
        err = y_ref[...] - t_ref[...]
        dy_ref[...] = err * (1.0 / d)
        part = jnp.sum(jnp.sum(err * err, axis=1, keepdims=True), axis=0, keepdims=True) * (0.5 / d)
        _acc(loss_ref, jnp.broadcast_to(part, (1, LANES)), pl.program_id(0) == 0)

    return pl.pallas_call(
        body,
        name=name,
        out_shape=(jax.ShapeDtypeStruct((s, d), F32), jax.ShapeDtypeStruct((1, LANES), F32)),
        grid=(s // tm,),
        in_specs=[pl.BlockSpec((tm, d), lambda i: (i, 0)), pl.BlockSpec((tm, d), lambda i: (i, 0))],
        out_specs=(pl.BlockSpec((tm, d), lambda i: (i, 0)), _full((1, LANES))),
        compiler_params=_cparams(("arbitrary",)),
    )(y, target)


IN_SHARD = IN_COLS // N_DEV
P_OF_PIECE = (P_UV, P_CQ, P_CKV, P_KR + MLA_NOPE, P_Z, P_XBC, P_DT, P_GATES)


def _pad_lanes(w, n=LANES):
    return jnp.pad(w, [(0, 0)] * (w.ndim - 1) + [(0, n - w.shape[-1])])


def _in_proj_layout(blocks):
    def cols(i):
        a, b, out = IN_OFFSETS[i], IN_OFFSETS[i] + IN_WIDTHS[i], []
        while a < b:
            k, lo = divmod(a, IN_SHARD)
            hi = min(IN_SHARD, lo + b - a)
            out.append(blocks[k, :, lo:hi])
            a += hi - lo
        return out

    zeros = lambda n: jnp.zeros((D_MODEL, n), blocks.dtype)
    uv, cq, ckv, kr, z, xbc, dt, gates = (cols(i) for i in range(8))
    return jnp.concatenate(uv + xbc + z + [zeros(MLA_NOPE)] + kr + [zeros(LANES - MLA_QK_DIM)] + cq + gates + ckv
                           + dt + [zeros(LANES - SSD_HEADS)], axis=1)


def _in_proj_unlayout(dw):
    out = []
    for k in range(N_DEV):
        a, b, parts = k * IN_SHARD, (k + 1) * IN_SHARD, []
        for i in range(8):
            lo, hi = max(a, IN_OFFSETS[i]), min(b, IN_OFFSETS[i] + IN_WIDTHS[i])
            if lo < hi:
                at = P_OF_PIECE[i] + lo - IN_OFFSETS[i]
                parts.append(dw[:, at:at + hi - lo])
        out.append(jnp.concatenate(parts, axis=1))
    return jnp.stack(out)


def _row(v, n=None):
    v = v.reshape(1, -1)
    return v if n is None else jnp.pad(v, ((0, 0), (0, n - v.shape[1])))


def _layer_weights(gw, vec, l):
    kv = gw["mla_w_ukv"][:, l]
    branch = jnp.moveaxis(gw["w_branch"][:, l], 0, 2).reshape(N_BRANCH, GM_WIDTH, D_MODEL)
    return dict(
        ffn_w_in=gw["ffn_w_in"], ffn_w_out=gw["ffn_w_out"], ffn1_at=l, ffn2_at=gw["w_out"].shape[1] + l,
        ffn1_norm=_row(vec["ffn1_norm"][l]), ffn2_norm=_row(vec["ffn2_norm"][l]),
        mix_norm=_row(vec["mix_norm"][l]), w_in=_in_proj_layout(gw["w_in"][:, l]),
        gm_v_norm=_row(vec["gm_v_norm"][l]), gm_w_s=vec["gm_w_s"][l], gm_b_s=vec["gm_b_s"][l][..., None],
        q_norm=_row(vec["mla_q_norm"][l]), kv_norm=_row(vec["mla_kv_norm"][l]),
        wq=_pad_lanes(gw["mla_w_uq"][:, l]), wk=_pad_lanes(kv[:, :, :MLA_NOPE]), wv=_pad_lanes(kv[:, :, MLA_NOPE:]),
        q_gain=_row(vec["mla_q_gain"][l], LANES), k_gain=_row(vec["mla_k_gain"][l], LANES),
        conv_w=jnp.moveaxis(gw["ssd_conv_w"][:, l], 0, 1).reshape(SSD_CONV, SSD_CONV_DIM).astype(F32),
        conv_b=_row(vec["ssd_conv_b"][l]),
        dt_bias=_row(vec["ssd_dt_bias"][l], LANES), a_log=_row(vec["ssd_a_log"][l], LANES),
        d_skip=_row(vec["ssd_d"][l], LANES), ssd_norm=_row(vec["ssd_norm"][l]),
        pa=branch[0],
        pb=jnp.pad(branch[1].reshape(MLA_HEADS, MLA_V, D_MODEL), ((0, 0), (0, LANES - MLA_V), (0, 0))).reshape(HEAD_BLOCK, D_MODEL),
        pc=branch[2], wo=gw["w_out"][:, l].reshape(D_MODEL, D_MODEL),
    )


def _layer_fwd(x, k, rope):
    cosf, sinf, rot = rope
    x1, gu1 = _ffn_fwd(x, k["ffn1_norm"], k["ffn_w_in"], k["ffn_w_out"], k["ffn1_at"], "ffn1_fwd")
    h = _rmsnorm_fwd(x1, k["mix_norm"], "mix_norm_fwd")
    proj = _matmul(h, k["w_in"], "nn", F32, "in_proj_fwd", tn=2176)
    ya = _gmlp_fwd(proj, k["gm_v_norm"], k["gm_w_s"], k["gm_b_s"], "gmlp_fwd")
    q, kk, v = _mla_pre_fwd(proj, cosf, sinf, rot, k["q_norm"], k["kv_norm"], k["wq"], k["wk"], k["wv"],
                            k["q_gain"], k["k_gain"], "mla_pre_fwd")
    yb, lse = _attention_fwd(q, kk, v, "attention_fwd")
    act = _conv_fwd(proj, k["conv_w"], k["conv_b"], "conv_fwd")
    yc, states = _ssd_fwd(act, proj, k["dt_bias"], k["a_log"], k["d_skip"], k["ssd_norm"], "ssd_fwd")
    x2, merged = _merge_fwd(ya, yb, yc, proj, x1, k["pa"], k["pb"], k["pc"], k["wo"], "merge_fwd")
    x3, gu2 = _ffn_fwd(x2, k["ffn2_norm"], k["ffn_w_in"], k["ffn_w_out"], k["ffn2_at"], "ffn2_fwd")
    saved = dict(x=x, x1=x1, x2=x2, gu1=gu1, gu2=gu2, h=h, proj=proj, ya=ya, yb=yb, yc=yc, q=q, k=kk, v=v,
                 lse=lse, act=act, states=states, merged=merged)
    return x3, saved


def _layer_bwd(dx3, k, sv, rope):
    cosf, sinf, rot = rope
    g = {}
    dx2, g["ffn2_norm"], g["ffn2_w_in"], g["ffn2_w_out"] = _ffn_bwd(
        sv["x2"], k["ffn2_norm"], k["ffn_w_in"], k["ffn_w_out"], k["ffn2_at"], sv["gu2"], dx3, "ffn2_bwd")
    proj = sv["proj"]
    dya, dyb, dyc, dgates, ta, tb, tc = _merge_bwd(sv["ya"], sv["yb"], sv["yc"], proj, dx2, k["pa"], k["pb"], k["pc"],
                                                   k["wo"], "merge_bwd")
    g["w_out"] = _matmul(sv["merged"], dx2, "tn", BF16, "w_out_grad").reshape(N_DEV, D_MODEL // N_DEV, D_MODEL)
    dpa = _matmul(sv["ya"], ta, "tn", BF16, "branch_a_grad")
    dpb = _matmul(sv["yb"], tb, "tn", BF16, "branch_b_grad")
    dpc = _matmul(sv["yc"], tc, "tn", BF16, "branch_c_grad")
    branch = jnp.stack([dpa, dpb.reshape(MLA_HEADS, LANES, D_MODEL)[:, :MLA_V].reshape(GM_WIDTH, D_MODEL), dpc])
    g["w_branch"] = jnp.moveaxis(branch.reshape(N_BRANCH, GM_WIDTH, N_DEV, LANES), 2, 0)
    duv, g["gm_v_norm"], g["gm_w_s"], dbs = _gmlp_bwd(proj, k["gm_v_norm"], k["gm_w_s"], k["gm_b_s"], dya, "gmlp_bwd")
    g["gm_b_s"] = dbs[..., 0]
    dq, dk, dv = _attention_bwd(sv["q"], sv["k"], sv["v"], sv["yb"], sv["lse"], dyb, "attention_bwd")
    dcq, dckv, dkr, dqn, dkvn, dwq, dwk, dwv, dqg, dkg = _mla_pre_bwd(
        proj, cosf, sinf, rot, k["q_norm"], k["kv_norm"], k["wq"], k["wk"], k["wv"], k["q_gain"], k["k_gain"],
        dq, dk, dv, "mla_pre_bwd")
    g["mla_q_norm"], g["mla_kv_norm"] = dqn, dkvn
    g["mla_w_uq"] = dwq[:, :, :MLA_QK_DIM].astype(BF16)
    g["mla_w_ukv"] = jnp.concatenate([dwk[:, :, :MLA_NOPE], dwv[:, :, :MLA_V]], axis=-1).astype(BF16)
    g["mla_q_gain"], g["mla_k_gain"] = dqg[:, :MLA_QK_DIM], dkg[:, :MLA_QK_DIM]
    dact, dz, ddt, ddtb, dalog, ddsk, g["ssd_norm"] = _ssd_bwd(
        sv["act"], proj, k["dt_bias"], k["a_log"], k["d_skip"], k["ssd_norm"], sv["states"], dyc, "ssd_bwd")
    g["ssd_dt_bias"], g["ssd_a_log"], g["ssd_d"] = ddtb[:, :SSD_HEADS], dalog[:, :SSD_HEADS], ddsk[:, :SSD_HEADS]
    dxbc, dcw, g["ssd_conv_b"] = _conv_bwd(proj, k["conv_w"], k["conv_b"], dact, "conv_bwd")
    g["ssd_conv_w"] = jnp.moveaxis(dcw.reshape(SSD_CONV, N_DEV, LANES), 1, 0).astype(BF16)
    dproj = jnp.concatenate([duv, dxbc, dz, dkr, dcq, dgates, dckv, ddt], axis=1)
    dh = _matmul(dproj, k["w_in"], "nt", BF16, "in_proj_dh", tk=2176)
    g["w_in"] = _in_proj_unlayout(_matmul(sv["h"], dproj, "tn", BF16, "in_proj_grad", tn=2176))
    dx1, g["mix_norm"] = _rmsnorm_bwd(sv["x1"], k["mix_norm"], dh, dx2, "mix_norm_bwd")
    dx, g["ffn1_norm"], g["ffn1_w_in"], g["ffn1_w_out"] = _ffn_bwd(
        sv["x"], k["ffn1_norm"], k["ffn_w_in"], k["ffn_w_out"], k["ffn1_at"], sv["gu1"], dx1, "ffn1_bwd")
    return dx, g


def _rope_tables(positions):
    s = positions.shape[0]
    inv_freq = 1.0 / (ROPE_THETA ** (jnp.arange(0, MLA_ROPE, 2, dtype=F32) / MLA_ROPE))
    ang = positions.astype(F32)[:, None] * inv_freq
    cos, sin = jnp.cos(ang), jnp.sin(ang)
    tail = LANES - MLA_QK_DIM
    cosf = jnp.concatenate([jnp.ones((s, MLA_NOPE), F32), cos, cos, jnp.ones((s, tail), F32)], axis=1)
    sinf = jnp.concatenate([jnp.zeros((s, MLA_NOPE), F32), sin, sin, jnp.zeros((s, tail), F32)], axis=1)
    half = MLA_ROPE // 2
    rot = np.zeros((LANES, LANES), np.float32)
    for i in range(half):
        rot[MLA_NOPE + half + i, MLA_NOPE + i] = -1.0
        rot[MLA_NOPE + i, MLA_NOPE + half + i] = 1.0
    return cosf, sinf, jnp.asarray(rot)


MATRICES = ("ffn1_w_in", "ffn1_w_out", "w_in", "mla_w_uq", "mla_w_ukv", "ssd_conv_w", "w_branch", "w_out", "ffn2_w_in",
            "ffn2_w_out")
VECTORS = ("ffn1_norm", "mix_norm", "gm_v_norm", "gm_w_s", "gm_b_s", "mla_q_norm", "mla_kv_norm", "mla_q_gain",
           "mla_k_gain", "ssd_conv_b", "ssd_dt_bias", "ssd_a_log", "ssd_d", "ssd_norm", "ffn2_norm")
WEIGHTS = ("ffn1_norm", "ffn1_w_in", "ffn1_w_out", "mix_norm", "w_in", "gm_v_norm", "gm_w_s", "gm_b_s", "mla_q_norm",
           "mla_kv_norm", "mla_w_uq", "mla_w_ukv", "mla_q_gain", "mla_k_gain", "ssd_conv_w", "ssd_conv_b", "ssd_dt_bias",
           "ssd_a_log", "ssd_d", "ssd_norm", "w_branch", "w_out", "ffn2_norm", "ffn2_w_in", "ffn2_w_out")


def _local_step(x, positions, target, gw, vec):
    rope = _rope_tables(positions)
    depth = gw["w_out"].shape[1]
    saved = []
    for l in range(depth):
        k = _layer_weights(gw, vec, l)
        x, sv = _layer_fwd(x, k, rope)
        saved.append((k, sv))
    dy, loss = _loss_head(x, target, "loss_head")
    grads = []
    for k, sv in reversed(saved):
        dy, g = _layer_bwd(dy, k, sv, rope)
        grads.append(g)
    grads.reverse()
    out = {n: [g[n] for g in grads] for n in MATRICES}
    out.update({n: jnp.stack([g[n].reshape(vec[n].shape[1:]) for g in grads]) for n in VECTORS})
    return loss[0, 0], dy, out


MESH = pl.DeviceIdType.MESH
N_CHIP = 4
ANY = pl.BlockSpec(memory_space=pl.ANY)


def _place():
    return lax.axis_index("x"), lax.axis_index("y"), lax.axis_index("c")


def _all_gather(shard, name):
    m, n = shard.shape

    def body(x_ref, out_ref, send_sems, recv_sems, local_sem):
        x, y, c = _place()
        me, sibling = (x, y, c), (x, y, 1 - c)
        chips = [(1 - x, y), (x, 1 - y), (1 - x, 1 - y)]

        def rows(px, py, pc):
            return out_ref.at[pl.ds((4 * px + 2 * py + pc) * m, m), :]

        def copy(k, block, to, src=None):
            return pltpu.make_async_remote_copy(
                src_ref=rows(*block) if src is None else src, dst_ref=rows(*block),
                send_sem=send_sems.at[k], recv_sem=recv_sems.at[k], device_id=to, device_id_type=MESH)

        mine = pltpu.make_async_copy(x_ref, rows(*me), local_sem)
        mine.start()
        first = [copy(0, me, sibling, src=x_ref)]
        first += [copy(1 + j, me, (*chip, c), src=x_ref) for j, chip in enumerate(chips)]
        for cp in first:
            cp.start()
        passed = [copy(4 + j, (*chip, c), sibling) for j, chip in enumerate(chips)]
        for j, chip in enumerate(chips):
            copy(1 + j, (*chip, c), me).wait_recv()
            passed[j].start()
        copy(0, sibling, me).wait_recv()
        for j, chip in enumerate(chips):
            copy(4 + j, (*chip, 1 - c), me).wait_recv()
        for cp in first + passed:
            cp.wait_send()
        mine.wait()

    return pl.pallas_call(
        body,
        name=name,
        out_shape=jax.ShapeDtypeStruct((N_DEV * m, n), shard.dtype),
        in_specs=[ANY],
        out_specs=ANY,
        scratch_shapes=[pltpu.SemaphoreType.DMA((7,)), pltpu.SemaphoreType.DMA((7,)), pltpu.SemaphoreType.DMA],
    )(shard)


def _pair_exchange(contrib, name):
    _, r, n = contrib.shape

    def body(g_ref, got_ref, send_sems, recv_sems):
        x, y, c = _place()
        remote = [pltpu.make_async_remote_copy(
            src_ref=g_ref.at[2 * j + (1 - c)], dst_ref=got_ref.at[j], send_sem=send_sems.at[j], recv_sem=recv_sems.at[j],
            device_id=(x, y, 1 - c), device_id_type=MESH) for j in range(N_CHIP)]
        for cp in remote:
            cp.start()
        for cp in remote:
            cp.wait()

    return pl.pallas_call(
        body,
        name=name,
        out_shape=jax.ShapeDtypeStruct((N_CHIP, r, n), contrib.dtype),
        in_specs=[ANY],
        out_specs=ANY,
        scratch_shapes=[pltpu.SemaphoreType.DMA((N_CHIP,)), pltpu.SemaphoreType.DMA((N_CHIP,))],
    )(contrib)


def _pair_sum(contrib, theirs, name, tr):
    _, r, n = contrib.shape
    side = lax.axis_index("c").astype(jnp.int32).reshape(1)

    def body(c_ref, a_ref, b_ref, o_ref):
        o_ref[...] = (a_ref[...].astype(F32) + b_ref[...].astype(F32)).astype(BF16)

    spec = pl.BlockSpec((1, tr, n), lambda j, i, c_ref: (j, i, 0))
    return pl.pallas_call(
        body,
        name=name,
        out_shape=jax.ShapeDtypeStruct(theirs.shape, BF16),
        grid_spec=pltpu.PrefetchScalarGridSpec(
            num_scalar_prefetch=1,
            grid=(N_CHIP, r // tr),
            in_specs=[pl.BlockSpec((1, tr, n), lambda j, i, c_ref: (2 * j + c_ref[0], i, 0)), spec],
            out_specs=spec,
        ),
        compiler_params=_cparams(("parallel", "parallel")),
    )(side, contrib, theirs)


def _chip_exchange(part, name):
    _, r, n = part.shape

    def body(p_ref, got_ref, send_sems, recv_sems, local_sem):
        x, y, c = _place()
        mine = 2 * x + y
        chips = [(1 - x, y), (x, 1 - y), (1 - x, 1 - y)]
        local = pltpu.make_async_copy(p_ref.at[mine], got_ref.at[mine], local_sem)
        local.start()
        sends = [pltpu.make_async_remote_copy(
            src_ref=p_ref.at[2 * cx + cy], dst_ref=got_ref.at[mine], send_sem=send_sems.at[k], recv_sem=recv_sems.at[k],
            device_id=(cx, cy, c), device_id_type=MESH) for k, (cx, cy) in enumerate(chips)]
        for cp in sends:
            cp.start()
        for k, (cx, cy) in enumerate(chips):
            pltpu.make_async_remote_copy(
                src_ref=p_ref.at[mine], dst_ref=got_ref.at[2 * cx + cy], send_sem=send_sems.at[k],
                recv_sem=recv_sems.at[k], device_id=(cx, cy, c), device_id_type=MESH).wait_recv()
        for cp in sends:
            cp.wait_send()
        local.wait()

    return pl.pallas_call(
        body,
        name=name,
        out_shape=jax.ShapeDtypeStruct(part.shape, part.dtype),
        in_specs=[ANY],
        out_specs=ANY,
        scratch_shapes=[pltpu.SemaphoreType.DMA((3,)), pltpu.SemaphoreType.DMA((3,)), pltpu.SemaphoreType.DMA],
    )(part)


def _adamw(parts, w, m, v, name, tr, at=0):
    k = parts.shape[0]
    r, n = w.shape
    first = at // tr

    def body(p_ref, w_ref, m_ref, v_ref, g_ref, d_ref, nm_ref, nv_ref):
        g = p_ref[0].astype(F32)
        for i in range(1, k):
            g = g + p_ref[i].astype(F32)
        m_new = ADAM_B1 * m_ref[...] + (1.0 - ADAM_B1) * g
        v_new = ADAM_B2 * v_ref[...] + (1.0 - ADAM_B2) * (g * g)
        m_hat = m_new / (1.0 - ADAM_B1 ** ADAM_STEP)
        v_hat = v_new / (1.0 - ADAM_B2 ** ADAM_STEP)
        g_ref[...] = g
        d_ref[...] = -ADAM_LR * (m_hat / (jnp.sqrt(v_hat) + ADAM_EPS) + ADAM_WD * w_ref[...])
        nm_ref[...] = m_new
        nv_ref[...] = v_new

    spec = pl.BlockSpec((tr, n), lambda i: (i, 0))
    out = jax.ShapeDtypeStruct((r, n), F32)
    return pl.pallas_call(
        body,
        name=name,
        out_shape=(out, out, out, out),
        grid=(r // tr,),
        in_specs=[pl.BlockSpec((k, tr, n), lambda i: (0, i + first, 0)), spec, spec, spec],
        out_specs=(spec, spec, spec, spec),
        compiler_params=_cparams(("parallel",)),
    )(parts, w, m, v)


TILE_BYTES = 2 * 1024 * 1024
SUBLANES_16BIT = 16


def _tile_rows(r, n):
    best = None
    for t in range(SUBLANES_16BIT, r, SUBLANES_16BIT):
        if r % t == 0 and t * n * 4 <= TILE_BYTES:
            best = t
    return best or r


def _rows(a):
    return a.reshape(-1, a.shape[-1])


def _gather_rows(shard, name):
    return _all_gather(_rows(shard), name).reshape(N_DEV, *shard.shape)


def _reduce_scatter(contrib, name):
    theirs = _pair_exchange(contrib, name + "_pair_exchange")
    part = _pair_sum(contrib, theirs, name + "_pair_sum", _tile_rows(contrib.shape[1], contrib.shape[2]))
    return _chip_exchange(part, name + "_chip_exchange")


SMALL = ("ffn1_norm", "mix_norm", "gm_v_norm", "gm_b_s", "mla_q_norm", "mla_kv_norm", "mla_q_gain", "mla_k_gain",
         "ssd_conv_b", "ssd_dt_bias", "ssd_a_log", "ssd_d", "ssd_norm", "ffn2_norm")
SMALL_ROWS = 8
SMALL_COLS = 7040


def _side_by_side(d):
    cols = jnp.concatenate([d[n].reshape(d[n].shape[0], -1) for n in SMALL], axis=1)
    return jnp.pad(cols, ((0, SMALL_ROWS - cols.shape[0]), (0, SMALL_COLS - cols.shape[1])))


def _apart(packed, like):
    out, off = {}, 0
    for n in SMALL:
        size = like[n][0].size
        out[n] = packed[:like[n].shape[0], off:off + size].reshape(like[n].shape)
        off += size
    return out


def kernel(x, positions, ffn1_norm, ffn1_w_in, ffn1_w_out, mix_norm, w_in, gm_v_norm, gm_w_s, gm_b_s, mla_q_norm, mla_kv_norm, mla_w_uq, mla_w_ukv, mla_q_gain, mla_k_gain, ssd_conv_w, ssd_conv_b, ssd_dt_bias, ssd_a_log, ssd_d, ssd_norm, w_branch, w_out, ffn2_norm, ffn2_w_in, ffn2_w_out, loss_target, m_ffn1_norm, m_ffn1_w_in, m_ffn1_w_out, m_mix_norm, m_w_in, m_gm_v_norm, m_gm_w_s, m_gm_b_s, m_mla_q_norm, m_mla_kv_norm, m_mla_w_uq, m_mla_w_ukv, m_mla_q_gain, m_mla_k_gain, m_ssd_conv_w, m_ssd_conv_b, m_ssd_dt_bias, m_ssd_a_log, m_ssd_d, m_ssd_norm, m_w_branch, m_w_out, m_ffn2_norm, m_ffn2_w_in, m_ffn2_w_out, v_ffn1_norm, v_ffn1_w_in, v_ffn1_w_out, v_mix_norm, v_w_in, v_gm_v_norm, v_gm_w_s, v_gm_b_s, v_mla_q_norm, v_mla_kv_norm, v_mla_w_uq, v_mla_w_ukv, v_mla_q_gain, v_mla_k_gain, v_ssd_conv_w, v_ssd_conv_b, v_ssd_dt_bias, v_ssd_a_log, v_ssd_d, v_ssd_norm, v_w_branch, v_w_out, v_ffn2_norm, v_ffn2_w_in, v_ffn2_w_out):
    given = dict(locals())
    w = {n: given[n] for n in WEIGHTS}
    mom = {n: given["m_" + n] for n in WEIGHTS}
    var = {n: given["v_" + n] for n in WEIGHTS}
    groups = {"ffn_w_in": ("ffn1_w_in", "ffn2_w_in"), "ffn_w_out": ("ffn1_w_out", "ffn2_w_out"), "w_in": ("w_in",),
              "mla_w_uq": ("mla_w_uq",), "mla_w_ukv": ("mla_w_ukv",), "ssd_conv_w": ("ssd_conv_w",),
              "w_branch": ("w_branch",), "w_out": ("w_out",)}
    gw = {g: _gather_rows(jnp.concatenate([w[n].astype(BF16) for n in names]), g + "_all_gather")
          for g, names in groups.items()}

    loss, dx, grads = _local_step(x[0], positions[0], loss_target[0], gw, {n: w[n] for n in VECTORS})

    outs = [{}, {}, {}, {}]
    for g, names in groups.items():
        contrib = jnp.stack([a for n in names for a in grads[n]], axis=1)
        parts = _reduce_scatter(contrib.reshape(N_DEV, -1, contrib.shape[-1]), g + "_grad")
        at = 0
        for n in names:
            wr = _rows(w[n])
            res = _adamw(parts, wr, _rows(mom[n]), _rows(var[n]), "adamw_" + n, _tile_rows(*wr.shape), at=at)
            at += wr.shape[0]
            for o, r in zip(outs, res):
                o[n] = r.reshape(w[n].shape)

    small_parts = _gather_rows(_side_by_side(grads), "small_grads_all_gather")
    small = _adamw(small_parts, _side_by_side(w), _side_by_side(mom), _side_by_side(var), "adamw_small", SMALL_ROWS)
    ws_parts = _gather_rows(_rows(grads["gm_w_s"]), "gm_w_s_grads_all_gather")
    ws = _adamw(ws_parts, _rows(w["gm_w_s"]), _rows(mom["gm_w_s"]), _rows(var["gm_w_s"]), "adamw_gm_w_s",
                _tile_rows(ws_parts.shape[1], LANES))
    for o, sm, r in zip(outs, small, ws):
        o.update(_apart(sm, w))
        o["gm_w_s"] = r.reshape(w["gm_w_s"].shape)

    loss = lax.psum(loss, ("x", "y", "c"))
    return (loss, dx[None], *[o[n] for o in outs for n in WEIGHTS])
```

```python
import functools

import jax
import jax.numpy as jnp
import numpy as np
from jax import lax
from jax.experimental import pallas as pl
from jax.experimental.pallas import tpu as pltpu

F32 = jnp.float32
BF16 = jnp.bfloat16

D_MODEL = 1024
DEPTH = 4
D_FF = 2816
FFN_RESID = 0.5
EPS = 1e-6
GM_WIDTH = 512
GM_GROUPS = 4
GM_CHUNK = 128
MLA_HEADS = 8
MLA_Q_RANK = 384
MLA_KV_RANK = 256
MLA_NOPE = 64
MLA_ROPE = 32
MLA_QK_DIM = 96
MLA_V = 64
ROPE_THETA = 10000.0
SSD_HEADS = 8
SSD_HEAD_DIM = 64
SSD_INNER = 512
SSD_GROUPS = 2
SSD_STATE = 128
SSD_CONV = 4
SSD_CHUNK = 128
SSD_CONV_DIM = 1024
N_BRANCH = 3
IN_WIDTHS = (1024, 384, 256, 32, 512, 1024, 8, 3072)
IN_OFFSETS = (0, 1024, 1408, 1664, 1696, 2208, 3232, 3240)
IN_COLS = 6312
LANES = 128
N_DEV = 8

ADAM_LR = 0.001
ADAM_B1 = 0.9
ADAM_B2 = 0.999
ADAM_EPS = 1e-08
ADAM_WD = 0.01
ADAM_STEP = 10

VMEM_LIMIT = 56 * 1024 * 1024

P_UV, P_XBC, P_Z, P_KR, P_CQ, P_GATES, P_CKV, P_DT = 0, 1024, 2048, 2560, 2688, 3072, 6144, 6400
P_COLS = 6528


def _cparams(sem):
    return pltpu.CompilerParams(dimension_semantics=sem, vmem_limit_bytes=VMEM_LIMIT)


def _bdot(a, b, dims):
    return lax.dot_general(a.astype(BF16), b.astype(BF16), (dims, ((), ())), preferred_element_type=F32)


@jax.custom_vjp
def _nn(a, b):
    return _bdot(a, b, ((1,), (0,)))


@jax.custom_vjp
def _nt(a, b):
    return _bdot(a, b, ((1,), (1,)))


@jax.custom_vjp
def _tn(a, b):
    return _bdot(a, b, ((0,), (0,)))


def _dot_fwd(dims):
    return lambda a, b: (_bdot(a, b, dims), (a, b))


_nn.defvjp(_dot_fwd(((1,), (0,))), lambda r, g: (_nt(g, r[1]).astype(r[0].dtype), _tn(r[0], g).astype(r[1].dtype)))
_nt.defvjp(_dot_fwd(((1,), (1,))), lambda r, g: (_nn(g, r[1]).astype(r[0].dtype), _tn(g, r[0]).astype(r[1].dtype)))
_tn.defvjp(_dot_fwd(((0,), (0,))), lambda r, g: (_nt(r[1], g).astype(r[0].dtype), _nn(r[0], g).astype(r[1].dtype)))


def _exact_nn(a, b):
    return lax.dot_general(a, b, (((1,), (0,)), ((), ())), precision=lax.Precision.HIGHEST, preferred_element_type=F32)


def _sigmoid(x):
    return 1.0 / (1.0 + jnp.exp(-x))


def _silu(x):
    return x * _sigmoid(x)


def _softplus(x):
    return jnp.maximum(x, 0.0) + jnp.log(1.0 + jnp.exp(-jnp.abs(x)))


def _gelu(x):
    return 0.5 * x * (1.0 + lax.erf(x * 0.7071067811865476))


def _pick(n, cands):
    for c in cands:
        if n % c == 0:
            return c
    return n


def _matmul(a, b, mode, out_dtype, name, alpha=1.0, tm=None, tn=None, tk=None):
    if mode == "nn":
        (m, k), (_, n) = a.shape, b.shape
    elif mode == "nt":
        (m, k), (n, _) = a.shape, b.shape
    else:
        (k, m), (_, n) = a.shape, b.shape
    tm = tm or _pick(m, (512, 384, 256, 128))
    tn = tn or _pick(n, (1024, 768, 512, 384, 256, 128))
    tk = tk or _pick(k, (1024, 512, 256, 128))
    nk = k // tk
    if mode == "nn":
        a_spec = pl.BlockSpec((tm, tk), lambda i, j, kk: (i, kk))
        b_spec = pl.BlockSpec((tk, tn), lambda i, j, kk: (kk, j))
        dot = _nn
    elif mode == "nt":
        a_spec = pl.BlockSpec((tm, tk), lambda i, j, kk: (i, kk))
        b_spec = pl.BlockSpec((tn, tk), lambda i, j, kk: (j, kk))
        dot = _nt
    else:
        a_spec = pl.BlockSpec((tk, tm), lambda i, j, kk: (kk, i))
        b_spec = pl.BlockSpec((tk, tn), lambda i, j, kk: (kk, j))
        dot = _tn

    def body(a_ref, b_ref, o_ref, acc_ref):
        kk = pl.program_id(2)

        @pl.when(kk == 0)
        def _():
            acc_ref[...] = jnp.zeros_like(acc_ref)

        acc_ref[...] += dot(a_ref[...], b_ref[...])

        @pl.when(kk == nk - 1)
        def _():
            o_ref[...] = (alpha * acc_ref[...]).astype(out_dtype)

    return pl.pallas_call(
        body,
        name=name,
        out_shape=jax.ShapeDtypeStruct((m, n), out_dtype),
        grid=(m // tm, n // tn, nk),
        in_specs=[a_spec, b_spec],
        out_specs=pl.BlockSpec((tm, tn), lambda i, j, kk: (i, j)),
        scratch_shapes=[pltpu.VMEM((tm, tn), F32)],
        compiler_params=_cparams(("parallel", "parallel", "arbitrary")),
    )(a, b)


def _rms_stats(x):
    r = lax.rsqrt(jnp.mean(x * x, axis=-1, keepdims=True) + EPS)
    return x * r, r


def _rms_bwd(xhat, r, gain, dy):
    dxhat = dy * gain
    return r * (dxhat - xhat * jnp.mean(dxhat * xhat, axis=-1, keepdims=True))


def _acc_rows(ref, val, first):
    s = jnp.sum(val, axis=0, keepdims=True)

    @pl.when(first)
    def _():
        ref[...] = s

    @pl.when(jnp.logical_not(first))
    def _():
        ref[...] += s


def _rmsnorm_fwd(x, gain, name, tm=512):
    s, d = x.shape

    def body(x_ref, g_ref, h_ref):
        xhat, _ = _rms_stats(x_ref[...])
        h_ref[...] = (xhat * g_ref[...]).astype(BF16)

    return pl.pallas_call(
        body,
        name=name,
        out_shape=jax.ShapeDtypeStruct((s, d), BF16),
        grid=(s // tm,),
        in_specs=[pl.BlockSpec((tm, d), lambda i: (i, 0)), pl.BlockSpec((1, d), lambda i: (0, 0))],
        out_specs=pl.BlockSpec((tm, d), lambda i: (i, 0)),
        compiler_params=_cparams(("parallel",)),
    )(x, gain)


def _rmsnorm_bwd(x, gain, dh, dres, name, tm=512):
    s, d = x.shape

    def body(x_ref, g_ref, dh_ref, dres_ref, dx_ref, dg_ref):
        xhat, r = _rms_stats(x_ref[...])
        dh = dh_ref[...].astype(F32)
        dx_ref[...] = dres_ref[...] + _rms_bwd(xhat, r, g_ref[...], dh)
        _acc_rows(dg_ref, dh * xhat, pl.program_id(0) == 0)

    return pl.pallas_call(
        body,
        name=name,
        out_shape=(jax.ShapeDtypeStruct((s, d), F32), jax.ShapeDtypeStruct((1, d), F32)),
        grid=(s // tm,),
        in_specs=[
            pl.BlockSpec((tm, d), lambda i: (i, 0)),
            pl.BlockSpec((1, d), lambda i: (0, 0)),
            pl.BlockSpec((tm, d), lambda i: (i, 0)),
            pl.BlockSpec((tm, d), lambda i: (i, 0)),
        ],
        out_specs=(pl.BlockSpec((tm, d), lambda i: (i, 0)), pl.BlockSpec((1, d), lambda i: (0, 0))),
        compiler_params=_cparams(("arbitrary",)),
    )(x, gain, dh, dres)


FF_BLOCK = 2 * D_FF // N_DEV
FF_BLOCKS = D_FF // FF_BLOCK
FF_ROWS = D_FF // N_DEV


def _ffn_weight_specs(layer):
    return [
        pl.BlockSpec((None, None, D_MODEL, FF_BLOCK), lambda i, j: (j, layer, 0, 0)),
        pl.BlockSpec((None, None, D_MODEL, FF_BLOCK), lambda i, j: (j + FF_BLOCKS, layer, 0, 0)),
        pl.BlockSpec((2, None, FF_ROWS, D_MODEL), lambda i, j: (j, layer, 0, 0)),
    ]


def _ffn_fwd(x, gain, w_in, w_out, layer, name, tm=512):
    s, d = x.shape

    def body(x_ref, gain_ref, wg_ref, wu_ref, wo_ref, y_ref, gu_ref, h_scr, acc_scr):
        j = pl.program_id(1)

        @pl.when(j == 0)
        def _():
            xhat, _ = _rms_stats(x_ref[...])
            h_scr[...] = (xhat * gain_ref[...]).astype(BF16)
            acc_scr[...] = jnp.zeros_like(acc_scr)

        h = h_scr[...]
        g = _nn(h, wg_ref[...])
        u = _nn(h, wu_ref[...])
        gu_ref[0] = g.astype(BF16)
        gu_ref[1] = u.astype(BF16)
        acc_scr[...] += _nn(_silu(g) * u, wo_ref[...].reshape(FF_BLOCK, d))

        @pl.when(j == FF_BLOCKS - 1)
        def _():
            y_ref[...] = x_ref[...] + FFN_RESID * acc_scr[...]

    return pl.pallas_call(
        body,
        name=name,
        out_shape=(
            jax.ShapeDtypeStruct((s, d), F32),
            jax.ShapeDtypeStruct((2, FF_BLOCKS, s, FF_BLOCK), BF16),
        ),
        grid=(s // tm, FF_BLOCKS),
        in_specs=[
            pl.BlockSpec((tm, d), lambda i, j: (i, 0)),
            pl.BlockSpec((1, d), lambda i, j: (0, 0)),
        ] + _ffn_weight_specs(layer),
        out_specs=(
            pl.BlockSpec((tm, d), lambda i, j: (i, 0)),
            pl.BlockSpec((2, None, tm, FF_BLOCK), lambda i, j: (0, j, i, 0)),
        ),
        scratch_shapes=[pltpu.VMEM((tm, d), BF16), pltpu.VMEM((tm, d), F32)],
        compiler_params=_cparams(("parallel", "arbitrary")),
    )(x, gain, w_in, w_in, w_out)


def _ffn_bwd(x, gain, w_in, w_out, layer, gu, dy, name, tm=512, tk=1024):
    s, d = x.shape
    tk = min(tk, s)

    def body(x_ref, gain_ref, wg_ref, wu_ref, wo_ref, gu_ref, dy_ref,
             dx_ref, dgain_ref, h_ref, a_ref, dgu_ref, dyb_scr, acc_scr):
        i = pl.program_id(0)
        j = pl.program_id(1)

        @pl.when(j == 0)
        def _():
            xhat, _ = _rms_stats(x_ref[...])
            h_ref[...] = (xhat * gain_ref[...]).astype(BF16)
            dyb_scr[...] = (FFN_RESID * dy_ref[...]).astype(BF16)
            acc_scr[...] = jnp.zeros_like(acc_scr)

        da = _nt(dyb_scr[...], wo_ref[...].reshape(FF_BLOCK, d))
        gv = gu_ref[0].astype(F32)
        uv = gu_ref[1].astype(F32)
        sg = _sigmoid(gv)
        sl = gv * sg
        a_ref[...] = (sl * uv).astype(BF16)
        du = (da * sl).astype(BF16)
        dg = (da * uv * (sg * (1.0 + gv * (1.0 - sg)))).astype(BF16)
        dgu_ref[0] = dg
        dgu_ref[1] = du
        acc_scr[...] += _nt(dg, wg_ref[...]) + _nt(du, wu_ref[...])

        @pl.when(j == FF_BLOCKS - 1)
        def _():
            xhat, r = _rms_stats(x_ref[...])
            dh = acc_scr[...]
            dx_ref[...] = dy_ref[...] + _rms_bwd(xhat, r, gain_ref[...], dh)
            _acc_rows(dgain_ref, dh * xhat, i == 0)

    gu_spec = pl.BlockSpec((2, None, tm, FF_BLOCK), lambda i, j: (0, j, i, 0))
    dx, dgain, h, a, dgu = pl.pallas_call(
        body,
        name=name,
        out_shape=(
            jax.ShapeDtypeStruct((s, d), F32),
            jax.ShapeDtypeStruct((1, d), F32),
            jax.ShapeDtypeStruct((s, d), BF16),
            jax.ShapeDtypeStruct((FF_BLOCKS, s, FF_BLOCK), BF16),
            jax.ShapeDtypeStruct((2, FF_BLOCKS, s, FF_BLOCK), BF16),
        ),
        grid=(s // tm, FF_BLOCKS),
        in_specs=[
            pl.BlockSpec((tm, d), lambda i, j: (i, 0)),
            pl.BlockSpec((1, d), lambda i, j: (0, 0)),
        ] + _ffn_weight_specs(layer) + [gu_spec, pl.BlockSpec((tm, d), lambda i, j: (i, 0))],
        out_specs=(
            pl.BlockSpec((tm, d), lambda i, j: (i, 0)),
            pl.BlockSpec((1, d), lambda i, j: (0, 0)),
            pl.BlockSpec((tm, d), lambda i, j: (i, 0)),
            pl.BlockSpec((None, tm, FF_BLOCK), lambda i, j: (j, i, 0)),
            gu_spec,
        ),
        scratch_shapes=[pltpu.VMEM((tm, d), BF16), pltpu.VMEM((tm, d), F32)],
        compiler_params=_cparams(("arbitrary", "arbitrary")),
    )(x, gain, w_in, w_in, w_out, gu, dy)
    nk = s // tk

    def acc_matmul(first, last, acc_ref, o_ref, val, alpha):
        @pl.when(first)
        def _():
            acc_ref[...] = jnp.zeros_like(acc_ref)

        acc_ref[...] += val

        @pl.when(last)
        def _():
            o_ref[...] = (alpha * acc_ref[...]).astype(BF16)

    def dwin_body(h_ref, dgu_ref, o_ref, acc_ref):
        kk = pl.program_id(2)
        acc_matmul(kk == 0, kk == nk - 1, acc_ref, o_ref, _tn(h_ref[...], dgu_ref[...]), 1.0)

    tmw = 512
    dw_in = pl.pallas_call(
        dwin_body,
        name=name + "_dwin",
        out_shape=jax.ShapeDtypeStruct((N_DEV, d, FF_BLOCK), BF16),
        grid=(N_DEV, d // tmw, nk),
        in_specs=[
            pl.BlockSpec((tk, tmw), lambda n, i, kk: (kk, i)),
            pl.BlockSpec((None, tk, FF_BLOCK), lambda n, i, kk: (n, kk, 0)),
        ],
        out_specs=pl.BlockSpec((None, tmw, FF_BLOCK), lambda n, i, kk: (n, i, 0)),
        scratch_shapes=[pltpu.VMEM((tmw, FF_BLOCK), F32)],
        compiler_params=_cparams(("parallel", "parallel", "arbitrary")),
    )(h, dgu.reshape(N_DEV, s, FF_BLOCK))

    def dwout_body(a_ref, dy_ref, o_ref, acc_ref):
        kk = pl.program_id(1)
        acc_matmul(kk == 0, kk == nk - 1, acc_ref, o_ref, _tn(a_ref[...], dy_ref[...]), FFN_RESID)

    dw_out = pl.pallas_call(
        dwout_body,
        name=name + "_dwout",
        out_shape=jax.ShapeDtypeStruct((FF_BLOCKS, FF_BLOCK, d), BF16),
        grid=(FF_BLOCKS, nk),
        in_specs=[
            pl.BlockSpec((None, tk, FF_BLOCK), lambda j, kk: (j, kk, 0)),
            pl.BlockSpec((tk, d), lambda j, kk: (kk, 0)),
        ],
        out_specs=pl.BlockSpec((None, FF_BLOCK, d), lambda j, kk: (j, 0, 0)),
        scratch_shapes=[pltpu.VMEM((FF_BLOCK, d), F32)],
        compiler_params=_cparams(("parallel", "arbitrary")),
    )(a, dy)
    return dx, dgain, dw_in, dw_out.reshape(N_DEV, FF_ROWS, d)


def _acc(ref, val, first):
    @pl.when(first)
    def _():
        ref[...] = val

    @pl.when(jnp.logical_not(first))
    def _():
        ref[...] += val


def _full(shape):
    nd = len(shape)
    return pl.BlockSpec(shape, lambda *_: (0,) * nd)


def _iota(shape, dim):
    return lax.broadcasted_iota(jnp.int32, shape, dim)


def _gmlp_chunk(u, v, gain, w_s, b_s):
    va = [_gelu(t) for t in v]
    ms = sum(jnp.sum(t * t, axis=-1, keepdims=True) for t in va) * (1.0 / GM_WIDTH)
    r = lax.rsqrt(ms + EPS)
    tri = _iota((GM_CHUNK, GM_CHUNK), 0) >= _iota((GM_CHUNK, GM_CHUNK), 1)
    out = []
    for g in range(GM_GROUPS):
        vn = va[g] * r * gain[g]
        sp = _nn(jnp.where(tri, w_s[g], 0.0), vn) + b_s[g]
        out.append(_gelu(u[g]) * sp)
    return out


def _gmlp_load(uv_ref, c):
    rows = pl.ds(c * GM_CHUNK, GM_CHUNK)
    u = [uv_ref[rows, pl.ds(g * LANES, LANES)] for g in range(GM_GROUPS)]
    v = [uv_ref[rows, pl.ds(GM_WIDTH + g * LANES, LANES)] for g in range(GM_GROUPS)]
    return u, v


def _gmlp_params(gain_ref, ws_ref, bs_ref):
    gain = [gain_ref[:, pl.ds(g * LANES, LANES)] for g in range(GM_GROUPS)]
    w_s = [ws_ref[g] for g in range(GM_GROUPS)]
    b_s = [bs_ref[g] for g in range(GM_GROUPS)]
    return gain, w_s, b_s


def _gmlp_fwd(proj, gain, w_s, b_s, name, tm=512):
    s = proj.shape[0]

    def body(uv_ref, gain_ref, ws_ref, bs_ref, y_ref):
        params = _gmlp_params(gain_ref, ws_ref, bs_ref)
        for c in range(tm // GM_CHUNK):
            u, v = _gmlp_load(uv_ref, c)
            y = _gmlp_chunk(u, v, *params)
            for g in range(GM_GROUPS):
                y_ref[pl.ds(c * GM_CHUNK, GM_CHUNK), pl.ds(g * LANES, LANES)] = y[g]

    return pl.pallas_call(
        body,
        name=name,
        out_shape=jax.ShapeDtypeStruct((s, GM_WIDTH), F32),
        grid=(s // tm,),
        in_specs=[
            pl.BlockSpec((tm, 2 * GM_WIDTH), lambda i: (i, P_UV // (2 * GM_WIDTH))),
            _full((1, GM_WIDTH)),
            _full((GM_GROUPS, GM_CHUNK, GM_CHUNK)),
            _full((GM_GROUPS, GM_CHUNK, 1)),
        ],
        out_specs=pl.BlockSpec((tm, GM_WIDTH), lambda i: (i, 0)),
        compiler_params=_cparams(("parallel",)),
    )(proj, gain, w_s, b_s)


def _gmlp_bwd(proj, gain, w_s, b_s, dy, name, tm=512):
    s = proj.shape[0]

    def body(uv_ref, gain_ref, ws_ref, bs_ref, dy_ref, duv_ref, dgain_ref, dws_ref, dbs_ref):
        params = _gmlp_params(gain_ref, ws_ref, bs_ref)
        dgain = dws = dbs = None
        for c in range(tm // GM_CHUNK):
            rows = pl.ds(c * GM_CHUNK, GM_CHUNK)
            u, v = _gmlp_load(uv_ref, c)
            _, vjp = jax.vjp(_gmlp_chunk, u, v, *params)
            du, dv, dg, dw, db = vjp([dy_ref[rows, pl.ds(g * LANES, LANES)] for g in range(GM_GROUPS)])
            for g in range(GM_GROUPS):
                duv_ref[rows, pl.ds(g * LANES, LANES)] = du[g].astype(BF16)
                duv_ref[rows, pl.ds(GM_WIDTH + g * LANES, LANES)] = dv[g].astype(BF16)
            if c == 0:
                dgain, dws, dbs = dg, dw, db
            else:
                dgain = [p + q for p, q in zip(dgain, dg)]
                dws = [p + q for p, q in zip(dws, dw)]
                dbs = [p + q for p, q in zip(dbs, db)]
        first = pl.program_id(0) == 0
        for g in range(GM_GROUPS):
            _acc(dgain_ref.at[:, pl.ds(g * LANES, LANES)], dgain[g], first)
            _acc(dws_ref.at[g], dws[g], first)
            _acc(dbs_ref.at[g], dbs[g], first)

    return pl.pallas_call(
        body,
        name=name,
        out_shape=(
            jax.ShapeDtypeStruct((s, 2 * GM_WIDTH), BF16),
            jax.ShapeDtypeStruct((1, GM_WIDTH), F32),
            jax.ShapeDtypeStruct((GM_GROUPS, GM_CHUNK, GM_CHUNK), F32),
            jax.ShapeDtypeStruct((GM_GROUPS, GM_CHUNK, 1), F32),
        ),
        grid=(s // tm,),
        in_specs=[
            pl.BlockSpec((tm, 2 * GM_WIDTH), lambda i: (i, P_UV // (2 * GM_WIDTH))),
            _full((1, GM_WIDTH)),
            _full((GM_GROUPS, GM_CHUNK, GM_CHUNK)),
            _full((GM_GROUPS, GM_CHUNK, 1)),
            pl.BlockSpec((tm, GM_WIDTH), lambda i: (i, 0)),
        ],
        out_specs=(
            pl.BlockSpec((tm, 2 * GM_WIDTH), lambda i: (i, 0)),
            _full((1, GM_WIDTH)),
            _full((GM_GROUPS, GM_CHUNK, GM_CHUNK)),
            _full((GM_GROUPS, GM_CHUNK, 1)),
        ),
        compiler_params=_cparams(("arbitrary",)),
    )(proj, gain, w_s, b_s, dy)


HEAD_BLOCK = MLA_HEADS * LANES


def _mla_pre(rope, cq, ckv, kr, qn_g, kvn_g, wq, wk, wv, qg, kg):
    cosf, sinf, rot = rope
    xq, _ = _rms_stats(cq)
    qn = xq * qn_g
    xk, _ = _rms_stats(ckv)
    kvn = xk * kvn_g

    def head_norm(t, gain):
        r = lax.rsqrt(jnp.sum(t * t, axis=-1, keepdims=True) * (1.0 / MLA_QK_DIM) + EPS)
        th = t * r * gain
        return th * cosf + _nn(th, rot) * sinf

    q = [head_norm(_nn(qn, wq[h]), qg) for h in range(MLA_HEADS)]
    k = [head_norm(_nn(kvn, wk[h]) + kr, kg) for h in range(MLA_HEADS)]
    v = [_nn(kvn, wv[h]) for h in range(MLA_HEADS)]
    return q, k, v


def _mla_pre_specs(tm):
    row = lambda w, off: pl.BlockSpec((tm, w), lambda i: (i, off // w))
    return [
        row(MLA_Q_RANK, P_CQ),
        row(MLA_KV_RANK, P_CKV),
        row(LANES, P_KR),
        pl.BlockSpec((tm, LANES), lambda i: (i, 0)),
        pl.BlockSpec((tm, LANES), lambda i: (i, 0)),
        _full((LANES, LANES)),
        _full((1, MLA_Q_RANK)),
        _full((1, MLA_KV_RANK)),
        _full((MLA_HEADS, MLA_Q_RANK, LANES)),
        _full((MLA_HEADS, MLA_KV_RANK, LANES)),
        _full((MLA_HEADS, MLA_KV_RANK, LANES)),
        _full((1, LANES)),
        _full((1, LANES)),
    ]


def _mla_pre_args(cq_ref, ckv_ref, kr_ref, cos_ref, sin_ref, rot_ref, qn_ref, kvn_ref, wq_ref, wk_ref, wv_ref,
                  qg_ref, kg_ref):
    heads = lambda ref: [ref[h].astype(F32) for h in range(MLA_HEADS)]
    rope = (cos_ref[...], sin_ref[...], rot_ref[...])
    args = (cq_ref[...], ckv_ref[...], kr_ref[...], qn_ref[...], kvn_ref[...], heads(wq_ref), heads(wk_ref),
            heads(wv_ref), qg_ref[...], kg_ref[...])
    return rope, args


def _mla_pre_fwd(proj, cosf, sinf, rot, qn_g, kvn_g, wq, wk, wv, qg, kg, name, tm=256):
    s = proj.shape[0]

    def body(*refs):
        q_ref, k_ref, v_ref = refs[13:]
        rope, args = _mla_pre_args(*refs[:13])
        q, k, v = _mla_pre(rope, *args)
        for h in range(MLA_HEADS):
            cols = pl.ds(h * LANES, LANES)
            q_ref[:, cols] = q[h].astype(BF16)
            k_ref[:, cols] = k[h].astype(BF16)
            v_ref[:, cols] = v[h].astype(BF16)

    out = jax.ShapeDtypeStruct((s, HEAD_BLOCK), BF16)
    blk = pl.BlockSpec((tm, HEAD_BLOCK), lambda i: (i, 0))
    return pl.pallas_call(
        body,
        name=name,
        out_shape=(out, out, out),
        grid=(s // tm,),
        in_specs=_mla_pre_specs(tm),
        out_specs=(blk, blk, blk),
        compiler_params=_cparams(("parallel",)),
    )(proj, proj, proj, cosf, sinf, rot, qn_g, kvn_g, wq, wk, wv, qg, kg)


def _mla_pre_bwd(proj, cosf, sinf, rot, qn_g, kvn_g, wq, wk, wv, qg, kg, dq, dk, dv, name, tm=256):
    s = proj.shape[0]

    def body(*refs):
        dq_ref, dk_ref, dv_ref = refs[13:16]
        dcq_ref, dckv_ref, dkr_ref, dqn_ref, dkvn_ref, dwq_ref, dwk_ref, dwv_ref, dqg_ref, dkg_ref = refs[16:]
        rope, args = _mla_pre_args(*refs[:13])
        _, vjp = jax.vjp(functools.partial(_mla_pre, rope), *args)
        heads = lambda ref: [ref[:, pl.ds(h * LANES, LANES)] for h in range(MLA_HEADS)]
        dcq, dckv, dkr, dqn, dkvn, dwq, dwk, dwv, dqg, dkg = vjp((heads(dq_ref), heads(dk_ref), heads(dv_ref)))
        dcq_ref[...] = dcq.astype(BF16)
        dckv_ref[...] = dckv.astype(BF16)
        dkr_ref[...] = dkr.astype(BF16)
        first = pl.program_id(0) == 0
        _acc(dqn_ref, dqn, first)
        _acc(dkvn_ref, dkvn, first)
        _acc(dqg_ref, dqg, first)
        _acc(dkg_ref, dkg, first)
        for h in range(MLA_HEADS):
            _acc(dwq_ref.at[h], dwq[h], first)
            _acc(dwk_ref.at[h], dwk[h], first)
            _acc(dwv_ref.at[h], dwv[h], first)

    hb = pl.BlockSpec((tm, HEAD_BLOCK), lambda i: (i, 0))
    row = lambda w: pl.BlockSpec((tm, w), lambda i: (i, 0))
    sds = jax.ShapeDtypeStruct
    return pl.pallas_call(
        body,
        name=name,
        out_shape=(
            sds((s, MLA_Q_RANK), BF16), sds((s, MLA_KV_RANK), BF16), sds((s, LANES), BF16),
            sds((1, MLA_Q_RANK), F32), sds((1, MLA_KV_RANK), F32),
            sds((MLA_HEADS, MLA_Q_RANK, LANES), F32), sds((MLA_HEADS, MLA_KV_RANK, LANES), F32),
            sds((MLA_HEADS, MLA_KV_RANK, LANES), F32),
            sds((1, LANES), F32), sds((1, LANES), F32),
        ),
        grid=(s // tm,),
        in_specs=_mla_pre_specs(tm) + [hb, hb, hb],
        out_specs=(
            row(MLA_Q_RANK), row(MLA_KV_RANK), row(LANES),
            _full((1, MLA_Q_RANK)), _full((1, MLA_KV_RANK)),
            _full((MLA_HEADS, MLA_Q_RANK, LANES)), _full((MLA_HEADS, MLA_KV_RANK, LANES)),
            _full((MLA_HEADS, MLA_KV_RANK, LANES)),
            _full((1, LANES)), _full((1, LANES)),
        ),
        compiler_params=_cparams(("arbitrary",)),
    )(proj, proj, proj, cosf, sinf, rot, qn_g, kvn_g, wq, wk, wv, qg, kg, dq, dk, dv)


ATT_SCALE = MLA_QK_DIM ** -0.5
NEG_BIG = -1e30


def _att_scores(q, k, diagonal):
    s = _nt(q, k) * ATT_SCALE
    if diagonal:
        s = jnp.where(_iota(s.shape, 0) >= _iota(s.shape, 1), s, NEG_BIG)
    return s


def _att_scores_t(k, q, diagonal):
    s = _nt(k, q) * ATT_SCALE
    if diagonal:
        s = jnp.where(_iota(s.shape, 0) <= _iota(s.shape, 1), s, NEG_BIG)
    return s


SUBLANES = 8


def _as_row(col_lanes):
    return jnp.transpose(col_lanes)[0:SUBLANES, :]


def _key_loop(lo, hi, t, step):
    def body(i, carry):
        step(pl.ds(pl.multiple_of(i * t, t), t))
        return carry

    lax.fori_loop(lo, hi, body, 0)


def _attention_fwd(q, k, v, name, t=512):
    s = q.shape[0]
    n = s // t

    def body(q_ref, k_ref, v_ref, o_ref, lse_ref, lse_t_ref, m_scr, acc_scr):
        qi = pl.program_id(1)
        lane = _iota((1, LANES), 1)
        m_scr[...] = jnp.full_like(m_scr, NEG_BIG)
        acc_scr[...] = jnp.zeros_like(acc_scr)

        def step(rows, diagonal=False):
            sc = _att_scores(q_ref[...], k_ref[rows, :], diagonal)
            m_old = m_scr[...]
            m_new = jnp.maximum(m_old, jnp.max(sc, axis=-1, keepdims=True))
            p = jnp.exp(sc - m_new)
            vb = jnp.where(lane == MLA_V, 1.0, v_ref[rows, :].astype(F32))
            acc_scr[...] = jnp.exp(m_old - m_new) * acc_scr[...] + _nn(p, vb)
            m_scr[...] = m_new

        _key_loop(0, qi, t, step)
        step(pl.ds(pl.multiple_of(qi * t, t), t), diagonal=True)
        acc = acc_scr[...]
        l = jnp.sum(jnp.where(lane == MLA_V, acc, 0.0), axis=-1, keepdims=True)
        o_ref[...] = jnp.where(lane < MLA_V, acc / l, 0.0)
        lse = jnp.broadcast_to(m_scr[...] + jnp.log(l), (t, LANES))
        lse_ref[...] = lse
        lse_t_ref[...] = _as_row(lse)

    qspec = pl.BlockSpec((t, LANES), lambda h, qi: (qi, h))
    kspec = pl.BlockSpec((s, LANES), lambda h, qi: (0, h))
    out = jax.ShapeDtypeStruct((s, HEAD_BLOCK), F32)
    return pl.pallas_call(
        body,
        name=name,
        out_shape=(out, out, jax.ShapeDtypeStruct((MLA_HEADS * SUBLANES, s), F32)),
        grid=(MLA_HEADS, n),
        in_specs=[qspec, kspec, kspec],
        out_specs=(qspec, qspec, pl.BlockSpec((SUBLANES, t), lambda h, qi: (h, qi))),
        scratch_shapes=[pltpu.VMEM((t, 1), F32), pltpu.VMEM((t, LANES), F32)],
        compiler_params=_cparams(("parallel", "parallel")),
    )(q, k, v)


def _attention_bwd(q, k, v, o, lse, lse_t, do, name, t=512):
    s = q.shape[0]
    n = s // t

    def dq_body(q_ref, k_ref, v_ref, o_ref, lse_ref, do_ref, dq_ref, delta_t_ref, acc_scr):
        qi = pl.program_id(1)
        do = do_ref[...]
        delta = jnp.sum(do * o_ref[...], axis=-1, keepdims=True)
        delta_t_ref[...] = _as_row(jnp.broadcast_to(delta, (t, LANES)))
        acc_scr[...] = jnp.zeros_like(acc_scr)

        def step(rows, diagonal=False):
            kb = k_ref[rows, :]
            p = jnp.exp(_att_scores(q_ref[...], kb, diagonal) - lse_ref[:, 0:1])
            ds = p * (_nt(do, v_ref[rows, :]) - delta) * ATT_SCALE
            acc_scr[...] += _nn(ds, kb)

        _key_loop(0, qi, t, step)
        step(pl.ds(pl.multiple_of(qi * t, t), t), diagonal=True)
        dq_ref[...] = acc_scr[...]

    def dkv_body(q_ref, k_ref, v_ref, lse_t_ref, delta_t_ref, do_ref, dk_ref, dv_ref, dk_scr, dv_scr):
        ki = pl.program_id(1)
        dk_scr[...] = jnp.zeros_like(dk_scr)
        dv_scr[...] = jnp.zeros_like(dv_scr)

        def step(rows, diagonal=False):
            qb = q_ref[rows, :]
            dob = do_ref[rows, :]
            p = jnp.exp(_att_scores_t(k_ref[...], qb, diagonal) - lse_t_ref[0:1, rows])
            dv_scr[...] += _nn(p, dob)
            ds = p * (_nt(v_ref[...], dob) - delta_t_ref[0:1, rows]) * ATT_SCALE
            dk_scr[...] += _nn(ds, qb)

        step(pl.ds(pl.multiple_of(ki * t, t), t), diagonal=True)
        _key_loop(ki + 1, n, t, step)
        dk_ref[...] = dk_scr[...]
        dv_ref[...] = dv_scr[...]

    out = jax.ShapeDtypeStruct((s, HEAD_BLOCK), F32)
    blk = pl.BlockSpec((t, LANES), lambda h, i: (i, h))
    head = pl.BlockSpec((s, LANES), lambda h, i: (0, h))
    row_blk = pl.BlockSpec((SUBLANES, t), lambda h, i: (h, i))
    row_head = pl.BlockSpec((SUBLANES, s), lambda h, i: (h, 0))
    dq, delta_t = pl.pallas_call(
        dq_body,
        name=name + "_dq",
        out_shape=(out, jax.ShapeDtypeStruct((MLA_HEADS * SUBLANES, s), F32)),
        grid=(MLA_HEADS, n),
        in_specs=[blk, head, head, blk, blk, blk],
        out_specs=(blk, row_blk),
        scratch_shapes=[pltpu.VMEM((t, LANES), F32)],
        compiler_params=_cparams(("parallel", "parallel")),
    )(q, k, v, o, lse, do)
    dk, dv = pl.pallas_call(
        dkv_body,
        name=name + "_dkv",
        out_shape=(out, out),
        grid=(MLA_HEADS, n),
        in_specs=[head, blk, blk, row_head, row_head, head],
        out_specs=(blk, blk),
        scratch_shapes=[pltpu.VMEM((t, LANES), F32), pltpu.VMEM((t, LANES), F32)],
        compiler_params=_cparams(("parallel", "parallel")),
    )(q, k, v, lse_t, delta_t, do)
    return dq, dk, dv


HALO = 8


def _conv_fwd(proj, w, b, name, tm=512):
    s = proj.shape[0]
    cb = P_XBC // SSD_CONV_DIM

    def body(x_ref, halo_ref, w_ref, b_ref, y_ref, cat_scr):
        i = pl.program_id(0)
        cat_scr[pl.ds(0, HALO), :] = jnp.where(i > 0, halo_ref[...], 0.0)
        cat_scr[pl.ds(HALO, tm), :] = x_ref[...]
        pre = b_ref[...]
        for j in range(SSD_CONV):
            pre = pre + w_ref[pl.ds(SSD_CONV - 1 - j, 1), :] * cat_scr[pl.ds(HALO - j, tm), :]
        y_ref[...] = _silu(pre)

    return pl.pallas_call(
        body,
        name=name,
        out_shape=jax.ShapeDtypeStruct((s, SSD_CONV_DIM), F32),
        grid=(s // tm,),
        in_specs=[
            pl.BlockSpec((tm, SSD_CONV_DIM), lambda i: (i, cb)),
            pl.BlockSpec((HALO, SSD_CONV_DIM), lambda i: (jnp.maximum(i * (tm // HALO) - 1, 0), cb)),
            _full((SSD_CONV, SSD_CONV_DIM)),
            _full((1, SSD_CONV_DIM)),
        ],
        out_specs=pl.BlockSpec((tm, SSD_CONV_DIM), lambda i: (i, 0)),
        scratch_shapes=[pltpu.VMEM((tm + HALO, SSD_CONV_DIM), F32)],
        compiler_params=_cparams(("parallel",)),
    )(proj, proj, w, b)


def _conv_bwd(proj, w, b, dact, name, tm=512):
    s = proj.shape[0]
    cb = P_XBC // SSD_CONV_DIM
    n = s // tm

    def pre_body(x_ref, halo_ref, w_ref, b_ref, dact_ref, dpre_ref, dw_ref, db_ref, cat_scr):
        i = pl.program_id(0)
        cat_scr[pl.ds(0, HALO), :] = jnp.where(i > 0, halo_ref[...], 0.0)
        cat_scr[pl.ds(HALO, tm), :] = x_ref[...]
        pre = b_ref[...]
        for j in range(SSD_CONV):
            pre = pre + w_ref[pl.ds(SSD_CONV - 1 - j, 1), :] * cat_scr[pl.ds(HALO - j, tm), :]
        sg = _sigmoid(pre)
        dpre = dact_ref[...] * (sg * (1.0 + pre * (1.0 - sg)))
        dpre_ref[...] = dpre
        first = i == 0
        _acc_rows(db_ref, dpre, first)
        for j in range(SSD_CONV):
            _acc_rows(dw_ref.at[pl.ds(SSD_CONV - 1 - j, 1), :], dpre * cat_scr[pl.ds(HALO - j, tm), :], first)

    dpre, dw, db = pl.pallas_call(
        pre_body,
        name=name + "_pre",
        out_shape=(
            jax.ShapeDtypeStruct((s, SSD_CONV_DIM), F32),
            jax.ShapeDtypeStruct((SSD_CONV, SSD_CONV_DIM), F32),
            jax.ShapeDtypeStruct((1, SSD_CONV_DIM), F32),
        ),
        grid=(n,),
        in_specs=[
            pl.BlockSpec((tm, SSD_CONV_DIM), lambda i: (i, cb)),
            pl.BlockSpec((HALO, SSD_CONV_DIM), lambda i: (jnp.maximum(i * (tm // HALO) - 1, 0), cb)),
            _full((SSD_CONV, SSD_CONV_DIM)),
            _full((1, SSD_CONV_DIM)),
            pl.BlockSpec((tm, SSD_CONV_DIM), lambda i: (i, 0)),
        ],
        out_specs=(
            pl.BlockSpec((tm, SSD_CONV_DIM), lambda i: (i, 0)),
            _full((SSD_CONV, SSD_CONV_DIM)),
            _full((1, SSD_CONV_DIM)),
        ),
        scratch_shapes=[pltpu.VMEM((tm + HALO, SSD_CONV_DIM), F32)],
        compiler_params=_cparams(("arbitrary",)),
    )(proj, proj, w, b, dact)

    def dx_body(d_ref, halo_ref, w_ref, dx_ref, cat_scr):
        i = pl.program_id(0)
        cat_scr[pl.ds(0, tm), :] = d_ref[...]
        cat_scr[pl.ds(tm, HALO), :] = jnp.where(i < n - 1, halo_ref[...], 0.0)
        dx = jnp.zeros((tm, SSD_CONV_DIM), F32)
        for j in range(SSD_CONV):
            dx = dx + w_ref[pl.ds(SSD_CONV - 1 - j, 1), :] * cat_scr[pl.ds(j, tm), :]
        dx_ref[...] = dx.astype(BF16)

    dx = pl.pallas_call(
        dx_body,
        name=name + "_dx",
        out_shape=jax.ShapeDtypeStruct((s, SSD_CONV_DIM), BF16),
        grid=(n,),
        in_specs=[
            pl.BlockSpec((tm, SSD_CONV_DIM), lambda i: (i, 0)),
            pl.BlockSpec((HALO, SSD_CONV_DIM), lambda i: (jnp.minimum((i + 1) * (tm // HALO), s // HALO - 1), 0)),
            _full((SSD_CONV, SSD_CONV_DIM)),
        ],
        out_specs=pl.BlockSpec((tm, SSD_CONV_DIM), lambda i: (i, 0)),
        scratch_shapes=[pltpu.VMEM((tm + HALO, SSD_CONV_DIM), F32)],
        compiler_params=_cparams(("parallel",)),
    )(dpre, dpre, w)
    return dx, dw, db


N_PAIR = SSD_HEADS // 2


def _ssd_chunk(xs, bm, cm, z, dtp, state, dtb, alog, dskip, ng):
    t = SSD_CHUNK
    lane = _iota((1, LANES), 1)
    row = _iota((LANES, 1), 0)
    dt_all = jnp.where(lane < SSD_HEADS, _softplus(dtp + dtb), 0.0)
    da = dt_all * (-jnp.exp(alog))
    causal = _iota((t, t), 0) >= _iota((t, t), 1)
    cs = _exact_nn(causal.astype(F32), da)
    cs_t = cs.T
    tot = jnp.sum(da, axis=0, keepdims=True)
    col = lambda m, h: jnp.sum(jnp.where(lane == h, m, 0.0), axis=1, keepdims=True)
    rowv = lambda m, h: jnp.sum(jnp.where(row == h, m, 0.0), axis=0, keepdims=True)
    low = lane < SSD_HEAD_DIM
    cb = [_nt(cm[g], bm[g]) for g in range(SSD_GROUPS)]
    gated, new_state = [], []
    for j in range(N_PAIR):
        g = j // (N_PAIR // SSD_GROUPS)
        h0, h1 = 2 * j, 2 * j + 1
        y = jnp.zeros((t, LANES), F32)
        for h, mask in ((h0, low), (h1, jnp.logical_not(low))):
            lmat = jnp.exp(jnp.where(causal, col(cs, h) - rowv(cs_t, h), NEG_BIG))
            y = y + _nn(cb[g] * lmat, jnp.where(mask, xs[j] * col(dt_all, h), 0.0))
        cs_p = jnp.where(low, col(cs, h0), col(cs, h1))
        dt_p = jnp.where(low, col(dt_all, h0), col(dt_all, h1))
        tot_p = jnp.where(low, col(tot, h0), col(tot, h1))
        tot_c = jnp.where(row < SSD_HEAD_DIM, col(tot, h0), col(tot, h1))
        d_p = jnp.where(low, col(dskip, h0), col(dskip, h1))
        xdt = xs[j] * dt_p
        y = y + _nt(cm[g], state[j]) * jnp.exp(cs_p) + xs[j] * d_p
        new_state.append(state[j] * jnp.exp(tot_c) + _tn(xdt * jnp.exp(tot_p - cs_p), bm[g]))
        gated.append(y * _silu(z[j]))
    out = []
    per_group = N_PAIR // SSD_GROUPS
    for g in range(SSD_GROUPS):
        blocks = gated[g * per_group:(g + 1) * per_group]
        ms = sum(jnp.sum(v * v, axis=-1, keepdims=True) for v in blocks) * (1.0 / (per_group * LANES))
        r = lax.rsqrt(ms + EPS)
        out += [v * r * ng[g * per_group + i] for i, v in enumerate(blocks)]
    return out, new_state


def _ssd_specs(rev, nc):
    idx = (lambda c: nc - 1 - c) if rev else (lambda c: c)
    t = SSD_CHUNK
    return [
        pl.BlockSpec((t, SSD_CONV_DIM), lambda c: (idx(c), 0)),
        pl.BlockSpec((t, SSD_INNER), lambda c: (idx(c), P_Z // SSD_INNER)),
        pl.BlockSpec((t, LANES), lambda c: (idx(c), P_DT // LANES)),
        _full((1, LANES)), _full((1, LANES)), _full((1, LANES)), _full((1, SSD_INNER)),
    ]


def _ssd_args(act_ref, z_ref, dt_ref, dtb_ref, alog_ref, dskip_ref, ng_ref):
    blk = lambda ref, off, n: [ref[:, pl.ds(off + i * LANES, LANES)] for i in range(n)]
    xs = blk(act_ref, 0, N_PAIR)
    bm = blk(act_ref, SSD_INNER, SSD_GROUPS)
    cm = blk(act_ref, SSD_INNER + SSD_GROUPS * SSD_STATE, SSD_GROUPS)
    return xs, bm, cm, blk(z_ref, 0, N_PAIR), dt_ref[...], dtb_ref[...], alog_ref[...], dskip_ref[...], blk(ng_ref, 0, N_PAIR)


def _ssd_fwd(act, proj, dtb, alog, dskip, ng, name):
    s = act.shape[0]
    nc = s // SSD_CHUNK

    def body(act_ref, z_ref, dt_ref, dtb_ref, alog_ref, dskip_ref, ng_ref, y_ref, st_ref, st_scr):
        @pl.when(pl.program_id(0) == 0)
        def _():
            st_scr[...] = jnp.zeros_like(st_scr)

        xs, bm, cm, z, dtp, dtb_v, alog_v, dskip_v, ng_v = _ssd_args(act_ref, z_ref, dt_ref, dtb_ref, alog_ref, dskip_ref, ng_ref)
        state = [st_scr[j] for j in range(N_PAIR)]
        st_ref[0] = st_scr[...]
        y, new_state = _ssd_chunk(xs, bm, cm, z, dtp, state, dtb_v, alog_v, dskip_v, ng_v)
        for j in range(N_PAIR):
            y_ref[:, pl.ds(j * LANES, LANES)] = y[j]
            st_scr[j] = new_state[j]

    return pl.pallas_call(
        body,
        name=name,
        out_shape=(
            jax.ShapeDtypeStruct((s, SSD_INNER), F32),
            jax.ShapeDtypeStruct((nc, N_PAIR, LANES, SSD_STATE), F32),
        ),
        grid=(nc,),
        in_specs=_ssd_specs(False, nc),
        out_specs=(
            pl.BlockSpec((SSD_CHUNK, SSD_INNER), lambda c: (c, 0)),
            pl.BlockSpec((1, N_PAIR, LANES, SSD_STATE), lambda c: (c, 0, 0, 0)),
        ),
        scratch_shapes=[pltpu.VMEM((N_PAIR, LANES, SSD_STATE), F32)],
        compiler_params=_cparams(("arbitrary",)),
    )(act, proj, proj, dtb, alog, dskip, ng)


def _ssd_bwd(act, proj, dtb, alog, dskip, ng, states, dy, name):
    s = act.shape[0]
    nc = s // SSD_CHUNK

    def body(act_ref, z_ref, dt_ref, dtb_ref, alog_ref, dskip_ref, ng_ref, st_ref, dy_ref,
             dact_ref, dz_ref, ddt_ref, ddtb_ref, dalog_ref, ddskip_ref, dng_ref, dst_scr):
        first = pl.program_id(0) == 0

        @pl.when(first)
        def _():
            dst_scr[...] = jnp.zeros_like(dst_scr)

        xs, bm, cm, z, dtp, dtb_v, alog_v, dskip_v, ng_v = _ssd_args(act_ref, z_ref, dt_ref, dtb_ref, alog_ref, dskip_ref, ng_ref)
        state = [st_ref[0, j] for j in range(N_PAIR)]
        _, vjp = jax.vjp(_ssd_chunk, xs, bm, cm, z, dtp, state, dtb_v, alog_v, dskip_v, ng_v)
        dy_v = [dy_ref[:, pl.ds(j * LANES, LANES)] for j in range(N_PAIR)]
        dxs, dbm, dcm, dz, ddtp, dstate, ddtb, dalog, ddskip, dng = vjp((dy_v, [dst_scr[j] for j in range(N_PAIR)]))
        for i, v in enumerate(dxs + dbm + dcm):
            dact_ref[:, pl.ds(i * LANES, LANES)] = v
        for j in range(N_PAIR):
            dz_ref[:, pl.ds(j * LANES, LANES)] = dz[j].astype(BF16)
            dst_scr[j] = dstate[j]
            _acc(dng_ref.at[:, pl.ds(j * LANES, LANES)], dng[j], first)
        ddt_ref[...] = ddtp.astype(BF16)
        _acc(ddtb_ref, ddtb, first)
        _acc(dalog_ref, dalog, first)
        _acc(ddskip_ref, ddskip, first)

    rv = lambda c: nc - 1 - c
    sds = jax.ShapeDtypeStruct
    return pl.pallas_call(
        body,
        name=name,
        out_shape=(
            sds((s, SSD_CONV_DIM), F32), sds((s, SSD_INNER), BF16), sds((s, LANES), BF16),
            sds((1, LANES), F32), sds((1, LANES), F32), sds((1, LANES), F32), sds((1, SSD_INNER), F32),
        ),
        grid=(nc,),
        in_specs=_ssd_specs(True, nc) + [
            pl.BlockSpec((1, N_PAIR, LANES, SSD_STATE), lambda c: (rv(c), 0, 0, 0)),
            pl.BlockSpec((SSD_CHUNK, SSD_INNER), lambda c: (rv(c), 0)),
        ],
        out_specs=(
            pl.BlockSpec((SSD_CHUNK, SSD_CONV_DIM), lambda c: (rv(c), 0)),
            pl.BlockSpec((SSD_CHUNK, SSD_INNER), lambda c: (rv(c), 0)),
            pl.BlockSpec((SSD_CHUNK, LANES), lambda c: (rv(c), 0)),
            _full((1, LANES)), _full((1, LANES)), _full((1, LANES)), _full((1, SSD_INNER)),
        ),
        scratch_shapes=[pltpu.VMEM((N_PAIR, LANES, SSD_STATE), F32)],
        compiler_params=_cparams(("arbitrary",)),
    )(act, proj, proj, dtb, alog, dskip, ng, states, dy)


def _merge_specs(tm):
    row = lambda w: pl.BlockSpec((tm, w), lambda i: (i, 0))
    return [
        row(GM_WIDTH), row(HEAD_BLOCK), row(SSD_INNER),
        pl.BlockSpec((tm, N_BRANCH * D_MODEL), lambda i: (i, P_GATES // (N_BRANCH * D_MODEL))),
        row(D_MODEL),
        _full((GM_WIDTH, D_MODEL)), _full((HEAD_BLOCK, D_MODEL)), _full((SSD_INNER, D_MODEL)), _full((D_MODEL, D_MODEL)),
    ]


def _merge_fwd(ya, yb, yc, proj, x1, pa, pb, pc, wo, name, tm=256):
    s = x1.shape[0]

    def body(ya_ref, yb_ref, yc_ref, gates_ref, x1_ref, pa_ref, pb_ref, pc_ref, wo_ref, x2_ref, mg_ref):
        merged = jnp.zeros((tm, D_MODEL), F32)
        for i, (y_ref, p_ref) in enumerate(((ya_ref, pa_ref), (yb_ref, pb_ref), (yc_ref, pc_ref))):
            gate = _sigmoid(gates_ref[:, pl.ds(i * D_MODEL, D_MODEL)])
            merged = merged + gate * _nn(y_ref[...], p_ref[...])
        mg_ref[...] = merged.astype(BF16)
        x2_ref[...] = x1_ref[...] + _nn(merged, wo_ref[...])

    row = lambda w: pl.BlockSpec((tm, w), lambda i: (i, 0))
    return pl.pallas_call(
        body,
        name=name,
        out_shape=(jax.ShapeDtypeStruct((s, D_MODEL), F32), jax.ShapeDtypeStruct((s, D_MODEL), BF16)),
        grid=(s // tm,),
        in_specs=_merge_specs(tm),
        out_specs=(row(D_MODEL), row(D_MODEL)),
        compiler_params=_cparams(("parallel",)),
    )(ya, yb, yc, proj, x1, pa, pb, pc, wo)


def _merge_bwd(ya, yb, yc, proj, dx2, pa, pb, pc, wo, name, tm=256):
    s = dx2.shape[0]

    def body(ya_ref, yb_ref, yc_ref, gates_ref, dx2_ref, pa_ref, pb_ref, pc_ref, wo_ref,
             dya_ref, dyb_ref, dyc_ref, dgates_ref, ta_ref, tb_ref, tc_ref):
        dmerged = _nt(dx2_ref[...], wo_ref[...])
        branches = ((ya_ref, pa_ref, dya_ref, ta_ref), (yb_ref, pb_ref, dyb_ref, tb_ref), (yc_ref, pc_ref, dyc_ref, tc_ref))
        for i, (y_ref, p_ref, dy_ref, t_ref) in enumerate(branches):
            cols = pl.ds(i * D_MODEL, D_MODEL)
            gate = _sigmoid(gates_ref[:, cols])
            dgates_ref[:, cols] = (dmerged * _nn(y_ref[...], p_ref[...]) * gate * (1.0 - gate)).astype(BF16)
            dt = (dmerged * gate).astype(BF16)
            t_ref[...] = dt
            dy_ref[...] = _nt(dt, p_ref[...])

    row = lambda w: pl.BlockSpec((tm, w), lambda i: (i, 0))
    sds = jax.ShapeDtypeStruct
    return pl.pallas_call(
        body,
        name=name,
        out_shape=(
            sds((s, GM_WIDTH), F32), sds((s, HEAD_BLOCK), F32), sds((s, SSD_INNER), F32),
            sds((s, N_BRANCH * D_MODEL), BF16),
            sds((s, D_MODEL), BF16), sds((s, D_MODEL), BF16), sds((s, D_MODEL), BF16),
        ),
        grid=(s // tm,),
        in_specs=_merge_specs(tm),
        out_specs=(row(GM_WIDTH), row(HEAD_BLOCK), row(SSD_INNER), row(N_BRANCH * D_MODEL),
                   row(D_MODEL), row(D_MODEL), row(D_MODEL)),
        compiler_params=_cparams(("parallel",)),
    )(ya, yb, yc, proj, dx2, pa, pb, pc, wo)


def _loss_head(y, target, name, tm=512):
    s, d = y.shape

    def body(y_ref, t_ref, dy_ref, loss_ref):
        err = y_ref[...] - t_ref[...]
        dy_ref[...] = err * (1.0 / d)
        part = jnp.sum(jnp.sum(err * err, axis=1, keepdims=True), axis=0, keepdims=True) * (0.5 / d)
        _acc(loss_ref, jnp.broadcast_to(part, (1, LANES)), pl.program_id(0) == 0)

    return pl.pallas_call(
        body,
        name=name,
        out_shape=(jax.ShapeDtypeStruct((s, d), F32), jax.ShapeDtypeStruct((1, LANES), F32)),
        grid=(s // tm,),
        in_specs=[pl.BlockSpec((tm, d), lambda i: (i, 0)), pl.BlockSpec((tm, d), lambda i: (i, 0))],
        out_specs=(pl.BlockSpec((tm, d), lambda i: (i, 0)), _full((1, LANES))),
        compiler_params=_cparams(("arbitrary",)),
    )(y, target)


IN_SHARD = IN_COLS // N_DEV
P_OF_PIECE = (P_UV, P_CQ, P_CKV, P_KR + MLA_NOPE, P_Z, P_XBC, P_DT, P_GATES)


def _pad_lanes(w, n=LANES):
    return jnp.pad(w, [(0, 0)] * (w.ndim - 1) + [(0, n - w.shape[-1])])


def _in_proj_layout(blocks):
    def cols(i):
        a, b, out = IN_OFFSETS[i], IN_OFFSETS[i] + IN_WIDTHS[i], []
        while a < b:
            k, lo = divmod(a, IN_SHARD)
            hi = min(IN_SHARD, lo + b - a)
            out.append(blocks[k, :, lo:hi])
            a += hi - lo
        return out

    zeros = lambda n: jnp.zeros((D_MODEL, n), blocks.dtype)
    uv, cq, ckv, kr, z, xbc, dt, gates = (cols(i) for i in range(8))
    return jnp.concatenate(uv + xbc + z + [zeros(MLA_NOPE)] + kr + [zeros(LANES - MLA_QK_DIM)] + cq + gates + ckv
                           + dt + [zeros(LANES - SSD_HEADS)], axis=1)


def _in_proj_unlayout(dw):
    out = []
    for k in range(N_DEV):
        a, b, parts = k * IN_SHARD, (k + 1) * IN_SHARD, []
        for i in range(8):
            lo, hi = max(a, IN_OFFSETS[i]), min(b, IN_OFFSETS[i] + IN_WIDTHS[i])
            if lo < hi:
                at = P_OF_PIECE[i] + lo - IN_OFFSETS[i]
                parts.append(dw[:, at:at + hi - lo])
        out.append(jnp.concatenate(parts, axis=1))
    return jnp.stack(out)


def _row(v, n=None):
    v = v.reshape(1, -1)
    return v if n is None else jnp.pad(v, ((0, 0), (0, n - v.shape[1])))


def _layer_weights(gw, vec, l):
    kv = gw["mla_w_ukv"][:, l]
    branch = jnp.moveaxis(gw["w_branch"][:, l], 0, 2).reshape(N_BRANCH, GM_WIDTH, D_MODEL)
    return dict(
        ffn_w_in=gw["ffn_w_in"], ffn_w_out=gw["ffn_w_out"], ffn1_at=l, ffn2_at=gw["w_out"].shape[1] + l,
        ffn1_norm=_row(vec["ffn1_norm"][l]), ffn2_norm=_row(vec["ffn2_norm"][l]),
        mix_norm=_row(vec["mix_norm"][l]), w_in=_in_proj_layout(gw["w_in"][:, l]),
        gm_v_norm=_row(vec["gm_v_norm"][l]), gm_w_s=vec["gm_w_s"][l], gm_b_s=vec["gm_b_s"][l][..., None],
        q_norm=_row(vec["mla_q_norm"][l]), kv_norm=_row(vec["mla_kv_norm"][l]),
        wq=_pad_lanes(gw["mla_w_uq"][:, l]), wk=_pad_lanes(kv[:, :, :MLA_NOPE]), wv=_pad_lanes(kv[:, :, MLA_NOPE:]),
        q_gain=_row(vec["mla_q_gain"][l], LANES), k_gain=_row(vec["mla_k_gain"][l], LANES),
        conv_w=jnp.moveaxis(gw["ssd_conv_w"][:, l], 0, 1).reshape(SSD_CONV, SSD_CONV_DIM).astype(F32),
        conv_b=_row(vec["ssd_conv_b"][l]),
        dt_bias=_row(vec["ssd_dt_bias"][l], LANES), a_log=_row(vec["ssd_a_log"][l], LANES),
        d_skip=_row(vec["ssd_d"][l], LANES), ssd_norm=_row(vec["ssd_norm"][l]),
        pa=branch[0],
        pb=jnp.pad(branch[1].reshape(MLA_HEADS, MLA_V, D_MODEL), ((0, 0), (0, LANES - MLA_V), (0, 0))).reshape(HEAD_BLOCK, D_MODEL),
        pc=branch[2], wo=gw["w_out"][:, l].reshape(D_MODEL, D_MODEL),
    )


def _layer_fwd(x, k, rope):
    cosf, sinf, rot = rope
    x1, gu1 = _ffn_fwd(x, k["ffn1_norm"], k["ffn_w_in"], k["ffn_w_out"], k["ffn1_at"], "ffn1_fwd")
    h = _rmsnorm_fwd(x1, k["mix_norm"], "mix_norm_fwd")
    proj = _matmul(h, k["w_in"], "nn", F32, "in_proj_fwd", tn=2176)
    ya = _gmlp_fwd(proj, k["gm_v_norm"], k["gm_w_s"], k["gm_b_s"], "gmlp_fwd")
    q, kk, v = _mla_pre_fwd(proj, cosf, sinf, rot, k["q_norm"], k["kv_norm"], k["wq"], k["wk"], k["wv"],
                            k["q_gain"], k["k_gain"], "mla_pre_fwd")
    yb, lse, lse_t = _attention_fwd(q, kk, v, "attention_fwd")
    act = _conv_fwd(proj, k["conv_w"], k["conv_b"], "conv_fwd")
    yc, states = _ssd_fwd(act, proj, k["dt_bias"], k["a_log"], k["d_skip"], k["ssd_norm"], "ssd_fwd")
    x2, merged = _merge_fwd(ya, yb, yc, proj, x1, k["pa"], k["pb"], k["pc"], k["wo"], "merge_fwd")
    x3, gu2 = _ffn_fwd(x2, k["ffn2_norm"], k["ffn_w_in"], k["ffn_w_out"], k["ffn2_at"], "ffn2_fwd")
    saved = dict(x=x, x1=x1, x2=x2, gu1=gu1, gu2=gu2, h=h, proj=proj, ya=ya, yb=yb, yc=yc, q=q, k=kk, v=v,
                 lse=lse, lse_t=lse_t, act=act, states=states, merged=merged)
    return x3, saved


def _layer_bwd(dx3, k, sv, rope):
    cosf, sinf, rot = rope
    g = {}
    dx2, g["ffn2_norm"], g["ffn2_w_in"], g["ffn2_w_out"] = _ffn_bwd(
        sv["x2"], k["ffn2_norm"], k["ffn_w_in"], k["ffn_w_out"], k["ffn2_at"], sv["gu2"], dx3, "ffn2_bwd")
    proj = sv["proj"]
    dya, dyb, dyc, dgates, ta, tb, tc = _merge_bwd(sv["ya"], sv["yb"], sv["yc"], proj, dx2, k["pa"], k["pb"], k["pc"],
                                                   k["wo"], "merge_bwd")
    g["w_out"] = _matmul(sv["merged"], dx2, "tn", BF16, "w_out_grad").reshape(N_DEV, D_MODEL // N_DEV, D_MODEL)
    dpa = _matmul(sv["ya"], ta, "tn", BF16, "branch_a_grad")
    dpb = _matmul(sv["yb"], tb, "tn", BF16, "branch_b_grad")
    dpc = _matmul(sv["yc"], tc, "tn", BF16, "branch_c_grad")
    branch = jnp.stack([dpa, dpb.reshape(MLA_HEADS, LANES, D_MODEL)[:, :MLA_V].reshape(GM_WIDTH, D_MODEL), dpc])
    g["w_branch"] = jnp.moveaxis(branch.reshape(N_BRANCH, GM_WIDTH, N_DEV, LANES), 2, 0)
    duv, g["gm_v_norm"], g["gm_w_s"], dbs = _gmlp_bwd(proj, k["gm_v_norm"], k["gm_w_s"], k["gm_b_s"], dya, "gmlp_bwd")
    g["gm_b_s"] = dbs[..., 0]
    dq, dk, dv = _attention_bwd(sv["q"], sv["k"], sv["v"], sv["yb"], sv["lse"], sv["lse_t"], dyb, "attention_bwd")
    dcq, dckv, dkr, dqn, dkvn, dwq, dwk, dwv, dqg, dkg = _mla_pre_bwd(
        proj, cosf, sinf, rot, k["q_norm"], k["kv_norm"], k["wq"], k["wk"], k["wv"], k["q_gain"], k["k_gain"],
        dq, dk, dv, "mla_pre_bwd")
    g["mla_q_norm"], g["mla_kv_norm"] = dqn, dkvn
    g["mla_w_uq"] = dwq[:, :, :MLA_QK_DIM].astype(BF16)
    g["mla_w_ukv"] = jnp.concatenate([dwk[:, :, :MLA_NOPE], dwv[:, :, :MLA_V]], axis=-1).astype(BF16)
    g["mla_q_gain"], g["mla_k_gain"] = dqg[:, :MLA_QK_DIM], dkg[:, :MLA_QK_DIM]
    dact, dz, ddt, ddtb, dalog, ddsk, g["ssd_norm"] = _ssd_bwd(
        sv["act"], proj, k["dt_bias"], k["a_log"], k["d_skip"], k["ssd_norm"], sv["states"], dyc, "ssd_bwd")
    g["ssd_dt_bias"], g["ssd_a_log"], g["ssd_d"] = ddtb[:, :SSD_HEADS], dalog[:, :SSD_HEADS], ddsk[:, :SSD_HEADS]
    dxbc, dcw, g["ssd_conv_b"] = _conv_bwd(proj, k["conv_w"], k["conv_b"], dact, "conv_bwd")
    g["ssd_conv_w"] = jnp.moveaxis(dcw.reshape(SSD_CONV, N_DEV, LANES), 1, 0).astype(BF16)
    dproj = jnp.concatenate([duv, dxbc, dz, dkr, dcq, dgates, dckv, ddt], axis=1)
    dh = _matmul(dproj, k["w_in"], "nt", BF16, "in_proj_dh", tk=2176)
    g["w_in"] = _in_proj_unlayout(_matmul(sv["h"], dproj, "tn", BF16, "in_proj_grad", tn=2176))
    dx1, g["mix_norm"] = _rmsnorm_bwd(sv["x1"], k["mix_norm"], dh, dx2, "mix_norm_bwd")
    dx, g["ffn1_norm"], g["ffn1_w_in"], g["ffn1_w_out"] = _ffn_bwd(
        sv["x"], k["ffn1_norm"], k["ffn_w_in"], k["ffn_w_out"], k["ffn1_at"], sv["gu1"], dx1, "ffn1_bwd")
    return dx, g


def _rope_tables(positions):
    s = positions.shape[0]
    inv_freq = 1.0 / (ROPE_THETA ** (jnp.arange(0, MLA_ROPE, 2, dtype=F32) / MLA_ROPE))
    ang = positions.astype(F32)[:, None] * inv_freq
    cos, sin = jnp.cos(ang), jnp.sin(ang)
    tail = LANES - MLA_QK_DIM
    cosf = jnp.concatenate([jnp.ones((s, MLA_NOPE), F32), cos, cos, jnp.ones((s, tail), F32)], axis=1)
    sinf = jnp.concatenate([jnp.zeros((s, MLA_NOPE), F32), sin, sin, jnp.zeros((s, tail), F32)], axis=1)
    half = MLA_ROPE // 2
    rot = np.zeros((LANES, LANES), np.float32)
    for i in range(half):
        rot[MLA_NOPE + half + i, MLA_NOPE + i] = -1.0
        rot[MLA_NOPE + i, MLA_NOPE + half + i] = 1.0
    return cosf, sinf, jnp.asarray(rot)


MATRICES = ("ffn1_w_in", "ffn1_w_out", "w_in", "mla_w_uq", "mla_w_ukv", "ssd_conv_w", "w_branch", "w_out", "ffn2_w_in",
            "ffn2_w_out")
VECTORS = ("ffn1_norm", "mix_norm", "gm_v_norm", "gm_w_s", "gm_b_s", "mla_q_norm", "mla_kv_norm", "mla_q_gain",
           "mla_k_gain", "ssd_conv_b", "ssd_dt_bias", "ssd_a_log", "ssd_d", "ssd_norm", "ffn2_norm")
WEIGHTS = ("ffn1_norm", "ffn1_w_in", "ffn1_w_out", "mix_norm", "w_in", "gm_v_norm", "gm_w_s", "gm_b_s", "mla_q_norm",
           "mla_kv_norm", "mla_w_uq", "mla_w_ukv", "mla_q_gain", "mla_k_gain", "ssd_conv_w", "ssd_conv_b", "ssd_dt_bias",
           "ssd_a_log", "ssd_d", "ssd_norm", "w_branch", "w_out", "ffn2_norm", "ffn2_w_in", "ffn2_w_out")


def _local_step(x, positions, target, gw, vec):
    rope = _rope_tables(positions)
    depth = gw["w_out"].shape[1]
    saved = []
    for l in range(depth):
        k = _layer_weights(gw, vec, l)
        x, sv = _layer_fwd(x, k, rope)
        saved.append((k, sv))
    dy, loss = _loss_head(x, target, "loss_head")
    grads = []
    for k, sv in reversed(saved):
        dy, g = _layer_bwd(dy, k, sv, rope)
        grads.append(g)
    grads.reverse()
    out = {n: [g[n] for g in grads] for n in MATRICES}
    out.update({n: jnp.stack([g[n].reshape(vec[n].shape[1:]) for g in grads]) for n in VECTORS})
    return loss[0, 0], dy, out


MESH = pl.DeviceIdType.MESH
N_CHIP = 4
ANY = pl.BlockSpec(memory_space=pl.ANY)


def _place():
    return lax.axis_index("x"), lax.axis_index("y"), lax.axis_index("c")


def _all_gather(shard, name):
    m, n = shard.shape

    def body(x_ref, out_ref, send_sems, recv_sems, local_sem):
        x, y, c = _place()
        me, sibling = (x, y, c), (x, y, 1 - c)
        chips = [(1 - x, y), (x, 1 - y), (1 - x, 1 - y)]

        def rows(px, py, pc):
            return out_ref.at[pl.ds((4 * px + 2 * py + pc) * m, m), :]

        def copy(k, block, to, src=None):
            return pltpu.make_async_remote_copy(
                src_ref=rows(*block) if src is None else src, dst_ref=rows(*block),
                send_sem=send_sems.at[k], recv_sem=recv_sems.at[k], device_id=to, device_id_type=MESH)

        mine = pltpu.make_async_copy(x_ref, rows(*me), local_sem)
        mine.start()
        first = [copy(0, me, sibling, src=x_ref)]
        first += [copy(1 + j, me, (*chip, c), src=x_ref) for j, chip in enumerate(chips)]
        for cp in first:
            cp.start()
        passed = [copy(4 + j, (*chip, c), sibling) for j, chip in enumerate(chips)]
        for j, chip in enumerate(chips):
            copy(1 + j, (*chip, c), me).wait_recv()
            passed[j].start()
        copy(0, sibling, me).wait_recv()
        for j, chip in enumerate(chips):
            copy(4 + j, (*chip, 1 - c), me).wait_recv()
        for cp in first + passed:
            cp.wait_send()
        mine.wait()

    return pl.pallas_call(
        body,
        name=name,
        out_shape=jax.ShapeDtypeStruct((N_DEV * m, n), shard.dtype),
        in_specs=[ANY],
        out_specs=ANY,
        scratch_shapes=[pltpu.SemaphoreType.DMA((7,)), pltpu.SemaphoreType.DMA((7,)), pltpu.SemaphoreType.DMA],
    )(shard)


def _pair_exchange(contrib, name):
    _, r, n = contrib.shape

    def body(g_ref, got_ref, send_sems, recv_sems):
        x, y, c = _place()
        remote = [pltpu.make_async_remote_copy(
            src_ref=g_ref.at[2 * j + (1 - c)], dst_ref=got_ref.at[j], send_sem=send_sems.at[j], recv_sem=recv_sems.at[j],
            device_id=(x, y, 1 - c), device_id_type=MESH) for j in range(N_CHIP)]
        for cp in remote:
            cp.start()
        for cp in remote:
            cp.wait()

    return pl.pallas_call(
        body,
        name=name,
        out_shape=jax.ShapeDtypeStruct((N_CHIP, r, n), contrib.dtype),
        in_specs=[ANY],
        out_specs=ANY,
        scratch_shapes=[pltpu.SemaphoreType.DMA((N_CHIP,)), pltpu.SemaphoreType.DMA((N_CHIP,))],
    )(contrib)


def _pair_sum(contrib, theirs, name, tr):
    _, r, n = contrib.shape
    side = lax.axis_index("c").astype(jnp.int32).reshape(1)

    def body(c_ref, a_ref, b_ref, o_ref):
        o_ref[...] = (a_ref[...].astype(F32) + b_ref[...].astype(F32)).astype(BF16)

    spec = pl.BlockSpec((1, tr, n), lambda j, i, c_ref: (j, i, 0))
    return pl.pallas_call(
        body,
        name=name,
        out_shape=jax.ShapeDtypeStruct(theirs.shape, BF16),
        grid_spec=pltpu.PrefetchScalarGridSpec(
            num_scalar_prefetch=1,
            grid=(N_CHIP, r // tr),
            in_specs=[pl.BlockSpec((1, tr, n), lambda j, i, c_ref: (2 * j + c_ref[0], i, 0)), spec],
            out_specs=spec,
        ),
        compiler_params=_cparams(("parallel", "parallel")),
    )(side, contrib, theirs)


def _chip_exchange(part, name):
    _, r, n = part.shape

    def body(p_ref, got_ref, send_sems, recv_sems, local_sem):
        x, y, c = _place()
        mine = 2 * x + y
        chips = [(1 - x, y), (x, 1 - y), (1 - x, 1 - y)]
        local = pltpu.make_async_copy(p_ref.at[mine], got_ref.at[mine], local_sem)
        local.start()
        sends = [pltpu.make_async_remote_copy(
            src_ref=p_ref.at[2 * cx + cy], dst_ref=got_ref.at[mine], send_sem=send_sems.at[k], recv_sem=recv_sems.at[k],
            device_id=(cx, cy, c), device_id_type=MESH) for k, (cx, cy) in enumerate(chips)]
        for cp in sends:
            cp.start()
        for k, (cx, cy) in enumerate(chips):
            pltpu.make_async_remote_copy(
                src_ref=p_ref.at[mine], dst_ref=got_ref.at[2 * cx + cy], send_sem=send_sems.at[k],
                recv_sem=recv_sems.at[k], device_id=(cx, cy, c), device_id_type=MESH).wait_recv()
        for cp in sends:
            cp.wait_send()
        local.wait()

    return pl.pallas_call(
        body,
        name=name,
        out_shape=jax.ShapeDtypeStruct(part.shape, part.dtype),
        in_specs=[ANY],
        out_specs=ANY,
        scratch_shapes=[pltpu.SemaphoreType.DMA((3,)), pltpu.SemaphoreType.DMA((3,)), pltpu.SemaphoreType.DMA],
    )(part)


def _adamw(parts, w, m, v, name, tr, at=0):
    k = parts.shape[0]
    r, n = w.shape
    first = at // tr

    def body(p_ref, w_ref, m_ref, v_ref, g_ref, d_ref, nm_ref, nv_ref):
        g = p_ref[0].astype(F32)
        for i in range(1, k):
            g = g + p_ref[i].astype(F32)
        m_new = ADAM_B1 * m_ref[...] + (1.0 - ADAM_B1) * g
        v_new = ADAM_B2 * v_ref[...] + (1.0 - ADAM_B2) * (g * g)
        m_hat = m_new / (1.0 - ADAM_B1 ** ADAM_STEP)
        v_hat = v_new / (1.0 - ADAM_B2 ** ADAM_STEP)
        g_ref[...] = g
        d_ref[...] = -ADAM_LR * (m_hat / (jnp.sqrt(v_hat) + ADAM_EPS) + ADAM_WD * w_ref[...])
        nm_ref[...] = m_new
        nv_ref[...] = v_new

    spec = pl.BlockSpec((tr, n), lambda i: (i, 0))
    out = jax.ShapeDtypeStruct((r, n), F32)
    return pl.pallas_call(
        body,
        name=name,
        out_shape=(out, out, out, out),
        grid=(r // tr,),
        in_specs=[pl.BlockSpec((k, tr, n), lambda i: (0, i + first, 0)), spec, spec, spec],
        out_specs=(spec, spec, spec, spec),
        compiler_params=_cparams(("parallel",)),
    )(parts, w, m, v)


TILE_BYTES = 2 * 1024 * 1024
SUBLANES_16BIT = 16


def _tile_rows(r, n):
    best = None
    for t in range(SUBLANES_16BIT, r, SUBLANES_16BIT):
        if r % t == 0 and t * n * 4 <= TILE_BYTES:
            best = t
    return best or r


def _rows(a):
    return a.reshape(-1, a.shape[-1])


def _gather_rows(shard, name):
    return _all_gather(_rows(shard), name).reshape(N_DEV, *shard.shape)


def _reduce_scatter(contrib, name):
    theirs = _pair_exchange(contrib, name + "_pair_exchange")
    part = _pair_sum(contrib, theirs, name + "_pair_sum", _tile_rows(contrib.shape[1], contrib.shape[2]))
    return _chip_exchange(part, name + "_chip_exchange")


SMALL = ("ffn1_norm", "mix_norm", "gm_v_norm", "gm_b_s", "mla_q_norm", "mla_kv_norm", "mla_q_gain", "mla_k_gain",
         "ssd_conv_b", "ssd_dt_bias", "ssd_a_log", "ssd_d", "ssd_norm", "ffn2_norm")
SMALL_ROWS = 8
SMALL_COLS = 7040


def _side_by_side(d):
    cols = jnp.concatenate([d[n].reshape(d[n].shape[0], -1) for n in SMALL], axis=1)
    return jnp.pad(cols, ((0, SMALL_ROWS - cols.shape[0]), (0, SMALL_COLS - cols.shape[1])))


def _apart(packed, like):
    out, off = {}, 0
    for n in SMALL:
        size = like[n][0].size
        out[n] = packed[:like[n].shape[0], off:off + size].reshape(like[n].shape)
        off += size
    return out


def kernel(x, positions, ffn1_norm, ffn1_w_in, ffn1_w_out, mix_norm, w_in, gm_v_norm, gm_w_s, gm_b_s, mla_q_norm, mla_kv_norm, mla_w_uq, mla_w_ukv, mla_q_gain, mla_k_gain, ssd_conv_w, ssd_conv_b, ssd_dt_bias, ssd_a_log, ssd_d, ssd_norm, w_branch, w_out, ffn2_norm, ffn2_w_in, ffn2_w_out, loss_target, m_ffn1_norm, m_ffn1_w_in, m_ffn1_w_out, m_mix_norm, m_w_in, m_gm_v_norm, m_gm_w_s, m_gm_b_s, m_mla_q_norm, m_mla_kv_norm, m_mla_w_uq, m_mla_w_ukv, m_mla_q_gain, m_mla_k_gain, m_ssd_conv_w, m_ssd_conv_b, m_ssd_dt_bias, m_ssd_a_log, m_ssd_d, m_ssd_norm, m_w_branch, m_w_out, m_ffn2_norm, m_ffn2_w_in, m_ffn2_w_out, v_ffn1_norm, v_ffn1_w_in, v_ffn1_w_out, v_mix_norm, v_w_in, v_gm_v_norm, v_gm_w_s, v_gm_b_s, v_mla_q_norm, v_mla_kv_norm, v_mla_w_uq, v_mla_w_ukv, v_mla_q_gain, v_mla_k_gain, v_ssd_conv_w, v_ssd_conv_b, v_ssd_dt_bias, v_ssd_a_log, v_ssd_d, v_ssd_norm, v_w_branch, v_w_out, v_ffn2_norm, v_ffn2_w_in, v_ffn2_w_out):
    given = dict(locals())
    w = {n: given[n] for n in WEIGHTS}
    mom = {n: given["m_" + n] for n in WEIGHTS}
    var = {n: given["v_" + n] for n in WEIGHTS}
    groups = {"ffn_w_in": ("ffn1_w_in", "ffn2_w_in"), "ffn_w_out": ("ffn1_w_out", "ffn2_w_out"), "w_in": ("w_in",),
              "mla_w_uq": ("mla_w_uq",), "mla_w_ukv": ("mla_w_ukv",), "ssd_conv_w": ("ssd_conv_w",),
              "w_branch": ("w_branch",), "w_out": ("w_out",)}
    gw = {g: _gather_rows(jnp.concatenate([w[n].astype(BF16) for n in names]), g + "_all_gather")
          for g, names in groups.items()}

    loss, dx, grads = _local_step(x[0], positions[0], loss_target[0], gw, {n: w[n] for n in VECTORS})

    outs = [{}, {}, {}, {}]
    for g, names in groups.items():
        contrib = jnp.stack([a for n in names for a in grads[n]], axis=1)
        parts = _reduce_scatter(contrib.reshape(N_DEV, -1, contrib.shape[-1]), g + "_grad")
        at = 0
        for n in names:
            wr = _rows(w[n])
            res = _adamw(parts, wr, _rows(mom[n]), _rows(var[n]), "adamw_" + n, _tile_rows(*wr.shape), at=at)
            at += wr.shape[0]
            for o, r in zip(outs, res):
                o[n] = r.reshape(w[n].shape)

    small_parts = _gather_rows(_side_by_side(grads), "small_grads_all_gather")
    small = _adamw(small_parts, _side_by_side(w), _side_by_side(mom), _side_by_side(var), "adamw_small", SMALL_ROWS)
    ws_parts = _gather_rows(_rows(grads["gm_w_s"]), "gm_w_s_grads_all_gather")
    ws = _adamw(ws_parts, _rows(w["gm_w_s"]), _rows(mom["gm_w_s"]), _rows(var["gm_w_s"]), "adamw_gm_w_s",
                _tile_rows(ws_parts.shape[1], LANES))
    for o, sm, r in zip(outs, small, ws):
        o.update(_apart(sm, w))
        o["gm_w_s"] = r.reshape(w["gm_w_s"].shape)

    loss = lax.psum(loss, ("x", "y", "c"))
    return (loss, dx[None], *[o[n] for o in outs for n in WEIGHTS])
```

```python
import functools

import jax
import jax.numpy as jnp
import numpy as np
from jax import lax
from jax.experimental import pallas as pl
from jax.experimental.pallas import tpu as pltpu

F32 = jnp.float32
BF16 = jnp.bfloat16

D_MODEL = 1024
DEPTH = 4
D_FF = 2816
FFN_RESID = 0.5
EPS = 1e-6
GM_WIDTH = 512
GM_GROUPS = 4
GM_CHUNK = 128
MLA_HEADS = 8
MLA_Q_RANK = 384
MLA_KV_RANK = 256
MLA_NOPE = 64
MLA_ROPE = 32
MLA_QK_DIM = 96
MLA_V = 64
ROPE_THETA = 10000.0
SSD_HEADS = 8
SSD_HEAD_DIM = 64
SSD_INNER = 512
SSD_GROUPS = 2
SSD_STATE = 128
SSD_CONV = 4
SSD_CHUNK = 128
SSD_CONV_DIM = 1024
N_BRANCH = 3
IN_WIDTHS = (1024, 384, 256, 32, 512, 1024, 8, 3072)
IN_OFFSETS = (0, 1024, 1408, 1664, 1696, 2208, 3232, 3240)
IN_COLS = 6312
LANES = 128
N_DEV = 8

ADAM_LR = 0.001
ADAM_B1 = 0.9
ADAM_B2 = 0.999
ADAM_EPS = 1e-08
ADAM_WD = 0.01
ADAM_STEP = 10

VMEM_LIMIT = 56 * 1024 * 1024

P_UV, P_XBC, P_Z, P_KR, P_CQ, P_GATES, P_CKV, P_DT = 0, 1024, 2048, 2560, 2688, 3072, 6144, 6400
P_COLS = 6528


def _cparams(sem):
    return pltpu.CompilerParams(dimension_semantics=sem, vmem_limit_bytes=VMEM_LIMIT)


def _bdot(a, b, dims):
    return lax.dot_general(a.astype(BF16), b.astype(BF16), (dims, ((), ())), preferred_element_type=F32)


@jax.custom_vjp
def _nn(a, b):
    return _bdot(a, b, ((1,), (0,)))


@jax.custom_vjp
def _nt(a, b):
    return _bdot(a, b, ((1,), (1,)))


@jax.custom_vjp
def _tn(a, b):
    return _bdot(a, b, ((0,), (0,)))


def _dot_fwd(dims):
    return lambda a, b: (_bdot(a, b, dims), (a, b))


_nn.defvjp(_dot_fwd(((1,), (0,))), lambda r, g: (_nt(g, r[1]).astype(r[0].dtype), _tn(r[0], g).astype(r[1].dtype)))
_nt.defvjp(_dot_fwd(((1,), (1,))), lambda r, g: (_nn(g, r[1]).astype(r[0].dtype), _tn(g, r[0]).astype(r[1].dtype)))
_tn.defvjp(_dot_fwd(((0,), (0,))), lambda r, g: (_nt(r[1], g).astype(r[0].dtype), _nn(r[0], g).astype(r[1].dtype)))


def _exact_nn(a, b):
    return lax.dot_general(a, b, (((1,), (0,)), ((), ())), precision=lax.Precision.HIGHEST, preferred_element_type=F32)


def _sigmoid(x):
    return 1.0 / (1.0 + jnp.exp(-x))


def _silu(x):
    return x * _sigmoid(x)


def _softplus(x):
    return jnp.maximum(x, 0.0) + jnp.log(1.0 + jnp.exp(-jnp.abs(x)))


def _gelu(x):
    return 0.5 * x * (1.0 + lax.erf(x * 0.7071067811865476))


def _pick(n, cands):
    for c in cands:
        if n % c == 0:
            return c
    return n


def _matmul(a, b, mode, out_dtype, name, alpha=1.0, tm=None, tn=None, tk=None):
    if mode == "nn":
        (m, k), (_, n) = a.shape, b.shape
    elif mode == "nt":
        (m, k), (n, _) = a.shape, b.shape
    else:
        (k, m), (_, n) = a.shape, b.shape
    tm = tm or _pick(m, (512, 384, 256, 128))
    tn = tn or _pick(n, (1024, 768, 512, 384, 256, 128))
    tk = tk or _pick(k, (1024, 512, 256, 128))
    nk = k // tk
    if mode == "nn":
        a_spec = pl.BlockSpec((tm, tk), lambda i, j, kk: (i, kk))
        b_spec = pl.BlockSpec((tk, tn), lambda i, j, kk: (kk, j))
        dot = _nn
    elif mode == "nt":
        a_spec = pl.BlockSpec((tm, tk), lambda i, j, kk: (i, kk))
        b_spec = pl.BlockSpec((tn, tk), lambda i, j, kk: (j, kk))
        dot = _nt
    else:
        a_spec = pl.BlockSpec((tk, tm), lambda i, j, kk: (kk, i))
        b_spec = pl.BlockSpec((tk, tn), lambda i, j, kk: (kk, j))
        dot = _tn

    def body(a_ref, b_ref, o_ref, acc_ref):
        kk = pl.program_id(2)

        @pl.when(kk == 0)
        def _():
            acc_ref[...] = jnp.zeros_like(acc_ref)

        acc_ref[...] += dot(a_ref[...], b_ref[...])

        @pl.when(kk == nk - 1)
        def _():
            o_ref[...] = (alpha * acc_ref[...]).astype(out_dtype)

    return pl.pallas_call(
        body,
        name=name,
        out_shape=jax.ShapeDtypeStruct((m, n), out_dtype),
        grid=(m // tm, n // tn, nk),
        in_specs=[a_spec, b_spec],
        out_specs=pl.BlockSpec((tm, tn), lambda i, j, kk: (i, j)),
        scratch_shapes=[pltpu.VMEM((tm, tn), F32)],
        compiler_params=_cparams(("parallel", "parallel", "arbitrary")),
    )(a, b)


def _rms_stats(x):
    r = lax.rsqrt(jnp.mean(x * x, axis=-1, keepdims=True) + EPS)
    return x * r, r


def _rms_bwd(xhat, r, gain, dy):
    dxhat = dy * gain
    return r * (dxhat - xhat * jnp.mean(dxhat * xhat, axis=-1, keepdims=True))


def _acc_rows(ref, val, first):
    s = jnp.sum(val, axis=0, keepdims=True)

    @pl.when(first)
    def _():
        ref[...] = s

    @pl.when(jnp.logical_not(first))
    def _():
        ref[...] += s


def _rmsnorm_fwd(x, gain, name, tm=512):
    s, d = x.shape

    def body(x_ref, g_ref, h_ref):
        xhat, _ = _rms_stats(x_ref[...])
        h_ref[...] = (xhat * g_ref[...]).astype(BF16)

    return pl.pallas_call(
        body,
        name=name,
        out_shape=jax.ShapeDtypeStruct((s, d), BF16),
        grid=(s // tm,),
        in_specs=[pl.BlockSpec((tm, d), lambda i: (i, 0)), pl.BlockSpec((1, d), lambda i: (0, 0))],
        out_specs=pl.BlockSpec((tm, d), lambda i: (i, 0)),
        compiler_params=_cparams(("parallel",)),
    )(x, gain)


def _rmsnorm_bwd(x, gain, dh, dres, name, tm=512):
    s, d = x.shape

    def body(x_ref, g_ref, dh_ref, dres_ref, dx_ref, dg_ref):
        xhat, r = _rms_stats(x_ref[...])
        dh = dh_ref[...].astype(F32)
        dx_ref[...] = dres_ref[...] + _rms_bwd(xhat, r, g_ref[...], dh)
        _acc_rows(dg_ref, dh * xhat, pl.program_id(0) == 0)

    return pl.pallas_call(
        body,
        name=name,
        out_shape=(jax.ShapeDtypeStruct((s, d), F32), jax.ShapeDtypeStruct((1, d), F32)),
        grid=(s // tm,),
        in_specs=[
            pl.BlockSpec((tm, d), lambda i: (i, 0)),
            pl.BlockSpec((1, d), lambda i: (0, 0)),
            pl.BlockSpec((tm, d), lambda i: (i, 0)),
            pl.BlockSpec((tm, d), lambda i: (i, 0)),
        ],
        out_specs=(pl.BlockSpec((tm, d), lambda i: (i, 0)), pl.BlockSpec((1, d), lambda i: (0, 0))),
        compiler_params=_cparams(("arbitrary",)),
    )(x, gain, dh, dres)


FF_BLOCK = 2 * D_FF // N_DEV
FF_BLOCKS = D_FF // FF_BLOCK
FF_ROWS = D_FF // N_DEV


def _ffn_weight_specs(layer):
    return [
        pl.BlockSpec((None, None, D_MODEL, FF_BLOCK), lambda i, j: (j, layer, 0, 0)),
        pl.BlockSpec((None, None, D_MODEL, FF_BLOCK), lambda i, j: (j + FF_BLOCKS, layer, 0, 0)),
        pl.BlockSpec((2, None, FF_ROWS, D_MODEL), lambda i, j: (j, layer, 0, 0)),
    ]


def _ffn_fwd(x, gain, w_in, w_out, layer, name, tm=512):
    s, d = x.shape

    def body(x_ref, gain_ref, wg_ref, wu_ref, wo_ref, y_ref, gu_ref, h_scr, acc_scr):
        j = pl.program_id(1)

        @pl.when(j == 0)
        def _():
            xhat, _ = _rms_stats(x_ref[...])
            h_scr[...] = (xhat * gain_ref[...]).astype(BF16)
            acc_scr[...] = jnp.zeros_like(acc_scr)

        h = h_scr[...]
        g = _nn(h, wg_ref[...])
        u = _nn(h, wu_ref[...])
        gu_ref[0] = g.astype(BF16)
        gu_ref[1] = u.astype(BF16)
        acc_scr[...] += _nn(_silu(g) * u, wo_ref[...].reshape(FF_BLOCK, d))

        @pl.when(j == FF_BLOCKS - 1)
        def _():
            y_ref[...] = x_ref[...] + FFN_RESID * acc_scr[...]

    return pl.pallas_call(
        body,
        name=name,
        out_shape=(
            jax.ShapeDtypeStruct((s, d), F32),
            jax.ShapeDtypeStruct((2, FF_BLOCKS, s, FF_BLOCK), BF16),
        ),
        grid=(s // tm, FF_BLOCKS),
        in_specs=[
            pl.BlockSpec((tm, d), lambda i, j: (i, 0)),
            pl.BlockSpec((1, d), lambda i, j: (0, 0)),
        ] + _ffn_weight_specs(layer),
        out_specs=(
            pl.BlockSpec((tm, d), lambda i, j: (i, 0)),
            pl.BlockSpec((2, None, tm, FF_BLOCK), lambda i, j: (0, j, i, 0)),
        ),
        scratch_shapes=[pltpu.VMEM((tm, d), BF16), pltpu.VMEM((tm, d), F32)],
        compiler_params=_cparams(("parallel", "arbitrary")),
    )(x, gain, w_in, w_in, w_out)


def _ffn_bwd(x, gain, w_in, w_out, layer, gu, dy, name, tm=512, tk=1024):
    s, d = x.shape
    tk = min(tk, s)

    def body(x_ref, gain_ref, wg_ref, wu_ref, wo_ref, gu_ref, dy_ref,
             dx_ref, dgain_ref, h_ref, a_ref, dgu_ref, dyb_scr, acc_scr):
        i = pl.program_id(0)
        j = pl.program_id(1)

        @pl.when(j == 0)
        def _():
            xhat, _ = _rms_stats(x_ref[...])
            h_ref[...] = (xhat * gain_ref[...]).astype(BF16)
            dyb_scr[...] = (FFN_RESID * dy_ref[...]).astype(BF16)
            acc_scr[...] = jnp.zeros_like(acc_scr)

        da = _nt(dyb_scr[...], wo_ref[...].reshape(FF_BLOCK, d))
        gv = gu_ref[0].astype(F32)
        uv = gu_ref[1].astype(F32)
        sg = _sigmoid(gv)
        sl = gv * sg
        a_ref[...] = (sl * uv).astype(BF16)
        du = (da * sl).astype(BF16)
        dg = (da * uv * (sg * (1.0 + gv * (1.0 - sg)))).astype(BF16)
        dgu_ref[0] = dg
        dgu_ref[1] = du
        acc_scr[...] += _nt(dg, wg_ref[...]) + _nt(du, wu_ref[...])

        @pl.when(j == FF_BLOCKS - 1)
        def _():
            xhat, r = _rms_stats(x_ref[...])
            dh = acc_scr[...]
            dx_ref[...] = dy_ref[...] + _rms_bwd(xhat, r, gain_ref[...], dh)
            _acc_rows(dgain_ref, dh * xhat, i == 0)

    gu_spec = pl.BlockSpec((2, None, tm, FF_BLOCK), lambda i, j: (0, j, i, 0))
    dx, dgain, h, a, dgu = pl.pallas_call(
        body,
        name=name,
        out_shape=(
            jax.ShapeDtypeStruct((s, d), F32),
            jax.ShapeDtypeStruct((1, d), F32),
            jax.ShapeDtypeStruct((s, d), BF16),
            jax.ShapeDtypeStruct((FF_BLOCKS, s, FF_BLOCK), BF16),
            jax.ShapeDtypeStruct((2, FF_BLOCKS, s, FF_BLOCK), BF16),
        ),
        grid=(s // tm, FF_BLOCKS),
        in_specs=[
            pl.BlockSpec((tm, d), lambda i, j: (i, 0)),
            pl.BlockSpec((1, d), lambda i, j: (0, 0)),
        ] + _ffn_weight_specs(layer) + [gu_spec, pl.BlockSpec((tm, d), lambda i, j: (i, 0))],
        out_specs=(
            pl.BlockSpec((tm, d), lambda i, j: (i, 0)),
            pl.BlockSpec((1, d), lambda i, j: (0, 0)),
            pl.BlockSpec((tm, d), lambda i, j: (i, 0)),
            pl.BlockSpec((None, tm, FF_BLOCK), lambda i, j: (j, i, 0)),
            gu_spec,
        ),
        scratch_shapes=[pltpu.VMEM((tm, d), BF16), pltpu.VMEM((tm, d), F32)],
        compiler_params=_cparams(("arbitrary", "arbitrary")),
    )(x, gain, w_in, w_in, w_out, gu, dy)
    nk = s // tk

    def acc_matmul(first, last, acc_ref, o_ref, val, alpha):
        @pl.when(first)
        def _():
            acc_ref[...] = jnp.zeros_like(acc_ref)

        acc_ref[...] += val

        @pl.when(last)
        def _():
            o_ref[...] = (alpha * acc_ref[...]).astype(BF16)

    def dwin_body(h_ref, dgu_ref, o_ref, acc_ref):
        kk = pl.program_id(2)
        acc_matmul(kk == 0, kk == nk - 1, acc_ref, o_ref, _tn(h_ref[...], dgu_ref[...]), 1.0)

    tmw = 512
    dw_in = pl.pallas_call(
        dwin_body,
        name=name + "_dwin",
        out_shape=jax.ShapeDtypeStruct((N_DEV, d, FF_BLOCK), BF16),
        grid=(N_DEV, d // tmw, nk),
        in_specs=[
            pl.BlockSpec((tk, tmw), lambda n, i, kk: (kk, i)),
            pl.BlockSpec((None, tk, FF_BLOCK), lambda n, i, kk: (n, kk, 0)),
        ],
        out_specs=pl.BlockSpec((None, tmw, FF_BLOCK), lambda n, i, kk: (n, i, 0)),
        scratch_shapes=[pltpu.VMEM((tmw, FF_BLOCK), F32)],
        compiler_params=_cparams(("parallel", "parallel", "arbitrary")),
    )(h, dgu.reshape(N_DEV, s, FF_BLOCK))

    def dwout_body(a_ref, dy_ref, o_ref, acc_ref):
        kk = pl.program_id(1)
        acc_matmul(kk == 0, kk == nk - 1, acc_ref, o_ref, _tn(a_ref[...], dy_ref[...]), FFN_RESID)

    dw_out = pl.pallas_call(
        dwout_body,
        name=name + "_dwout",
        out_shape=jax.ShapeDtypeStruct((FF_BLOCKS, FF_BLOCK, d), BF16),
        grid=(FF_BLOCKS, nk),
        in_specs=[
            pl.BlockSpec((None, tk, FF_BLOCK), lambda j, kk: (j, kk, 0)),
            pl.BlockSpec((tk, d), lambda j, kk: (kk, 0)),
        ],
        out_specs=pl.BlockSpec((None, FF_BLOCK, d), lambda j, kk: (j, 0, 0)),
        scratch_shapes=[pltpu.VMEM((FF_BLOCK, d), F32)],
        compiler_params=_cparams(("parallel", "arbitrary")),
    )(a, dy)
    return dx, dgain, dw_in, dw_out.reshape(N_DEV, FF_ROWS, d)


def _acc(ref, val, first):
    @pl.when(first)
    def _():
        ref[...] = val

    @pl.when(jnp.logical_not(first))
    def _():
        ref[...] += val


def _full(shape):
    nd = len(shape)
    return pl.BlockSpec(shape, lambda *_: (0,) * nd)


def _iota(shape, dim):
    return lax.broadcasted_iota(jnp.int32, shape, dim)


def _gmlp_chunk(u, v, gain, w_s, b_s):
    va = [_gelu(t) for t in v]
    ms = sum(jnp.sum(t * t, axis=-1, keepdims=True) for t in va) * (1.0 / GM_WIDTH)
    r = lax.rsqrt(ms + EPS)
    tri = _iota((GM_CHUNK, GM_CHUNK), 0) >= _iota((GM_CHUNK, GM_CHUNK), 1)
    out = []
    for g in range(GM_GROUPS):
        vn = va[g] * r * gain[g]
        sp = _nn(jnp.where(tri, w_s[g], 0.0), vn) + b_s[g]
        out.append(_gelu(u[g]) * sp)
    return out


def _gmlp_load(uv_ref, c):
    rows = pl.ds(c * GM_CHUNK, GM_CHUNK)
    u = [uv_ref[rows, pl.ds(g * LANES, LANES)] for g in range(GM_GROUPS)]
    v = [uv_ref[rows, pl.ds(GM_WIDTH + g * LANES, LANES)] for g in range(GM_GROUPS)]
    return u, v


def _gmlp_params(gain_ref, ws_ref, bs_ref):
    gain = [gain_ref[:, pl.ds(g * LANES, LANES)] for g in range(GM_GROUPS)]
    w_s = [ws_ref[g] for g in range(GM_GROUPS)]
    b_s = [bs_ref[g] for g in range(GM_GROUPS)]
    return gain, w_s, b_s


def _gmlp_fwd(proj, gain, w_s, b_s, name, tm=512):
    s = proj.shape[0]

    def body(uv_ref, gain_ref, ws_ref, bs_ref, y_ref):
        params = _gmlp_params(gain_ref, ws_ref, bs_ref)
        for c in range(tm // GM_CHUNK):
            u, v = _gmlp_load(uv_ref, c)
            y = _gmlp_chunk(u, v, *params)
            for g in range(GM_GROUPS):
                y_ref[pl.ds(c * GM_CHUNK, GM_CHUNK), pl.ds(g * LANES, LANES)] = y[g]

    return pl.pallas_call(
        body,
        name=name,
        out_shape=jax.ShapeDtypeStruct((s, GM_WIDTH), F32),
        grid=(s // tm,),
        in_specs=[
            pl.BlockSpec((tm, 2 * GM_WIDTH), lambda i: (i, P_UV // (2 * GM_WIDTH))),
            _full((1, GM_WIDTH)),
            _full((GM_GROUPS, GM_CHUNK, GM_CHUNK)),
            _full((GM_GROUPS, GM_CHUNK, 1)),
        ],
        out_specs=pl.BlockSpec((tm, GM_WIDTH), lambda i: (i, 0)),
        compiler_params=_cparams(("parallel",)),
    )(proj, gain, w_s, b_s)


def _gmlp_bwd(proj, gain, w_s, b_s, dy, name, tm=512):
    s = proj.shape[0]

    def body(uv_ref, gain_ref, ws_ref, bs_ref, dy_ref, duv_ref, dgain_ref, dws_ref, dbs_ref):
        params = _gmlp_params(gain_ref, ws_ref, bs_ref)
        dgain = dws = dbs = None
        for c in range(tm // GM_CHUNK):
            rows = pl.ds(c * GM_CHUNK, GM_CHUNK)
            u, v = _gmlp_load(uv_ref, c)
            _, vjp = jax.vjp(_gmlp_chunk, u, v, *params)
            du, dv, dg, dw, db = vjp([dy_ref[rows, pl.ds(g * LANES, LANES)] for g in range(GM_GROUPS)])
            for g in range(GM_GROUPS):
                duv_ref[rows, pl.ds(g * LANES, LANES)] = du[g].astype(BF16)
                duv_ref[rows, pl.ds(GM_WIDTH + g * LANES, LANES)] = dv[g].astype(BF16)
            if c == 0:
                dgain, dws, dbs = dg, dw, db
            else:
                dgain = [p + q for p, q in zip(dgain, dg)]
                dws = [p + q for p, q in zip(dws, dw)]
                dbs = [p + q for p, q in zip(dbs, db)]
        first = pl.program_id(0) == 0
        for g in range(GM_GROUPS):
            _acc(dgain_ref.at[:, pl.ds(g * LANES, LANES)], dgain[g], first)
            _acc(dws_ref.at[g], dws[g], first)
            _acc(dbs_ref.at[g], dbs[g], first)

    return pl.pallas_call(
        body,
        name=name,
        out_shape=(
            jax.ShapeDtypeStruct((s, 2 * GM_WIDTH), BF16),
            jax.ShapeDtypeStruct((1, GM_WIDTH), F32),
            jax.ShapeDtypeStruct((GM_GROUPS, GM_CHUNK, GM_CHUNK), F32),
            jax.ShapeDtypeStruct((GM_GROUPS, GM_CHUNK, 1), F32),
        ),
        grid=(s // tm,),
        in_specs=[
            pl.BlockSpec((tm, 2 * GM_WIDTH), lambda i: (i, P_UV // (2 * GM_WIDTH))),
            _full((1, GM_WIDTH)),
            _full((GM_GROUPS, GM_CHUNK, GM_CHUNK)),
            _full((GM_GROUPS, GM_CHUNK, 1)),
            pl.BlockSpec((tm, GM_WIDTH), lambda i: (i, 0)),
        ],
        out_specs=(
            pl.BlockSpec((tm, 2 * GM_WIDTH), lambda i: (i, 0)),
            _full((1, GM_WIDTH)),
            _full((GM_GROUPS, GM_CHUNK, GM_CHUNK)),
            _full((GM_GROUPS, GM_CHUNK, 1)),
        ),
        compiler_params=_cparams(("arbitrary",)),
    )(proj, gain, w_s, b_s, dy)


HEAD_BLOCK = MLA_HEADS * LANES


def _mla_pre(rope, cq, ckv, kr, qn_g, kvn_g, wq, wk, wv, qg, kg):
    cosf, sinf, rot = rope
    xq, _ = _rms_stats(cq)
    qn = xq * qn_g
    xk, _ = _rms_stats(ckv)
    kvn = xk * kvn_g

    def head_norm(t, gain):
        r = lax.rsqrt(jnp.sum(t * t, axis=-1, keepdims=True) * (1.0 / MLA_QK_DIM) + EPS)
        th = t * r * gain
        return th * cosf + _nn(th, rot) * sinf

    q = [head_norm(_nn(qn, wq[h]), qg) for h in range(MLA_HEADS)]
    k = [head_norm(_nn(kvn, wk[h]) + kr, kg) for h in range(MLA_HEADS)]
    v = [_nn(kvn, wv[h]) for h in range(MLA_HEADS)]
    return q, k, v


def _mla_pre_specs(tm):
    row = lambda w, off: pl.BlockSpec((tm, w), lambda i: (i, off // w))
    return [
        row(MLA_Q_RANK, P_CQ),
        row(MLA_KV_RANK, P_CKV),
        row(LANES, P_KR),
        pl.BlockSpec((tm, LANES), lambda i: (i, 0)),
        pl.BlockSpec((tm, LANES), lambda i: (i, 0)),
        _full((LANES, LANES)),
        _full((1, MLA_Q_RANK)),
        _full((1, MLA_KV_RANK)),
        _full((MLA_HEADS, MLA_Q_RANK, LANES)),
        _full((MLA_HEADS, MLA_KV_RANK, LANES)),
        _full((MLA_HEADS, MLA_KV_RANK, LANES)),
        _full((1, LANES)),
        _full((1, LANES)),
    ]


def _mla_pre_args(cq_ref, ckv_ref, kr_ref, cos_ref, sin_ref, rot_ref, qn_ref, kvn_ref, wq_ref, wk_ref, wv_ref,
                  qg_ref, kg_ref):
    heads = lambda ref: [ref[h].astype(F32) for h in range(MLA_HEADS)]
    rope = (cos_ref[...], sin_ref[...], rot_ref[...])
    args = (cq_ref[...], ckv_ref[...], kr_ref[...], qn_ref[...], kvn_ref[...], heads(wq_ref), heads(wk_ref),
            heads(wv_ref), qg_ref[...], kg_ref[...])
    return rope, args


def _mla_pre_fwd(proj, cosf, sinf, rot, qn_g, kvn_g, wq, wk, wv, qg, kg, name, tm=256):
    s = proj.shape[0]

    def body(*refs):
        q_ref, k_ref, v_ref = refs[13:]
        rope, args = _mla_pre_args(*refs[:13])
        q, k, v = _mla_pre(rope, *args)
        for h in range(MLA_HEADS):
            cols = pl.ds(h * LANES, LANES)
            q_ref[:, cols] = q[h].astype(BF16)
            k_ref[:, cols] = k[h].astype(BF16)
            v_ref[:, cols] = v[h].astype(BF16)

    out = jax.ShapeDtypeStruct((s, HEAD_BLOCK), BF16)
    blk = pl.BlockSpec((tm, HEAD_BLOCK), lambda i: (i, 0))
    return pl.pallas_call(
        body,
        name=name,
        out_shape=(out, out, out),
        grid=(s // tm,),
        in_specs=_mla_pre_specs(tm),
        out_specs=(blk, blk, blk),
        compiler_params=_cparams(("parallel",)),
    )(proj, proj, proj, cosf, sinf, rot, qn_g, kvn_g, wq, wk, wv, qg, kg)


def _mla_pre_bwd(proj, cosf, sinf, rot, qn_g, kvn_g, wq, wk, wv, qg, kg, dq, dk, dv, name, tm=256):
    s = proj.shape[0]

    def body(*refs):
        dq_ref, dk_ref, dv_ref = refs[13:16]
        dcq_ref, dckv_ref, dkr_ref, dqn_ref, dkvn_ref, dwq_ref, dwk_ref, dwv_ref, dqg_ref, dkg_ref = refs[16:]
        rope, args = _mla_pre_args(*refs[:13])
        _, vjp = jax.vjp(functools.partial(_mla_pre, rope), *args)
        heads = lambda ref: [ref[:, pl.ds(h * LANES, LANES)] for h in range(MLA_HEADS)]
        dcq, dckv, dkr, dqn, dkvn, dwq, dwk, dwv, dqg, dkg = vjp((heads(dq_ref), heads(dk_ref), heads(dv_ref)))
        dcq_ref[...] = dcq.astype(BF16)
        dckv_ref[...] = dckv.astype(BF16)
        dkr_ref[...] = dkr.astype(BF16)
        first = pl.program_id(0) == 0
        _acc(dqn_ref, dqn, first)
        _acc(dkvn_ref, dkvn, first)
        _acc(dqg_ref, dqg, first)
        _acc(dkg_ref, dkg, first)
        for h in range(MLA_HEADS):
            _acc(dwq_ref.at[h], dwq[h], first)
            _acc(dwk_ref.at[h], dwk[h], first)
            _acc(dwv_ref.at[h], dwv[h], first)

    hb = pl.BlockSpec((tm, HEAD_BLOCK), lambda i: (i, 0))
    row = lambda w: pl.BlockSpec((tm, w), lambda i: (i, 0))
    sds = jax.ShapeDtypeStruct
    return pl.pallas_call(
        body,
        name=name,
        out_shape=(
            sds((s, MLA_Q_RANK), BF16), sds((s, MLA_KV_RANK), BF16), sds((s, LANES), BF16),
            sds((1, MLA_Q_RANK), F32), sds((1, MLA_KV_RANK), F32),
            sds((MLA_HEADS, MLA_Q_RANK, LANES), F32), sds((MLA_HEADS, MLA_KV_RANK, LANES), F32),
            sds((MLA_HEADS, MLA_KV_RANK, LANES), F32),
            sds((1, LANES), F32), sds((1, LANES), F32),
        ),
        grid=(s // tm,),
        in_specs=_mla_pre_specs(tm) + [hb, hb, hb],
        out_specs=(
            row(MLA_Q_RANK), row(MLA_KV_RANK), row(LANES),
            _full((1, MLA_Q_RANK)), _full((1, MLA_KV_RANK)),
            _full((MLA_HEADS, MLA_Q_RANK, LANES)), _full((MLA_HEADS, MLA_KV_RANK, LANES)),
            _full((MLA_HEADS, MLA_KV_RANK, LANES)),
            _full((1, LANES)), _full((1, LANES)),
        ),
        compiler_params=_cparams(("arbitrary",)),
    )(proj, proj, proj, cosf, sinf, rot, qn_g, kvn_g, wq, wk, wv, qg, kg, dq, dk, dv)


ATT_SCALE = MLA_QK_DIM ** -0.5
NEG_BIG = -1e30


def _att_scores(q, k, diagonal):
    s = _nt(q, k) * ATT_SCALE
    if diagonal:
        s = jnp.where(_iota(s.shape, 0) >= _iota(s.shape, 1), s, NEG_BIG)
    return s


def _att_scores_t(k, q, diagonal):
    s = _nt(k, q) * ATT_SCALE
    if diagonal:
        s = jnp.where(_iota(s.shape, 0) <= _iota(s.shape, 1), s, NEG_BIG)
    return s


SUBLANES = 8


def _as_row(col_lanes):
    return jnp.transpose(col_lanes)[0:SUBLANES, :]


def _key_loop(lo, hi, t, step):
    def body(i, carry):
        step(pl.ds(pl.multiple_of(i * t, t), t))
        return carry

    lax.fori_loop(lo, hi, body, 0)


def _attention_fwd(q, k, v, name, t=512, gather=()):
    s = q.shape[0]
    n = s // t
    ng = len(gather)

    def body(q_ref, k_ref, v_ref, *rest):
        shard_refs, rest = rest[:ng], rest[ng:]
        o_ref, lse_ref, lse_t_ref = rest[:3]
        out_refs, rest = rest[3:3 + ng], rest[3 + ng:]
        m_scr, acc_scr = rest[:2]
        qi = pl.program_id(1)
        if ng:
            @pl.when(jnp.logical_and(pl.program_id(0) == 0, qi == 0))
            def _():
                _gather_start(_gather_plan(shard_refs, out_refs, *rest[2:]))

        lane = _iota((1, LANES), 1)
        m_scr[...] = jnp.full_like(m_scr, NEG_BIG)
        acc_scr[...] = jnp.zeros_like(acc_scr)

        def step(rows, diagonal=False):
            sc = _att_scores(q_ref[...], k_ref[rows, :], diagonal)
            m_old = m_scr[...]
            m_new = jnp.maximum(m_old, jnp.max(sc, axis=-1, keepdims=True))
            p = jnp.exp(sc - m_new)
            vb = jnp.where(lane == MLA_V, 1.0, v_ref[rows, :].astype(F32))
            acc_scr[...] = jnp.exp(m_old - m_new) * acc_scr[...] + _nn(p, vb)
            m_scr[...] = m_new

        _key_loop(0, qi, t, step)
        step(pl.ds(pl.multiple_of(qi * t, t), t), diagonal=True)
        acc = acc_scr[...]
        l = jnp.sum(jnp.where(lane == MLA_V, acc, 0.0), axis=-1, keepdims=True)
        o_ref[...] = jnp.where(lane < MLA_V, acc / l, 0.0)
        lse = jnp.broadcast_to(m_scr[...] + jnp.log(l), (t, LANES))
        lse_ref[...] = lse
        lse_t_ref[...] = _as_row(lse)
        if ng:
            @pl.when(jnp.logical_and(pl.program_id(0) == MLA_HEADS - 1, qi == n - 1))
            def _():
                _gather_finish(_gather_plan(shard_refs, out_refs, *rest[2:]))

    qspec = pl.BlockSpec((t, LANES), lambda h, qi: (qi, h))
    kspec = pl.BlockSpec((s, LANES), lambda h, qi: (0, h))
    out = jax.ShapeDtypeStruct((s, HEAD_BLOCK), F32)
    res = pl.pallas_call(
        body,
        name=name,
        out_shape=(out, out, jax.ShapeDtypeStruct((MLA_HEADS * SUBLANES, s), F32)) + _gathered_shapes(gather),
        grid=(MLA_HEADS, n),
        in_specs=[qspec, kspec, kspec] + [ANY] * ng,
        out_specs=(qspec, qspec, pl.BlockSpec((SUBLANES, t), lambda h, qi: (h, qi))) + (ANY,) * ng,
        scratch_shapes=[pltpu.VMEM((t, 1), F32), pltpu.VMEM((t, LANES), F32)] + (_gather_scratch(ng) if ng else []),
        compiler_params=_cparams(("arbitrary", "arbitrary") if ng else ("parallel", "parallel")),
    )(q, k, v, *gather)
    return res[0], res[1], res[2], list(res[3:])


def _attention_bwd(q, k, v, o, lse, lse_t, do, name, t=512):
    s = q.shape[0]
    n = s // t

    def dq_body(q_ref, k_ref, v_ref, o_ref, lse_ref, do_ref, dq_ref, delta_t_ref, acc_scr):
        qi = pl.program_id(1)
        do = do_ref[...]
        delta = jnp.sum(do * o_ref[...], axis=-1, keepdims=True)
        delta_t_ref[...] = _as_row(jnp.broadcast_to(delta, (t, LANES)))
        acc_scr[...] = jnp.zeros_like(acc_scr)

        def step(rows, diagonal=False):
            kb = k_ref[rows, :]
            p = jnp.exp(_att_scores(q_ref[...], kb, diagonal) - lse_ref[:, 0:1])
            ds = p * (_nt(do, v_ref[rows, :]) - delta) * ATT_SCALE
            acc_scr[...] += _nn(ds, kb)

        _key_loop(0, qi, t, step)
        step(pl.ds(pl.multiple_of(qi * t, t), t), diagonal=True)
        dq_ref[...] = acc_scr[...]

    def dkv_body(q_ref, k_ref, v_ref, lse_t_ref, delta_t_ref, do_ref, dk_ref, dv_ref, dk_scr, dv_scr):
        ki = pl.program_id(1)
        dk_scr[...] = jnp.zeros_like(dk_scr)
        dv_scr[...] = jnp.zeros_like(dv_scr)

        def step(rows, diagonal=False):
            qb = q_ref[rows, :]
            dob = do_ref[rows, :]
            p = jnp.exp(_att_scores_t(k_ref[...], qb, diagonal) - lse_t_ref[0:1, rows])
            dv_scr[...] += _nn(p, dob)
            ds = p * (_nt(v_ref[...], dob) - delta_t_ref[0:1, rows]) * ATT_SCALE
            dk_scr[...] += _nn(ds, qb)

        step(pl.ds(pl.multiple_of(ki * t, t), t), diagonal=True)
        _key_loop(ki + 1, n, t, step)
        dk_ref[...] = dk_scr[...]
        dv_ref[...] = dv_scr[...]

    out = jax.ShapeDtypeStruct((s, HEAD_BLOCK), F32)
    blk = pl.BlockSpec((t, LANES), lambda h, i: (i, h))
    head = pl.BlockSpec((s, LANES), lambda h, i: (0, h))
    row_blk = pl.BlockSpec((SUBLANES, t), lambda h, i: (h, i))
    row_head = pl.BlockSpec((SUBLANES, s), lambda h, i: (h, 0))
    dq, delta_t = pl.pallas_call(
        dq_body,
        name=name + "_dq",
        out_shape=(out, jax.ShapeDtypeStruct((MLA_HEADS * SUBLANES, s), F32)),
        grid=(MLA_HEADS, n),
        in_specs=[blk, head, head, blk, blk, blk],
        out_specs=(blk, row_blk),
        scratch_shapes=[pltpu.VMEM((t, LANES), F32)],
        compiler_params=_cparams(("parallel", "parallel")),
    )(q, k, v, o, lse, do)
    dk, dv = pl.pallas_call(
        dkv_body,
        name=name + "_dkv",
        out_shape=(out, out),
        grid=(MLA_HEADS, n),
        in_specs=[head, blk, blk, row_head, row_head, head],
        out_specs=(blk, blk),
        scratch_shapes=[pltpu.VMEM((t, LANES), F32), pltpu.VMEM((t, LANES), F32)],
        compiler_params=_cparams(("parallel", "parallel")),
    )(q, k, v, lse_t, delta_t, do)
    return dq, dk, dv


HALO = 8


def _conv_fwd(proj, w, b, name, tm=512):
    s = proj.shape[0]
    cb = P_XBC // SSD_CONV_DIM

    def body(x_ref, halo_ref, w_ref, b_ref, y_ref, cat_scr):
        i = pl.program_id(0)
        cat_scr[pl.ds(0, HALO), :] = jnp.where(i > 0, halo_ref[...], 0.0)
        cat_scr[pl.ds(HALO, tm), :] = x_ref[...]
        pre = b_ref[...]
        for j in range(SSD_CONV):
            pre = pre + w_ref[pl.ds(SSD_CONV - 1 - j, 1), :] * cat_scr[pl.ds(HALO - j, tm), :]
        y_ref[...] = _silu(pre)

    return pl.pallas_call(
        body,
        name=name,
        out_shape=jax.ShapeDtypeStruct((s, SSD_CONV_DIM), F32),
        grid=(s // tm,),
        in_specs=[
            pl.BlockSpec((tm, SSD_CONV_DIM), lambda i: (i, cb)),
            pl.BlockSpec((HALO, SSD_CONV_DIM), lambda i: (jnp.maximum(i * (tm // HALO) - 1, 0), cb)),
            _full((SSD_CONV, SSD_CONV_DIM)),
            _full((1, SSD_CONV_DIM)),
        ],
        out_specs=pl.BlockSpec((tm, SSD_CONV_DIM), lambda i: (i, 0)),
        scratch_shapes=[pltpu.VMEM((tm + HALO, SSD_CONV_DIM), F32)],
        compiler_params=_cparams(("parallel",)),
    )(proj, proj, w, b)


def _conv_bwd(proj, w, b, dact, name, tm=512):
    s = proj.shape[0]
    cb = P_XBC // SSD_CONV_DIM
    n = s // tm

    def pre_body(x_ref, halo_ref, w_ref, b_ref, dact_ref, dpre_ref, dw_ref, db_ref, cat_scr):
        i = pl.program_id(0)
        cat_scr[pl.ds(0, HALO), :] = jnp.where(i > 0, halo_ref[...], 0.0)
        cat_scr[pl.ds(HALO, tm), :] = x_ref[...]
        pre = b_ref[...]
        for j in range(SSD_CONV):
            pre = pre + w_ref[pl.ds(SSD_CONV - 1 - j, 1), :] * cat_scr[pl.ds(HALO - j, tm), :]
        sg = _sigmoid(pre)
        dpre = dact_ref[...] * (sg * (1.0 + pre * (1.0 - sg)))
        dpre_ref[...] = dpre
        first = i == 0
        _acc_rows(db_ref, dpre, first)
        for j in range(SSD_CONV):
            _acc_rows(dw_ref.at[pl.ds(SSD_CONV - 1 - j, 1), :], dpre * cat_scr[pl.ds(HALO - j, tm), :], first)

    dpre, dw, db = pl.pallas_call(
        pre_body,
        name=name + "_pre",
        out_shape=(
            jax.ShapeDtypeStruct((s, SSD_CONV_DIM), F32),
            jax.ShapeDtypeStruct((SSD_CONV, SSD_CONV_DIM), F32),
            jax.ShapeDtypeStruct((1, SSD_CONV_DIM), F32),
        ),
        grid=(n,),
        in_specs=[
            pl.BlockSpec((tm, SSD_CONV_DIM), lambda i: (i, cb)),
            pl.BlockSpec((HALO, SSD_CONV_DIM), lambda i: (jnp.maximum(i * (tm // HALO) - 1, 0), cb)),
            _full((SSD_CONV, SSD_CONV_DIM)),
            _full((1, SSD_CONV_DIM)),
            pl.BlockSpec((tm, SSD_CONV_DIM), lambda i: (i, 0)),
        ],
        out_specs=(
            pl.BlockSpec((tm, SSD_CONV_DIM), lambda i: (i, 0)),
            _full((SSD_CONV, SSD_CONV_DIM)),
            _full((1, SSD_CONV_DIM)),
        ),
        scratch_shapes=[pltpu.VMEM((tm + HALO, SSD_CONV_DIM), F32)],
        compiler_params=_cparams(("arbitrary",)),
    )(proj, proj, w, b, dact)

    def dx_body(d_ref, halo_ref, w_ref, dx_ref, cat_scr):
        i = pl.program_id(0)
        cat_scr[pl.ds(0, tm), :] = d_ref[...]
        cat_scr[pl.ds(tm, HALO), :] = jnp.where(i < n - 1, halo_ref[...], 0.0)
        dx = jnp.zeros((tm, SSD_CONV_DIM), F32)
        for j in range(SSD_CONV):
            dx = dx + w_ref[pl.ds(SSD_CONV - 1 - j, 1), :] * cat_scr[pl.ds(j, tm), :]
        dx_ref[...] = dx.astype(BF16)

    dx = pl.pallas_call(
        dx_body,
        name=name + "_dx",
        out_shape=jax.ShapeDtypeStruct((s, SSD_CONV_DIM), BF16),
        grid=(n,),
        in_specs=[
            pl.BlockSpec((tm, SSD_CONV_DIM), lambda i: (i, 0)),
            pl.BlockSpec((HALO, SSD_CONV_DIM), lambda i: (jnp.minimum((i + 1) * (tm // HALO), s // HALO - 1), 0)),
            _full((SSD_CONV, SSD_CONV_DIM)),
        ],
        out_specs=pl.BlockSpec((tm, SSD_CONV_DIM), lambda i: (i, 0)),
        scratch_shapes=[pltpu.VMEM((tm + HALO, SSD_CONV_DIM), F32)],
        compiler_params=_cparams(("parallel",)),
    )(dpre, dpre, w)
    return dx, dw, db


N_PAIR = SSD_HEADS // 2


def _ssd_chunk(xs, bm, cm, z, dtp, state, dtb, alog, dskip, ng):
    t = SSD_CHUNK
    lane = _iota((1, LANES), 1)
    row = _iota((LANES, 1), 0)
    dt_all = jnp.where(lane < SSD_HEADS, _softplus(dtp + dtb), 0.0)
    da = dt_all * (-jnp.exp(alog))
    causal = _iota((t, t), 0) >= _iota((t, t), 1)
    cs = _exact_nn(causal.astype(F32), da)
    cs_t = cs.T
    tot = jnp.sum(da, axis=0, keepdims=True)
    col = lambda m, h: jnp.sum(jnp.where(lane == h, m, 0.0), axis=1, keepdims=True)
    rowv = lambda m, h: jnp.sum(jnp.where(row == h, m, 0.0), axis=0, keepdims=True)
    low = lane < SSD_HEAD_DIM
    cb = [_nt(cm[g], bm[g]) for g in range(SSD_GROUPS)]
    gated, new_state = [], []
    for j in range(N_PAIR):
        g = j // (N_PAIR // SSD_GROUPS)
        h0, h1 = 2 * j, 2 * j + 1
        y = jnp.zeros((t, LANES), F32)
        for h, mask in ((h0, low), (h1, jnp.logical_not(low))):
            lmat = jnp.exp(jnp.where(causal, col(cs, h) - rowv(cs_t, h), NEG_BIG))
            y = y + _nn(cb[g] * lmat, jnp.where(mask, xs[j] * col(dt_all, h), 0.0))
        cs_p = jnp.where(low, col(cs, h0), col(cs, h1))
        dt_p = jnp.where(low, col(dt_all, h0), col(dt_all, h1))
        tot_p = jnp.where(low, col(tot, h0), col(tot, h1))
        tot_c = jnp.where(row < SSD_HEAD_DIM, col(tot, h0), col(tot, h1))
        d_p = jnp.where(low, col(dskip, h0), col(dskip, h1))
        xdt = xs[j] * dt_p
        y = y + _nt(cm[g], state[j]) * jnp.exp(cs_p) + xs[j] * d_p
        new_state.append(state[j] * jnp.exp(tot_c) + _tn(xdt * jnp.exp(tot_p - cs_p), bm[g]))
        gated.append(y * _silu(z[j]))
    out = []
    per_group = N_PAIR // SSD_GROUPS
    for g in range(SSD_GROUPS):
        blocks = gated[g * per_group:(g + 1) * per_group]
        ms = sum(jnp.sum(v * v, axis=-1, keepdims=True) for v in blocks) * (1.0 / (per_group * LANES))
        r = lax.rsqrt(ms + EPS)
        out += [v * r * ng[g * per_group + i] for i, v in enumerate(blocks)]
    return out, new_state


def _ssd_specs(rev, nc):
    idx = (lambda c: nc - 1 - c) if rev else (lambda c: c)
    t = SSD_CHUNK
    return [
        pl.BlockSpec((t, SSD_CONV_DIM), lambda c: (idx(c), 0)),
        pl.BlockSpec((t, SSD_INNER), lambda c: (idx(c), P_Z // SSD_INNER)),
        pl.BlockSpec((t, LANES), lambda c: (idx(c), P_DT // LANES)),
        _full((1, LANES)), _full((1, LANES)), _full((1, LANES)), _full((1, SSD_INNER)),
    ]


def _ssd_args(act_ref, z_ref, dt_ref, dtb_ref, alog_ref, dskip_ref, ng_ref):
    blk = lambda ref, off, n: [ref[:, pl.ds(off + i * LANES, LANES)] for i in range(n)]
    xs = blk(act_ref, 0, N_PAIR)
    bm = blk(act_ref, SSD_INNER, SSD_GROUPS)
    cm = blk(act_ref, SSD_INNER + SSD_GROUPS * SSD_STATE, SSD_GROUPS)
    return xs, bm, cm, blk(z_ref, 0, N_PAIR), dt_ref[...], dtb_ref[...], alog_ref[...], dskip_ref[...], blk(ng_ref, 0, N_PAIR)


def _ssd_fwd(act, proj, dtb, alog, dskip, ng, name):
    s = act.shape[0]
    nc = s // SSD_CHUNK

    def body(act_ref, z_ref, dt_ref, dtb_ref, alog_ref, dskip_ref, ng_ref, y_ref, st_ref, st_scr):
        @pl.when(pl.program_id(0) == 0)
        def _():
            st_scr[...] = jnp.zeros_like(st_scr)

        xs, bm, cm, z, dtp, dtb_v, alog_v, dskip_v, ng_v = _ssd_args(act_ref, z_ref, dt_ref, dtb_ref, alog_ref, dskip_ref, ng_ref)
        state = [st_scr[j] for j in range(N_PAIR)]
        st_ref[0] = st_scr[...]
        y, new_state = _ssd_chunk(xs, bm, cm, z, dtp, state, dtb_v, alog_v, dskip_v, ng_v)
        for j in range(N_PAIR):
            y_ref[:, pl.ds(j * LANES, LANES)] = y[j]
            st_scr[j] = new_state[j]

    return pl.pallas_call(
        body,
        name=name,
        out_shape=(
            jax.ShapeDtypeStruct((s, SSD_INNER), F32),
            jax.ShapeDtypeStruct((nc, N_PAIR, LANES, SSD_STATE), F32),
        ),
        grid=(nc,),
        in_specs=_ssd_specs(False, nc),
        out_specs=(
            pl.BlockSpec((SSD_CHUNK, SSD_INNER), lambda c: (c, 0)),
            pl.BlockSpec((1, N_PAIR, LANES, SSD_STATE), lambda c: (c, 0, 0, 0)),
        ),
        scratch_shapes=[pltpu.VMEM((N_PAIR, LANES, SSD_STATE), F32)],
        compiler_params=_cparams(("arbitrary",)),
    )(act, proj, proj, dtb, alog, dskip, ng)


def _ssd_bwd(act, proj, dtb, alog, dskip, ng, states, dy, name):
    s = act.shape[0]
    nc = s // SSD_CHUNK

    def body(act_ref, z_ref, dt_ref, dtb_ref, alog_ref, dskip_ref, ng_ref, st_ref, dy_ref,
             dact_ref, dz_ref, ddt_ref, ddtb_ref, dalog_ref, ddskip_ref, dng_ref, dst_scr):
        first = pl.program_id(0) == 0

        @pl.when(first)
        def _():
            dst_scr[...] = jnp.zeros_like(dst_scr)

        xs, bm, cm, z, dtp, dtb_v, alog_v, dskip_v, ng_v = _ssd_args(act_ref, z_ref, dt_ref, dtb_ref, alog_ref, dskip_ref, ng_ref)
        state = [st_ref[0, j] for j in range(N_PAIR)]
        _, vjp = jax.vjp(_ssd_chunk, xs, bm, cm, z, dtp, state, dtb_v, alog_v, dskip_v, ng_v)
        dy_v = [dy_ref[:, pl.ds(j * LANES, LANES)] for j in range(N_PAIR)]
        dxs, dbm, dcm, dz, ddtp, dstate, ddtb, dalog, ddskip, dng = vjp((dy_v, [dst_scr[j] for j in range(N_PAIR)]))
        for i, v in enumerate(dxs + dbm + dcm):
            dact_ref[:, pl.ds(i * LANES, LANES)] = v
        for j in range(N_PAIR):
            dz_ref[:, pl.ds(j * LANES, LANES)] = dz[j].astype(BF16)
            dst_scr[j] = dstate[j]
            _acc(dng_ref.at[:, pl.ds(j * LANES, LANES)], dng[j], first)
        ddt_ref[...] = ddtp.astype(BF16)
        _acc(ddtb_ref, ddtb, first)
        _acc(dalog_ref, dalog, first)
        _acc(ddskip_ref, ddskip, first)

    rv = lambda c: nc - 1 - c
    sds = jax.ShapeDtypeStruct
    return pl.pallas_call(
        body,
        name=name,
        out_shape=(
            sds((s, SSD_CONV_DIM), F32), sds((s, SSD_INNER), BF16), sds((s, LANES), BF16),
            sds((1, LANES), F32), sds((1, LANES), F32), sds((1, LANES), F32), sds((1, SSD_INNER), F32),
        ),
        grid=(nc,),
        in_specs=_ssd_specs(True, nc) + [
            pl.BlockSpec((1, N_PAIR, LANES, SSD_STATE), lambda c: (rv(c), 0, 0, 0)),
            pl.BlockSpec((SSD_CHUNK, SSD_INNER), lambda c: (rv(c), 0)),
        ],
        out_specs=(
            pl.BlockSpec((SSD_CHUNK, SSD_CONV_DIM), lambda c: (rv(c), 0)),
            pl.BlockSpec((SSD_CHUNK, SSD_INNER), lambda c: (rv(c), 0)),
            pl.BlockSpec((SSD_CHUNK, LANES), lambda c: (rv(c), 0)),
            _full((1, LANES)), _full((1, LANES)), _full((1, LANES)), _full((1, SSD_INNER)),
        ),
        scratch_shapes=[pltpu.VMEM((N_PAIR, LANES, SSD_STATE), F32)],
        compiler_params=_cparams(("arbitrary",)),
    )(act, proj, proj, dtb, alog, dskip, ng, states, dy)


def _merge_specs(tm):
    row = lambda w: pl.BlockSpec((tm, w), lambda i: (i, 0))
    return [
        row(GM_WIDTH), row(HEAD_BLOCK), row(SSD_INNER),
        pl.BlockSpec((tm, N_BRANCH * D_MODEL), lambda i: (i, P_GATES // (N_BRANCH * D_MODEL))),
        row(D_MODEL),
        _full((GM_WIDTH, D_MODEL)), _full((HEAD_BLOCK, D_MODEL)), _full((SSD_INNER, D_MODEL)), _full((D_MODEL, D_MODEL)),
    ]


def _merge_fwd(ya, yb, yc, proj, x1, pa, pb, pc, wo, name, tm=256):
    s = x1.shape[0]

    def body(ya_ref, yb_ref, yc_ref, gates_ref, x1_ref, pa_ref, pb_ref, pc_ref, wo_ref, x2_ref, mg_ref):
        merged = jnp.zeros((tm, D_MODEL), F32)
        for i, (y_ref, p_ref) in enumerate(((ya_ref, pa_ref), (yb_ref, pb_ref), (yc_ref, pc_ref))):
            gate = _sigmoid(gates_ref[:, pl.ds(i * D_MODEL, D_MODEL)])
            merged = merged + gate * _nn(y_ref[...], p_ref[...])
        mg_ref[...] = merged.astype(BF16)
        x2_ref[...] = x1_ref[...] + _nn(merged, wo_ref[...])

    row = lambda w: pl.BlockSpec((tm, w), lambda i: (i, 0))
    return pl.pallas_call(
        body,
        name=name,
        out_shape=(jax.ShapeDtypeStruct((s, D_MODEL), F32), jax.ShapeDtypeStruct((s, D_MODEL), BF16)),
        grid=(s // tm,),
        in_specs=_merge_specs(tm),
        out_specs=(row(D_MODEL), row(D_MODEL)),
        compiler_params=_cparams(("parallel",)),
    )(ya, yb, yc, proj, x1, pa, pb, pc, wo)


def _merge_bwd(ya, yb, yc, proj, dx2, pa, pb, pc, wo, name, tm=256):
    s = dx2.shape[0]

    def body(ya_ref, yb_ref, yc_ref, gates_ref, dx2_ref, pa_ref, pb_ref, pc_ref, wo_ref,
             dya_ref, dyb_ref, dyc_ref, dgates_ref, ta_ref, tb_ref, tc_ref):
        dmerged = _nt(dx2_ref[...], wo_ref[...])
        branches = ((ya_ref, pa_ref, dya_ref, ta_ref), (yb_ref, pb_ref, dyb_ref, tb_ref), (yc_ref, pc_ref, dyc_ref, tc_ref))
        for i, (y_ref, p_ref, dy_ref, t_ref) in enumerate(branches):
            cols = pl.ds(i * D_MODEL, D_MODEL)
            gate = _sigmoid(gates_ref[:, cols])
            dgates_ref[:, cols] = (dmerged * _nn(y_ref[...], p_ref[...]) * gate * (1.0 - gate)).astype(BF16)
            dt = (dmerged * gate).astype(BF16)
            t_ref[...] = dt
            dy_ref[...] = _nt(dt, p_ref[...])

    row = lambda w: pl.BlockSpec((tm, w), lambda i: (i, 0))
    sds = jax.ShapeDtypeStruct
    return pl.pallas_call(
        body,
        name=name,
        out_shape=(
            sds((s, GM_WIDTH), F32), sds((s, HEAD_BLOCK), F32), sds((s, SSD_INNER), F32),
            sds((s, N_BRANCH * D_MODEL), BF16),
            sds((s, D_MODEL), BF16), sds((s, D_MODEL), BF16), sds((s, D_MODEL), BF16),
        ),
        grid=(s // tm,),
        in_specs=_merge_specs(tm),
        out_specs=(row(GM_WIDTH), row(HEAD_BLOCK), row(SSD_INNER), row(N_BRANCH * D_MODEL),
                   row(D_MODEL), row(D_MODEL), row(D_MODEL)),
        compiler_params=_cparams(("parallel",)),
    )(ya, yb, yc, proj, dx2, pa, pb, pc, wo)


def _loss_head(y, target, name, tm=512):
    s, d = y.shape

    def body(y_ref, t_ref, dy_ref, loss_ref):
        err = y_ref[...] - t_ref[...]
        dy_ref[...] = err * (1.0 / d)
        part = jnp.sum(jnp.sum(err * err, axis=1, keepdims=True), axis=0, keepdims=True) * (0.5 / d)
        _acc(loss_ref, jnp.broadcast_to(part, (1, LANES)), pl.program_id(0) == 0)

    return pl.pallas_call(
        body,
        name=name,
        out_shape=(jax.ShapeDtypeStruct((s, d), F32), jax.ShapeDtypeStruct((1, LANES), F32)),
        grid=(s // tm,),
        in_specs=[pl.BlockSpec((tm, d), lambda i: (i, 0)), pl.BlockSpec((tm, d), lambda i: (i, 0))],
        out_specs=(pl.BlockSpec((tm, d), lambda i: (i, 0)), _full((1, LANES))),
        compiler_params=_cparams(("arbitrary",)),
    )(y, target)


IN_SHARD = IN_COLS // N_DEV
P_OF_PIECE = (P_UV, P_CQ, P_CKV, P_KR + MLA_NOPE, P_Z, P_XBC, P_DT, P_GATES)


def _pad_lanes(w, n=LANES):
    return jnp.pad(w, [(0, 0)] * (w.ndim - 1) + [(0, n - w.shape[-1])])


def _in_proj_layout(blocks):
    def cols(i):
        a, b, out = IN_OFFSETS[i], IN_OFFSETS[i] + IN_WIDTHS[i], []
        while a < b:
            k, lo = divmod(a, IN_SHARD)
            hi = min(IN_SHARD, lo + b - a)
            out.append(blocks[k, :, lo:hi])
            a += hi - lo
        return out

    zeros = lambda n: jnp.zeros((D_MODEL, n), blocks.dtype)
    uv, cq, ckv, kr, z, xbc, dt, gates = (cols(i) for i in range(8))
    return jnp.concatenate(uv + xbc + z + [zeros(MLA_NOPE)] + kr + [zeros(LANES - MLA_QK_DIM)] + cq + gates + ckv
                           + dt + [zeros(LANES - SSD_HEADS)], axis=1)


def _in_proj_unlayout(dw):
    out = []
    for k in range(N_DEV):
        a, b, parts = k * IN_SHARD, (k + 1) * IN_SHARD, []
        for i in range(8):
            lo, hi = max(a, IN_OFFSETS[i]), min(b, IN_OFFSETS[i] + IN_WIDTHS[i])
            if lo < hi:
                at = P_OF_PIECE[i] + lo - IN_OFFSETS[i]
                parts.append(dw[:, at:at + hi - lo])
        out.append(jnp.concatenate(parts, axis=1))
    return jnp.stack(out)


def _row(v, n=None):
    v = v.reshape(1, -1)
    return v if n is None else jnp.pad(v, ((0, 0), (0, n - v.shape[1])))


GATHERED = ("ffn_w_in", "ffn_w_out", "w_in", "mla_w_uq", "mla_w_ukv", "ssd_conv_w", "w_branch", "w_out")


def _layer_shards(w, l):
    pair = lambda a, b: jnp.stack([w[a][l], w[b][l]]).astype(BF16)
    one = lambda n: w[n][l].astype(BF16)
    return [pair("ffn1_w_in", "ffn2_w_in"), pair("ffn1_w_out", "ffn2_w_out"), one("w_in"), one("mla_w_uq"),
            one("mla_w_ukv"), one("ssd_conv_w"), one("w_branch"), one("w_out")]


def _layer_weights(gathered, vec, l):
    gw = dict(zip(GATHERED, gathered))
    kv = gw["mla_w_ukv"]
    branch = jnp.moveaxis(gw["w_branch"], 0, 2).reshape(N_BRANCH, GM_WIDTH, D_MODEL)
    return dict(
        ffn_w_in=gw["ffn_w_in"], ffn_w_out=gw["ffn_w_out"], ffn1_at=0, ffn2_at=1,
        ffn1_norm=_row(vec["ffn1_norm"][l]), ffn2_norm=_row(vec["ffn2_norm"][l]),
        mix_norm=_row(vec["mix_norm"][l]), w_in=_in_proj_layout(gw["w_in"]),
        gm_v_norm=_row(vec["gm_v_norm"][l]), gm_w_s=vec["gm_w_s"][l], gm_b_s=vec["gm_b_s"][l][..., None],
        q_norm=_row(vec["mla_q_norm"][l]), kv_norm=_row(vec["mla_kv_norm"][l]),
        wq=_pad_lanes(gw["mla_w_uq"]), wk=_pad_lanes(kv[:, :, :MLA_NOPE]), wv=_pad_lanes(kv[:, :, MLA_NOPE:]),
        q_gain=_row(vec["mla_q_gain"][l], LANES), k_gain=_row(vec["mla_k_gain"][l], LANES),
        conv_w=jnp.moveaxis(gw["ssd_conv_w"], 0, 1).reshape(SSD_CONV, SSD_CONV_DIM).astype(F32),
        conv_b=_row(vec["ssd_conv_b"][l]),
        dt_bias=_row(vec["ssd_dt_bias"][l], LANES), a_log=_row(vec["ssd_a_log"][l], LANES),
        d_skip=_row(vec["ssd_d"][l], LANES), ssd_norm=_row(vec["ssd_norm"][l]),
        pa=branch[0],
        pb=jnp.pad(branch[1].reshape(MLA_HEADS, MLA_V, D_MODEL), ((0, 0), (0, LANES - MLA_V), (0, 0))).reshape(HEAD_BLOCK, D_MODEL),
        pc=branch[2], wo=gw["w_out"].reshape(D_MODEL, D_MODEL),
    )


def _layer_fwd(x, k, rope, next_shards=()):
    cosf, sinf, rot = rope
    x1, gu1 = _ffn_fwd(x, k["ffn1_norm"], k["ffn_w_in"], k["ffn_w_out"], k["ffn1_at"], "ffn1_fwd")
    h = _rmsnorm_fwd(x1, k["mix_norm"], "mix_norm_fwd")
    proj = _matmul(h, k["w_in"], "nn", F32, "in_proj_fwd", tn=2176)
    ya = _gmlp_fwd(proj, k["gm_v_norm"], k["gm_w_s"], k["gm_b_s"], "gmlp_fwd")
    q, kk, v = _mla_pre_fwd(proj, cosf, sinf, rot, k["q_norm"], k["kv_norm"], k["wq"], k["wk"], k["wv"],
                            k["q_gain"], k["k_gain"], "mla_pre_fwd")
    yb, lse, lse_t, next_gathered = _attention_fwd(q, kk, v, "attention_fwd", gather=next_shards)
    act = _conv_fwd(proj, k["conv_w"], k["conv_b"], "conv_fwd")
    yc, states = _ssd_fwd(act, proj, k["dt_bias"], k["a_log"], k["d_skip"], k["ssd_norm"], "ssd_fwd")
    x2, merged = _merge_fwd(ya, yb, yc, proj, x1, k["pa"], k["pb"], k["pc"], k["wo"], "merge_fwd")
    x3, gu2 = _ffn_fwd(x2, k["ffn2_norm"], k["ffn_w_in"], k["ffn_w_out"], k["ffn2_at"], "ffn2_fwd")
    saved = dict(x=x, x1=x1, x2=x2, gu1=gu1, gu2=gu2, h=h, proj=proj, ya=ya, yb=yb, yc=yc, q=q, k=kk, v=v,
                 lse=lse, lse_t=lse_t, act=act, states=states, merged=merged)
    return x3, saved, next_gathered


def _layer_bwd(dx3, k, sv, rope):
    cosf, sinf, rot = rope
    g = {}
    dx2, g["ffn2_norm"], g["ffn2_w_in"], g["ffn2_w_out"] = _ffn_bwd(
        sv["x2"], k["ffn2_norm"], k["ffn_w_in"], k["ffn_w_out"], k["ffn2_at"], sv["gu2"], dx3, "ffn2_bwd")
    proj = sv["proj"]
    dya, dyb, dyc, dgates, ta, tb, tc = _merge_bwd(sv["ya"], sv["yb"], sv["yc"], proj, dx2, k["pa"], k["pb"], k["pc"],
                                                   k["wo"], "merge_bwd")
    g["w_out"] = _matmul(sv["merged"], dx2, "tn", BF16, "w_out_grad").reshape(N_DEV, D_MODEL // N_DEV, D_MODEL)
    dpa = _matmul(sv["ya"], ta, "tn", BF16, "branch_a_grad")
    dpb = _matmul(sv["yb"], tb, "tn", BF16, "branch_b_grad")
    dpc = _matmul(sv["yc"], tc, "tn", BF16, "branch_c_grad")
    branch = jnp.stack([dpa, dpb.reshape(MLA_HEADS, LANES, D_MODEL)[:, :MLA_V].reshape(GM_WIDTH, D_MODEL), dpc])
    g["w_branch"] = jnp.moveaxis(branch.reshape(N_BRANCH, GM_WIDTH, N_DEV, LANES), 2, 0)
    duv, g["gm_v_norm"], g["gm_w_s"], dbs = _gmlp_bwd(proj, k["gm_v_norm"], k["gm_w_s"], k["gm_b_s"], dya, "gmlp_bwd")
    g["gm_b_s"] = dbs[..., 0]
    dq, dk, dv = _attention_bwd(sv["q"], sv["k"], sv["v"], sv["yb"], sv["lse"], sv["lse_t"], dyb, "attention_bwd")
    dcq, dckv, dkr, dqn, dkvn, dwq, dwk, dwv, dqg, dkg = _mla_pre_bwd(
        proj, cosf, sinf, rot, k["q_norm"], k["kv_norm"], k["wq"], k["wk"], k["wv"], k["q_gain"], k["k_gain"],
        dq, dk, dv, "mla_pre_bwd")
    g["mla_q_norm"], g["mla_kv_norm"] = dqn, dkvn
    g["mla_w_uq"] = dwq[:, :, :MLA_QK_DIM].astype(BF16)
    g["mla_w_ukv"] = jnp.concatenate([dwk[:, :, :MLA_NOPE], dwv[:, :, :MLA_V]], axis=-1).astype(BF16)
    g["mla_q_gain"], g["mla_k_gain"] = dqg[:, :MLA_QK_DIM], dkg[:, :MLA_QK_DIM]
    dact, dz, ddt, ddtb, dalog, ddsk, g["ssd_norm"] = _ssd_bwd(
        sv["act"], proj, k["dt_bias"], k["a_log"], k["d_skip"], k["ssd_norm"], sv["states"], dyc, "ssd_bwd")
    g["ssd_dt_bias"], g["ssd_a_log"], g["ssd_d"] = ddtb[:, :SSD_HEADS], dalog[:, :SSD_HEADS], ddsk[:, :SSD_HEADS]
    dxbc, dcw, g["ssd_conv_b"] = _conv_bwd(proj, k["conv_w"], k["conv_b"], dact, "conv_bwd")
    g["ssd_conv_w"] = jnp.moveaxis(dcw.reshape(SSD_CONV, N_DEV, LANES), 1, 0).astype(BF16)
    dproj = jnp.concatenate([duv, dxbc, dz, dkr, dcq, dgates, dckv, ddt], axis=1)
    dh = _matmul(dproj, k["w_in"], "nt", BF16, "in_proj_dh", tk=2176)
    g["w_in"] = _in_proj_unlayout(_matmul(sv["h"], dproj, "tn", BF16, "in_proj_grad", tn=2176))
    dx1, g["mix_norm"] = _rmsnorm_bwd(sv["x1"], k["mix_norm"], dh, dx2, "mix_norm_bwd")
    dx, g["ffn1_norm"], g["ffn1_w_in"], g["ffn1_w_out"] = _ffn_bwd(
        sv["x"], k["ffn1_norm"], k["ffn_w_in"], k["ffn_w_out"], k["ffn1_at"], sv["gu1"], dx1, "ffn1_bwd")
    return dx, g


def _rope_tables(positions):
    s = positions.shape[0]
    inv_freq = 1.0 / (ROPE_THETA ** (jnp.arange(0, MLA_ROPE, 2, dtype=F32) / MLA_ROPE))
    ang = positions.astype(F32)[:, None] * inv_freq
    cos, sin = jnp.cos(ang), jnp.sin(ang)
    tail = LANES - MLA_QK_DIM
    cosf = jnp.concatenate([jnp.ones((s, MLA_NOPE), F32), cos, cos, jnp.ones((s, tail), F32)], axis=1)
    sinf = jnp.concatenate([jnp.zeros((s, MLA_NOPE), F32), sin, sin, jnp.zeros((s, tail), F32)], axis=1)
    half = MLA_ROPE // 2
    rot = np.zeros((LANES, LANES), np.float32)
    for i in range(half):
        rot[MLA_NOPE + half + i, MLA_NOPE + i] = -1.0
        rot[MLA_NOPE + i, MLA_NOPE + half + i] = 1.0
    return cosf, sinf, jnp.asarray(rot)


MATRICES = ("ffn1_w_in", "ffn1_w_out", "w_in", "mla_w_uq", "mla_w_ukv", "ssd_conv_w", "w_branch", "w_out", "ffn2_w_in",
            "ffn2_w_out")
VECTORS = ("ffn1_norm", "mix_norm", "gm_v_norm", "gm_w_s", "gm_b_s", "mla_q_norm", "mla_kv_norm", "mla_q_gain",
           "mla_k_gain", "ssd_conv_b", "ssd_dt_bias", "ssd_a_log", "ssd_d", "ssd_norm", "ffn2_norm")
WEIGHTS = ("ffn1_norm", "ffn1_w_in", "ffn1_w_out", "mix_norm", "w_in", "gm_v_norm", "gm_w_s", "gm_b_s", "mla_q_norm",
           "mla_kv_norm", "mla_w_uq", "mla_w_ukv", "mla_q_gain", "mla_k_gain", "ssd_conv_w", "ssd_conv_b", "ssd_dt_bias",
           "ssd_a_log", "ssd_d", "ssd_norm", "w_branch", "w_out", "ffn2_norm", "ffn2_w_in", "ffn2_w_out")


def _local_step(x, positions, target, vec, shards=None, gathered=None):
    rope = _rope_tables(positions)
    depth = vec["ffn1_norm"].shape[0]
    saved = []
    here = _gather_layer(shards[0], "layer0_all_gather") if gathered is None else gathered[0]
    for l in range(depth):
        k = _layer_weights(here, vec, l)
        ahead = shards[l + 1] if gathered is None and l + 1 < depth else ()
        x, sv, here = _layer_fwd(x, k, rope, ahead)
        if gathered is not None and l + 1 < depth:
            here = gathered[l + 1]
        saved.append((k, sv))
    dy, loss = _loss_head(x, target, "loss_head")
    grads = []
    for k, sv in reversed(saved):
        dy, g = _layer_bwd(dy, k, sv, rope)
        grads.append(g)
    grads.reverse()
    out = {n: [g[n] for g in grads] for n in MATRICES}
    out.update({n: jnp.stack([g[n].reshape(vec[n].shape[1:]) for g in grads]) for n in VECTORS})
    return loss[0, 0], dy, out


MESH = pl.DeviceIdType.MESH
N_CHIP = 4
ANY = pl.BlockSpec(memory_space=pl.ANY)


def _place():
    return lax.axis_index("x"), lax.axis_index("y"), lax.axis_index("c")


GATHER_COPIES = 7


def _gather_plan(shard_refs, out_refs, send_sems, recv_sems, local_sems):
    x, y, c = _place()
    me, sibling = (x, y, c), (x, y, 1 - c)
    chips = [(1 - x, y), (x, 1 - y), (1 - x, 1 - y)]
    slot = lambda px, py, pc: 4 * px + 2 * py + pc
    plans = []
    for i, (src, out) in enumerate(zip(shard_refs, out_refs)):
        def copy(k, block, to, from_shard=False, i=i, src=src, out=out):
            return pltpu.make_async_remote_copy(
                src_ref=src if from_shard else out.at[slot(*block)], dst_ref=out.at[slot(*block)],
                send_sem=send_sems.at[GATHER_COPIES * i + k], recv_sem=recv_sems.at[GATHER_COPIES * i + k],
                device_id=to, device_id_type=MESH)

        plans.append(dict(
            mine=lambda i=i, src=src, out=out: pltpu.make_async_copy(src, out.at[slot(*me)], local_sems.at[i]),
            first=lambda copy=copy: [copy(0, me, sibling, True)] + [copy(1 + j, me, (*chip, c), True)
                                                                    for j, chip in enumerate(chips)],
            arrived=lambda j, copy=copy: copy(1 + j, (*chips[j], c), me),
            passed=lambda j, copy=copy: copy(4 + j, (*chips[j], c), sibling),
            from_sibling=lambda copy=copy: [copy(0, sibling, me)] + [copy(4 + j, (*chip, 1 - c), me)
                                                                     for j, chip in enumerate(chips)],
        ))
    return plans


def _gather_start(plans):
    for p in plans:
        p["mine"]().start()
        for cp in p["first"]():
            cp.start()


def _gather_finish(plans):
    for j in range(N_CHIP - 1):
        for p in plans:
            p["arrived"](j).wait_recv()
            p["passed"](j).start()
    for p in plans:
        for cp in p["from_sibling"]():
            cp.wait_recv()
        for cp in p["first"]() + [p["passed"](j) for j in range(N_CHIP - 1)]:
            cp.wait_send()
        p["mine"]().wait()


def _gather_scratch(n):
    return [pltpu.SemaphoreType.DMA((GATHER_COPIES * n,)), pltpu.SemaphoreType.DMA((GATHER_COPIES * n,)),
            pltpu.SemaphoreType.DMA((n,))]


def _gathered_shapes(shards):
    return tuple(jax.ShapeDtypeStruct((N_DEV, *a.shape), a.dtype) for a in shards)


def _gather_layer(shards, name):
    n = len(shards)

    def body(*refs):
        plans = _gather_plan(refs[:n], refs[n:2 * n], *refs[2 * n:])
        _gather_start(plans)
        _gather_finish(plans)

    return pl.pallas_call(
        body,
        name=name,
        out_shape=_gathered_shapes(shards),
        in_specs=[ANY] * n,
        out_specs=(ANY,) * n,
        scratch_shapes=_gather_scratch(n),
    )(*shards)


def _all_gather(shard, name):
    m, n = shard.shape

    def body(x_ref, out_ref, send_sems, recv_sems, local_sem):
        x, y, c = _place()
        me, sibling = (x, y, c), (x, y, 1 - c)
        chips = [(1 - x, y), (x, 1 - y), (1 - x, 1 - y)]

        def rows(px, py, pc):
            return out_ref.at[pl.ds((4 * px + 2 * py + pc) * m, m), :]

        def copy(k, block, to, src=None):
            return pltpu.make_async_remote_copy(
                src_ref=rows(*block) if src is None else src, dst_ref=rows(*block),
                send_sem=send_sems.at[k], recv_sem=recv_sems.at[k], device_id=to, device_id_type=MESH)

        mine = pltpu.make_async_copy(x_ref, rows(*me), local_sem)
        mine.start()
        first = [copy(0, me, sibling, src=x_ref)]
        first += [copy(1 + j, me, (*chip, c), src=x_ref) for j, chip in enumerate(chips)]
        for cp in first:
            cp.start()
        passed = [copy(4 + j, (*chip, c), sibling) for j, chip in enumerate(chips)]
        for j, chip in enumerate(chips):
            copy(1 + j, (*chip, c), me).wait_recv()
            passed[j].start()
        copy(0, sibling, me).wait_recv()
        for j, chip in enumerate(chips):
            copy(4 + j, (*chip, 1 - c), me).wait_recv()
        for cp in first + passed:
            cp.wait_send()
        mine.wait()

    return pl.pallas_call(
        body,
        name=name,
        out_shape=jax.ShapeDtypeStruct((N_DEV * m, n), shard.dtype),
        in_specs=[ANY],
        out_specs=ANY,
        scratch_shapes=[pltpu.SemaphoreType.DMA((7,)), pltpu.SemaphoreType.DMA((7,)), pltpu.SemaphoreType.DMA],
    )(shard)


def _pair_exchange(contrib, name):
    _, r, n = contrib.shape

    def body(g_ref, got_ref, send_sems, recv_sems):
        x, y, c = _place()
        remote = [pltpu.make_async_remote_copy(
            src_ref=g_ref.at[2 * j + (1 - c)], dst_ref=got_ref.at[j], send_sem=send_sems.at[j], recv_sem=recv_sems.at[j],
            device_id=(x, y, 1 - c), device_id_type=MESH) for j in range(N_CHIP)]
        for cp in remote:
            cp.start()
        for cp in remote:
            cp.wait()

    return pl.pallas_call(
        body,
        name=name,
        out_shape=jax.ShapeDtypeStruct((N_CHIP, r, n), contrib.dtype),
        in_specs=[ANY],
        out_specs=ANY,
        scratch_shapes=[pltpu.SemaphoreType.DMA((N_CHIP,)), pltpu.SemaphoreType.DMA((N_CHIP,))],
    )(contrib)


def _pair_sum(contrib, theirs, name, tr):
    _, r, n = contrib.shape
    side = lax.axis_index("c").astype(jnp.int32).reshape(1)

    def body(c_ref, a_ref, b_ref, o_ref):
        o_ref[...] = (a_ref[...].astype(F32) + b_ref[...].astype(F32)).astype(BF16)

    spec = pl.BlockSpec((1, tr, n), lambda j, i, c_ref: (j, i, 0))
    return pl.pallas_call(
        body,
        name=name,
        out_shape=jax.ShapeDtypeStruct(theirs.shape, BF16),
        grid_spec=pltpu.PrefetchScalarGridSpec(
            num_scalar_prefetch=1,
            grid=(N_CHIP, r // tr),
            in_specs=[pl.BlockSpec((1, tr, n), lambda j, i, c_ref: (2 * j + c_ref[0], i, 0)), spec],
            out_specs=spec,
        ),
        compiler_params=_cparams(("parallel", "parallel")),
    )(side, contrib, theirs)


def _chip_exchange(part, name):
    _, r, n = part.shape

    def body(p_ref, got_ref, send_sems, recv_sems, local_sem):
        x, y, c = _place()
        mine = 2 * x + y
        chips = [(1 - x, y), (x, 1 - y), (1 - x, 1 - y)]
        local = pltpu.make_async_copy(p_ref.at[mine], got_ref.at[mine], local_sem)
        local.start()
        sends = [pltpu.make_async_remote_copy(
            src_ref=p_ref.at[2 * cx + cy], dst_ref=got_ref.at[mine], send_sem=send_sems.at[k], recv_sem=recv_sems.at[k],
            device_id=(cx, cy, c), device_id_type=MESH) for k, (cx, cy) in enumerate(chips)]
        for cp in sends:
            cp.start()
        for k, (cx, cy) in enumerate(chips):
            pltpu.make_async_remote_copy(
                src_ref=p_ref.at[mine], dst_ref=got_ref.at[2 * cx + cy], send_sem=send_sems.at[k],
                recv_sem=recv_sems.at[k], device_id=(cx, cy, c), device_id_type=MESH).wait_recv()
        for cp in sends:
            cp.wait_send()
        local.wait()

    return pl.pallas_call(
        body,
        name=name,
        out_shape=jax.ShapeDtypeStruct(part.shape, part.dtype),
        in_specs=[ANY],
        out_specs=ANY,
        scratch_shapes=[pltpu.SemaphoreType.DMA((3,)), pltpu.SemaphoreType.DMA((3,)), pltpu.SemaphoreType.DMA],
    )(part)


def _adamw(parts, w, m, v, name, tr, at=0):
    k = parts.shape[0]
    r, n = w.shape
    first = at // tr

    def body(p_ref, w_ref, m_ref, v_ref, g_ref, d_ref, nm_ref, nv_ref):
        g = p_ref[0].astype(F32)
        for i in range(1, k):
            g = g + p_ref[i].astype(F32)
        m_new = ADAM_B1 * m_ref[...] + (1.0 - ADAM_B1) * g
        v_new = ADAM_B2 * v_ref[...] + (1.0 - ADAM_B2) * (g * g)
        m_hat = m_new / (1.0 - ADAM_B1 ** ADAM_STEP)
        v_hat = v_new / (1.0 - ADAM_B2 ** ADAM_STEP)
        g_ref[...] = g
        d_ref[...] = -ADAM_LR * (m_hat / (jnp.sqrt(v_hat) + ADAM_EPS) + ADAM_WD * w_ref[...])
        nm_ref[...] = m_new
        nv_ref[...] = v_new

    spec = pl.BlockSpec((tr, n), lambda i: (i, 0))
    out = jax.ShapeDtypeStruct((r, n), F32)
    return pl.pallas_call(
        body,
        name=name,
        out_shape=(out, out, out, out),
        grid=(r // tr,),
        in_specs=[pl.BlockSpec((k, tr, n), lambda i: (0, i + first, 0)), spec, spec, spec],
        out_specs=(spec, spec, spec, spec),
        compiler_params=_cparams(("parallel",)),
    )(parts, w, m, v)


TILE_BYTES = 2 * 1024 * 1024
SUBLANES_16BIT = 16


def _tile_rows(r, n):
    best = None
    for t in range(SUBLANES_16BIT, r, SUBLANES_16BIT):
        if r % t == 0 and t * n * 4 <= TILE_BYTES:
            best = t
    return best or r


def _rows(a):
    return a.reshape(-1, a.shape[-1])


def _gather_rows(shard, name):
    return _all_gather(_rows(shard), name).reshape(N_DEV, *shard.shape)


def _reduce_scatter(contrib, name):
    theirs = _pair_exchange(contrib, name + "_pair_exchange")
    part = _pair_sum(contrib, theirs, name + "_pair_sum", _tile_rows(contrib.shape[1], contrib.shape[2]))
    return _chip_exchange(part, name + "_chip_exchange")


SMALL = ("ffn1_norm", "mix_norm", "gm_v_norm", "gm_b_s", "mla_q_norm", "mla_kv_norm", "mla_q_gain", "mla_k_gain",
         "ssd_conv_b", "ssd_dt_bias", "ssd_a_log", "ssd_d", "ssd_norm", "ffn2_norm")
SMALL_ROWS = 8
SMALL_COLS = 7040


def _side_by_side(d):
    cols = jnp.concatenate([d[n].reshape(d[n].shape[0], -1) for n in SMALL], axis=1)
    return jnp.pad(cols, ((0, SMALL_ROWS - cols.shape[0]), (0, SMALL_COLS - cols.shape[1])))


def _apart(packed, like):
    out, off = {}, 0
    for n in SMALL:
        size = like[n][0].size
        out[n] = packed[:like[n].shape[0], off:off + size].reshape(like[n].shape)
        off += size
    return out


def kernel(x, positions, ffn1_norm, ffn1_w_in, ffn1_w_out, mix_norm, w_in, gm_v_norm, gm_w_s, gm_b_s, mla_q_norm, mla_kv_norm, mla_w_uq, mla_w_ukv, mla_q_gain, mla_k_gain, ssd_conv_w, ssd_conv_b, ssd_dt_bias, ssd_a_log, ssd_d, ssd_norm, w_branch, w_out, ffn2_norm, ffn2_w_in, ffn2_w_out, loss_target, m_ffn1_norm, m_ffn1_w_in, m_ffn1_w_out, m_mix_norm, m_w_in, m_gm_v_norm, m_gm_w_s, m_gm_b_s, m_mla_q_norm, m_mla_kv_norm, m_mla_w_uq, m_mla_w_ukv, m_mla_q_gain, m_mla_k_gain, m_ssd_conv_w, m_ssd_conv_b, m_ssd_dt_bias, m_ssd_a_log, m_ssd_d, m_ssd_norm, m_w_branch, m_w_out, m_ffn2_norm, m_ffn2_w_in, m_ffn2_w_out, v_ffn1_norm, v_ffn1_w_in, v_ffn1_w_out, v_mix_norm, v_w_in, v_gm_v_norm, v_gm_w_s, v_gm_b_s, v_mla_q_norm, v_mla_kv_norm, v_mla_w_uq, v_mla_w_ukv, v_mla_q_gain, v_mla_k_gain, v_ssd_conv_w, v_ssd_conv_b, v_ssd_dt_bias, v_ssd_a_log, v_ssd_d, v_ssd_norm, v_w_branch, v_w_out, v_ffn2_norm, v_ffn2_w_in, v_ffn2_w_out):
    given = dict(locals())
    w = {n: given[n] for n in WEIGHTS}
    mom = {n: given["m_" + n] for n in WEIGHTS}
    var = {n: given["v_" + n] for n in WEIGHTS}
    groups = {"ffn_w_in": ("ffn1_w_in", "ffn2_w_in"), "ffn_w_out": ("ffn1_w_out", "ffn2_w_out"), "w_in": ("w_in",),
              "mla_w_uq": ("mla_w_uq",), "mla_w_ukv": ("mla_w_ukv",), "ssd_conv_w": ("ssd_conv_w",),
              "w_branch": ("w_branch",), "w_out": ("w_out",)}

    shards = [_layer_shards(w, l) for l in range(w_out.shape[0])]
    loss, dx, grads = _local_step(x[0], positions[0], loss_target[0], {n: w[n] for n in VECTORS}, shards=shards)

    outs = [{}, {}, {}, {}]
    for g, names in groups.items():
        contrib = jnp.stack([a for n in names for a in grads[n]], axis=1)
        parts = _reduce_scatter(contrib.reshape(N_DEV, -1, contrib.shape[-1]), g + "_grad")
        at = 0
        for n in names:
            wr = _rows(w[n])
            res = _adamw(parts, wr, _rows(mom[n]), _rows(var[n]), "adamw_" + n, _tile_rows(*wr.shape), at=at)
            at += wr.shape[0]
            for o, r in zip(outs, res):
                o[n] = r.reshape(w[n].shape)

    small_parts = _gather_rows(_side_by_side(grads), "small_grads_all_gather")
    small = _adamw(small_parts, _side_by_side(w), _side_by_side(mom), _side_by_side(var), "adamw_small", SMALL_ROWS)
    ws_parts = _gather_rows(_rows(grads["gm_w_s"]), "gm_w_s_grads_all_gather")
    ws = _adamw(ws_parts, _rows(w["gm_w_s"]), _rows(mom["gm_w_s"]), _rows(var["gm_w_s"]), "adamw_gm_w_s",
                _tile_rows(ws_parts.shape[1], LANES))
    for o, sm, r in zip(outs, small, ws):
        o.update(_apart(sm, w))
        o["gm_w_s"] = r.reshape(w["gm_w_s"].shape)

    loss = lax.psum(loss, ("x", "y", "c"))
    return (loss, dx[None], *[o[n] for o in outs for n in WEIGHTS])
```

```python
import functools

import jax
import jax.numpy as jnp
import numpy as np
from jax import lax
from jax.experimental import pallas as pl
from jax.experimental.pallas import tpu as pltpu

F32 = jnp.float32
BF16 = jnp.bfloat16

D_MODEL = 1024
DEPTH = 4
D_FF = 2816
FFN_RESID = 0.5
EPS = 1e-6
GM_WIDTH = 512
GM_GROUPS = 4
GM_CHUNK = 128
MLA_HEADS = 8
MLA_Q_RANK = 384
MLA_KV_RANK = 256
MLA_NOPE = 64
MLA_ROPE = 32
MLA_QK_DIM = 96
MLA_V = 64
ROPE_THETA = 10000.0
SSD_HEADS = 8
SSD_HEAD_DIM = 64
SSD_INNER = 512
SSD_GROUPS = 2
SSD_STATE = 128
SSD_CONV = 4
SSD_CHUNK = 128
SSD_CONV_DIM = 1024
N_BRANCH = 3
IN_WIDTHS = (1024, 384, 256, 32, 512, 1024, 8, 3072)
IN_OFFSETS = (0, 1024, 1408, 1664, 1696, 2208, 3232, 3240)
IN_COLS = 6312
LANES = 128
N_DEV = 8

ADAM_LR = 0.001
ADAM_B1 = 0.9
ADAM_B2 = 0.999
ADAM_EPS = 1e-08
ADAM_WD = 0.01
ADAM_STEP = 10

VMEM_LIMIT = 56 * 1024 * 1024

P_UV, P_XBC, P_Z, P_KR, P_CQ, P_GATES, P_CKV, P_DT = 0, 1024, 2048, 2560, 2688, 3072, 6144, 6400
P_COLS = 6528


def _cparams(sem):
    return pltpu.CompilerParams(dimension_semantics=sem, vmem_limit_bytes=VMEM_LIMIT)


def _bdot(a, b, dims):
    return lax.dot_general(a.astype(BF16), b.astype(BF16), (dims, ((), ())), preferred_element_type=F32)


@jax.custom_vjp
def _nn(a, b):
    return _bdot(a, b, ((1,), (0,)))


@jax.custom_vjp
def _nt(a, b):
    return _bdot(a, b, ((1,), (1,)))


@jax.custom_vjp
def _tn(a, b):
    return _bdot(a, b, ((0,), (0,)))


def _dot_fwd(dims):
    return lambda a, b: (_bdot(a, b, dims), (a, b))


_nn.defvjp(_dot_fwd(((1,), (0,))), lambda r, g: (_nt(g, r[1]).astype(r[0].dtype), _tn(r[0], g).astype(r[1].dtype)))
_nt.defvjp(_dot_fwd(((1,), (1,))), lambda r, g: (_nn(g, r[1]).astype(r[0].dtype), _tn(g, r[0]).astype(r[1].dtype)))
_tn.defvjp(_dot_fwd(((0,), (0,))), lambda r, g: (_nt(r[1], g).astype(r[0].dtype), _nn(r[0], g).astype(r[1].dtype)))


def _exact_nn(a, b):
    return lax.dot_general(a, b, (((1,), (0,)), ((), ())), precision=lax.Precision.HIGHEST, preferred_element_type=F32)


def _sigmoid(x):
    return 1.0 / (1.0 + jnp.exp(-x))


def _silu(x):
    return x * _sigmoid(x)


def _softplus(x):
    return jnp.maximum(x, 0.0) + jnp.log(1.0 + jnp.exp(-jnp.abs(x)))


def _gelu(x):
    return 0.5 * x * (1.0 + lax.erf(x * 0.7071067811865476))


def _pick(n, cands):
    for c in cands:
        if n % c == 0:
            return c
    return n


def _matmul(a, b, mode, out_dtype, name, alpha=1.0, tm=None, tn=None, tk=None):
    if mode == "nn":
        (m, k), (_, n) = a.shape, b.shape
    elif mode == "nt":
        (m, k), (n, _) = a.shape, b.shape
    else:
        (k, m), (_, n) = a.shape, b.shape
    tm = tm or _pick(m, (512, 384, 256, 128))
    tn = tn or _pick(n, (1024, 768, 512, 384, 256, 128))
    tk = tk or _pick(k, (1024, 512, 256, 128))
    nk = k // tk
    if mode == "nn":
        a_spec = pl.BlockSpec((tm, tk), lambda i, j, kk: (i, kk))
        b_spec = pl.BlockSpec((tk, tn), lambda i, j, kk: (kk, j))
        dot = _nn
    elif mode == "nt":
        a_spec = pl.BlockSpec((tm, tk), lambda i, j, kk: (i, kk))
        b_spec = pl.BlockSpec((tn, tk), lambda i, j, kk: (j, kk))
        dot = _nt
    else:
        a_spec = pl.BlockSpec((tk, tm), lambda i, j, kk: (kk, i))
        b_spec = pl.BlockSpec((tk, tn), lambda i, j, kk: (kk, j))
        dot = _tn

    def body(a_ref, b_ref, o_ref, acc_ref):
        kk = pl.program_id(2)

        @pl.when(kk == 0)
        def _():
            acc_ref[...] = jnp.zeros_like(acc_ref)

        acc_ref[...] += dot(a_ref[...], b_ref[...])

        @pl.when(kk == nk - 1)
        def _():
            o_ref[...] = (alpha * acc_ref[...]).astype(out_dtype)

    return pl.pallas_call(
        body,
        name=name,
        out_shape=jax.ShapeDtypeStruct((m, n), out_dtype),
        grid=(m // tm, n // tn, nk),
        in_specs=[a_spec, b_spec],
        out_specs=pl.BlockSpec((tm, tn), lambda i, j, kk: (i, j)),
        scratch_shapes=[pltpu.VMEM((tm, tn), F32)],
        compiler_params=_cparams(("parallel", "parallel", "arbitrary")),
    )(a, b)


def _rms_stats(x):
    r = lax.rsqrt(jnp.mean(x * x, axis=-1, keepdims=True) + EPS)
    return x * r, r


def _rms_bwd(xhat, r, gain, dy):
    dxhat = dy * gain
    return r * (dxhat - xhat * jnp.mean(dxhat * xhat, axis=-1, keepdims=True))


def _acc_rows(ref, val, first):
    s = jnp.sum(val, axis=0, keepdims=True)

    @pl.when(first)
    def _():
        ref[...] = s

    @pl.when(jnp.logical_not(first))
    def _():
        ref[...] += s


def _rmsnorm_fwd(x, gain, name, tm=512):
    s, d = x.shape

    def body(x_ref, g_ref, h_ref):
        xhat, _ = _rms_stats(x_ref[...])
        h_ref[...] = (xhat * g_ref[...]).astype(BF16)

    return pl.pallas_call(
        body,
        name=name,
        out_shape=jax.ShapeDtypeStruct((s, d), BF16),
        grid=(s // tm,),
        in_specs=[pl.BlockSpec((tm, d), lambda i: (i, 0)), pl.BlockSpec((1, d), lambda i: (0, 0))],
        out_specs=pl.BlockSpec((tm, d), lambda i: (i, 0)),
        compiler_params=_cparams(("parallel",)),
    )(x, gain)


def _rmsnorm_bwd(x, gain, dh, dres, name, tm=512):
    s, d = x.shape

    def body(x_ref, g_ref, dh_ref, dres_ref, dx_ref, dg_ref):
        xhat, r = _rms_stats(x_ref[...])
        dh = dh_ref[...].astype(F32)
        dx_ref[...] = dres_ref[...] + _rms_bwd(xhat, r, g_ref[...], dh)
        _acc_rows(dg_ref, dh * xhat, pl.program_id(0) == 0)

    return pl.pallas_call(
        body,
        name=name,
        out_shape=(jax.ShapeDtypeStruct((s, d), F32), jax.ShapeDtypeStruct((1, d), F32)),
        grid=(s // tm,),
        in_specs=[
            pl.BlockSpec((tm, d), lambda i: (i, 0)),
            pl.BlockSpec((1, d), lambda i: (0, 0)),
            pl.BlockSpec((tm, d), lambda i: (i, 0)),
            pl.BlockSpec((tm, d), lambda i: (i, 0)),
        ],
        out_specs=(pl.BlockSpec((tm, d), lambda i: (i, 0)), pl.BlockSpec((1, d), lambda i: (0, 0))),
        compiler_params=_cparams(("arbitrary",)),
    )(x, gain, dh, dres)


FF_BLOCK = 2 * D_FF // N_DEV
FF_BLOCKS = D_FF // FF_BLOCK
FF_ROWS = D_FF // N_DEV


def _ffn_weight_specs(layer):
    return [
        pl.BlockSpec((None, None, D_MODEL, FF_BLOCK), lambda i, j: (j, layer, 0, 0)),
        pl.BlockSpec((None, None, D_MODEL, FF_BLOCK), lambda i, j: (j + FF_BLOCKS, layer, 0, 0)),
        pl.BlockSpec((2, None, FF_ROWS, D_MODEL), lambda i, j: (j, layer, 0, 0)),
    ]


def _ffn_fwd(x, gain, w_in, w_out, layer, name, tm=512):
    s, d = x.shape

    def body(x_ref, gain_ref, wg_ref, wu_ref, wo_ref, y_ref, gu_ref, h_scr, acc_scr):
        j = pl.program_id(1)

        @pl.when(j == 0)
        def _():
            xhat, _ = _rms_stats(x_ref[...])
            h_scr[...] = (xhat * gain_ref[...]).astype(BF16)
            acc_scr[...] = jnp.zeros_like(acc_scr)

        h = h_scr[...]
        g = _nn(h, wg_ref[...])
        u = _nn(h, wu_ref[...])
        gu_ref[0] = g.astype(BF16)
        gu_ref[1] = u.astype(BF16)
        acc_scr[...] += _nn(_silu(g) * u, wo_ref[...].reshape(FF_BLOCK, d))

        @pl.when(j == FF_BLOCKS - 1)
        def _():
            y_ref[...] = x_ref[...] + FFN_RESID * acc_scr[...]

    return pl.pallas_call(
        body,
        name=name,
        out_shape=(
            jax.ShapeDtypeStruct((s, d), F32),
            jax.ShapeDtypeStruct((2, FF_BLOCKS, s, FF_BLOCK), BF16),
        ),
        grid=(s // tm, FF_BLOCKS),
        in_specs=[
            pl.BlockSpec((tm, d), lambda i, j: (i, 0)),
            pl.BlockSpec((1, d), lambda i, j: (0, 0)),
        ] + _ffn_weight_specs(layer),
        out_specs=(
            pl.BlockSpec((tm, d), lambda i, j: (i, 0)),
            pl.BlockSpec((2, None, tm, FF_BLOCK), lambda i, j: (0, j, i, 0)),
        ),
        scratch_shapes=[pltpu.VMEM((tm, d), BF16), pltpu.VMEM((tm, d), F32)],
        compiler_params=_cparams(("parallel", "arbitrary")),
    )(x, gain, w_in, w_in, w_out)


def _ffn_bwd(x, gain, w_in, w_out, layer, gu, dy, name, tm=512, tk=1024):
    s, d = x.shape
    tk = min(tk, s)

    def body(x_ref, gain_ref, wg_ref, wu_ref, wo_ref, gu_ref, dy_ref,
             dx_ref, dgain_ref, h_ref, a_ref, dgu_ref, dyb_scr, acc_scr):
        i = pl.program_id(0)
        j = pl.program_id(1)

        @pl.when(j == 0)
        def _():
            xhat, _ = _rms_stats(x_ref[...])
            h_ref[...] = (xhat * gain_ref[...]).astype(BF16)
            dyb_scr[...] = (FFN_RESID * dy_ref[...]).astype(BF16)
            acc_scr[...] = jnp.zeros_like(acc_scr)

        da = _nt(dyb_scr[...], wo_ref[...].reshape(FF_BLOCK, d))
        gv = gu_ref[0].astype(F32)
        uv = gu_ref[1].astype(F32)
        sg = _sigmoid(gv)
        sl = gv * sg
        a_ref[...] = (sl * uv).astype(BF16)
        du = (da * sl).astype(BF16)
        dg = (da * uv * (sg * (1.0 + gv * (1.0 - sg)))).astype(BF16)
        dgu_ref[0] = dg
        dgu_ref[1] = du
        acc_scr[...] += _nt(dg, wg_ref[...]) + _nt(du, wu_ref[...])

        @pl.when(j == FF_BLOCKS - 1)
        def _():
            xhat, r = _rms_stats(x_ref[...])
            dh = acc_scr[...]
            dx_ref[...] = dy_ref[...] + _rms_bwd(xhat, r, gain_ref[...], dh)
            _acc_rows(dgain_ref, dh * xhat, i == 0)

    gu_spec = pl.BlockSpec((2, None, tm, FF_BLOCK), lambda i, j: (0, j, i, 0))
    dx, dgain, h, a, dgu = pl.pallas_call(
        body,
        name=name,
        out_shape=(
            jax.ShapeDtypeStruct((s, d), F32),
            jax.ShapeDtypeStruct((1, d), F32),
            jax.ShapeDtypeStruct((s, d), BF16),
            jax.ShapeDtypeStruct((FF_BLOCKS, s, FF_BLOCK), BF16),
            jax.ShapeDtypeStruct((2, FF_BLOCKS, s, FF_BLOCK), BF16),
        ),
        grid=(s // tm, FF_BLOCKS),
        in_specs=[
            pl.BlockSpec((tm, d), lambda i, j: (i, 0)),
            pl.BlockSpec((1, d), lambda i, j: (0, 0)),
        ] + _ffn_weight_specs(layer) + [gu_spec, pl.BlockSpec((tm, d), lambda i, j: (i, 0))],
        out_specs=(
            pl.BlockSpec((tm, d), lambda i, j: (i, 0)),
            pl.BlockSpec((1, d), lambda i, j: (0, 0)),
            pl.BlockSpec((tm, d), lambda i, j: (i, 0)),
            pl.BlockSpec((None, tm, FF_BLOCK), lambda i, j: (j, i, 0)),
            gu_spec,
        ),
        scratch_shapes=[pltpu.VMEM((tm, d), BF16), pltpu.VMEM((tm, d), F32)],
        compiler_params=_cparams(("arbitrary", "arbitrary")),
    )(x, gain, w_in, w_in, w_out, gu, dy)
    nk = s // tk

    def acc_matmul(first, last, acc_ref, o_ref, val, alpha):
        @pl.when(first)
        def _():
            acc_ref[...] = jnp.zeros_like(acc_ref)

        acc_ref[...] += val

        @pl.when(last)
        def _():
            o_ref[...] = (alpha * acc_ref[...]).astype(BF16)

    def dwin_body(h_ref, dgu_ref, o_ref, acc_ref):
        kk = pl.program_id(2)
        acc_matmul(kk == 0, kk == nk - 1, acc_ref, o_ref, _tn(h_ref[...], dgu_ref[...]), 1.0)

    tmw = 512
    dw_in = pl.pallas_call(
        dwin_body,
        name=name + "_dwin",
        out_shape=jax.ShapeDtypeStruct((N_DEV, d, FF_BLOCK), BF16),
        grid=(N_DEV, d // tmw, nk),
        in_specs=[
            pl.BlockSpec((tk, tmw), lambda n, i, kk: (kk, i)),
            pl.BlockSpec((None, tk, FF_BLOCK), lambda n, i, kk: (n, kk, 0)),
        ],
        out_specs=pl.BlockSpec((None, tmw, FF_BLOCK), lambda n, i, kk: (n, i, 0)),
        scratch_shapes=[pltpu.VMEM((tmw, FF_BLOCK), F32)],
        compiler_params=_cparams(("parallel", "parallel", "arbitrary")),
    )(h, dgu.reshape(N_DEV, s, FF_BLOCK))

    def dwout_body(a_ref, dy_ref, o_ref, acc_ref):
        kk = pl.program_id(1)
        acc_matmul(kk == 0, kk == nk - 1, acc_ref, o_ref, _tn(a_ref[...], dy_ref[...]), FFN_RESID)

    dw_out = pl.pallas_call(
        dwout_body,
        name=name + "_dwout",
        out_shape=jax.ShapeDtypeStruct((FF_BLOCKS, FF_BLOCK, d), BF16),
        grid=(FF_BLOCKS, nk),
        in_specs=[
            pl.BlockSpec((None, tk, FF_BLOCK), lambda j, kk: (j, kk, 0)),
            pl.BlockSpec((tk, d), lambda j, kk: (kk, 0)),
        ],
        out_specs=pl.BlockSpec((None, FF_BLOCK, d), lambda j, kk: (j, 0, 0)),
        scratch_shapes=[pltpu.VMEM((FF_BLOCK, d), F32)],
        compiler_params=_cparams(("parallel", "arbitrary")),
    )(a, dy)
    return dx, dgain, dw_in, dw_out.reshape(N_DEV, FF_ROWS, d)


def _acc(ref, val, first):
    @pl.when(first)
    def _():
        ref[...] = val

    @pl.when(jnp.logical_not(first))
    def _():
        ref[...] += val


def _full(shape):
    nd = len(shape)
    return pl.BlockSpec(shape, lambda *_: (0,) * nd)


def _iota(shape, dim):
    return lax.broadcasted_iota(jnp.int32, shape, dim)


def _gmlp_chunk(u, v, gain, w_s, b_s):
    va = [_gelu(t) for t in v]
    ms = sum(jnp.sum(t * t, axis=-1, keepdims=True) for t in va) * (1.0 / GM_WIDTH)
    r = lax.rsqrt(ms + EPS)
    tri = _iota((GM_CHUNK, GM_CHUNK), 0) >= _iota((GM_CHUNK, GM_CHUNK), 1)
    out = []
    for g in range(GM_GROUPS):
        vn = va[g] * r * gain[g]
        sp = _nn(jnp.where(tri, w_s[g], 0.0), vn) + b_s[g]
        out.append(_gelu(u[g]) * sp)
    return out


def _gmlp_load(uv_ref, c):
    rows = pl.ds(c * GM_CHUNK, GM_CHUNK)
    u = [uv_ref[rows, pl.ds(g * LANES, LANES)] for g in range(GM_GROUPS)]
    v = [uv_ref[rows, pl.ds(GM_WIDTH + g * LANES, LANES)] for g in range(GM_GROUPS)]
    return u, v


def _gmlp_params(gain_ref, ws_ref, bs_ref):
    gain = [gain_ref[:, pl.ds(g * LANES, LANES)] for g in range(GM_GROUPS)]
    w_s = [ws_ref[g] for g in range(GM_GROUPS)]
    b_s = [bs_ref[g] for g in range(GM_GROUPS)]
    return gain, w_s, b_s


def _gmlp_fwd(proj, gain, w_s, b_s, name, tm=512):
    s = proj.shape[0]

    def body(uv_ref, gain_ref, ws_ref, bs_ref, y_ref):
        params = _gmlp_params(gain_ref, ws_ref, bs_ref)
        for c in range(tm // GM_CHUNK):
            u, v = _gmlp_load(uv_ref, c)
            y = _gmlp_chunk(u, v, *params)
            for g in range(GM_GROUPS):
                y_ref[pl.ds(c * GM_CHUNK, GM_CHUNK), pl.ds(g * LANES, LANES)] = y[g]

    return pl.pallas_call(
        body,
        name=name,
        out_shape=jax.ShapeDtypeStruct((s, GM_WIDTH), F32),
        grid=(s // tm,),
        in_specs=[
            pl.BlockSpec((tm, 2 * GM_WIDTH), lambda i: (i, P_UV // (2 * GM_WIDTH))),
            _full((1, GM_WIDTH)),
            _full((GM_GROUPS, GM_CHUNK, GM_CHUNK)),
            _full((GM_GROUPS, GM_CHUNK, 1)),
        ],
        out_specs=pl.BlockSpec((tm, GM_WIDTH), lambda i: (i, 0)),
        compiler_params=_cparams(("parallel",)),
    )(proj, gain, w_s, b_s)


def _gmlp_bwd(proj, gain, w_s, b_s, dy, name, tm=512):
    s = proj.shape[0]

    def body(uv_ref, gain_ref, ws_ref, bs_ref, dy_ref, duv_ref, dgain_ref, dws_ref, dbs_ref):
        params = _gmlp_params(gain_ref, ws_ref, bs_ref)
        dgain = dws = dbs = None
        for c in range(tm // GM_CHUNK):
            rows = pl.ds(c * GM_CHUNK, GM_CHUNK)
            u, v = _gmlp_load(uv_ref, c)
            _, vjp = jax.vjp(_gmlp_chunk, u, v, *params)
            du, dv, dg, dw, db = vjp([dy_ref[rows, pl.ds(g * LANES, LANES)] for g in range(GM_GROUPS)])
            for g in range(GM_GROUPS):
                duv_ref[rows, pl.ds(g * LANES, LANES)] = du[g].astype(BF16)
                duv_ref[rows, pl.ds(GM_WIDTH + g * LANES, LANES)] = dv[g].astype(BF16)
            if c == 0:
                dgain, dws, dbs = dg, dw, db
            else:
                dgain = [p + q for p, q in zip(dgain, dg)]
                dws = [p + q for p, q in zip(dws, dw)]
                dbs = [p + q for p, q in zip(dbs, db)]
        first = pl.program_id(0) == 0
        for g in range(GM_GROUPS):
            _acc(dgain_ref.at[:, pl.ds(g * LANES, LANES)], dgain[g], first)
            _acc(dws_ref.at[g], dws[g], first)
            _acc(dbs_ref.at[g], dbs[g], first)

    return pl.pallas_call(
        body,
        name=name,
        out_shape=(
            jax.ShapeDtypeStruct((s, 2 * GM_WIDTH), BF16),
            jax.ShapeDtypeStruct((1, GM_WIDTH), F32),
            jax.ShapeDtypeStruct((GM_GROUPS, GM_CHUNK, GM_CHUNK), F32),
            jax.ShapeDtypeStruct((GM_GROUPS, GM_CHUNK, 1), F32),
        ),
        grid=(s // tm,),
        in_specs=[
            pl.BlockSpec((tm, 2 * GM_WIDTH), lambda i: (i, P_UV // (2 * GM_WIDTH))),
            _full((1, GM_WIDTH)),
            _full((GM_GROUPS, GM_CHUNK, GM_CHUNK)),
            _full((GM_GROUPS, GM_CHUNK, 1)),
            pl.BlockSpec((tm, GM_WIDTH), lambda i: (i, 0)),
        ],
        out_specs=(
            pl.BlockSpec((tm, 2 * GM_WIDTH), lambda i: (i, 0)),
            _full((1, GM_WIDTH)),
            _full((GM_GROUPS, GM_CHUNK, GM_CHUNK)),
            _full((GM_GROUPS, GM_CHUNK, 1)),
        ),
        compiler_params=_cparams(("arbitrary",)),
    )(proj, gain, w_s, b_s, dy)


HEAD_BLOCK = MLA_HEADS * LANES


def _mla_pre(rope, cq, ckv, kr, qn_g, kvn_g, wq, wk, wv, qg, kg):
    cosf, sinf, rot = rope
    xq, _ = _rms_stats(cq)
    qn = xq * qn_g
    xk, _ = _rms_stats(ckv)
    kvn = xk * kvn_g

    def head_norm(t, gain):
        r = lax.rsqrt(jnp.sum(t * t, axis=-1, keepdims=True) * (1.0 / MLA_QK_DIM) + EPS)
        th = t * r * gain
        return th * cosf + _nn(th, rot) * sinf

    q = [head_norm(_nn(qn, wq[h]), qg) for h in range(MLA_HEADS)]
    k = [head_norm(_nn(kvn, wk[h]) + kr, kg) for h in range(MLA_HEADS)]
    v = [_nn(kvn, wv[h]) for h in range(MLA_HEADS)]
    return q, k, v


def _mla_pre_specs(tm):
    row = lambda w, off: pl.BlockSpec((tm, w), lambda i: (i, off // w))
    return [
        row(MLA_Q_RANK, P_CQ),
        row(MLA_KV_RANK, P_CKV),
        row(LANES, P_KR),
        pl.BlockSpec((tm, LANES), lambda i: (i, 0)),
        pl.BlockSpec((tm, LANES), lambda i: (i, 0)),
        _full((LANES, LANES)),
        _full((1, MLA_Q_RANK)),
        _full((1, MLA_KV_RANK)),
        _full((MLA_HEADS, MLA_Q_RANK, LANES)),
        _full((MLA_HEADS, MLA_KV_RANK, LANES)),
        _full((MLA_HEADS, MLA_KV_RANK, LANES)),
        _full((1, LANES)),
        _full((1, LANES)),
    ]


def _mla_pre_args(cq_ref, ckv_ref, kr_ref, cos_ref, sin_ref, rot_ref, qn_ref, kvn_ref, wq_ref, wk_ref, wv_ref,
                  qg_ref, kg_ref):
    heads = lambda ref: [ref[h].astype(F32) for h in range(MLA_HEADS)]
    rope = (cos_ref[...], sin_ref[...], rot_ref[...])
    args = (cq_ref[...], ckv_ref[...], kr_ref[...], qn_ref[...], kvn_ref[...], heads(wq_ref), heads(wk_ref),
            heads(wv_ref), qg_ref[...], kg_ref[...])
    return rope, args


def _mla_pre_fwd(proj, cosf, sinf, rot, qn_g, kvn_g, wq, wk, wv, qg, kg, name, tm=256):
    s = proj.shape[0]

    def body(*refs):
        q_ref, k_ref, v_ref = refs[13:]
        rope, args = _mla_pre_args(*refs[:13])
        q, k, v = _mla_pre(rope, *args)
        for h in range(MLA_HEADS):
            cols = pl.ds(h * LANES, LANES)
            q_ref[:, cols] = q[h].astype(BF16)
            k_ref[:, cols] = k[h].astype(BF16)
            v_ref[:, cols] = v[h].astype(BF16)

    out = jax.ShapeDtypeStruct((s, HEAD_BLOCK), BF16)
    blk = pl.BlockSpec((tm, HEAD_BLOCK), lambda i: (i, 0))
    return pl.pallas_call(
        body,
        name=name,
        out_shape=(out, out, out),
        grid=(s // tm,),
        in_specs=_mla_pre_specs(tm),
        out_specs=(blk, blk, blk),
        compiler_params=_cparams(("parallel",)),
    )(proj, proj, proj, cosf, sinf, rot, qn_g, kvn_g, wq, wk, wv, qg, kg)


def _mla_pre_bwd(proj, cosf, sinf, rot, qn_g, kvn_g, wq, wk, wv, qg, kg, dq, dk, dv, name, tm=256):
    s = proj.shape[0]

    def body(*refs):
        dq_ref, dk_ref, dv_ref = refs[13:16]
        dcq_ref, dckv_ref, dkr_ref, dqn_ref, dkvn_ref, dwq_ref, dwk_ref, dwv_ref, dqg_ref, dkg_ref = refs[16:]
        rope, args = _mla_pre_args(*refs[:13])
        _, vjp = jax.vjp(functools.partial(_mla_pre, rope), *args)
        heads = lambda ref: [ref[:, pl.ds(h * LANES, LANES)] for h in range(MLA_HEADS)]
        dcq, dckv, dkr, dqn, dkvn, dwq, dwk, dwv, dqg, dkg = vjp((heads(dq_ref), heads(dk_ref), heads(dv_ref)))
        dcq_ref[...] = dcq.astype(BF16)
        dckv_ref[...] = dckv.astype(BF16)
        dkr_ref[...] = dkr.astype(BF16)
        first = pl.program_id(0) == 0
        _acc(dqn_ref, dqn, first)
        _acc(dkvn_ref, dkvn, first)
        _acc(dqg_ref, dqg, first)
        _acc(dkg_ref, dkg, first)
        for h in range(MLA_HEADS):
            _acc(dwq_ref.at[h], dwq[h], first)
            _acc(dwk_ref.at[h], dwk[h], first)
            _acc(dwv_ref.at[h], dwv[h], first)

    hb = pl.BlockSpec((tm, HEAD_BLOCK), lambda i: (i, 0))
    row = lambda w: pl.BlockSpec((tm, w), lambda i: (i, 0))
    sds = jax.ShapeDtypeStruct
    return pl.pallas_call(
        body,
        name=name,
        out_shape=(
            sds((s, MLA_Q_RANK), BF16), sds((s, MLA_KV_RANK), BF16), sds((s, LANES), BF16),
            sds((1, MLA_Q_RANK), F32), sds((1, MLA_KV_RANK), F32),
            sds((MLA_HEADS, MLA_Q_RANK, LANES), F32), sds((MLA_HEADS, MLA_KV_RANK, LANES), F32),
            sds((MLA_HEADS, MLA_KV_RANK, LANES), F32),
            sds((1, LANES), F32), sds((1, LANES), F32),
        ),
        grid=(s // tm,),
        in_specs=_mla_pre_specs(tm) + [hb, hb, hb],
        out_specs=(
            row(MLA_Q_RANK), row(MLA_KV_RANK), row(LANES),
            _full((1, MLA_Q_RANK)), _full((1, MLA_KV_RANK)),
            _full((MLA_HEADS, MLA_Q_RANK, LANES)), _full((MLA_HEADS, MLA_KV_RANK, LANES)),
            _full((MLA_HEADS, MLA_KV_RANK, LANES)),
            _full((1, LANES)), _full((1, LANES)),
        ),
        compiler_params=_cparams(("arbitrary",)),
    )(proj, proj, proj, cosf, sinf, rot, qn_g, kvn_g, wq, wk, wv, qg, kg, dq, dk, dv)


ATT_SCALE = MLA_QK_DIM ** -0.5
NEG_BIG = -1e30


def _att_scores(q, k, diagonal):
    s = _nt(q, k) * ATT_SCALE
    if diagonal:
        s = jnp.where(_iota(s.shape, 0) >= _iota(s.shape, 1), s, NEG_BIG)
    return s


def _att_scores_t(k, q, diagonal):
    s = _nt(k, q) * ATT_SCALE
    if diagonal:
        s = jnp.where(_iota(s.shape, 0) <= _iota(s.shape, 1), s, NEG_BIG)
    return s


SUBLANES = 8


def _as_row(col_lanes):
    return jnp.transpose(col_lanes)[0:SUBLANES, :]


def _key_loop(lo, hi, t, step):
    def body(i, carry):
        step(pl.ds(pl.multiple_of(i * t, t), t))
        return carry

    lax.fori_loop(lo, hi, body, 0)


def _attention_fwd(q, k, v, name, t=512, gather=()):
    s = q.shape[0]
    n = s // t
    ng = len(gather)

    def body(q_ref, k_ref, v_ref, *rest):
        shard_refs, rest = rest[:ng], rest[ng:]
        o_ref, lse_ref, lse_t_ref = rest[:3]
        out_refs, rest = rest[3:3 + ng], rest[3 + ng:]
        m_scr, acc_scr = rest[:2]
        qi = pl.program_id(1)
        if ng:
            @pl.when(jnp.logical_and(pl.program_id(0) == 0, qi == 0))
            def _():
                _gather_start(_gather_plan(shard_refs, out_refs, *rest[2:]))

        lane = _iota((1, LANES), 1)
        m_scr[...] = jnp.full_like(m_scr, NEG_BIG)
        acc_scr[...] = jnp.zeros_like(acc_scr)

        def step(rows, diagonal=False):
            sc = _att_scores_t(k_ref[rows, :], q_ref[...], diagonal)
            m_old = m_scr[...]
            m_new = jnp.maximum(m_old, jnp.max(sc, axis=0, keepdims=True))
            p = jnp.exp(sc - m_new)
            vb = jnp.where(lane == MLA_V, 1.0, v_ref[rows, :].astype(F32))
            acc_scr[...] = jnp.exp(m_old - m_new) * acc_scr[...] + _tn(vb, p)
            m_scr[...] = m_new

        _key_loop(0, qi, t, step)
        step(pl.ds(pl.multiple_of(qi * t, t), t), diagonal=True)
        acc = acc_scr[...]
        row = _iota((LANES, 1), 0)
        l = jnp.sum(jnp.where(row == MLA_V, acc, 0.0), axis=0, keepdims=True)
        o_ref[...] = jnp.transpose(jnp.where(row < MLA_V, acc / l, 0.0))
        lse = jnp.broadcast_to(m_scr[...] + jnp.log(l), (LANES, t))
        lse_ref[...] = jnp.transpose(lse)
        lse_t_ref[...] = lse[0:SUBLANES, :]
        if ng:
            @pl.when(jnp.logical_and(pl.program_id(0) == MLA_HEADS - 1, qi == n - 1))
            def _():
                _gather_finish(_gather_plan(shard_refs, out_refs, *rest[2:]))

    qspec = pl.BlockSpec((t, LANES), lambda h, qi: (qi, h))
    kspec = pl.BlockSpec((s, LANES), lambda h, qi: (0, h))
    out = jax.ShapeDtypeStruct((s, HEAD_BLOCK), F32)
    res = pl.pallas_call(
        body,
        name=name,
        out_shape=(out, out, jax.ShapeDtypeStruct((MLA_HEADS * SUBLANES, s), F32)) + _gathered_shapes(gather),
        grid=(MLA_HEADS, n),
        in_specs=[qspec, kspec, kspec] + [ANY] * ng,
        out_specs=(qspec, qspec, pl.BlockSpec((SUBLANES, t), lambda h, qi: (h, qi))) + (ANY,) * ng,
        scratch_shapes=[pltpu.VMEM((1, t), F32), pltpu.VMEM((LANES, t), F32)] + (_gather_scratch(ng) if ng else []),
        compiler_params=_cparams(("arbitrary", "arbitrary") if ng else ("parallel", "parallel")),
    )(q, k, v, *gather)
    return res[0], res[1], res[2], list(res[3:])


def _host_exchange(body, n_in, n_out, parts, last):
    na = len(parts)
    if not na:
        return body

    def hosted(*refs):
        ins, part_refs = refs[:n_in], refs[n_in:n_in + na]
        outs = refs[n_in + na:n_in + na + n_out]
        got_refs = refs[n_in + na + n_out:n_in + 2 * na + n_out]
        scratch, sems = refs[n_in + 2 * na + n_out:-3], refs[-3:]
        at = lambda step: jnp.logical_and(pl.program_id(0) == step[0], pl.program_id(1) == step[1])

        @pl.when(at((0, 0)))
        def _():
            _exchange_start(_exchange_plan(part_refs, got_refs, *sems))

        body(*ins, *outs, *scratch)

        @pl.when(at(last))
        def _():
            _exchange_finish(_exchange_plan(part_refs, got_refs, *sems))

    return hosted


def _attention_bwd(q, k, v, o, lse, lse_t, do, name, t=512, exchange_dq=(), exchange_dkv=()):
    s = q.shape[0]
    n = s // t
    last = (MLA_HEADS - 1, n - 1)
    sem = lambda parts: ("arbitrary", "arbitrary") if parts else ("parallel", "parallel")

    def dq_body(q_ref, k_ref, v_ref, o_ref, lse_ref, do_ref, dq_ref, delta_t_ref, acc_scr):
        qi = pl.program_id(1)
        do = do_ref[...]
        delta = jnp.sum(do * o_ref[...], axis=-1, keepdims=True)
        delta_t_ref[...] = _as_row(jnp.broadcast_to(delta, (t, LANES)))
        acc_scr[...] = jnp.zeros_like(acc_scr)

        def step(rows, diagonal=False):
            kb = k_ref[rows, :]
            p = jnp.exp(_att_scores(q_ref[...], kb, diagonal) - lse_ref[:, 0:1])
            ds = p * (_nt(do, v_ref[rows, :]) - delta) * ATT_SCALE
            acc_scr[...] += _nn(ds, kb)

        _key_loop(0, qi, t, step)
        step(pl.ds(pl.multiple_of(qi * t, t), t), diagonal=True)
        dq_ref[...] = acc_scr[...]

    def dkv_body(q_ref, k_ref, v_ref, lse_t_ref, delta_t_ref, do_ref, dk_ref, dv_ref, dk_scr, dv_scr):
        ki = pl.program_id(1)
        dk_scr[...] = jnp.zeros_like(dk_scr)
        dv_scr[...] = jnp.zeros_like(dv_scr)

        def step(rows, diagonal=False):
            qb = q_ref[rows, :]
            dob = do_ref[rows, :]
            p = jnp.exp(_att_scores_t(k_ref[...], qb, diagonal) - lse_t_ref[0:1, rows])
            dv_scr[...] += _nn(p, dob)
            ds = p * (_nt(v_ref[...], dob) - delta_t_ref[0:1, rows]) * ATT_SCALE
            dk_scr[...] += _nn(ds, qb)

        step(pl.ds(pl.multiple_of(ki * t, t), t), diagonal=True)
        _key_loop(ki + 1, n, t, step)
        dk_ref[...] = dk_scr[...]
        dv_ref[...] = dv_scr[...]

    out = jax.ShapeDtypeStruct((s, HEAD_BLOCK), F32)
    blk = pl.BlockSpec((t, LANES), lambda h, i: (i, h))
    head = pl.BlockSpec((s, LANES), lambda h, i: (0, h))
    row_blk = pl.BlockSpec((SUBLANES, t), lambda h, i: (h, i))
    row_head = pl.BlockSpec((SUBLANES, s), lambda h, i: (h, 0))
    na, nb = len(exchange_dq), len(exchange_dkv)
    dq, delta_t, *got_dq = pl.pallas_call(
        _host_exchange(dq_body, 6, 2, exchange_dq, last),
        name=name + "_dq",
        out_shape=(out, jax.ShapeDtypeStruct((MLA_HEADS * SUBLANES, s), F32)) + _same_shapes(exchange_dq),
        grid=(MLA_HEADS, n),
        in_specs=[blk, head, head, blk, blk, blk] + [ANY] * na,
        out_specs=(blk, row_blk) + (ANY,) * na,
        scratch_shapes=[pltpu.VMEM((t, LANES), F32)] + (_exchange_scratch(na) if na else []),
        compiler_params=_cparams(sem(exchange_dq)),
    )(q, k, v, o, lse, do, *exchange_dq)
    dk, dv, *got_dkv = pl.pallas_call(
        _host_exchange(dkv_body, 6, 2, exchange_dkv, last),
        name=name + "_dkv",
        out_shape=(out, out) + _same_shapes(exchange_dkv),
        grid=(MLA_HEADS, n),
        in_specs=[head, blk, blk, row_head, row_head, head] + [ANY] * nb,
        out_specs=(blk, blk) + (ANY,) * nb,
        scratch_shapes=[pltpu.VMEM((t, LANES), F32), pltpu.VMEM((t, LANES), F32)]
        + (_exchange_scratch(nb) if nb else []),
        compiler_params=_cparams(sem(exchange_dkv)),
    )(q, k, v, lse_t, delta_t, do, *exchange_dkv)
    return dq, dk, dv, got_dq, got_dkv


HALO = 8


def _conv_fwd(proj, w, b, name, tm=512):
    s = proj.shape[0]
    cb = P_XBC // SSD_CONV_DIM

    def body(x_ref, halo_ref, w_ref, b_ref, y_ref, cat_scr):
        i = pl.program_id(0)
        cat_scr[pl.ds(0, HALO), :] = jnp.where(i > 0, halo_ref[...], 0.0)
        cat_scr[pl.ds(HALO, tm), :] = x_ref[...]
        pre = b_ref[...]
        for j in range(SSD_CONV):
            pre = pre + w_ref[pl.ds(SSD_CONV - 1 - j, 1), :] * cat_scr[pl.ds(HALO - j, tm), :]
        y_ref[...] = _silu(pre)

    return pl.pallas_call(
        body,
        name=name,
        out_shape=jax.ShapeDtypeStruct((s, SSD_CONV_DIM), F32),
        grid=(s // tm,),
        in_specs=[
            pl.BlockSpec((tm, SSD_CONV_DIM), lambda i: (i, cb)),
            pl.BlockSpec((HALO, SSD_CONV_DIM), lambda i: (jnp.maximum(i * (tm // HALO) - 1, 0), cb)),
            _full((SSD_CONV, SSD_CONV_DIM)),
            _full((1, SSD_CONV_DIM)),
        ],
        out_specs=pl.BlockSpec((tm, SSD_CONV_DIM), lambda i: (i, 0)),
        scratch_shapes=[pltpu.VMEM((tm + HALO, SSD_CONV_DIM), F32)],
        compiler_params=_cparams(("parallel",)),
    )(proj, proj, w, b)


def _conv_bwd(proj, w, b, dact, name, tm=512):
    s = proj.shape[0]
    cb = P_XBC // SSD_CONV_DIM
    n = s // tm

    def pre_body(x_ref, halo_ref, w_ref, b_ref, dact_ref, dpre_ref, dw_ref, db_ref, cat_scr):
        i = pl.program_id(0)
        cat_scr[pl.ds(0, HALO), :] = jnp.where(i > 0, halo_ref[...], 0.0)
        cat_scr[pl.ds(HALO, tm), :] = x_ref[...]
        pre = b_ref[...]
        for j in range(SSD_CONV):
            pre = pre + w_ref[pl.ds(SSD_CONV - 1 - j, 1), :] * cat_scr[pl.ds(HALO - j, tm), :]
        sg = _sigmoid(pre)
        dpre = dact_ref[...] * (sg * (1.0 + pre * (1.0 - sg)))
        dpre_ref[...] = dpre
        first = i == 0
        _acc_rows(db_ref, dpre, first)
        for j in range(SSD_CONV):
            _acc_rows(dw_ref.at[pl.ds(SSD_CONV - 1 - j, 1), :], dpre * cat_scr[pl.ds(HALO - j, tm), :], first)

    dpre, dw, db = pl.pallas_call(
        pre_body,
        name=name + "_pre",
        out_shape=(
            jax.ShapeDtypeStruct((s, SSD_CONV_DIM), F32),
            jax.ShapeDtypeStruct((SSD_CONV, SSD_CONV_DIM), F32),
            jax.ShapeDtypeStruct((1, SSD_CONV_DIM), F32),
        ),
        grid=(n,),
        in_specs=[
            pl.BlockSpec((tm, SSD_CONV_DIM), lambda i: (i, cb)),
            pl.BlockSpec((HALO, SSD_CONV_DIM), lambda i: (jnp.maximum(i * (tm // HALO) - 1, 0), cb)),
            _full((SSD_CONV, SSD_CONV_DIM)),
            _full((1, SSD_CONV_DIM)),
            pl.BlockSpec((tm, SSD_CONV_DIM), lambda i: (i, 0)),
        ],
        out_specs=(
            pl.BlockSpec((tm, SSD_CONV_DIM), lambda i: (i, 0)),
            _full((SSD_CONV, SSD_CONV_DIM)),
            _full((1, SSD_CONV_DIM)),
        ),
        scratch_shapes=[pltpu.VMEM((tm + HALO, SSD_CONV_DIM), F32)],
        compiler_params=_cparams(("arbitrary",)),
    )(proj, proj, w, b, dact)

    def dx_body(d_ref, halo_ref, w_ref, dx_ref, cat_scr):
        i = pl.program_id(0)
        cat_scr[pl.ds(0, tm), :] = d_ref[...]
        cat_scr[pl.ds(tm, HALO), :] = jnp.where(i < n - 1, halo_ref[...], 0.0)
        dx = jnp.zeros((tm, SSD_CONV_DIM), F32)
        for j in range(SSD_CONV):
            dx = dx + w_ref[pl.ds(SSD_CONV - 1 - j, 1), :] * cat_scr[pl.ds(j, tm), :]
        dx_ref[...] = dx.astype(BF16)

    dx = pl.pallas_call(
        dx_body,
        name=name + "_dx",
        out_shape=jax.ShapeDtypeStruct((s, SSD_CONV_DIM), BF16),
        grid=(n,),
        in_specs=[
            pl.BlockSpec((tm, SSD_CONV_DIM), lambda i: (i, 0)),
            pl.BlockSpec((HALO, SSD_CONV_DIM), lambda i: (jnp.minimum((i + 1) * (tm // HALO), s // HALO - 1), 0)),
            _full((SSD_CONV, SSD_CONV_DIM)),
        ],
        out_specs=pl.BlockSpec((tm, SSD_CONV_DIM), lambda i: (i, 0)),
        scratch_shapes=[pltpu.VMEM((tm + HALO, SSD_CONV_DIM), F32)],
        compiler_params=_cparams(("parallel",)),
    )(dpre, dpre, w)
    return dx, dw, db


N_PAIR = SSD_HEADS // 2


def _ssd_chunk(xs, bm, cm, z, dtp, state, dtb, alog, dskip, ng):
    t = SSD_CHUNK
    lane = _iota((1, LANES), 1)
    row = _iota((LANES, 1), 0)
    dt_all = jnp.where(lane < SSD_HEADS, _softplus(dtp + dtb), 0.0)
    da = dt_all * (-jnp.exp(alog))
    causal = _iota((t, t), 0) >= _iota((t, t), 1)
    cs = _exact_nn(causal.astype(F32), da)
    cs_t = cs.T
    tot = jnp.sum(da, axis=0, keepdims=True)
    col = lambda m, h: jnp.sum(jnp.where(lane == h, m, 0.0), axis=1, keepdims=True)
    rowv = lambda m, h: jnp.sum(jnp.where(row == h, m, 0.0), axis=0, keepdims=True)
    low = lane < SSD_HEAD_DIM
    cb = [_nt(cm[g], bm[g]) for g in range(SSD_GROUPS)]
    gated, new_state = [], []
    for j in range(N_PAIR):
        g = j // (N_PAIR // SSD_GROUPS)
        h0, h1 = 2 * j, 2 * j + 1
        y = jnp.zeros((t, LANES), F32)
        for h, mask in ((h0, low), (h1, jnp.logical_not(low))):
            lmat = jnp.exp(jnp.where(causal, col(cs, h) - rowv(cs_t, h), NEG_BIG))
            y = y + _nn(cb[g] * lmat, jnp.where(mask, xs[j] * col(dt_all, h), 0.0))
        cs_p = jnp.where(low, col(cs, h0), col(cs, h1))
        dt_p = jnp.where(low, col(dt_all, h0), col(dt_all, h1))
        tot_p = jnp.where(low, col(tot, h0), col(tot, h1))
        tot_c = jnp.where(row < SSD_HEAD_DIM, col(tot, h0), col(tot, h1))
        d_p = jnp.where(low, col(dskip, h0), col(dskip, h1))
        xdt = xs[j] * dt_p
        y = y + _nt(cm[g], state[j]) * jnp.exp(cs_p) + xs[j] * d_p
        new_state.append(state[j] * jnp.exp(tot_c) + _tn(xdt * jnp.exp(tot_p - cs_p), bm[g]))
        gated.append(y * _silu(z[j]))
    out = []
    per_group = N_PAIR // SSD_GROUPS
    for g in range(SSD_GROUPS):
        blocks = gated[g * per_group:(g + 1) * per_group]
        ms = sum(jnp.sum(v * v, axis=-1, keepdims=True) for v in blocks) * (1.0 / (per_group * LANES))
        r = lax.rsqrt(ms + EPS)
        out += [v * r * ng[g * per_group + i] for i, v in enumerate(blocks)]
    return out, new_state


def _ssd_specs(rev, nc):
    idx = (lambda c: nc - 1 - c) if rev else (lambda c: c)
    t = SSD_CHUNK
    return [
        pl.BlockSpec((t, SSD_CONV_DIM), lambda c: (idx(c), 0)),
        pl.BlockSpec((t, SSD_INNER), lambda c: (idx(c), P_Z // SSD_INNER)),
        pl.BlockSpec((t, LANES), lambda c: (idx(c), P_DT // LANES)),
        _full((1, LANES)), _full((1, LANES)), _full((1, LANES)), _full((1, SSD_INNER)),
    ]


def _ssd_args(act_ref, z_ref, dt_ref, dtb_ref, alog_ref, dskip_ref, ng_ref):
    blk = lambda ref, off, n: [ref[:, pl.ds(off + i * LANES, LANES)] for i in range(n)]
    xs = blk(act_ref, 0, N_PAIR)
    bm = blk(act_ref, SSD_INNER, SSD_GROUPS)
    cm = blk(act_ref, SSD_INNER + SSD_GROUPS * SSD_STATE, SSD_GROUPS)
    return xs, bm, cm, blk(z_ref, 0, N_PAIR), dt_ref[...], dtb_ref[...], alog_ref[...], dskip_ref[...], blk(ng_ref, 0, N_PAIR)


def _ssd_fwd(act, proj, dtb, alog, dskip, ng, name):
    s = act.shape[0]
    nc = s // SSD_CHUNK

    def body(act_ref, z_ref, dt_ref, dtb_ref, alog_ref, dskip_ref, ng_ref, y_ref, st_ref, st_scr):
        @pl.when(pl.program_id(0) == 0)
        def _():
            st_scr[...] = jnp.zeros_like(st_scr)

        xs, bm, cm, z, dtp, dtb_v, alog_v, dskip_v, ng_v = _ssd_args(act_ref, z_ref, dt_ref, dtb_ref, alog_ref, dskip_ref, ng_ref)
        state = [st_scr[j] for j in range(N_PAIR)]
        st_ref[0] = st_scr[...]
        y, new_state = _ssd_chunk(xs, bm, cm, z, dtp, state, dtb_v, alog_v, dskip_v, ng_v)
        for j in range(N_PAIR):
            y_ref[:, pl.ds(j * LANES, LANES)] = y[j]
            st_scr[j] = new_state[j]

    return pl.pallas_call(
        body,
        name=name,
        out_shape=(
            jax.ShapeDtypeStruct((s, SSD_INNER), F32),
            jax.ShapeDtypeStruct((nc, N_PAIR, LANES, SSD_STATE), F32),
        ),
        grid=(nc,),
        in_specs=_ssd_specs(False, nc),
        out_specs=(
            pl.BlockSpec((SSD_CHUNK, SSD_INNER), lambda c: (c, 0)),
            pl.BlockSpec((1, N_PAIR, LANES, SSD_STATE), lambda c: (c, 0, 0, 0)),
        ),
        scratch_shapes=[pltpu.VMEM((N_PAIR, LANES, SSD_STATE), F32)],
        compiler_params=_cparams(("arbitrary",)),
    )(act, proj, proj, dtb, alog, dskip, ng)


def _ssd_bwd(act, proj, dtb, alog, dskip, ng, states, dy, name):
    s = act.shape[0]
    nc = s // SSD_CHUNK

    def body(act_ref, z_ref, dt_ref, dtb_ref, alog_ref, dskip_ref, ng_ref, st_ref, dy_ref,
             dact_ref, dz_ref, ddt_ref, ddtb_ref, dalog_ref, ddskip_ref, dng_ref, dst_scr):
        first = pl.program_id(0) == 0

        @pl.when(first)
        def _():
            dst_scr[...] = jnp.zeros_like(dst_scr)

        xs, bm, cm, z, dtp, dtb_v, alog_v, dskip_v, ng_v = _ssd_args(act_ref, z_ref, dt_ref, dtb_ref, alog_ref, dskip_ref, ng_ref)
        state = [st_ref[0, j] for j in range(N_PAIR)]
        _, vjp = jax.vjp(_ssd_chunk, xs, bm, cm, z, dtp, state, dtb_v, alog_v, dskip_v, ng_v)
        dy_v = [dy_ref[:, pl.ds(j * LANES, LANES)] for j in range(N_PAIR)]
        dxs, dbm, dcm, dz, ddtp, dstate, ddtb, dalog, ddskip, dng = vjp((dy_v, [dst_scr[j] for j in range(N_PAIR)]))
        for i, v in enumerate(dxs + dbm + dcm):
            dact_ref[:, pl.ds(i * LANES, LANES)] = v
        for j in range(N_PAIR):
            dz_ref[:, pl.ds(j * LANES, LANES)] = dz[j].astype(BF16)
            dst_scr[j] = dstate[j]
            _acc(dng_ref.at[:, pl.ds(j * LANES, LANES)], dng[j], first)
        ddt_ref[...] = ddtp.astype(BF16)
        _acc(ddtb_ref, ddtb, first)
        _acc(dalog_ref, dalog, first)
        _acc(ddskip_ref, ddskip, first)

    rv = lambda c: nc - 1 - c
    sds = jax.ShapeDtypeStruct
    return pl.pallas_call(
        body,
        name=name,
        out_shape=(
            sds((s, SSD_CONV_DIM), F32), sds((s, SSD_INNER), BF16), sds((s, LANES), BF16),
            sds((1, LANES), F32), sds((1, LANES), F32), sds((1, LANES), F32), sds((1, SSD_INNER), F32),
        ),
        grid=(nc,),
        in_specs=_ssd_specs(True, nc) + [
            pl.BlockSpec((1, N_PAIR, LANES, SSD_STATE), lambda c: (rv(c), 0, 0, 0)),
            pl.BlockSpec((SSD_CHUNK, SSD_INNER), lambda c: (rv(c), 0)),
        ],
        out_specs=(
            pl.BlockSpec((SSD_CHUNK, SSD_CONV_DIM), lambda c: (rv(c), 0)),
            pl.BlockSpec((SSD_CHUNK, SSD_INNER), lambda c: (rv(c), 0)),
            pl.BlockSpec((SSD_CHUNK, LANES), lambda c: (rv(c), 0)),
            _full((1, LANES)), _full((1, LANES)), _full((1, LANES)), _full((1, SSD_INNER)),
        ),
        scratch_shapes=[pltpu.VMEM((N_PAIR, LANES, SSD_STATE), F32)],
        compiler_params=_cparams(("arbitrary",)),
    )(act, proj, proj, dtb, alog, dskip, ng, states, dy)


def _merge_specs(tm):
    row = lambda w: pl.BlockSpec((tm, w), lambda i: (i, 0))
    return [
        row(GM_WIDTH), row(HEAD_BLOCK), row(SSD_INNER),
        pl.BlockSpec((tm, N_BRANCH * D_MODEL), lambda i: (i, P_GATES // (N_BRANCH * D_MODEL))),
        row(D_MODEL),
        _full((GM_WIDTH, D_MODEL)), _full((HEAD_BLOCK, D_MODEL)), _full((SSD_INNER, D_MODEL)), _full((D_MODEL, D_MODEL)),
    ]


def _merge_fwd(ya, yb, yc, proj, x1, pa, pb, pc, wo, name, tm=256):
    s = x1.shape[0]

    def body(ya_ref, yb_ref, yc_ref, gates_ref, x1_ref, pa_ref, pb_ref, pc_ref, wo_ref, x2_ref, mg_ref):
        merged = jnp.zeros((tm, D_MODEL), F32)
        for i, (y_ref, p_ref) in enumerate(((ya_ref, pa_ref), (yb_ref, pb_ref), (yc_ref, pc_ref))):
            gate = _sigmoid(gates_ref[:, pl.ds(i * D_MODEL, D_MODEL)])
            merged = merged + gate * _nn(y_ref[...], p_ref[...])
        mg_ref[...] = merged.astype(BF16)
        x2_ref[...] = x1_ref[...] + _nn(merged, wo_ref[...])

    row = lambda w: pl.BlockSpec((tm, w), lambda i: (i, 0))
    return pl.pallas_call(
        body,
        name=name,
        out_shape=(jax.ShapeDtypeStruct((s, D_MODEL), F32), jax.ShapeDtypeStruct((s, D_MODEL), BF16)),
        grid=(s // tm,),
        in_specs=_merge_specs(tm),
        out_specs=(row(D_MODEL), row(D_MODEL)),
        compiler_params=_cparams(("parallel",)),
    )(ya, yb, yc, proj, x1, pa, pb, pc, wo)


def _merge_bwd(ya, yb, yc, proj, dx2, pa, pb, pc, wo, name, tm=256):
    s = dx2.shape[0]

    def body(ya_ref, yb_ref, yc_ref, gates_ref, dx2_ref, pa_ref, pb_ref, pc_ref, wo_ref,
             dya_ref, dyb_ref, dyc_ref, dgates_ref, ta_ref, tb_ref, tc_ref):
        dmerged = _nt(dx2_ref[...], wo_ref[...])
        branches = ((ya_ref, pa_ref, dya_ref, ta_ref), (yb_ref, pb_ref, dyb_ref, tb_ref), (yc_ref, pc_ref, dyc_ref, tc_ref))
        for i, (y_ref, p_ref, dy_ref, t_ref) in enumerate(branches):
            cols = pl.ds(i * D_MODEL, D_MODEL)
            gate = _sigmoid(gates_ref[:, cols])
            dgates_ref[:, cols] = (dmerged * _nn(y_ref[...], p_ref[...]) * gate * (1.0 - gate)).astype(BF16)
            dt = (dmerged * gate).astype(BF16)
            t_ref[...] = dt
            dy_ref[...] = _nt(dt, p_ref[...])

    row = lambda w: pl.BlockSpec((tm, w), lambda i: (i, 0))
    sds = jax.ShapeDtypeStruct
    return pl.pallas_call(
        body,
        name=name,
        out_shape=(
            sds((s, GM_WIDTH), F32), sds((s, HEAD_BLOCK), F32), sds((s, SSD_INNER), F32),
            sds((s, N_BRANCH * D_MODEL), BF16),
            sds((s, D_MODEL), BF16), sds((s, D_MODEL), BF16), sds((s, D_MODEL), BF16),
        ),
        grid=(s // tm,),
        in_specs=_merge_specs(tm),
        out_specs=(row(GM_WIDTH), row(HEAD_BLOCK), row(SSD_INNER), row(N_BRANCH * D_MODEL),
                   row(D_MODEL), row(D_MODEL), row(D_MODEL)),
        compiler_params=_cparams(("parallel",)),
    )(ya, yb, yc, proj, dx2, pa, pb, pc, wo)


def _loss_head(y, target, name, tm=512):
    s, d = y.shape

    def body(y_ref, t_ref, dy_ref, loss_ref):
        err = y_ref[...] - t_ref[...]
        dy_ref[...] = err * (1.0 / d)
        part = jnp.sum(jnp.sum(err * err, axis=1, keepdims=True), axis=0, keepdims=True) * (0.5 / d)
        _acc(loss_ref, jnp.broadcast_to(part, (1, LANES)), pl.program_id(0) == 0)

    return pl.pallas_call(
        body,
        name=name,
        out_shape=(jax.ShapeDtypeStruct((s, d), F32), jax.ShapeDtypeStruct((1, LANES), F32)),
        grid=(s // tm,),
        in_specs=[pl.BlockSpec((tm, d), lambda i: (i, 0)), pl.BlockSpec((tm, d), lambda i: (i, 0))],
        out_specs=(pl.BlockSpec((tm, d), lambda i: (i, 0)), _full((1, LANES))),
        compiler_params=_cparams(("arbitrary",)),
    )(y, target)


IN_SHARD = IN_COLS // N_DEV
P_OF_PIECE = (P_UV, P_CQ, P_CKV, P_KR + MLA_NOPE, P_Z, P_XBC, P_DT, P_GATES)


def _pad_lanes(w, n=LANES):
    return jnp.pad(w, [(0, 0)] * (w.ndim - 1) + [(0, n - w.shape[-1])])


def _in_proj_layout(blocks):
    def cols(i):
        a, b, out = IN_OFFSETS[i], IN_OFFSETS[i] + IN_WIDTHS[i], []
        while a < b:
            k, lo = divmod(a, IN_SHARD)
            hi = min(IN_SHARD, lo + b - a)
            out.append(blocks[k, :, lo:hi])
            a += hi - lo
        return out

    zeros = lambda n: jnp.zeros((D_MODEL, n), blocks.dtype)
    uv, cq, ckv, kr, z, xbc, dt, gates = (cols(i) for i in range(8))
    return jnp.concatenate(uv + xbc + z + [zeros(MLA_NOPE)] + kr + [zeros(LANES - MLA_QK_DIM)] + cq + gates + ckv
                           + dt + [zeros(LANES - SSD_HEADS)], axis=1)


def _in_proj_unlayout(dw):
    out = []
    for k in range(N_DEV):
        a, b, parts = k * IN_SHARD, (k + 1) * IN_SHARD, []
        for i in range(8):
            lo, hi = max(a, IN_OFFSETS[i]), min(b, IN_OFFSETS[i] + IN_WIDTHS[i])
            if lo < hi:
                at = P_OF_PIECE[i] + lo - IN_OFFSETS[i]
                parts.append(dw[:, at:at + hi - lo])
        out.append(jnp.concatenate(parts, axis=1))
    return jnp.stack(out)


def _row(v, n=None):
    v = v.reshape(1, -1)
    return v if n is None else jnp.pad(v, ((0, 0), (0, n - v.shape[1])))


GATHERED = ("ffn_w_in", "ffn_w_out", "w_in", "mla_w_uq", "mla_w_ukv", "ssd_conv_w", "w_branch", "w_out")


def _layer_shards(w, l):
    pair = lambda a, b: jnp.stack([w[a][l], w[b][l]]).astype(BF16)
    one = lambda n: w[n][l].astype(BF16)
    return [pair("ffn1_w_in", "ffn2_w_in"), pair("ffn1_w_out", "ffn2_w_out"), one("w_in"), one("mla_w_uq"),
            one("mla_w_ukv"), one("ssd_conv_w"), one("w_branch"), one("w_out")]


def _layer_weights(gathered, vec, l):
    gw = dict(zip(GATHERED, gathered))
    kv = gw["mla_w_ukv"]
    branch = jnp.moveaxis(gw["w_branch"], 0, 2).reshape(N_BRANCH, GM_WIDTH, D_MODEL)
    return dict(
        ffn_w_in=gw["ffn_w_in"], ffn_w_out=gw["ffn_w_out"], ffn1_at=0, ffn2_at=1,
        ffn1_norm=_row(vec["ffn1_norm"][l]), ffn2_norm=_row(vec["ffn2_norm"][l]),
        mix_norm=_row(vec["mix_norm"][l]), w_in=_in_proj_layout(gw["w_in"]),
        gm_v_norm=_row(vec["gm_v_norm"][l]), gm_w_s=vec["gm_w_s"][l], gm_b_s=vec["gm_b_s"][l][..., None],
        q_norm=_row(vec["mla_q_norm"][l]), kv_norm=_row(vec["mla_kv_norm"][l]),
        wq=_pad_lanes(gw["mla_w_uq"]), wk=_pad_lanes(kv[:, :, :MLA_NOPE]), wv=_pad_lanes(kv[:, :, MLA_NOPE:]),
        q_gain=_row(vec["mla_q_gain"][l], LANES), k_gain=_row(vec["mla_k_gain"][l], LANES),
        conv_w=jnp.moveaxis(gw["ssd_conv_w"], 0, 1).reshape(SSD_CONV, SSD_CONV_DIM).astype(F32),
        conv_b=_row(vec["ssd_conv_b"][l]),
        dt_bias=_row(vec["ssd_dt_bias"][l], LANES), a_log=_row(vec["ssd_a_log"][l], LANES),
        d_skip=_row(vec["ssd_d"][l], LANES), ssd_norm=_row(vec["ssd_norm"][l]),
        pa=branch[0],
        pb=jnp.pad(branch[1].reshape(MLA_HEADS, MLA_V, D_MODEL), ((0, 0), (0, LANES - MLA_V), (0, 0))).reshape(HEAD_BLOCK, D_MODEL),
        pc=branch[2], wo=gw["w_out"].reshape(D_MODEL, D_MODEL),
    )


def _layer_fwd(x, k, rope, next_shards=()):
    cosf, sinf, rot = rope
    x1, gu1 = _ffn_fwd(x, k["ffn1_norm"], k["ffn_w_in"], k["ffn_w_out"], k["ffn1_at"], "ffn1_fwd")
    h = _rmsnorm_fwd(x1, k["mix_norm"], "mix_norm_fwd")
    proj = _matmul(h, k["w_in"], "nn", F32, "in_proj_fwd", tn=2176)
    ya = _gmlp_fwd(proj, k["gm_v_norm"], k["gm_w_s"], k["gm_b_s"], "gmlp_fwd")
    q, kk, v = _mla_pre_fwd(proj, cosf, sinf, rot, k["q_norm"], k["kv_norm"], k["wq"], k["wk"], k["wv"],
                            k["q_gain"], k["k_gain"], "mla_pre_fwd")
    yb, lse, lse_t, next_gathered = _attention_fwd(q, kk, v, "attention_fwd", gather=next_shards)
    act = _conv_fwd(proj, k["conv_w"], k["conv_b"], "conv_fwd")
    yc, states = _ssd_fwd(act, proj, k["dt_bias"], k["a_log"], k["d_skip"], k["ssd_norm"], "ssd_fwd")
    x2, merged = _merge_fwd(ya, yb, yc, proj, x1, k["pa"], k["pb"], k["pc"], k["wo"], "merge_fwd")
    x3, gu2 = _ffn_fwd(x2, k["ffn2_norm"], k["ffn_w_in"], k["ffn_w_out"], k["ffn2_at"], "ffn2_fwd")
    saved = dict(x=x, x1=x1, x2=x2, gu1=gu1, gu2=gu2, h=h, proj=proj, ya=ya, yb=yb, yc=yc, q=q, k=kk, v=v,
                 lse=lse, lse_t=lse_t, act=act, states=states, merged=merged)
    return x3, saved, next_gathered


def _layer_bwd(dx3, k, sv, rope, exchange=()):
    cosf, sinf, rot = rope
    g = {}
    dx2, g["ffn2_norm"], g["ffn2_w_in"], g["ffn2_w_out"] = _ffn_bwd(
        sv["x2"], k["ffn2_norm"], k["ffn_w_in"], k["ffn_w_out"], k["ffn2_at"], sv["gu2"], dx3, "ffn2_bwd")
    proj = sv["proj"]
    dya, dyb, dyc, dgates, ta, tb, tc = _merge_bwd(sv["ya"], sv["yb"], sv["yc"], proj, dx2, k["pa"], k["pb"], k["pc"],
                                                   k["wo"], "merge_bwd")
    g["w_out"] = _matmul(sv["merged"], dx2, "tn", BF16, "w_out_grad").reshape(N_DEV, D_MODEL // N_DEV, D_MODEL)
    dpa = _matmul(sv["ya"], ta, "tn", BF16, "branch_a_grad")
    dpb = _matmul(sv["yb"], tb, "tn", BF16, "branch_b_grad")
    dpc = _matmul(sv["yc"], tc, "tn", BF16, "branch_c_grad")
    branch = jnp.stack([dpa, dpb.reshape(MLA_HEADS, LANES, D_MODEL)[:, :MLA_V].reshape(GM_WIDTH, D_MODEL), dpc])
    g["w_branch"] = jnp.moveaxis(branch.reshape(N_BRANCH, GM_WIDTH, N_DEV, LANES), 2, 0)
    duv, g["gm_v_norm"], g["gm_w_s"], dbs = _gmlp_bwd(proj, k["gm_v_norm"], k["gm_w_s"], k["gm_b_s"], dya, "gmlp_bwd")
    g["gm_b_s"] = dbs[..., 0]
    dq, dk, dv, got_a, got_b = _attention_bwd(sv["q"], sv["k"], sv["v"], sv["yb"], sv["lse"], sv["lse_t"], dyb,
                                               "attention_bwd", exchange_dq=tuple(exchange[:1]),
                                               exchange_dkv=tuple(exchange[1:]))
    dcq, dckv, dkr, dqn, dkvn, dwq, dwk, dwv, dqg, dkg = _mla_pre_bwd(
        proj, cosf, sinf, rot, k["q_norm"], k["kv_norm"], k["wq"], k["wk"], k["wv"], k["q_gain"], k["k_gain"],
        dq, dk, dv, "mla_pre_bwd")
    g["mla_q_norm"], g["mla_kv_norm"] = dqn, dkvn
    g["mla_w_uq"] = dwq[:, :, :MLA_QK_DIM].astype(BF16)
    g["mla_w_ukv"] = jnp.concatenate([dwk[:, :, :MLA_NOPE], dwv[:, :, :MLA_V]], axis=-1).astype(BF16)
    g["mla_q_gain"], g["mla_k_gain"] = dqg[:, :MLA_QK_DIM], dkg[:, :MLA_QK_DIM]
    dact, dz, ddt, ddtb, dalog, ddsk, g["ssd_norm"] = _ssd_bwd(
        sv["act"], proj, k["dt_bias"], k["a_log"], k["d_skip"], k["ssd_norm"], sv["states"], dyc, "ssd_bwd")
    g["ssd_dt_bias"], g["ssd_a_log"], g["ssd_d"] = ddtb[:, :SSD_HEADS], dalog[:, :SSD_HEADS], ddsk[:, :SSD_HEADS]
    dxbc, dcw, g["ssd_conv_b"] = _conv_bwd(proj, k["conv_w"], k["conv_b"], dact, "conv_bwd")
    g["ssd_conv_w"] = jnp.moveaxis(dcw.reshape(SSD_CONV, N_DEV, LANES), 1, 0).astype(BF16)
    dproj = jnp.concatenate([duv, dxbc, dz, dkr, dcq, dgates, dckv, ddt], axis=1)
    dh = _matmul(dproj, k["w_in"], "nt", BF16, "in_proj_dh", tk=2176)
    g["w_in"] = _in_proj_unlayout(_matmul(sv["h"], dproj, "tn", BF16, "in_proj_grad", tn=2176))
    dx1, g["mix_norm"] = _rmsnorm_bwd(sv["x1"], k["mix_norm"], dh, dx2, "mix_norm_bwd")
    dx, g["ffn1_norm"], g["ffn1_w_in"], g["ffn1_w_out"] = _ffn_bwd(
        sv["x"], k["ffn1_norm"], k["ffn_w_in"], k["ffn_w_out"], k["ffn1_at"], sv["gu1"], dx1, "ffn1_bwd")
    return dx, g, got_a + got_b


def _layer_contribs(g):
    pair = lambda a, b: jnp.stack([g[a], g[b]], axis=1)
    arrays = [pair("ffn1_w_in", "ffn2_w_in"), pair("ffn1_w_out", "ffn2_w_out"), g["w_in"], g["mla_w_uq"],
              g["mla_w_ukv"], g["ssd_conv_w"], g["w_branch"], g["w_out"]]
    return [a.reshape(N_DEV, -1, a.shape[-1]) for a in arrays]


def _chip_sums(contribs, name):
    theirs = _pair_exchange(contribs, name + "_pair_exchange")
    return [_pair_sum(c, t, "%s_pair_sum_%s" % (name, n), _tile_rows(c.shape[1], c.shape[2]))
            for n, c, t in zip(GATHERED, contribs, theirs)]


def _rope_tables(positions):
    s = positions.shape[0]
    inv_freq = 1.0 / (ROPE_THETA ** (jnp.arange(0, MLA_ROPE, 2, dtype=F32) / MLA_ROPE))
    ang = positions.astype(F32)[:, None] * inv_freq
    cos, sin = jnp.cos(ang), jnp.sin(ang)
    tail = LANES - MLA_QK_DIM
    cosf = jnp.concatenate([jnp.ones((s, MLA_NOPE), F32), cos, cos, jnp.ones((s, tail), F32)], axis=1)
    sinf = jnp.concatenate([jnp.zeros((s, MLA_NOPE), F32), sin, sin, jnp.zeros((s, tail), F32)], axis=1)
    half = MLA_ROPE // 2
    rot = np.zeros((LANES, LANES), np.float32)
    for i in range(half):
        rot[MLA_NOPE + half + i, MLA_NOPE + i] = -1.0
        rot[MLA_NOPE + i, MLA_NOPE + half + i] = 1.0
    return cosf, sinf, jnp.asarray(rot)


MATRICES = ("ffn1_w_in", "ffn1_w_out", "w_in", "mla_w_uq", "mla_w_ukv", "ssd_conv_w", "w_branch", "w_out", "ffn2_w_in",
            "ffn2_w_out")
VECTORS = ("ffn1_norm", "mix_norm", "gm_v_norm", "gm_w_s", "gm_b_s", "mla_q_norm", "mla_kv_norm", "mla_q_gain",
           "mla_k_gain", "ssd_conv_b", "ssd_dt_bias", "ssd_a_log", "ssd_d", "ssd_norm", "ffn2_norm")
WEIGHTS = ("ffn1_norm", "ffn1_w_in", "ffn1_w_out", "mix_norm", "w_in", "gm_v_norm", "gm_w_s", "gm_b_s", "mla_q_norm",
           "mla_kv_norm", "mla_w_uq", "mla_w_ukv", "mla_q_gain", "mla_k_gain", "ssd_conv_w", "ssd_conv_b", "ssd_dt_bias",
           "ssd_a_log", "ssd_d", "ssd_norm", "w_branch", "w_out", "ffn2_norm", "ffn2_w_in", "ffn2_w_out")


def _local_step(x, positions, target, vec, shards=None, gathered=None):
    rope = _rope_tables(positions)
    depth = vec["ffn1_norm"].shape[0]
    saved = []
    here = _gather_layer(shards[0], "layer0_all_gather") if gathered is None else gathered[0]
    for l in range(depth):
        k = _layer_weights(here, vec, l)
        ahead = shards[l + 1] if gathered is None and l + 1 < depth else ()
        x, sv, here = _layer_fwd(x, k, rope, ahead)
        if gathered is not None and l + 1 < depth:
            here = gathered[l + 1]
        saved.append((k, sv))
    dy, loss = _loss_head(x, target, "loss_head")
    grads, reduced, sums = [], [], ()
    for k, sv in reversed(saved):
        dy, g, got = _layer_bwd(dy, k, sv, rope, exchange=sums)
        grads.append(g)
        if gathered is None:
            if sums:
                reduced.append(got)
            sums = _chip_sums(_layer_contribs(g), "grads")
    if gathered is None:
        reduced.append(_chip_exchange(sums, "layer0_grads_chip_exchange"))
    grads.reverse()
    reduced.reverse()
    out = {n: jnp.stack([g[n].reshape(vec[n].shape[1:]) for g in grads]) for n in VECTORS}
    if gathered is None:
        out["reduced"] = reduced
    else:
        out.update({n: [g[n] for g in grads] for n in MATRICES})
    return loss[0, 0], dy, out


MESH = pl.DeviceIdType.MESH
N_CHIP = 4
ANY = pl.BlockSpec(memory_space=pl.ANY)


def _place():
    return lax.axis_index("x"), lax.axis_index("y"), lax.axis_index("c")


GATHER_COPIES = 7


def _gather_plan(shard_refs, out_refs, send_sems, recv_sems, local_sems):
    x, y, c = _place()
    me, sibling = (x, y, c), (x, y, 1 - c)
    chips = [(1 - x, y), (x, 1 - y), (1 - x, 1 - y)]
    slot = lambda px, py, pc: 4 * px + 2 * py + pc
    plans = []
    for i, (src, out) in enumerate(zip(shard_refs, out_refs)):
        def copy(k, block, to, from_shard=False, i=i, src=src, out=out):
            return pltpu.make_async_remote_copy(
                src_ref=src if from_shard else out.at[slot(*block)], dst_ref=out.at[slot(*block)],
                send_sem=send_sems.at[GATHER_COPIES * i + k], recv_sem=recv_sems.at[GATHER_COPIES * i + k],
                device_id=to, device_id_type=MESH)

        plans.append(dict(
            mine=lambda i=i, src=src, out=out: pltpu.make_async_copy(src, out.at[slot(*me)], local_sems.at[i]),
            first=lambda copy=copy: [copy(0, me, sibling, True)] + [copy(1 + j, me, (*chip, c), True)
                                                                    for j, chip in enumerate(chips)],
            arrived=lambda j, copy=copy: copy(1 + j, (*chips[j], c), me),
            passed=lambda j, copy=copy: copy(4 + j, (*chips[j], c), sibling),
            from_sibling=lambda copy=copy: [copy(0, sibling, me)] + [copy(4 + j, (*chip, 1 - c), me)
                                                                     for j, chip in enumerate(chips)],
        ))
    return plans


def _gather_start(plans):
    for p in plans:
        p["mine"]().start()
        for cp in p["first"]():
            cp.start()


def _gather_finish(plans):
    for j in range(N_CHIP - 1):
        for p in plans:
            p["arrived"](j).wait_recv()
            p["passed"](j).start()
    for p in plans:
        for cp in p["from_sibling"]():
            cp.wait_recv()
        for cp in p["first"]() + [p["passed"](j) for j in range(N_CHIP - 1)]:
            cp.wait_send()
        p["mine"]().wait()


def _gather_scratch(n):
    return [pltpu.SemaphoreType.DMA((GATHER_COPIES * n,)), pltpu.SemaphoreType.DMA((GATHER_COPIES * n,)),
            pltpu.SemaphoreType.DMA((n,))]


def _gathered_shapes(shards):
    return tuple(jax.ShapeDtypeStruct((N_DEV, *a.shape), a.dtype) for a in shards)


def _gather_layer(shards, name):
    n = len(shards)

    def body(*refs):
        plans = _gather_plan(refs[:n], refs[n:2 * n], *refs[2 * n:])
        _gather_start(plans)
        _gather_finish(plans)

    return pl.pallas_call(
        body,
        name=name,
        out_shape=_gathered_shapes(shards),
        in_specs=[ANY] * n,
        out_specs=(ANY,) * n,
        scratch_shapes=_gather_scratch(n),
    )(*shards)


def _all_gather(shard, name):
    m, n = shard.shape

    def body(x_ref, out_ref, send_sems, recv_sems, local_sem):
        x, y, c = _place()
        me, sibling = (x, y, c), (x, y, 1 - c)
        chips = [(1 - x, y), (x, 1 - y), (1 - x, 1 - y)]

        def rows(px, py, pc):
            return out_ref.at[pl.ds((4 * px + 2 * py + pc) * m, m), :]

        def copy(k, block, to, src=None):
            return pltpu.make_async_remote_copy(
                src_ref=rows(*block) if src is None else src, dst_ref=rows(*block),
                send_sem=send_sems.at[k], recv_sem=recv_sems.at[k], device_id=to, device_id_type=MESH)

        mine = pltpu.make_async_copy(x_ref, rows(*me), local_sem)
        mine.start()
        first = [copy(0, me, sibling, src=x_ref)]
        first += [copy(1 + j, me, (*chip, c), src=x_ref) for j, chip in enumerate(chips)]
        for cp in first:
            cp.start()
        passed = [copy(4 + j, (*chip, c), sibling) for j, chip in enumerate(chips)]
        for j, chip in enumerate(chips):
            copy(1 + j, (*chip, c), me).wait_recv()
            passed[j].start()
        copy(0, sibling, me).wait_recv()
        for j, chip in enumerate(chips):
            copy(4 + j, (*chip, 1 - c), me).wait_recv()
        for cp in first + passed:
            cp.wait_send()
        mine.wait()

    return pl.pallas_call(
        body,
        name=name,
        out_shape=jax.ShapeDtypeStruct((N_DEV * m, n), shard.dtype),
        in_specs=[ANY],
        out_specs=ANY,
        scratch_shapes=[pltpu.SemaphoreType.DMA((7,)), pltpu.SemaphoreType.DMA((7,)), pltpu.SemaphoreType.DMA],
    )(shard)


def _pair_exchange(contribs, name):
    na = len(contribs)

    def body(*refs):
        g_refs, got_refs, (send_sems, recv_sems) = refs[:na], refs[na:2 * na], refs[2 * na:]
        x, y, c = _place()
        remote = [pltpu.make_async_remote_copy(
            src_ref=g.at[2 * j + (1 - c)], dst_ref=got.at[j], send_sem=send_sems.at[N_CHIP * i + j],
            recv_sem=recv_sems.at[N_CHIP * i + j], device_id=(x, y, 1 - c), device_id_type=MESH)
            for i, (g, got) in enumerate(zip(g_refs, got_refs)) for j in range(N_CHIP)]
        for cp in remote:
            cp.start()
        for cp in remote:
            cp.wait()

    return pl.pallas_call(
        body,
        name=name,
        out_shape=tuple(jax.ShapeDtypeStruct((N_CHIP, *a.shape[1:]), a.dtype) for a in contribs),
        in_specs=[ANY] * na,
        out_specs=(ANY,) * na,
        scratch_shapes=[pltpu.SemaphoreType.DMA((N_CHIP * na,)), pltpu.SemaphoreType.DMA((N_CHIP * na,))],
    )(*contribs)


def _pair_sum(contrib, theirs, name, tr):
    _, r, n = contrib.shape
    side = lax.axis_index("c").astype(jnp.int32).reshape(1)

    def body(c_ref, a_ref, b_ref, o_ref):
        o_ref[...] = (a_ref[...].astype(F32) + b_ref[...].astype(F32)).astype(BF16)

    spec = pl.BlockSpec((1, tr, n), lambda j, i, c_ref: (j, i, 0))
    return pl.pallas_call(
        body,
        name=name,
        out_shape=jax.ShapeDtypeStruct(theirs.shape, BF16),
        grid_spec=pltpu.PrefetchScalarGridSpec(
            num_scalar_prefetch=1,
            grid=(N_CHIP, r // tr),
            in_specs=[pl.BlockSpec((1, tr, n), lambda j, i, c_ref: (2 * j + c_ref[0], i, 0)), spec],
            out_specs=spec,
        ),
        compiler_params=_cparams(("parallel", "parallel")),
    )(side, contrib, theirs)


OTHER_CHIPS = N_CHIP - 1


def _exchange_plan(part_refs, got_refs, send_sems, recv_sems, local_sems):
    x, y, c = _place()
    mine = 2 * x + y
    chips = [(1 - x, y), (x, 1 - y), (1 - x, 1 - y)]
    plans = []
    for i, (p, got) in enumerate(zip(part_refs, got_refs)):
        def copy(k, outbound, i=i, p=p, got=got):
            cx, cy = chips[k]
            return pltpu.make_async_remote_copy(
                src_ref=p.at[2 * cx + cy] if outbound else p.at[mine],
                dst_ref=got.at[mine] if outbound else got.at[2 * cx + cy],
                send_sem=send_sems.at[OTHER_CHIPS * i + k], recv_sem=recv_sems.at[OTHER_CHIPS * i + k],
                device_id=(cx, cy, c), device_id_type=MESH)

        plans.append(dict(copy=copy, local=lambda i=i, p=p, got=got: pltpu.make_async_copy(
            p.at[mine], got.at[mine], local_sems.at[i])))
    return plans


def _exchange_start(plans):
    for p in plans:
        p["local"]().start()
        for k in range(OTHER_CHIPS):
            p["copy"](k, True).start()


def _exchange_finish(plans):
    for p in plans:
        for k in range(OTHER_CHIPS):
            p["copy"](k, False).wait_recv()
        for k in range(OTHER_CHIPS):
            p["copy"](k, True).wait_send()
        p["local"]().wait()


def _exchange_scratch(n):
    return [pltpu.SemaphoreType.DMA((OTHER_CHIPS * n,)), pltpu.SemaphoreType.DMA((OTHER_CHIPS * n,)),
            pltpu.SemaphoreType.DMA((n,))]


def _same_shapes(arrays):
    return tuple(jax.ShapeDtypeStruct(a.shape, a.dtype) for a in arrays)


def _chip_exchange(parts, name):
    na = len(parts)

    def body(*refs):
        plans = _exchange_plan(refs[:na], refs[na:2 * na], *refs[2 * na:])
        _exchange_start(plans)
        _exchange_finish(plans)

    return pl.pallas_call(
        body,
        name=name,
        out_shape=_same_shapes(parts),
        in_specs=[ANY] * na,
        out_specs=(ANY,) * na,
        scratch_shapes=_exchange_scratch(na),
    )(*parts)


def _adamw(parts, w, m, v, name, tr, at=0):
    k = parts.shape[0]
    r, n = w.shape
    first = at // tr

    def body(p_ref, w_ref, m_ref, v_ref, g_ref, d_ref, nm_ref, nv_ref):
        g = p_ref[0].astype(F32)
        for i in range(1, k):
            g = g + p_ref[i].astype(F32)
        m_new = ADAM_B1 * m_ref[...] + (1.0 - ADAM_B1) * g
        v_new = ADAM_B2 * v_ref[...] + (1.0 - ADAM_B2) * (g * g)
        m_hat = m_new / (1.0 - ADAM_B1 ** ADAM_STEP)
        v_hat = v_new / (1.0 - ADAM_B2 ** ADAM_STEP)
        g_ref[...] = g
        d_ref[...] = -ADAM_LR * (m_hat / (jnp.sqrt(v_hat) + ADAM_EPS) + ADAM_WD * w_ref[...])
        nm_ref[...] = m_new
        nv_ref[...] = v_new

    spec = pl.BlockSpec((tr, n), lambda i: (i, 0))
    out = jax.ShapeDtypeStruct((r, n), F32)
    return pl.pallas_call(
        body,
        name=name,
        out_shape=(out, out, out, out),
        grid=(r // tr,),
        in_specs=[pl.BlockSpec((k, tr, n), lambda i: (0, i + first, 0)), spec, spec, spec],
        out_specs=(spec, spec, spec, spec),
        compiler_params=_cparams(("parallel",)),
    )(parts, w, m, v)


def _adamw_layers(parts, w, m, v, name, at=0):
    depth = len(parts)
    k = parts[0].shape[0]
    w3, m3, v3 = (a.reshape(depth, -1, a.shape[-1]) for a in (w, m, v))
    _, r, n = w3.shape
    tr = _tile_rows(r, n * depth)
    first = at // tr

    def body(*refs):
        p_refs = refs[:depth]
        w_ref, m_ref, v_ref, g_ref, d_ref, nm_ref, nv_ref = refs[depth:]
        for l in range(depth):
            @pl.when(pl.program_id(0) == l)
            def _(p_ref=p_refs[l]):
                g = p_ref[0].astype(F32)
                for i in range(1, k):
                    g = g + p_ref[i].astype(F32)
                m_new = ADAM_B1 * m_ref[...] + (1.0 - ADAM_B1) * g
                v_new = ADAM_B2 * v_ref[...] + (1.0 - ADAM_B2) * (g * g)
                m_hat = m_new / (1.0 - ADAM_B1 ** ADAM_STEP)
                v_hat = v_new / (1.0 - ADAM_B2 ** ADAM_STEP)
                g_ref[...] = g
                d_ref[...] = -ADAM_LR * (m_hat / (jnp.sqrt(v_hat) + ADAM_EPS) + ADAM_WD * w_ref[...])
                nm_ref[...] = m_new
                nv_ref[...] = v_new

    part_spec = lambda l: pl.BlockSpec((k, tr, n), lambda j, i: (0, jnp.where(j == l, i + first, first), 0))
    spec = pl.BlockSpec((None, tr, n), lambda j, i: (j, i, 0))
    out = jax.ShapeDtypeStruct(w3.shape, F32)
    res = pl.pallas_call(
        body,
        name=name,
        out_shape=(out, out, out, out),
        grid=(depth, r // tr),
        in_specs=[part_spec(l) for l in range(depth)] + [spec, spec, spec],
        out_specs=(spec, spec, spec, spec),
        compiler_params=_cparams(("parallel", "parallel")),
    )(*parts, w3, m3, v3)
    return [a.reshape(w.shape) for a in res]


TILE_BYTES = 2 * 1024 * 1024
SUBLANES_16BIT = 16


def _tile_rows(r, n):
    best = None
    for t in range(SUBLANES_16BIT, r, SUBLANES_16BIT):
        if r % t == 0 and t * n * 4 <= TILE_BYTES:
            best = t
    return best or r


def _rows(a):
    return a.reshape(-1, a.shape[-1])


def _gather_rows(shard, name):
    return _all_gather(_rows(shard), name).reshape(N_DEV, *shard.shape)


SMALL = ("ffn1_norm", "mix_norm", "gm_v_norm", "gm_b_s", "mla_q_norm", "mla_kv_norm", "mla_q_gain", "mla_k_gain",
         "ssd_conv_b", "ssd_dt_bias", "ssd_a_log", "ssd_d", "ssd_norm", "ffn2_norm")
SMALL_ROWS = 8
SMALL_COLS = 7040


def _side_by_side(d):
    cols = jnp.concatenate([d[n].reshape(d[n].shape[0], -1) for n in SMALL], axis=1)
    return jnp.pad(cols, ((0, SMALL_ROWS - cols.shape[0]), (0, SMALL_COLS - cols.shape[1])))


def _apart(packed, like):
    out, off = {}, 0
    for n in SMALL:
        size = like[n][0].size
        out[n] = packed[:like[n].shape[0], off:off + size].reshape(like[n].shape)
        off += size
    return out


def kernel(x, positions, ffn1_norm, ffn1_w_in, ffn1_w_out, mix_norm, w_in, gm_v_norm, gm_w_s, gm_b_s, mla_q_norm, mla_kv_norm, mla_w_uq, mla_w_ukv, mla_q_gain, mla_k_gain, ssd_conv_w, ssd_conv_b, ssd_dt_bias, ssd_a_log, ssd_d, ssd_norm, w_branch, w_out, ffn2_norm, ffn2_w_in, ffn2_w_out, loss_target, m_ffn1_norm, m_ffn1_w_in, m_ffn1_w_out, m_mix_norm, m_w_in, m_gm_v_norm, m_gm_w_s, m_gm_b_s, m_mla_q_norm, m_mla_kv_norm, m_mla_w_uq, m_mla_w_ukv, m_mla_q_gain, m_mla_k_gain, m_ssd_conv_w, m_ssd_conv_b, m_ssd_dt_bias, m_ssd_a_log, m_ssd_d, m_ssd_norm, m_w_branch, m_w_out, m_ffn2_norm, m_ffn2_w_in, m_ffn2_w_out, v_ffn1_norm, v_ffn1_w_in, v_ffn1_w_out, v_mix_norm, v_w_in, v_gm_v_norm, v_gm_w_s, v_gm_b_s, v_mla_q_norm, v_mla_kv_norm, v_mla_w_uq, v_mla_w_ukv, v_mla_q_gain, v_mla_k_gain, v_ssd_conv_w, v_ssd_conv_b, v_ssd_dt_bias, v_ssd_a_log, v_ssd_d, v_ssd_norm, v_w_branch, v_w_out, v_ffn2_norm, v_ffn2_w_in, v_ffn2_w_out):
    given = dict(locals())
    w = {n: given[n] for n in WEIGHTS}
    mom = {n: given["m_" + n] for n in WEIGHTS}
    var = {n: given["v_" + n] for n in WEIGHTS}
    groups = {"ffn_w_in": ("ffn1_w_in", "ffn2_w_in"), "ffn_w_out": ("ffn1_w_out", "ffn2_w_out"), "w_in": ("w_in",),
              "mla_w_uq": ("mla_w_uq",), "mla_w_ukv": ("mla_w_ukv",), "ssd_conv_w": ("ssd_conv_w",),
              "w_branch": ("w_branch",), "w_out": ("w_out",)}

    shards = [_layer_shards(w, l) for l in range(w_out.shape[0])]
    loss, dx, grads = _local_step(x[0], positions[0], loss_target[0], {n: w[n] for n in VECTORS}, shards=shards)

    outs = [{}, {}, {}, {}]
    for i, names in enumerate(groups.values()):
        parts = [layer[i] for layer in grads["reduced"]]
        at = 0
        for n in names:
            res = _adamw_layers(parts, w[n], mom[n], var[n], "adamw_" + n, at=at)
            at += w[n][0].size // w[n].shape[-1]
            for o, r in zip(outs, res):
                o[n] = r

    small_parts = _gather_rows(_side_by_side(grads), "small_grads_all_gather")
    small = _adamw(small_parts, _side_by_side(w), _side_by_side(mom), _side_by_side(var), "adamw_small", SMALL_ROWS)
    ws_parts = _gather_rows(_rows(grads["gm_w_s"]), "gm_w_s_grads_all_gather")
    ws = _adamw(ws_parts, _rows(w["gm_w_s"]), _rows(mom["gm_w_s"]), _rows(var["gm_w_s"]), "adamw_gm_w_s",
                _tile_rows(ws_parts.shape[1], LANES))
    for o, sm, r in zip(outs, small, ws):
        o.update(_apart(sm, w))
        o["gm_w_s"] = r.reshape(w["gm_w_s"].shape)

    loss = lax.psum(loss, ("x", "y", "c"))
    return (loss, dx[None], *[o[n] for o in outs for n in WEIGHTS])
```

```python
import functools

import jax
import jax.numpy as jnp
import numpy as np
from jax import lax
from jax.experimental import pallas as pl
from jax.experimental.pallas import tpu as pltpu

F32 = jnp.float32
BF16 = jnp.bfloat16

D_MODEL = 1024
DEPTH = 4
D_FF = 2816
FFN_RESID = 0.5
EPS = 1e-6
GM_WIDTH = 512
GM_GROUPS = 4
GM_CHUNK = 128
MLA_HEADS = 8
MLA_Q_RANK = 384
MLA_KV_RANK = 256
MLA_NOPE = 64
MLA_ROPE = 32
MLA_QK_DIM = 96
MLA_V = 64
ROPE_THETA = 10000.0
SSD_HEADS = 8
SSD_HEAD_DIM = 64
SSD_INNER = 512
SSD_GROUPS = 2
SSD_STATE = 128
SSD_CONV = 4
SSD_CHUNK = 128
SSD_CONV_DIM = 1024
N_BRANCH = 3
IN_WIDTHS = (1024, 384, 256, 32, 512, 1024, 8, 3072)
IN_OFFSETS = (0, 1024, 1408, 1664, 1696, 2208, 3232, 3240)
IN_COLS = 6312
LANES = 128
N_DEV = 8

ADAM_LR = 0.001
ADAM_B1 = 0.9
ADAM_B2 = 0.999
ADAM_EPS = 1e-08
ADAM_WD = 0.01
ADAM_STEP = 10

VMEM_LIMIT = 56 * 1024 * 1024

P_UV, P_XBC, P_Z, P_KR, P_CQ, P_GATES, P_CKV, P_DT = 0, 1024, 2048, 2560, 2688, 3072, 6144, 6400
P_COLS = 6528


def _cparams(sem):
    return pltpu.CompilerParams(dimension_semantics=sem, vmem_limit_bytes=VMEM_LIMIT)


def _bdot(a, b, dims):
    return lax.dot_general(a.astype(BF16), b.astype(BF16), (dims, ((), ())), preferred_element_type=F32)


@jax.custom_vjp
def _nn(a, b):
    return _bdot(a, b, ((1,), (0,)))


@jax.custom_vjp
def _nt(a, b):
    return _bdot(a, b, ((1,), (1,)))


@jax.custom_vjp
def _tn(a, b):
    return _bdot(a, b, ((0,), (0,)))


def _dot_fwd(dims):
    return lambda a, b: (_bdot(a, b, dims), (a, b))


_nn.defvjp(_dot_fwd(((1,), (0,))), lambda r, g: (_nt(g, r[1]).astype(r[0].dtype), _tn(r[0], g).astype(r[1].dtype)))
_nt.defvjp(_dot_fwd(((1,), (1,))), lambda r, g: (_nn(g, r[1]).astype(r[0].dtype), _tn(g, r[0]).astype(r[1].dtype)))
_tn.defvjp(_dot_fwd(((0,), (0,))), lambda r, g: (_nt(r[1], g).astype(r[0].dtype), _nn(r[0], g).astype(r[1].dtype)))


def _exact_nn(a, b):
    return lax.dot_general(a, b, (((1,), (0,)), ((), ())), precision=lax.Precision.HIGHEST, preferred_element_type=F32)


def _sigmoid(x):
    return 1.0 / (1.0 + jnp.exp(-x))


def _silu(x):
    return x * _sigmoid(x)


def _softplus(x):
    return jnp.maximum(x, 0.0) + jnp.log(1.0 + jnp.exp(-jnp.abs(x)))


def _gelu(x):
    return 0.5 * x * (1.0 + lax.erf(x * 0.7071067811865476))


def _pick(n, cands):
    for c in cands:
        if n % c == 0:
            return c
    return n


def _matmul(a, b, mode, out_dtype, name, alpha=1.0, tm=None, tn=None, tk=None):
    if mode == "nn":
        (m, k), (_, n) = a.shape, b.shape
    elif mode == "nt":
        (m, k), (n, _) = a.shape, b.shape
    else:
        (k, m), (_, n) = a.shape, b.shape
    tm = tm or _pick(m, (512, 384, 256, 128))
    tn = tn or _pick(n, (1024, 768, 512, 384, 256, 128))
    tk = tk or _pick(k, (1024, 512, 256, 128))
    nk = k // tk
    if mode == "nn":
        a_spec = pl.BlockSpec((tm, tk), lambda i, j, kk: (i, kk))
        b_spec = pl.BlockSpec((tk, tn), lambda i, j, kk: (kk, j))
        dot = _nn
    elif mode == "nt":
        a_spec = pl.BlockSpec((tm, tk), lambda i, j, kk: (i, kk))
        b_spec = pl.BlockSpec((tn, tk), lambda i, j, kk: (j, kk))
        dot = _nt
    else:
        a_spec = pl.BlockSpec((tk, tm), lambda i, j, kk: (kk, i))
        b_spec = pl.BlockSpec((tk, tn), lambda i, j, kk: (kk, j))
        dot = _tn

    def body(a_ref, b_ref, o_ref, acc_ref):
        kk = pl.program_id(2)

        @pl.when(kk == 0)
        def _():
            acc_ref[...] = jnp.zeros_like(acc_ref)

        acc_ref[...] += dot(a_ref[...], b_ref[...])

        @pl.when(kk == nk - 1)
        def _():
            o_ref[...] = (alpha * acc_ref[...]).astype(out_dtype)

    return pl.pallas_call(
        body,
        name=name,
        out_shape=jax.ShapeDtypeStruct((m, n), out_dtype),
        grid=(m // tm, n // tn, nk),
        in_specs=[a_spec, b_spec],
        out_specs=pl.BlockSpec((tm, tn), lambda i, j, kk: (i, j)),
        scratch_shapes=[pltpu.VMEM((tm, tn), F32)],
        compiler_params=_cparams(("parallel", "parallel", "arbitrary")),
    )(a, b)


def _rms_stats(x):
    r = lax.rsqrt(jnp.mean(x * x, axis=-1, keepdims=True) + EPS)
    return x * r, r


def _rms_bwd(xhat, r, gain, dy):
    dxhat = dy * gain
    return r * (dxhat - xhat * jnp.mean(dxhat * xhat, axis=-1, keepdims=True))


def _acc_rows(ref, val, first):
    s = jnp.sum(val, axis=0, keepdims=True)

    @pl.when(first)
    def _():
        ref[...] = s

    @pl.when(jnp.logical_not(first))
    def _():
        ref[...] += s


def _rmsnorm_fwd(x, gain, name, tm=512):
    s, d = x.shape

    def body(x_ref, g_ref, h_ref):
        xhat, _ = _rms_stats(x_ref[...])
        h_ref[...] = (xhat * g_ref[...]).astype(BF16)

    return pl.pallas_call(
        body,
        name=name,
        out_shape=jax.ShapeDtypeStruct((s, d), BF16),
        grid=(s // tm,),
        in_specs=[pl.BlockSpec((tm, d), lambda i: (i, 0)), pl.BlockSpec((1, d), lambda i: (0, 0))],
        out_specs=pl.BlockSpec((tm, d), lambda i: (i, 0)),
        compiler_params=_cparams(("parallel",)),
    )(x, gain)


def _rmsnorm_bwd(x, gain, dh, dres, name, tm=512):
    s, d = x.shape

    def body(x_ref, g_ref, dh_ref, dres_ref, dx_ref, dg_ref):
        xhat, r = _rms_stats(x_ref[...])
        dh = dh_ref[...].astype(F32)
        dx_ref[...] = dres_ref[...] + _rms_bwd(xhat, r, g_ref[...], dh)
        _acc_rows(dg_ref, dh * xhat, pl.program_id(0) == 0)

    return pl.pallas_call(
        body,
        name=name,
        out_shape=(jax.ShapeDtypeStruct((s, d), F32), jax.ShapeDtypeStruct((1, d), F32)),
        grid=(s // tm,),
        in_specs=[
            pl.BlockSpec((tm, d), lambda i: (i, 0)),
            pl.BlockSpec((1, d), lambda i: (0, 0)),
            pl.BlockSpec((tm, d), lambda i: (i, 0)),
            pl.BlockSpec((tm, d), lambda i: (i, 0)),
        ],
        out_specs=(pl.BlockSpec((tm, d), lambda i: (i, 0)), pl.BlockSpec((1, d), lambda i: (0, 0))),
        compiler_params=_cparams(("arbitrary",)),
    )(x, gain, dh, dres)


FF_BLOCK = 2 * D_FF // N_DEV
FF_BLOCKS = D_FF // FF_BLOCK
FF_ROWS = D_FF // N_DEV


def _ffn_weight_specs(layer):
    return [
        pl.BlockSpec((None, None, D_MODEL, FF_BLOCK), lambda i, j: (j, layer, 0, 0)),
        pl.BlockSpec((None, None, D_MODEL, FF_BLOCK), lambda i, j: (j + FF_BLOCKS, layer, 0, 0)),
        pl.BlockSpec((2, None, FF_ROWS, D_MODEL), lambda i, j: (j, layer, 0, 0)),
    ]


def _ffn_fwd(x, gain, w_in, w_out, layer, name, tm=512):
    s, d = x.shape

    def body(x_ref, gain_ref, wg_ref, wu_ref, wo_ref, y_ref, gu_ref, h_scr, acc_scr):
        j = pl.program_id(1)

        @pl.when(j == 0)
        def _():
            xhat, _ = _rms_stats(x_ref[...])
            h_scr[...] = (xhat * gain_ref[...]).astype(BF16)
            acc_scr[...] = jnp.zeros_like(acc_scr)

        h = h_scr[...]
        g = _nn(h, wg_ref[...])
        u = _nn(h, wu_ref[...])
        gu_ref[0] = g.astype(BF16)
        gu_ref[1] = u.astype(BF16)
        acc_scr[...] += _nn(_silu(g) * u, wo_ref[...].reshape(FF_BLOCK, d))

        @pl.when(j == FF_BLOCKS - 1)
        def _():
            y_ref[...] = x_ref[...] + FFN_RESID * acc_scr[...]

    return pl.pallas_call(
        body,
        name=name,
        out_shape=(
            jax.ShapeDtypeStruct((s, d), F32),
            jax.ShapeDtypeStruct((2, FF_BLOCKS, s, FF_BLOCK), BF16),
        ),
        grid=(s // tm, FF_BLOCKS),
        in_specs=[
            pl.BlockSpec((tm, d), lambda i, j: (i, 0)),
            pl.BlockSpec((1, d), lambda i, j: (0, 0)),
        ] + _ffn_weight_specs(layer),
        out_specs=(
            pl.BlockSpec((tm, d), lambda i, j: (i, 0)),
            pl.BlockSpec((2, None, tm, FF_BLOCK), lambda i, j: (0, j, i, 0)),
        ),
        scratch_shapes=[pltpu.VMEM((tm, d), BF16), pltpu.VMEM((tm, d), F32)],
        compiler_params=_cparams(("parallel", "arbitrary")),
    )(x, gain, w_in, w_in, w_out)


def _ffn_bwd(x, gain, w_in, w_out, layer, gu, dy, name, tm=512, tk=1024, pair=()):
    s, d = x.shape
    tk = min(tk, s)
    npair = len(pair)

    def body(x_ref, gain_ref, wg_ref, wu_ref, wo_ref, gu_ref, dy_ref,
             dx_ref, dgain_ref, h_ref, a_ref, dgu_ref, dyb_scr, acc_scr):
        i = pl.program_id(0)
        j = pl.program_id(1)

        @pl.when(j == 0)
        def _():
            xhat, _ = _rms_stats(x_ref[...])
            h_ref[...] = (xhat * gain_ref[...]).astype(BF16)
            dyb_scr[...] = (FFN_RESID * dy_ref[...]).astype(BF16)
            acc_scr[...] = jnp.zeros_like(acc_scr)

        da = _nt(dyb_scr[...], wo_ref[...].reshape(FF_BLOCK, d))
        gv = gu_ref[0].astype(F32)
        uv = gu_ref[1].astype(F32)
        sg = _sigmoid(gv)
        sl = gv * sg
        a_ref[...] = (sl * uv).astype(BF16)
        du = (da * sl).astype(BF16)
        dg = (da * uv * (sg * (1.0 + gv * (1.0 - sg)))).astype(BF16)
        dgu_ref[0] = dg
        dgu_ref[1] = du
        acc_scr[...] += _nt(dg, wg_ref[...]) + _nt(du, wu_ref[...])

        @pl.when(j == FF_BLOCKS - 1)
        def _():
            xhat, r = _rms_stats(x_ref[...])
            dh = acc_scr[...]
            dx_ref[...] = dy_ref[...] + _rms_bwd(xhat, r, gain_ref[...], dh)
            _acc_rows(dgain_ref, dh * xhat, i == 0)

    gu_spec = pl.BlockSpec((2, None, tm, FF_BLOCK), lambda i, j: (0, j, i, 0))
    dx, dgain, h, a, dgu, *theirs = pl.pallas_call(
        _host_exchange(body, 7, 5, pair, (s // tm - 1, FF_BLOCKS - 1), pair=True),
        name=name,
        out_shape=(
            jax.ShapeDtypeStruct((s, d), F32),
            jax.ShapeDtypeStruct((1, d), F32),
            jax.ShapeDtypeStruct((s, d), BF16),
            jax.ShapeDtypeStruct((FF_BLOCKS, s, FF_BLOCK), BF16),
            jax.ShapeDtypeStruct((2, FF_BLOCKS, s, FF_BLOCK), BF16),
        ) + _pair_shapes(pair),
        grid=(s // tm, FF_BLOCKS),
        in_specs=[
            pl.BlockSpec((tm, d), lambda i, j: (i, 0)),
            pl.BlockSpec((1, d), lambda i, j: (0, 0)),
        ] + _ffn_weight_specs(layer) + [gu_spec, pl.BlockSpec((tm, d), lambda i, j: (i, 0))] + [ANY] * npair,
        out_specs=(
            pl.BlockSpec((tm, d), lambda i, j: (i, 0)),
            pl.BlockSpec((1, d), lambda i, j: (0, 0)),
            pl.BlockSpec((tm, d), lambda i, j: (i, 0)),
            pl.BlockSpec((None, tm, FF_BLOCK), lambda i, j: (j, i, 0)),
            gu_spec,
        ) + (ANY,) * npair,
        scratch_shapes=[pltpu.VMEM((tm, d), BF16), pltpu.VMEM((tm, d), F32)] + (_pair_scratch(npair) if npair else []),
        compiler_params=_cparams(("arbitrary", "arbitrary")),
    )(x, gain, w_in, w_in, w_out, gu, dy, *pair)
    nk = s // tk

    def acc_matmul(first, last, acc_ref, o_ref, val, alpha):
        @pl.when(first)
        def _():
            acc_ref[...] = jnp.zeros_like(acc_ref)

        acc_ref[...] += val

        @pl.when(last)
        def _():
            o_ref[...] = (alpha * acc_ref[...]).astype(BF16)

    def dwin_body(h_ref, dgu_ref, o_ref, acc_ref):
        kk = pl.program_id(2)
        acc_matmul(kk == 0, kk == nk - 1, acc_ref, o_ref, _tn(h_ref[...], dgu_ref[...]), 1.0)

    tmw = 512
    dw_in = pl.pallas_call(
        dwin_body,
        name=name + "_dwin",
        out_shape=jax.ShapeDtypeStruct((N_DEV, d, FF_BLOCK), BF16),
        grid=(N_DEV, d // tmw, nk),
        in_specs=[
            pl.BlockSpec((tk, tmw), lambda n, i, kk: (kk, i)),
            pl.BlockSpec((None, tk, FF_BLOCK), lambda n, i, kk: (n, kk, 0)),
        ],
        out_specs=pl.BlockSpec((None, tmw, FF_BLOCK), lambda n, i, kk: (n, i, 0)),
        scratch_shapes=[pltpu.VMEM((tmw, FF_BLOCK), F32)],
        compiler_params=_cparams(("parallel", "parallel", "arbitrary")),
    )(h, dgu.reshape(N_DEV, s, FF_BLOCK))

    def dwout_body(a_ref, dy_ref, o_ref, acc_ref):
        kk = pl.program_id(1)
        acc_matmul(kk == 0, kk == nk - 1, acc_ref, o_ref, _tn(a_ref[...], dy_ref[...]), FFN_RESID)

    dw_out = pl.pallas_call(
        dwout_body,
        name=name + "_dwout",
        out_shape=jax.ShapeDtypeStruct((FF_BLOCKS, FF_BLOCK, d), BF16),
        grid=(FF_BLOCKS, nk),
        in_specs=[
            pl.BlockSpec((None, tk, FF_BLOCK), lambda j, kk: (j, kk, 0)),
            pl.BlockSpec((tk, d), lambda j, kk: (kk, 0)),
        ],
        out_specs=pl.BlockSpec((None, FF_BLOCK, d), lambda j, kk: (j, 0, 0)),
        scratch_shapes=[pltpu.VMEM((FF_BLOCK, d), F32)],
        compiler_params=_cparams(("parallel", "arbitrary")),
    )(a, dy)
    return dx, dgain, dw_in, dw_out.reshape(N_DEV, FF_ROWS, d), theirs


def _acc(ref, val, first):
    @pl.when(first)
    def _():
        ref[...] = val

    @pl.when(jnp.logical_not(first))
    def _():
        ref[...] += val


def _full(shape):
    nd = len(shape)
    return pl.BlockSpec(shape, lambda *_: (0,) * nd)


def _iota(shape, dim):
    return lax.broadcasted_iota(jnp.int32, shape, dim)


def _gmlp_chunk(u, v, gain, w_s, b_s):
    va = [_gelu(t) for t in v]
    ms = sum(jnp.sum(t * t, axis=-1, keepdims=True) for t in va) * (1.0 / GM_WIDTH)
    r = lax.rsqrt(ms + EPS)
    tri = _iota((GM_CHUNK, GM_CHUNK), 0) >= _iota((GM_CHUNK, GM_CHUNK), 1)
    out = []
    for g in range(GM_GROUPS):
        vn = va[g] * r * gain[g]
        sp = _nn(jnp.where(tri, w_s[g], 0.0), vn) + b_s[g]
        out.append(_gelu(u[g]) * sp)
    return out


def _gmlp_load(uv_ref, c):
    rows = pl.ds(c * GM_CHUNK, GM_CHUNK)
    u = [uv_ref[rows, pl.ds(g * LANES, LANES)] for g in range(GM_GROUPS)]
    v = [uv_ref[rows, pl.ds(GM_WIDTH + g * LANES, LANES)] for g in range(GM_GROUPS)]
    return u, v


def _gmlp_params(gain_ref, ws_ref, bs_ref):
    gain = [gain_ref[:, pl.ds(g * LANES, LANES)] for g in range(GM_GROUPS)]
    w_s = [ws_ref[g] for g in range(GM_GROUPS)]
    b_s = [bs_ref[g] for g in range(GM_GROUPS)]
    return gain, w_s, b_s


def _gmlp_fwd(proj, gain, w_s, b_s, name, tm=512):
    s = proj.shape[0]

    def body(uv_ref, gain_ref, ws_ref, bs_ref, y_ref):
        params = _gmlp_params(gain_ref, ws_ref, bs_ref)
        for c in range(tm // GM_CHUNK):
            u, v = _gmlp_load(uv_ref, c)
            y = _gmlp_chunk(u, v, *params)
            for g in range(GM_GROUPS):
                y_ref[pl.ds(c * GM_CHUNK, GM_CHUNK), pl.ds(g * LANES, LANES)] = y[g]

    return pl.pallas_call(
        body,
        name=name,
        out_shape=jax.ShapeDtypeStruct((s, GM_WIDTH), F32),
        grid=(s // tm,),
        in_specs=[
            pl.BlockSpec((tm, 2 * GM_WIDTH), lambda i: (i, P_UV // (2 * GM_WIDTH))),
            _full((1, GM_WIDTH)),
            _full((GM_GROUPS, GM_CHUNK, GM_CHUNK)),
            _full((GM_GROUPS, GM_CHUNK, 1)),
        ],
        out_specs=pl.BlockSpec((tm, GM_WIDTH), lambda i: (i, 0)),
        compiler_params=_cparams(("parallel",)),
    )(proj, gain, w_s, b_s)


def _gmlp_bwd(proj, gain, w_s, b_s, dy, name, tm=512):
    s = proj.shape[0]

    def body(uv_ref, gain_ref, ws_ref, bs_ref, dy_ref, duv_ref, dgain_ref, dws_ref, dbs_ref):
        params = _gmlp_params(gain_ref, ws_ref, bs_ref)
        dgain = dws = dbs = None
        for c in range(tm // GM_CHUNK):
            rows = pl.ds(c * GM_CHUNK, GM_CHUNK)
            u, v = _gmlp_load(uv_ref, c)
            _, vjp = jax.vjp(_gmlp_chunk, u, v, *params)
            du, dv, dg, dw, db = vjp([dy_ref[rows, pl.ds(g * LANES, LANES)] for g in range(GM_GROUPS)])
            for g in range(GM_GROUPS):
                duv_ref[rows, pl.ds(g * LANES, LANES)] = du[g].astype(BF16)
                duv_ref[rows, pl.ds(GM_WIDTH + g * LANES, LANES)] = dv[g].astype(BF16)
            if c == 0:
                dgain, dws, dbs = dg, dw, db
            else:
                dgain = [p + q for p, q in zip(dgain, dg)]
                dws = [p + q for p, q in zip(dws, dw)]
                dbs = [p + q for p, q in zip(dbs, db)]
        first = pl.program_id(0) == 0
        for g in range(GM_GROUPS):
            _acc(dgain_ref.at[:, pl.ds(g * LANES, LANES)], dgain[g], first)
            _acc(dws_ref.at[g], dws[g], first)
            _acc(dbs_ref.at[g], dbs[g], first)

    return pl.pallas_call(
        body,
        name=name,
        out_shape=(
            jax.ShapeDtypeStruct((s, 2 * GM_WIDTH), BF16),
            jax.ShapeDtypeStruct((1, GM_WIDTH), F32),
            jax.ShapeDtypeStruct((GM_GROUPS, GM_CHUNK, GM_CHUNK), F32),
            jax.ShapeDtypeStruct((GM_GROUPS, GM_CHUNK, 1), F32),
        ),
        grid=(s // tm,),
        in_specs=[
            pl.BlockSpec((tm, 2 * GM_WIDTH), lambda i: (i, P_UV // (2 * GM_WIDTH))),
            _full((1, GM_WIDTH)),
            _full((GM_GROUPS, GM_CHUNK, GM_CHUNK)),
            _full((GM_GROUPS, GM_CHUNK, 1)),
            pl.BlockSpec((tm, GM_WIDTH), lambda i: (i, 0)),
        ],
        out_specs=(
            pl.BlockSpec((tm, 2 * GM_WIDTH), lambda i: (i, 0)),
            _full((1, GM_WIDTH)),
            _full((GM_GROUPS, GM_CHUNK, GM_CHUNK)),
            _full((GM_GROUPS, GM_CHUNK, 1)),
        ),
        compiler_params=_cparams(("arbitrary",)),
    )(proj, gain, w_s, b_s, dy)


HEAD_BLOCK = MLA_HEADS * LANES


def _mla_heads(rope, q_all, k_all, kr, qg, kg):
    cosf, sinf, rot = rope

    def head_norm(t, gain):
        r = lax.rsqrt(jnp.sum(t * t, axis=-1, keepdims=True) * (1.0 / MLA_QK_DIM) + EPS)
        th = t * r * gain
        return th * cosf + _nn(th, rot) * sinf

    q = [head_norm(t, qg) for t in q_all]
    k = [head_norm(t + kr, kg) for t in k_all]
    return q, k


def _mla_up(refs, w_scr, up_scr):
    cq_ref, ckv_ref, qn_ref, kvn_ref, wq_ref, wk_ref, wv_ref = refs
    wq_scr, wk_scr, wv_scr = w_scr
    qa_scr, ka_scr = up_scr

    @pl.when(pl.program_id(0) == 0)
    def _():
        for h in range(MLA_HEADS):
            cols = pl.ds(h * LANES, LANES)
            wq_scr[:, cols] = wq_ref[h]
            wk_scr[:, cols] = wk_ref[h]
            wv_scr[:, cols] = wv_ref[h]

    xq, rq = _rms_stats(cq_ref[...])
    xk, rk = _rms_stats(ckv_ref[...])
    qn = (xq * qn_ref[...]).astype(BF16)
    kvn = (xk * kvn_ref[...]).astype(BF16)
    qa_scr[...] = _nn(qn, wq_scr[...])
    ka_scr[...] = _nn(kvn, wk_scr[...])
    heads = lambda scr: [scr[:, pl.ds(h * LANES, LANES)] for h in range(MLA_HEADS)]
    return (xq, rq, qn), (xk, rk, kvn), heads(qa_scr), heads(ka_scr)


def _mla_scratch(tm):
    w = lambda rank: pltpu.VMEM((rank, HEAD_BLOCK), BF16)
    up = pltpu.VMEM((tm, HEAD_BLOCK), F32)
    return [w(MLA_Q_RANK), w(MLA_KV_RANK), w(MLA_KV_RANK), up, up]


def _mla_pre_specs(tm):
    row = lambda w, off: pl.BlockSpec((tm, w), lambda i: (i, off // w))
    return [
        row(MLA_Q_RANK, P_CQ),
        row(MLA_KV_RANK, P_CKV),
        row(LANES, P_KR),
        pl.BlockSpec((tm, LANES), lambda i: (i, 0)),
        pl.BlockSpec((tm, LANES), lambda i: (i, 0)),
        _full((LANES, LANES)),
        _full((1, MLA_Q_RANK)),
        _full((1, MLA_KV_RANK)),
        _full((MLA_HEADS, MLA_Q_RANK, LANES)),
        _full((MLA_HEADS, MLA_KV_RANK, LANES)),
        _full((MLA_HEADS, MLA_KV_RANK, LANES)),
        _full((1, LANES)),
        _full((1, LANES)),
    ]


def _mla_pre_fwd(proj, cosf, sinf, rot, qn_g, kvn_g, wq, wk, wv, qg, kg, name, tm=256):
    s = proj.shape[0]

    def body(cq_ref, ckv_ref, kr_ref, cos_ref, sin_ref, rot_ref, qn_ref, kvn_ref, wq_ref, wk_ref, wv_ref, qg_ref, kg_ref,
             q_ref, k_ref, v_ref, *scr):
        _, (_, _, kvn), q_all, k_all = _mla_up((cq_ref, ckv_ref, qn_ref, kvn_ref, wq_ref, wk_ref, wv_ref), scr[:3], scr[3:])
        v_ref[...] = _nn(kvn, scr[2][...]).astype(BF16)
        rope = (cos_ref[...], sin_ref[...], rot_ref[...])
        q, k = _mla_heads(rope, q_all, k_all, kr_ref[...], qg_ref[...], kg_ref[...])
        for h in range(MLA_HEADS):
            cols = pl.ds(h * LANES, LANES)
            q_ref[:, cols] = q[h].astype(BF16)
            k_ref[:, cols] = k[h].astype(BF16)

    out = jax.ShapeDtypeStruct((s, HEAD_BLOCK), BF16)
    blk = pl.BlockSpec((tm, HEAD_BLOCK), lambda i: (i, 0))
    return pl.pallas_call(
        body,
        name=name,
        out_shape=(out, out, out),
        grid=(s // tm,),
        in_specs=_mla_pre_specs(tm),
        out_specs=(blk, blk, blk),
        scratch_shapes=_mla_scratch(tm),
        compiler_params=_cparams(("arbitrary",)),
    )(proj, proj, proj, cosf, sinf, rot, qn_g, kvn_g, wq, wk, wv, qg, kg)


def _mla_pre_bwd(proj, cosf, sinf, rot, qn_g, kvn_g, wq, wk, wv, qg, kg, dq, dk, dv, name, tm=256):
    s = proj.shape[0]

    def body(cq_ref, ckv_ref, kr_ref, cos_ref, sin_ref, rot_ref, qn_ref, kvn_ref, wq_ref, wk_ref, wv_ref, qg_ref, kg_ref,
             dq_ref, dk_ref, dv_ref,
             dcq_ref, dckv_ref, dkr_ref, dqn_ref, dkvn_ref, dwq_ref, dwk_ref, dwv_ref, dqg_ref, dkg_ref, *scr):
        (xq, rq, qn), (xk, rk, kvn), q_all, k_all = _mla_up(
            (cq_ref, ckv_ref, qn_ref, kvn_ref, wq_ref, wk_ref, wv_ref), scr[:3], scr[3:])
        wq_scr, wk_scr, wv_scr, qa_scr, ka_scr = scr
        rope = (cos_ref[...], sin_ref[...], rot_ref[...])
        _, vjp = jax.vjp(functools.partial(_mla_heads, rope), q_all, k_all, kr_ref[...], qg_ref[...], kg_ref[...])
        heads = lambda ref: [ref[:, pl.ds(h * LANES, LANES)] for h in range(MLA_HEADS)]
        dq_all, dk_all, dkr, dqg, dkg = vjp((heads(dq_ref), heads(dk_ref)))
        for h in range(MLA_HEADS):
            cols = pl.ds(h * LANES, LANES)
            qa_scr[:, cols] = dq_all[h]
            ka_scr[:, cols] = dk_all[h]
        dqa, dka, dva = qa_scr[...], ka_scr[...], dv_ref[...]
        dqn = _nt(dqa, wq_scr[...])
        dkvn = _nt(dka, wk_scr[...]) + _nt(dva, wv_scr[...])
        dcq_ref[...] = _rms_bwd(xq, rq, qn_ref[...], dqn).astype(BF16)
        dckv_ref[...] = _rms_bwd(xk, rk, kvn_ref[...], dkvn).astype(BF16)
        dkr_ref[...] = dkr.astype(BF16)
        first = pl.program_id(0) == 0
        _acc_rows(dqn_ref, dqn * xq, first)
        _acc_rows(dkvn_ref, dkvn * xk, first)
        _acc(dqg_ref, dqg, first)
        _acc(dkg_ref, dkg, first)
        dwq, dwk, dwv = _tn(qn, dqa), _tn(kvn, dka), _tn(kvn, dva)
        for h in range(MLA_HEADS):
            cols = slice(h * LANES, (h + 1) * LANES)
            _acc(dwq_ref.at[h], dwq[:, cols], first)
            _acc(dwk_ref.at[h], dwk[:, cols], first)
            _acc(dwv_ref.at[h], dwv[:, cols], first)

    hb = pl.BlockSpec((tm, HEAD_BLOCK), lambda i: (i, 0))
    row = lambda w: pl.BlockSpec((tm, w), lambda i: (i, 0))
    sds = jax.ShapeDtypeStruct
    return pl.pallas_call(
        body,
        name=name,
        out_shape=(
            sds((s, MLA_Q_RANK), BF16), sds((s, MLA_KV_RANK), BF16), sds((s, LANES), BF16),
            sds((1, MLA_Q_RANK), F32), sds((1, MLA_KV_RANK), F32),
            sds((MLA_HEADS, MLA_Q_RANK, LANES), F32), sds((MLA_HEADS, MLA_KV_RANK, LANES), F32),
            sds((MLA_HEADS, MLA_KV_RANK, LANES), F32),
            sds((1, LANES), F32), sds((1, LANES), F32),
        ),
        grid=(s // tm,),
        in_specs=_mla_pre_specs(tm) + [hb, hb, hb],
        out_specs=(
            row(MLA_Q_RANK), row(MLA_KV_RANK), row(LANES),
            _full((1, MLA_Q_RANK)), _full((1, MLA_KV_RANK)),
            _full((MLA_HEADS, MLA_Q_RANK, LANES)), _full((MLA_HEADS, MLA_KV_RANK, LANES)),
            _full((MLA_HEADS, MLA_KV_RANK, LANES)),
            _full((1, LANES)), _full((1, LANES)),
        ),
        scratch_shapes=_mla_scratch(tm),
        compiler_params=_cparams(("arbitrary",)),
    )(proj, proj, proj, cosf, sinf, rot, qn_g, kvn_g, wq, wk, wv, qg, kg, dq, dk, dv)


ATT_SCALE = MLA_QK_DIM ** -0.5
NEG_BIG = -1e30


def _att_scores(q, k, diagonal, q_at=0):
    s = _nt(q, k) * ATT_SCALE
    if diagonal:
        s = jnp.where(_iota(s.shape, 0) + q_at >= _iota(s.shape, 1), s, NEG_BIG)
    return s


def _att_scores_t(k, q, diagonal, q_at=0):
    s = _nt(k, q) * ATT_SCALE
    if diagonal:
        s = jnp.where(_iota(s.shape, 0) <= _iota(s.shape, 1) + q_at, s, NEG_BIG)
    return s


ATT_SPLIT = 1


SUBLANES = 8


def _as_row(col_lanes):
    return jnp.transpose(col_lanes)[0:SUBLANES, :]


def _key_loop(lo, hi, t, step):
    def body(i, carry):
        step(pl.ds(pl.multiple_of(i * t, t), t))
        return carry

    lax.fori_loop(lo, hi, body, 0)


def _attention_fwd(q, k, v, name, t=512, gather=()):
    s = q.shape[0]
    n = s // t
    ng = len(gather)

    def body(q_ref, k_ref, v_ref, *rest):
        shard_refs, rest = rest[:ng], rest[ng:]
        o_ref, lse_ref, lse_t_ref = rest[:3]
        out_refs, rest = rest[3:3 + ng], rest[3 + ng:]
        m_scr, acc_scr = rest[:2]
        qi = pl.program_id(1)
        if ng:
            @pl.when(jnp.logical_and(pl.program_id(0) == 0, qi == 0))
            def _():
                _gather_start(_gather_plan(shard_refs, out_refs, *rest[2:]))

        lane = _iota((1, LANES), 1)
        m_scr[...] = jnp.full_like(m_scr, NEG_BIG)
        acc_scr[...] = jnp.zeros_like(acc_scr)

        w = t // ATT_SPLIT

        def step(rows, diagonal=False):
            kb = k_ref[rows, :]
            vb = jnp.where(lane == MLA_V, 1.0, v_ref[rows, :].astype(F32)).astype(BF16)
            for c in range(ATT_SPLIT):
                cols = pl.ds(c * w, w)
                sc = _att_scores_t(kb, q_ref[cols, :], diagonal, c * w)
                m_old = m_scr[:, cols]
                m_new = jnp.maximum(m_old, jnp.max(sc, axis=0, keepdims=True))
                p = jnp.exp(sc - m_new)
                acc_scr[:, cols] = jnp.exp(m_old - m_new) * acc_scr[:, cols] + _tn(vb, p)
                m_scr[:, cols] = m_new

        _key_loop(0, qi, t, step)
        step(pl.ds(pl.multiple_of(qi * t, t), t), diagonal=True)
        acc = acc_scr[...]
        row = _iota((LANES, 1), 0)
        l = jnp.sum(jnp.where(row == MLA_V, acc, 0.0), axis=0, keepdims=True)
        o_ref[...] = jnp.transpose(jnp.where(row < MLA_V, acc / l, 0.0))
        lse = jnp.broadcast_to(m_scr[...] + jnp.log(l), (LANES, t))
        lse_ref[...] = jnp.transpose(lse)
        lse_t_ref[...] = lse[0:SUBLANES, :]
        if ng:
            @pl.when(jnp.logical_and(pl.program_id(0) == MLA_HEADS - 1, qi == n - 1))
            def _():
                _gather_finish(_gather_plan(shard_refs, out_refs, *rest[2:]))

    qspec = pl.BlockSpec((t, LANES), lambda h, qi: (qi, h))
    kspec = pl.BlockSpec((s, LANES), lambda h, qi: (0, h))
    out = jax.ShapeDtypeStruct((s, HEAD_BLOCK), F32)
    res = pl.pallas_call(
        body,
        name=name,
        out_shape=(out, out, jax.ShapeDtypeStruct((MLA_HEADS * SUBLANES, s), F32)) + _gathered_shapes(gather),
        grid=(MLA_HEADS, n),
        in_specs=[qspec, kspec, kspec] + [ANY] * ng,
        out_specs=(qspec, qspec, pl.BlockSpec((SUBLANES, t), lambda h, qi: (h, qi))) + (ANY,) * ng,
        scratch_shapes=[pltpu.VMEM((1, t), F32), pltpu.VMEM((LANES, t), F32)] + (_gather_scratch(ng) if ng else []),
        compiler_params=_cparams(("arbitrary", "arbitrary") if ng else ("parallel", "parallel")),
    )(q, k, v, *gather)
    return res[0], res[1], res[2], list(res[3:])


def _host_exchange(body, n_in, n_out, parts, last, pair=False):
    na = len(parts)
    if not na:
        return body
    plan, start, finish = (_pair_plan, _pair_start, _pair_finish) if pair else (
        _exchange_plan, _exchange_start, _exchange_finish)

    def hosted(*refs):
        ins, part_refs = refs[:n_in], refs[n_in:n_in + na]
        outs = refs[n_in + na:n_in + na + n_out]
        got_refs = refs[n_in + na + n_out:n_in + 2 * na + n_out]
        scratch, sems = refs[n_in + 2 * na + n_out:-3], refs[-3:]
        at = lambda step: jnp.logical_and(pl.program_id(0) == step[0], pl.program_id(1) == step[1])

        @pl.when(at((0, 0)))
        def _():
            start(plan(part_refs, got_refs, *sems))

        body(*ins, *outs, *scratch)

        @pl.when(at(last))
        def _():
            finish(plan(part_refs, got_refs, *sems))

    return hosted


def _attention_bwd(q, k, v, o, lse, lse_t, do, name, t=512, exchange_dq=(), exchange_dkv=()):
    s = q.shape[0]
    n = s // t
    last = (MLA_HEADS - 1, n - 1)
    sem = lambda parts: ("arbitrary", "arbitrary") if parts else ("parallel", "parallel")

    def dq_body(q_ref, k_ref, v_ref, o_ref, lse_ref, do_ref, dq_ref, delta_t_ref, acc_scr):
        qi = pl.program_id(1)
        do = do_ref[...]
        delta = jnp.sum(do * o_ref[...], axis=-1, keepdims=True)
        delta_t_ref[...] = _as_row(jnp.broadcast_to(delta, (t, LANES)))
        acc_scr[...] = jnp.zeros_like(acc_scr)

        w = t // ATT_SPLIT

        def step(rows, diagonal=False):
            kb, vb = k_ref[rows, :], v_ref[rows, :]
            for c in range(ATT_SPLIT):
                part = pl.ds(c * w, w)
                p = jnp.exp(_att_scores(q_ref[part, :], kb, diagonal, c * w) - lse_ref[part, 0:1])
                ds = p * (_nt(do_ref[part, :], vb) - delta[c * w:(c + 1) * w]) * ATT_SCALE
                acc_scr[part, :] += _nn(ds, kb)

        _key_loop(0, qi, t, step)
        step(pl.ds(pl.multiple_of(qi * t, t), t), diagonal=True)
        dq_ref[...] = acc_scr[...]

    def dkv_body(q_ref, k_ref, v_ref, lse_t_ref, delta_t_ref, do_ref, dk_ref, dv_ref, dk_scr, dv_scr):
        ki = pl.program_id(1)
        dk_scr[...] = jnp.zeros_like(dk_scr)
        dv_scr[...] = jnp.zeros_like(dv_scr)

        w = t // ATT_SPLIT

        def step(rows, diagonal=False):
            dv, dk = dv_scr[...], dk_scr[...]
            for c in range(ATT_SPLIT):
                part = pl.ds(pl.multiple_of(rows.start + c * w, w), w)
                qb = q_ref[part, :]
                dob = do_ref[part, :]
                p = jnp.exp(_att_scores_t(k_ref[...], qb, diagonal, c * w) - lse_t_ref[0:1, part])
                dv = dv + _nn(p, dob)
                ds = p * (_nt(v_ref[...], dob) - delta_t_ref[0:1, part]) * ATT_SCALE
                dk = dk + _nn(ds, qb)
            dv_scr[...] = dv
            dk_scr[...] = dk

        step(pl.ds(pl.multiple_of(ki * t, t), t), diagonal=True)
        _key_loop(ki + 1, n, t, step)
        dk_ref[...] = dk_scr[...]
        dv_ref[...] = dv_scr[...]

    out = jax.ShapeDtypeStruct((s, HEAD_BLOCK), F32)
    blk = pl.BlockSpec((t, LANES), lambda h, i: (i, h))
    head = pl.BlockSpec((s, LANES), lambda h, i: (0, h))
    row_blk = pl.BlockSpec((SUBLANES, t), lambda h, i: (h, i))
    row_head = pl.BlockSpec((SUBLANES, s), lambda h, i: (h, 0))
    na, nb = len(exchange_dq), len(exchange_dkv)
    dq, delta_t, *got_dq = pl.pallas_call(
        _host_exchange(dq_body, 6, 2, exchange_dq, last),
        name=name + "_dq",
        out_shape=(out, jax.ShapeDtypeStruct((MLA_HEADS * SUBLANES, s), F32)) + _same_shapes(exchange_dq),
        grid=(MLA_HEADS, n),
        in_specs=[blk, head, head, blk, blk, blk] + [ANY] * na,
        out_specs=(blk, row_blk) + (ANY,) * na,
        scratch_shapes=[pltpu.VMEM((t, LANES), F32)] + (_exchange_scratch(na) if na else []),
        compiler_params=_cparams(sem(exchange_dq)),
    )(q, k, v, o, lse, do, *exchange_dq)
    dk, dv, *got_dkv = pl.pallas_call(
        _host_exchange(dkv_body, 6, 2, exchange_dkv, last),
        name=name + "_dkv",
        out_shape=(out, out) + _same_shapes(exchange_dkv),
        grid=(MLA_HEADS, n),
        in_specs=[head, blk, blk, row_head, row_head, head] + [ANY] * nb,
        out_specs=(blk, blk) + (ANY,) * nb,
        scratch_shapes=[pltpu.VMEM((t, LANES), F32), pltpu.VMEM((t, LANES), F32)]
        + (_exchange_scratch(nb) if nb else []),
        compiler_params=_cparams(sem(exchange_dkv)),
    )(q, k, v, lse_t, delta_t, do, *exchange_dkv)
    return dq, dk, dv, got_dq, got_dkv


HALO = 8


def _conv_fwd(proj, w, b, name, tm=512):
    s = proj.shape[0]
    cb = P_XBC // SSD_CONV_DIM

    def body(x_ref, halo_ref, w_ref, b_ref, y_ref, cat_scr):
        i = pl.program_id(0)
        cat_scr[pl.ds(0, HALO), :] = jnp.where(i > 0, halo_ref[...], 0.0)
        cat_scr[pl.ds(HALO, tm), :] = x_ref[...]
        pre = b_ref[...]
        for j in range(SSD_CONV):
            pre = pre + w_ref[pl.ds(SSD_CONV - 1 - j, 1), :] * cat_scr[pl.ds(HALO - j, tm), :]
        y_ref[...] = _silu(pre)

    return pl.pallas_call(
        body,
        name=name,
        out_shape=jax.ShapeDtypeStruct((s, SSD_CONV_DIM), F32),
        grid=(s // tm,),
        in_specs=[
            pl.BlockSpec((tm, SSD_CONV_DIM), lambda i: (i, cb)),
            pl.BlockSpec((HALO, SSD_CONV_DIM), lambda i: (jnp.maximum(i * (tm // HALO) - 1, 0), cb)),
            _full((SSD_CONV, SSD_CONV_DIM)),
            _full((1, SSD_CONV_DIM)),
        ],
        out_specs=pl.BlockSpec((tm, SSD_CONV_DIM), lambda i: (i, 0)),
        scratch_shapes=[pltpu.VMEM((tm + HALO, SSD_CONV_DIM), F32)],
        compiler_params=_cparams(("parallel",)),
    )(proj, proj, w, b)


def _conv_bwd(proj, w, b, dact, name, tm=512):
    s = proj.shape[0]
    cb = P_XBC // SSD_CONV_DIM
    n = s // tm

    def pre_body(x_ref, halo_ref, w_ref, b_ref, dact_ref, dpre_ref, dw_ref, db_ref, cat_scr):
        i = pl.program_id(0)
        cat_scr[pl.ds(0, HALO), :] = jnp.where(i > 0, halo_ref[...], 0.0)
        cat_scr[pl.ds(HALO, tm), :] = x_ref[...]
        pre = b_ref[...]
        for j in range(SSD_CONV):
            pre = pre + w_ref[pl.ds(SSD_CONV - 1 - j, 1), :] * cat_scr[pl.ds(HALO - j, tm), :]
        sg = _sigmoid(pre)
        dpre = dact_ref[...] * (sg * (1.0 + pre * (1.0 - sg)))
        dpre_ref[...] = dpre
        first = i == 0
        _acc_rows(db_ref, dpre, first)
        for j in range(SSD_CONV):
            _acc_rows(dw_ref.at[pl.ds(SSD_CONV - 1 - j, 1), :], dpre * cat_scr[pl.ds(HALO - j, tm), :], first)

    dpre, dw, db = pl.pallas_call(
        pre_body,
        name=name + "_pre",
        out_shape=(
            jax.ShapeDtypeStruct((s, SSD_CONV_DIM), F32),
            jax.ShapeDtypeStruct((SSD_CONV, SSD_CONV_DIM), F32),
            jax.ShapeDtypeStruct((1, SSD_CONV_DIM), F32),
        ),
        grid=(n,),
        in_specs=[
            pl.BlockSpec((tm, SSD_CONV_DIM), lambda i: (i, cb)),
            pl.BlockSpec((HALO, SSD_CONV_DIM), lambda i: (jnp.maximum(i * (tm // HALO) - 1, 0), cb)),
            _full((SSD_CONV, SSD_CONV_DIM)),
            _full((1, SSD_CONV_DIM)),
            pl.BlockSpec((tm, SSD_CONV_DIM), lambda i: (i, 0)),
        ],
        out_specs=(
            pl.BlockSpec((tm, SSD_CONV_DIM), lambda i: (i, 0)),
            _full((SSD_CONV, SSD_CONV_DIM)),
            _full((1, SSD_CONV_DIM)),
        ),
        scratch_shapes=[pltpu.VMEM((tm + HALO, SSD_CONV_DIM), F32)],
        compiler_params=_cparams(("arbitrary",)),
    )(proj, proj, w, b, dact)

    def dx_body(d_ref, halo_ref, w_ref, dx_ref, cat_scr):
        i = pl.program_id(0)
        cat_scr[pl.ds(0, tm), :] = d_ref[...]
        cat_scr[pl.ds(tm, HALO), :] = jnp.where(i < n - 1, halo_ref[...], 0.0)
        dx = jnp.zeros((tm, SSD_CONV_DIM), F32)
        for j in range(SSD_CONV):
            dx = dx + w_ref[pl.ds(SSD_CONV - 1 - j, 1), :] * cat_scr[pl.ds(j, tm), :]
        dx_ref[...] = dx.astype(BF16)

    dx = pl.pallas_call(
        dx_body,
        name=name + "_dx",
        out_shape=jax.ShapeDtypeStruct((s, SSD_CONV_DIM), BF16),
        grid=(n,),
        in_specs=[
            pl.BlockSpec((tm, SSD_CONV_DIM), lambda i: (i, 0)),
            pl.BlockSpec((HALO, SSD_CONV_DIM), lambda i: (jnp.minimum((i + 1) * (tm // HALO), s // HALO - 1), 0)),
            _full((SSD_CONV, SSD_CONV_DIM)),
        ],
        out_specs=pl.BlockSpec((tm, SSD_CONV_DIM), lambda i: (i, 0)),
        scratch_shapes=[pltpu.VMEM((tm + HALO, SSD_CONV_DIM), F32)],
        compiler_params=_cparams(("parallel",)),
    )(dpre, dpre, w)
    return dx, dw, db


N_PAIR = SSD_HEADS // 2


def _ssd_chunk(xs, bm, cm, z, dtp, state, dtb, alog, dskip, ng):
    t = SSD_CHUNK
    lane = _iota((1, LANES), 1)
    row = _iota((LANES, 1), 0)
    dt_all = jnp.where(lane < SSD_HEADS, _softplus(dtp + dtb), 0.0)
    da = dt_all * (-jnp.exp(alog))
    causal = _iota((t, t), 0) >= _iota((t, t), 1)
    cs = _exact_nn(causal.astype(F32), da)
    cs_t = cs.T
    tot = jnp.sum(da, axis=0, keepdims=True)
    col = lambda m, h: jnp.sum(jnp.where(lane == h, m, 0.0), axis=1, keepdims=True)
    rowv = lambda m, h: jnp.sum(jnp.where(row == h, m, 0.0), axis=0, keepdims=True)
    low = lane < SSD_HEAD_DIM
    cb = [_nt(cm[g], bm[g]) for g in range(SSD_GROUPS)]
    gated, new_state = [], []
    for j in range(N_PAIR):
        g = j // (N_PAIR // SSD_GROUPS)
        h0, h1 = 2 * j, 2 * j + 1
        y = jnp.zeros((t, LANES), F32)
        for h, mask in ((h0, low), (h1, jnp.logical_not(low))):
            lmat = jnp.exp(jnp.where(causal, col(cs, h) - rowv(cs_t, h), NEG_BIG))
            y = y + _nn(cb[g] * lmat, jnp.where(mask, xs[j] * col(dt_all, h), 0.0))
        cs_p = jnp.where(low, col(cs, h0), col(cs, h1))
        dt_p = jnp.where(low, col(dt_all, h0), col(dt_all, h1))
        tot_p = jnp.where(low, col(tot, h0), col(tot, h1))
        tot_c = jnp.where(row < SSD_HEAD_DIM, col(tot, h0), col(tot, h1))
        d_p = jnp.where(low, col(dskip, h0), col(dskip, h1))
        xdt = xs[j] * dt_p
        y = y + _nt(cm[g], state[j]) * jnp.exp(cs_p) + xs[j] * d_p
        new_state.append(state[j] * jnp.exp(tot_c) + _tn(xdt * jnp.exp(tot_p - cs_p), bm[g]))
        gated.append(y * _silu(z[j]))
    out = []
    per_group = N_PAIR // SSD_GROUPS
    for g in range(SSD_GROUPS):
        blocks = gated[g * per_group:(g + 1) * per_group]
        ms = sum(jnp.sum(v * v, axis=-1, keepdims=True) for v in blocks) * (1.0 / (per_group * LANES))
        r = lax.rsqrt(ms + EPS)
        out += [v * r * ng[g * per_group + i] for i, v in enumerate(blocks)]
    return out, new_state


def _ssd_specs(rev, nc):
    idx = (lambda c: nc - 1 - c) if rev else (lambda c: c)
    t = SSD_CHUNK
    return [
        pl.BlockSpec((t, SSD_CONV_DIM), lambda c: (idx(c), 0)),
        pl.BlockSpec((t, SSD_INNER), lambda c: (idx(c), P_Z // SSD_INNER)),
        pl.BlockSpec((t, LANES), lambda c: (idx(c), P_DT // LANES)),
        _full((1, LANES)), _full((1, LANES)), _full((1, LANES)), _full((1, SSD_INNER)),
    ]


def _ssd_args(act_ref, z_ref, dt_ref, dtb_ref, alog_ref, dskip_ref, ng_ref):
    blk = lambda ref, off, n: [ref[:, pl.ds(off + i * LANES, LANES)] for i in range(n)]
    xs = blk(act_ref, 0, N_PAIR)
    bm = blk(act_ref, SSD_INNER, SSD_GROUPS)
    cm = blk(act_ref, SSD_INNER + SSD_GROUPS * SSD_STATE, SSD_GROUPS)
    return xs, bm, cm, blk(z_ref, 0, N_PAIR), dt_ref[...], dtb_ref[...], alog_ref[...], dskip_ref[...], blk(ng_ref, 0, N_PAIR)


def _ssd_fwd(act, proj, dtb, alog, dskip, ng, name):
    s = act.shape[0]
    nc = s // SSD_CHUNK

    def body(act_ref, z_ref, dt_ref, dtb_ref, alog_ref, dskip_ref, ng_ref, y_ref, st_ref, st_scr):
        @pl.when(pl.program_id(0) == 0)
        def _():
            st_scr[...] = jnp.zeros_like(st_scr)

        xs, bm, cm, z, dtp, dtb_v, alog_v, dskip_v, ng_v = _ssd_args(act_ref, z_ref, dt_ref, dtb_ref, alog_ref, dskip_ref, ng_ref)
        state = [st_scr[j] for j in range(N_PAIR)]
        st_ref[0] = st_scr[...]
        y, new_state = _ssd_chunk(xs, bm, cm, z, dtp, state, dtb_v, alog_v, dskip_v, ng_v)
        for j in range(N_PAIR):
            y_ref[:, pl.ds(j * LANES, LANES)] = y[j]
            st_scr[j] = new_state[j]

    return pl.pallas_call(
        body,
        name=name,
        out_shape=(
            jax.ShapeDtypeStruct((s, SSD_INNER), F32),
            jax.ShapeDtypeStruct((nc, N_PAIR, LANES, SSD_STATE), F32),
        ),
        grid=(nc,),
        in_specs=_ssd_specs(False, nc),
        out_specs=(
            pl.BlockSpec((SSD_CHUNK, SSD_INNER), lambda c: (c, 0)),
            pl.BlockSpec((1, N_PAIR, LANES, SSD_STATE), lambda c: (c, 0, 0, 0)),
        ),
        scratch_shapes=[pltpu.VMEM((N_PAIR, LANES, SSD_STATE), F32)],
        compiler_params=_cparams(("arbitrary",)),
    )(act, proj, proj, dtb, alog, dskip, ng)


def _ssd_bwd(act, proj, dtb, alog, dskip, ng, states, dy, name):
    s = act.shape[0]
    nc = s // SSD_CHUNK

    def body(act_ref, z_ref, dt_ref, dtb_ref, alog_ref, dskip_ref, ng_ref, st_ref, dy_ref,
             dact_ref, dz_ref, ddt_ref, ddtb_ref, dalog_ref, ddskip_ref, dng_ref, dst_scr):
        first = pl.program_id(0) == 0

        @pl.when(first)
        def _():
            dst_scr[...] = jnp.zeros_like(dst_scr)

        xs, bm, cm, z, dtp, dtb_v, alog_v, dskip_v, ng_v = _ssd_args(act_ref, z_ref, dt_ref, dtb_ref, alog_ref, dskip_ref, ng_ref)
        state = [st_ref[0, j] for j in range(N_PAIR)]
        _, vjp = jax.vjp(_ssd_chunk, xs, bm, cm, z, dtp, state, dtb_v, alog_v, dskip_v, ng_v)
        dy_v = [dy_ref[:, pl.ds(j * LANES, LANES)] for j in range(N_PAIR)]
        dxs, dbm, dcm, dz, ddtp, dstate, ddtb, dalog, ddskip, dng = vjp((dy_v, [dst_scr[j] for j in range(N_PAIR)]))
        for i, v in enumerate(dxs + dbm + dcm):
            dact_ref[:, pl.ds(i * LANES, LANES)] = v
        for j in range(N_PAIR):
            dz_ref[:, pl.ds(j * LANES, LANES)] = dz[j].astype(BF16)
            dst_scr[j] = dstate[j]
            _acc(dng_ref.at[:, pl.ds(j * LANES, LANES)], dng[j], first)
        ddt_ref[...] = ddtp.astype(BF16)
        _acc(ddtb_ref, ddtb, first)
        _acc(dalog_ref, dalog, first)
        _acc(ddskip_ref, ddskip, first)

    rv = lambda c: nc - 1 - c
    sds = jax.ShapeDtypeStruct
    return pl.pallas_call(
        body,
        name=name,
        out_shape=(
            sds((s, SSD_CONV_DIM), F32), sds((s, SSD_INNER), BF16), sds((s, LANES), BF16),
            sds((1, LANES), F32), sds((1, LANES), F32), sds((1, LANES), F32), sds((1, SSD_INNER), F32),
        ),
        grid=(nc,),
        in_specs=_ssd_specs(True, nc) + [
            pl.BlockSpec((1, N_PAIR, LANES, SSD_STATE), lambda c: (rv(c), 0, 0, 0)),
            pl.BlockSpec((SSD_CHUNK, SSD_INNER), lambda c: (rv(c), 0)),
        ],
        out_specs=(
            pl.BlockSpec((SSD_CHUNK, SSD_CONV_DIM), lambda c: (rv(c), 0)),
            pl.BlockSpec((SSD_CHUNK, SSD_INNER), lambda c: (rv(c), 0)),
            pl.BlockSpec((SSD_CHUNK, LANES), lambda c: (rv(c), 0)),
            _full((1, LANES)), _full((1, LANES)), _full((1, LANES)), _full((1, SSD_INNER)),
        ),
        scratch_shapes=[pltpu.VMEM((N_PAIR, LANES, SSD_STATE), F32)],
        compiler_params=_cparams(("arbitrary",)),
    )(act, proj, proj, dtb, alog, dskip, ng, states, dy)


def _merge_specs(tm):
    row = lambda w: pl.BlockSpec((tm, w), lambda i: (i, 0))
    return [
        row(GM_WIDTH), row(HEAD_BLOCK), row(SSD_INNER),
        pl.BlockSpec((tm, N_BRANCH * D_MODEL), lambda i: (i, P_GATES // (N_BRANCH * D_MODEL))),
        row(D_MODEL),
        _full((GM_WIDTH, D_MODEL)), _full((HEAD_BLOCK, D_MODEL)), _full((SSD_INNER, D_MODEL)), _full((D_MODEL, D_MODEL)),
    ]


def _merge_fwd(ya, yb, yc, proj, x1, pa, pb, pc, wo, name, tm=256):
    s = x1.shape[0]

    def body(ya_ref, yb_ref, yc_ref, gates_ref, x1_ref, pa_ref, pb_ref, pc_ref, wo_ref, x2_ref, mg_ref):
        merged = jnp.zeros((tm, D_MODEL), F32)
        for i, (y_ref, p_ref) in enumerate(((ya_ref, pa_ref), (yb_ref, pb_ref), (yc_ref, pc_ref))):
            gate = _sigmoid(gates_ref[:, pl.ds(i * D_MODEL, D_MODEL)])
            merged = merged + gate * _nn(y_ref[...], p_ref[...])
        mg_ref[...] = merged.astype(BF16)
        x2_ref[...] = x1_ref[...] + _nn(merged, wo_ref[...])

    row = lambda w: pl.BlockSpec((tm, w), lambda i: (i, 0))
    return pl.pallas_call(
        body,
        name=name,
        out_shape=(jax.ShapeDtypeStruct((s, D_MODEL), F32), jax.ShapeDtypeStruct((s, D_MODEL), BF16)),
        grid=(s // tm,),
        in_specs=_merge_specs(tm),
        out_specs=(row(D_MODEL), row(D_MODEL)),
        compiler_params=_cparams(("parallel",)),
    )(ya, yb, yc, proj, x1, pa, pb, pc, wo)


def _merge_bwd(ya, yb, yc, proj, dx2, pa, pb, pc, wo, name, tm=256):
    s = dx2.shape[0]

    def body(ya_ref, yb_ref, yc_ref, gates_ref, dx2_ref, pa_ref, pb_ref, pc_ref, wo_ref,
             dya_ref, dyb_ref, dyc_ref, dgates_ref, ta_ref, tb_ref, tc_ref):
        dmerged = _nt(dx2_ref[...], wo_ref[...])
        branches = ((ya_ref, pa_ref, dya_ref, ta_ref), (yb_ref, pb_ref, dyb_ref, tb_ref), (yc_ref, pc_ref, dyc_ref, tc_ref))
        for i, (y_ref, p_ref, dy_ref, t_ref) in enumerate(branches):
            cols = pl.ds(i * D_MODEL, D_MODEL)
            gate = _sigmoid(gates_ref[:, cols])
            dgates_ref[:, cols] = (dmerged * _nn(y_ref[...], p_ref[...]) * gate * (1.0 - gate)).astype(BF16)
            dt = (dmerged * gate).astype(BF16)
            t_ref[...] = dt
            dy_ref[...] = _nt(dt, p_ref[...])

    row = lambda w: pl.BlockSpec((tm, w), lambda i: (i, 0))
    sds = jax.ShapeDtypeStruct
    return pl.pallas_call(
        body,
        name=name,
        out_shape=(
            sds((s, GM_WIDTH), F32), sds((s, HEAD_BLOCK), F32), sds((s, SSD_INNER), F32),
            sds((s, N_BRANCH * D_MODEL), BF16),
            sds((s, D_MODEL), BF16), sds((s, D_MODEL), BF16), sds((s, D_MODEL), BF16),
        ),
        grid=(s // tm,),
        in_specs=_merge_specs(tm),
        out_specs=(row(GM_WIDTH), row(HEAD_BLOCK), row(SSD_INNER), row(N_BRANCH * D_MODEL),
                   row(D_MODEL), row(D_MODEL), row(D_MODEL)),
        compiler_params=_cparams(("parallel",)),
    )(ya, yb, yc, proj, dx2, pa, pb, pc, wo)


def _loss_head(y, target, name, tm=512):
    s, d = y.shape

    def body(y_ref, t_ref, dy_ref, loss_ref):
        err = y_ref[...] - t_ref[...]
        dy_ref[...] = err * (1.0 / d)
        part = jnp.sum(jnp.sum(err * err, axis=1, keepdims=True), axis=0, keepdims=True) * (0.5 / d)
        _acc(loss_ref, jnp.broadcast_to(part, (1, LANES)), pl.program_id(0) == 0)

    return pl.pallas_call(
        body,
        name=name,
        out_shape=(jax.ShapeDtypeStruct((s, d), F32), jax.ShapeDtypeStruct((1, LANES), F32)),
        grid=(s // tm,),
        in_specs=[pl.BlockSpec((tm, d), lambda i: (i, 0)), pl.BlockSpec((tm, d), lambda i: (i, 0))],
        out_specs=(pl.BlockSpec((tm, d), lambda i: (i, 0)), _full((1, LANES))),
        compiler_params=_cparams(("arbitrary",)),
    )(y, target)


IN_SHARD = IN_COLS // N_DEV
P_OF_PIECE = (P_UV, P_CQ, P_CKV, P_KR + MLA_NOPE, P_Z, P_XBC, P_DT, P_GATES)


def _pad_lanes(w, n=LANES):
    return jnp.pad(w, [(0, 0)] * (w.ndim - 1) + [(0, n - w.shape[-1])])


def _in_proj_layout(blocks):
    def cols(i):
        a, b, out = IN_OFFSETS[i], IN_OFFSETS[i] + IN_WIDTHS[i], []
        while a < b:
            k, lo = divmod(a, IN_SHARD)
            hi = min(IN_SHARD, lo + b - a)
            out.append(blocks[k, :, lo:hi])
            a += hi - lo
        return out

    zeros = lambda n: jnp.zeros((D_MODEL, n), blocks.dtype)
    uv, cq, ckv, kr, z, xbc, dt, gates = (cols(i) for i in range(8))
    return jnp.concatenate(uv + xbc + z + [zeros(MLA_NOPE)] + kr + [zeros(LANES - MLA_QK_DIM)] + cq + gates + ckv
                           + dt + [zeros(LANES - SSD_HEADS)], axis=1)


def _in_proj_unlayout(dw):
    out = []
    for k in range(N_DEV):
        a, b, parts = k * IN_SHARD, (k + 1) * IN_SHARD, []
        for i in range(8):
            lo, hi = max(a, IN_OFFSETS[i]), min(b, IN_OFFSETS[i] + IN_WIDTHS[i])
            if lo < hi:
                at = P_OF_PIECE[i] + lo - IN_OFFSETS[i]
                parts.append(dw[:, at:at + hi - lo])
        out.append(jnp.concatenate(parts, axis=1))
    return jnp.stack(out)


def _row(v, n=None):
    v = v.reshape(1, -1)
    return v if n is None else jnp.pad(v, ((0, 0), (0, n - v.shape[1])))


GATHERED = ("ffn_w_in", "ffn_w_out", "w_in", "mla_w_uq", "mla_w_ukv", "ssd_conv_w", "w_branch", "w_out")


def _layer_shards(w, l):
    pair = lambda a, b: jnp.stack([w[a][l], w[b][l]]).astype(BF16)
    one = lambda n: w[n][l].astype(BF16)
    return [pair("ffn1_w_in", "ffn2_w_in"), pair("ffn1_w_out", "ffn2_w_out"), one("w_in"), one("mla_w_uq"),
            one("mla_w_ukv"), one("ssd_conv_w"), one("w_branch"), one("w_out")]


def _layer_weights(gathered, vec, l):
    gw = dict(zip(GATHERED, gathered))
    kv = gw["mla_w_ukv"]
    branch = jnp.moveaxis(gw["w_branch"], 0, 2).reshape(N_BRANCH, GM_WIDTH, D_MODEL)
    return dict(
        ffn_w_in=gw["ffn_w_in"], ffn_w_out=gw["ffn_w_out"], ffn1_at=0, ffn2_at=1,
        ffn1_norm=_row(vec["ffn1_norm"][l]), ffn2_norm=_row(vec["ffn2_norm"][l]),
        mix_norm=_row(vec["mix_norm"][l]), w_in=_in_proj_layout(gw["w_in"]),
        gm_v_norm=_row(vec["gm_v_norm"][l]), gm_w_s=vec["gm_w_s"][l], gm_b_s=vec["gm_b_s"][l][..., None],
        q_norm=_row(vec["mla_q_norm"][l]), kv_norm=_row(vec["mla_kv_norm"][l]),
        wq=_pad_lanes(gw["mla_w_uq"]), wk=_pad_lanes(kv[:, :, :MLA_NOPE]), wv=_pad_lanes(kv[:, :, MLA_NOPE:]),
        q_gain=_row(vec["mla_q_gain"][l], LANES), k_gain=_row(vec["mla_k_gain"][l], LANES),
        conv_w=jnp.moveaxis(gw["ssd_conv_w"], 0, 1).reshape(SSD_CONV, SSD_CONV_DIM).astype(F32),
        conv_b=_row(vec["ssd_conv_b"][l]),
        dt_bias=_row(vec["ssd_dt_bias"][l], LANES), a_log=_row(vec["ssd_a_log"][l], LANES),
        d_skip=_row(vec["ssd_d"][l], LANES), ssd_norm=_row(vec["ssd_norm"][l]),
        pa=branch[0],
        pb=jnp.pad(branch[1].reshape(MLA_HEADS, MLA_V, D_MODEL), ((0, 0), (0, LANES - MLA_V), (0, 0))).reshape(HEAD_BLOCK, D_MODEL),
        pc=branch[2], wo=gw["w_out"].reshape(D_MODEL, D_MODEL),
    )


def _layer_fwd(x, k, rope, next_shards=()):
    cosf, sinf, rot = rope
    x1, gu1 = _ffn_fwd(x, k["ffn1_norm"], k["ffn_w_in"], k["ffn_w_out"], k["ffn1_at"], "ffn1_fwd")
    h = _rmsnorm_fwd(x1, k["mix_norm"], "mix_norm_fwd")
    proj = _matmul(h, k["w_in"], "nn", F32, "in_proj_fwd", tn=2176)
    ya = _gmlp_fwd(proj, k["gm_v_norm"], k["gm_w_s"], k["gm_b_s"], "gmlp_fwd")
    q, kk, v = _mla_pre_fwd(proj, cosf, sinf, rot, k["q_norm"], k["kv_norm"], k["wq"], k["wk"], k["wv"],
                            k["q_gain"], k["k_gain"], "mla_pre_fwd")
    yb, lse, lse_t, next_gathered = _attention_fwd(q, kk, v, "attention_fwd", gather=next_shards)
    act = _conv_fwd(proj, k["conv_w"], k["conv_b"], "conv_fwd")
    yc, states = _ssd_fwd(act, proj, k["dt_bias"], k["a_log"], k["d_skip"], k["ssd_norm"], "ssd_fwd")
    x2, merged = _merge_fwd(ya, yb, yc, proj, x1, k["pa"], k["pb"], k["pc"], k["wo"], "merge_fwd")
    x3, gu2 = _ffn_fwd(x2, k["ffn2_norm"], k["ffn_w_in"], k["ffn_w_out"], k["ffn2_at"], "ffn2_fwd")
    saved = dict(x=x, x1=x1, x2=x2, gu1=gu1, gu2=gu2, h=h, proj=proj, ya=ya, yb=yb, yc=yc, q=q, k=kk, v=v,
                 lse=lse, lse_t=lse_t, act=act, states=states, merged=merged)
    return x3, saved, next_gathered


def _layer_bwd(dx3, k, sv, rope, above=()):
    cosf, sinf, rot = rope
    g = {}
    dx2, g["ffn2_norm"], g["ffn2_w_in"], g["ffn2_w_out"], theirs = _ffn_bwd(
        sv["x2"], k["ffn2_norm"], k["ffn_w_in"], k["ffn_w_out"], k["ffn2_at"], sv["gu2"], dx3, "ffn2_bwd",
        pair=tuple(above))
    exchange = _pair_sums(above, theirs, "grads")
    proj = sv["proj"]
    dya, dyb, dyc, dgates, ta, tb, tc = _merge_bwd(sv["ya"], sv["yb"], sv["yc"], proj, dx2, k["pa"], k["pb"], k["pc"],
                                                   k["wo"], "merge_bwd")
    g["w_out"] = _matmul(sv["merged"], dx2, "tn", BF16, "w_out_grad").reshape(N_DEV, D_MODEL // N_DEV, D_MODEL)
    dpa = _matmul(sv["ya"], ta, "tn", BF16, "branch_a_grad")
    dpb = _matmul(sv["yb"], tb, "tn", BF16, "branch_b_grad")
    dpc = _matmul(sv["yc"], tc, "tn", BF16, "branch_c_grad")
    branch = jnp.stack([dpa, dpb.reshape(MLA_HEADS, LANES, D_MODEL)[:, :MLA_V].reshape(GM_WIDTH, D_MODEL), dpc])
    g["w_branch"] = jnp.moveaxis(branch.reshape(N_BRANCH, GM_WIDTH, N_DEV, LANES), 2, 0)
    duv, g["gm_v_norm"], g["gm_w_s"], dbs = _gmlp_bwd(proj, k["gm_v_norm"], k["gm_w_s"], k["gm_b_s"], dya, "gmlp_bwd")
    g["gm_b_s"] = dbs[..., 0]
    dq, dk, dv, got_a, got_b = _attention_bwd(sv["q"], sv["k"], sv["v"], sv["yb"], sv["lse"], sv["lse_t"], dyb,
                                               "attention_bwd", exchange_dq=tuple(exchange[:1]),
                                               exchange_dkv=tuple(exchange[1:]))
    dcq, dckv, dkr, dqn, dkvn, dwq, dwk, dwv, dqg, dkg = _mla_pre_bwd(
        proj, cosf, sinf, rot, k["q_norm"], k["kv_norm"], k["wq"], k["wk"], k["wv"], k["q_gain"], k["k_gain"],
        dq, dk, dv, "mla_pre_bwd")
    g["mla_q_norm"], g["mla_kv_norm"] = dqn, dkvn
    g["mla_w_uq"] = dwq[:, :, :MLA_QK_DIM].astype(BF16)
    g["mla_w_ukv"] = jnp.concatenate([dwk[:, :, :MLA_NOPE], dwv[:, :, :MLA_V]], axis=-1).astype(BF16)
    g["mla_q_gain"], g["mla_k_gain"] = dqg[:, :MLA_QK_DIM], dkg[:, :MLA_QK_DIM]
    dact, dz, ddt, ddtb, dalog, ddsk, g["ssd_norm"] = _ssd_bwd(
        sv["act"], proj, k["dt_bias"], k["a_log"], k["d_skip"], k["ssd_norm"], sv["states"], dyc, "ssd_bwd")
    g["ssd_dt_bias"], g["ssd_a_log"], g["ssd_d"] = ddtb[:, :SSD_HEADS], dalog[:, :SSD_HEADS], ddsk[:, :SSD_HEADS]
    dxbc, dcw, g["ssd_conv_b"] = _conv_bwd(proj, k["conv_w"], k["conv_b"], dact, "conv_bwd")
    g["ssd_conv_w"] = jnp.moveaxis(dcw.reshape(SSD_CONV, N_DEV, LANES), 1, 0).astype(BF16)
    dproj = jnp.concatenate([duv, dxbc, dz, dkr, dcq, dgates, dckv, ddt], axis=1)
    dh = _matmul(dproj, k["w_in"], "nt", BF16, "in_proj_dh", tk=2176)
    g["w_in"] = _in_proj_unlayout(_matmul(sv["h"], dproj, "tn", BF16, "in_proj_grad", tn=2176))
    dx1, g["mix_norm"] = _rmsnorm_bwd(sv["x1"], k["mix_norm"], dh, dx2, "mix_norm_bwd")
    dx, g["ffn1_norm"], g["ffn1_w_in"], g["ffn1_w_out"], _ = _ffn_bwd(
        sv["x"], k["ffn1_norm"], k["ffn_w_in"], k["ffn_w_out"], k["ffn1_at"], sv["gu1"], dx1, "ffn1_bwd")
    return dx, g, got_a + got_b


def _layer_contribs(g):
    pair = lambda a, b: jnp.stack([g[a], g[b]], axis=1)
    arrays = [pair("ffn1_w_in", "ffn2_w_in"), pair("ffn1_w_out", "ffn2_w_out"), g["w_in"], g["mla_w_uq"],
              g["mla_w_ukv"], g["ssd_conv_w"], g["w_branch"], g["w_out"]]
    return [a.reshape(N_DEV, -1, a.shape[-1]) for a in arrays]


def _pair_sums(contribs, theirs, name):
    return [_pair_sum(c, t, "%s_pair_sum_%s" % (name, n), _tile_rows(c.shape[1], c.shape[2]))
            for n, c, t in zip(GATHERED, contribs, theirs)]


def _rope_tables(positions):
    s = positions.shape[0]
    inv_freq = 1.0 / (ROPE_THETA ** (jnp.arange(0, MLA_ROPE, 2, dtype=F32) / MLA_ROPE))
    ang = positions.astype(F32)[:, None] * inv_freq
    cos, sin = jnp.cos(ang), jnp.sin(ang)
    tail = LANES - MLA_QK_DIM
    cosf = jnp.concatenate([jnp.ones((s, MLA_NOPE), F32), cos, cos, jnp.ones((s, tail), F32)], axis=1)
    sinf = jnp.concatenate([jnp.zeros((s, MLA_NOPE), F32), sin, sin, jnp.zeros((s, tail), F32)], axis=1)
    half = MLA_ROPE // 2
    rot = np.zeros((LANES, LANES), np.float32)
    for i in range(half):
        rot[MLA_NOPE + half + i, MLA_NOPE + i] = -1.0
        rot[MLA_NOPE + i, MLA_NOPE + half + i] = 1.0
    return cosf, sinf, jnp.asarray(rot)


MATRICES = ("ffn1_w_in", "ffn1_w_out", "w_in", "mla_w_uq", "mla_w_ukv", "ssd_conv_w", "w_branch", "w_out", "ffn2_w_in",
            "ffn2_w_out")
VECTORS = ("ffn1_norm", "mix_norm", "gm_v_norm", "gm_w_s", "gm_b_s", "mla_q_norm", "mla_kv_norm", "mla_q_gain",
           "mla_k_gain", "ssd_conv_b", "ssd_dt_bias", "ssd_a_log", "ssd_d", "ssd_norm", "ffn2_norm")
WEIGHTS = ("ffn1_norm", "ffn1_w_in", "ffn1_w_out", "mix_norm", "w_in", "gm_v_norm", "gm_w_s", "gm_b_s", "mla_q_norm",
           "mla_kv_norm", "mla_w_uq", "mla_w_ukv", "mla_q_gain", "mla_k_gain", "ssd_conv_w", "ssd_conv_b", "ssd_dt_bias",
           "ssd_a_log", "ssd_d", "ssd_norm", "w_branch", "w_out", "ffn2_norm", "ffn2_w_in", "ffn2_w_out")


def _local_step(x, positions, target, vec, shards=None, gathered=None):
    rope = _rope_tables(positions)
    depth = vec["ffn1_norm"].shape[0]
    saved = []
    here = _gather_layer(shards[0], "layer0_all_gather") if gathered is None else gathered[0]
    for l in range(depth):
        k = _layer_weights(here, vec, l)
        ahead = shards[l + 1] if gathered is None and l + 1 < depth else ()
        x, sv, here = _layer_fwd(x, k, rope, ahead)
        if gathered is not None and l + 1 < depth:
            here = gathered[l + 1]
        saved.append((k, sv))
    dy, loss = _loss_head(x, target, "loss_head")
    grads, reduced, above = [], [], ()
    for k, sv in reversed(saved):
        dy, g, got = _layer_bwd(dy, k, sv, rope, above=above)
        grads.append(g)
        if gathered is None:
            if above:
                reduced.append(got)
            above = _layer_contribs(g)
    if gathered is None:
        sums = _pair_sums(above, _pair_exchange(above, "layer0_grads_pair_exchange"), "layer0_grads")
        reduced.append(_chip_exchange(sums, "layer0_grads_chip_exchange"))
    grads.reverse()
    reduced.reverse()
    out = {n: jnp.stack([g[n].reshape(vec[n].shape[1:]) for g in grads]) for n in VECTORS}
    if gathered is None:
        out["reduced"] = reduced
    else:
        out.update({n: [g[n] for g in grads] for n in MATRICES})
    return loss[0, 0], dy, out


MESH = pl.DeviceIdType.MESH
N_CHIP = 4
ANY = pl.BlockSpec(memory_space=pl.ANY)


def _place():
    return lax.axis_index("x"), lax.axis_index("y"), lax.axis_index("c")


GATHER_COPIES = 7


def _gather_plan(shard_refs, out_refs, send_sems, recv_sems, local_sems):
    x, y, c = _place()
    me, sibling = (x, y, c), (x, y, 1 - c)
    chips = [(1 - x, y), (x, 1 - y), (1 - x, 1 - y)]
    slot = lambda px, py, pc: 4 * px + 2 * py + pc
    plans = []
    for i, (src, out) in enumerate(zip(shard_refs, out_refs)):
        def copy(k, block, to, from_shard=False, i=i, src=src, out=out):
            return pltpu.make_async_remote_copy(
                src_ref=src if from_shard else out.at[slot(*block)], dst_ref=out.at[slot(*block)],
                send_sem=send_sems.at[GATHER_COPIES * i + k], recv_sem=recv_sems.at[GATHER_COPIES * i + k],
                device_id=to, device_id_type=MESH)

        plans.append(dict(
            mine=lambda i=i, src=src, out=out: pltpu.make_async_copy(src, out.at[slot(*me)], local_sems.at[i]),
            first=lambda copy=copy: [copy(0, me, sibling, True)] + [copy(1 + j, me, (*chip, c), True)
                                                                    for j, chip in enumerate(chips)],
            arrived=lambda j, copy=copy: copy(1 + j, (*chips[j], c), me),
            passed=lambda j, copy=copy: copy(4 + j, (*chips[j], c), sibling),
            from_sibling=lambda copy=copy: [copy(0, sibling, me)] + [copy(4 + j, (*chip, 1 - c), me)
                                                                     for j, chip in enumerate(chips)],
        ))
    return plans


def _gather_start(plans):
    for p in plans:
        p["mine"]().start()
        for cp in p["first"]():
            cp.start()


def _gather_finish(plans):
    for j in range(N_CHIP - 1):
        for p in plans:
            p["arrived"](j).wait_recv()
            p["passed"](j).start()
    for p in plans:
        for cp in p["from_sibling"]():
            cp.wait_recv()
        for cp in p["first"]() + [p["passed"](j) for j in range(N_CHIP - 1)]:
            cp.wait_send()
        p["mine"]().wait()


def _gather_scratch(n):
    return [pltpu.SemaphoreType.DMA((GATHER_COPIES * n,)), pltpu.SemaphoreType.DMA((GATHER_COPIES * n,)),
            pltpu.SemaphoreType.DMA((n,))]


def _gathered_shapes(shards):
    return tuple(jax.ShapeDtypeStruct((N_DEV, *a.shape), a.dtype) for a in shards)


def _gather_layer(shards, name):
    n = len(shards)

    def body(*refs):
        plans = _gather_plan(refs[:n], refs[n:2 * n], *refs[2 * n:])
        _gather_start(plans)
        _gather_finish(plans)

    return pl.pallas_call(
        body,
        name=name,
        out_shape=_gathered_shapes(shards),
        in_specs=[ANY] * n,
        out_specs=(ANY,) * n,
        scratch_shapes=_gather_scratch(n),
    )(*shards)


def _all_gather(shard, name):
    m, n = shard.shape

    def body(x_ref, out_ref, send_sems, recv_sems, local_sem):
        x, y, c = _place()
        me, sibling = (x, y, c), (x, y, 1 - c)
        chips = [(1 - x, y), (x, 1 - y), (1 - x, 1 - y)]

        def rows(px, py, pc):
            return out_ref.at[pl.ds((4 * px + 2 * py + pc) * m, m), :]

        def copy(k, block, to, src=None):
            return pltpu.make_async_remote_copy(
                src_ref=rows(*block) if src is None else src, dst_ref=rows(*block),
                send_sem=send_sems.at[k], recv_sem=recv_sems.at[k], device_id=to, device_id_type=MESH)

        mine = pltpu.make_async_copy(x_ref, rows(*me), local_sem)
        mine.start()
        first = [copy(0, me, sibling, src=x_ref)]
        first += [copy(1 + j, me, (*chip, c), src=x_ref) for j, chip in enumerate(chips)]
        for cp in first:
            cp.start()
        passed = [copy(4 + j, (*chip, c), sibling) for j, chip in enumerate(chips)]
        for j, chip in enumerate(chips):
            copy(1 + j, (*chip, c), me).wait_recv()
            passed[j].start()
        copy(0, sibling, me).wait_recv()
        for j, chip in enumerate(chips):
            copy(4 + j, (*chip, 1 - c), me).wait_recv()
        for cp in first + passed:
            cp.wait_send()
        mine.wait()

    return pl.pallas_call(
        body,
        name=name,
        out_shape=jax.ShapeDtypeStruct((N_DEV * m, n), shard.dtype),
        in_specs=[ANY],
        out_specs=ANY,
        scratch_shapes=[pltpu.SemaphoreType.DMA((7,)), pltpu.SemaphoreType.DMA((7,)), pltpu.SemaphoreType.DMA],
    )(shard)


def _pair_exchange(contribs, name):
    na = len(contribs)

    def body(*refs):
        plan = _pair_plan(refs[:na], refs[na:2 * na], *refs[2 * na:])
        _pair_start(plan)
        _pair_finish(plan)

    return pl.pallas_call(
        body,
        name=name,
        out_shape=_pair_shapes(contribs),
        in_specs=[ANY] * na,
        out_specs=(ANY,) * na,
        scratch_shapes=_pair_scratch(na),
    )(*contribs)


def _pair_plan(g_refs, got_refs, send_sems, recv_sems, unused_sems):
    x, y, c = _place()
    return [lambda i=i, j=j, g=g, got=got: pltpu.make_async_remote_copy(
        src_ref=g.at[2 * j + (1 - c)], dst_ref=got.at[j], send_sem=send_sems.at[N_CHIP * i + j],
        recv_sem=recv_sems.at[N_CHIP * i + j], device_id=(x, y, 1 - c), device_id_type=MESH)
        for i, (g, got) in enumerate(zip(g_refs, got_refs)) for j in range(N_CHIP)]


def _pair_start(plan):
    for cp in plan:
        cp().start()


def _pair_finish(plan):
    for cp in plan:
        cp().wait()


def _pair_scratch(n):
    return [pltpu.SemaphoreType.DMA((N_CHIP * n,)), pltpu.SemaphoreType.DMA((N_CHIP * n,)), pltpu.SemaphoreType.DMA((1,))]


def _pair_shapes(contribs):
    return tuple(jax.ShapeDtypeStruct((N_CHIP, *a.shape[1:]), a.dtype) for a in contribs)


def _pair_sum(contrib, theirs, name, tr):
    _, r, n = contrib.shape
    side = lax.axis_index("c").astype(jnp.int32).reshape(1)

    def body(c_ref, a_ref, b_ref, o_ref):
        o_ref[...] = (a_ref[...].astype(F32) + b_ref[...].astype(F32)).astype(BF16)

    spec = pl.BlockSpec((1, tr, n), lambda j, i, c_ref: (j, i, 0))
    return pl.pallas_call(
        body,
        name=name,
        out_shape=jax.ShapeDtypeStruct(theirs.shape, BF16),
        grid_spec=pltpu.PrefetchScalarGridSpec(
            num_scalar_prefetch=1,
            grid=(N_CHIP, r // tr),
            in_specs=[pl.BlockSpec((1, tr, n), lambda j, i, c_ref: (2 * j + c_ref[0], i, 0)), spec],
            out_specs=spec,
        ),
        compiler_params=_cparams(("parallel", "parallel")),
    )(side, contrib, theirs)


OTHER_CHIPS = N_CHIP - 1


def _exchange_plan(part_refs, got_refs, send_sems, recv_sems, local_sems):
    x, y, c = _place()
    mine = 2 * x + y
    chips = [(1 - x, y), (x, 1 - y), (1 - x, 1 - y)]
    plans = []
    for i, (p, got) in enumerate(zip(part_refs, got_refs)):
        def copy(k, outbound, i=i, p=p, got=got):
            cx, cy = chips[k]
            return pltpu.make_async_remote_copy(
                src_ref=p.at[2 * cx + cy] if outbound else p.at[mine],
                dst_ref=got.at[mine] if outbound else got.at[2 * cx + cy],
                send_sem=send_sems.at[OTHER_CHIPS * i + k], recv_sem=recv_sems.at[OTHER_CHIPS * i + k],
                device_id=(cx, cy, c), device_id_type=MESH)

        plans.append(dict(copy=copy, local=lambda i=i, p=p, got=got: pltpu.make_async_copy(
            p.at[mine], got.at[mine], local_sems.at[i])))
    return plans


def _exchange_start(plans):
    for p in plans:
        p["local"]().start()
        for k in range(OTHER_CHIPS):
            p["copy"](k, True).start()


def _exchange_finish(plans):
    for p in plans:
        for k in range(OTHER_CHIPS):
            p["copy"](k, False).wait_recv()
        for k in range(OTHER_CHIPS):
            p["copy"](k, True).wait_send()
        p["local"]().wait()


def _exchange_scratch(n):
    return [pltpu.SemaphoreType.DMA((OTHER_CHIPS * n,)), pltpu.SemaphoreType.DMA((OTHER_CHIPS * n,)),
            pltpu.SemaphoreType.DMA((n,))]


def _same_shapes(arrays):
    return tuple(jax.ShapeDtypeStruct(a.shape, a.dtype) for a in arrays)


def _chip_exchange(parts, name):
    na = len(parts)

    def body(*refs):
        plans = _exchange_plan(refs[:na], refs[na:2 * na], *refs[2 * na:])
        _exchange_start(plans)
        _exchange_finish(plans)

    return pl.pallas_call(
        body,
        name=name,
        out_shape=_same_shapes(parts),
        in_specs=[ANY] * na,
        out_specs=(ANY,) * na,
        scratch_shapes=_exchange_scratch(na),
    )(*parts)


def _adamw(parts, w, m, v, name, tr, at=0):
    k = parts.shape[0]
    r, n = w.shape
    first = at // tr

    def body(p_ref, w_ref, m_ref, v_ref, g_ref, d_ref, nm_ref, nv_ref):
        g = p_ref[0].astype(F32)
        for i in range(1, k):
            g = g + p_ref[i].astype(F32)
        m_new = ADAM_B1 * m_ref[...] + (1.0 - ADAM_B1) * g
        v_new = ADAM_B2 * v_ref[...] + (1.0 - ADAM_B2) * (g * g)
        m_hat = m_new / (1.0 - ADAM_B1 ** ADAM_STEP)
        v_hat = v_new / (1.0 - ADAM_B2 ** ADAM_STEP)
        g_ref[...] = g
        d_ref[...] = -ADAM_LR * (m_hat / (jnp.sqrt(v_hat) + ADAM_EPS) + ADAM_WD * w_ref[...])
        nm_ref[...] = m_new
        nv_ref[...] = v_new

    spec = pl.BlockSpec((tr, n), lambda i: (i, 0))
    out = jax.ShapeDtypeStruct((r, n), F32)
    return pl.pallas_call(
        body,
        name=name,
        out_shape=(out, out, out, out),
        grid=(r // tr,),
        in_specs=[pl.BlockSpec((k, tr, n), lambda i: (0, i + first, 0)), spec, spec, spec],
        out_specs=(spec, spec, spec, spec),
        compiler_params=_cparams(("parallel",)),
    )(parts, w, m, v)


def _adamw_layers(parts, w, m, v, name, at=0):
    depth = len(parts)
    k = parts[0].shape[0]
    w3, m3, v3 = (a.reshape(depth, -1, a.shape[-1]) for a in (w, m, v))
    _, r, n = w3.shape
    tr = _tile_rows(r, n * depth)
    first = at // tr

    def body(*refs):
        p_refs = refs[:depth]
        w_ref, m_ref, v_ref, g_ref, d_ref, nm_ref, nv_ref = refs[depth:]
        for l in range(depth):
            @pl.when(pl.program_id(0) == l)
            def _(p_ref=p_refs[l]):
                g = p_ref[0].astype(F32)
                for i in range(1, k):
                    g = g + p_ref[i].astype(F32)
                m_new = ADAM_B1 * m_ref[...] + (1.0 - ADAM_B1) * g
                v_new = ADAM_B2 * v_ref[...] + (1.0 - ADAM_B2) * (g * g)
                m_hat = m_new / (1.0 - ADAM_B1 ** ADAM_STEP)
                v_hat = v_new / (1.0 - ADAM_B2 ** ADAM_STEP)
                g_ref[...] = g
                d_ref[...] = -ADAM_LR * (m_hat / (jnp.sqrt(v_hat) + ADAM_EPS) + ADAM_WD * w_ref[...])
                nm_ref[...] = m_new
                nv_ref[...] = v_new

    part_spec = lambda l: pl.BlockSpec((k, tr, n), lambda j, i: (0, jnp.where(j == l, i + first, first), 0))
    spec = pl.BlockSpec((None, tr, n), lambda j, i: (j, i, 0))
    out = jax.ShapeDtypeStruct(w3.shape, F32)
    res = pl.pallas_call(
        body,
        name=name,
        out_shape=(out, out, out, out),
        grid=(depth, r // tr),
        in_specs=[part_spec(l) for l in range(depth)] + [spec, spec, spec],
        out_specs=(spec, spec, spec, spec),
        compiler_params=_cparams(("parallel", "parallel")),
    )(*parts, w3, m3, v3)
    return [a.reshape(w.shape) for a in res]


TILE_BYTES = 2 * 1024 * 1024
SUBLANES_16BIT = 16


def _tile_rows(r, n):
    best = None
    for t in range(SUBLANES_16BIT, r, SUBLANES_16BIT):
        if r % t == 0 and t * n * 4 <= TILE_BYTES:
            best = t
    return best or r


def _rows(a):
    return a.reshape(-1, a.shape[-1])


def _gather_rows(shard, name):
    return _all_gather(_rows(shard), name).reshape(N_DEV, *shard.shape)


SMALL = ("ffn1_norm", "mix_norm", "gm_v_norm", "gm_b_s", "mla_q_norm", "mla_kv_norm", "mla_q_gain", "mla_k_gain",
         "ssd_conv_b", "ssd_dt_bias", "ssd_a_log", "ssd_d", "ssd_norm", "ffn2_norm")
SMALL_ROWS = 8
SMALL_COLS = 7040


def _side_by_side(d):
    cols = jnp.concatenate([d[n].reshape(d[n].shape[0], -1) for n in SMALL], axis=1)
    return jnp.pad(cols, ((0, SMALL_ROWS - cols.shape[0]), (0, SMALL_COLS - cols.shape[1])))


def _apart(packed, like):
    out, off = {}, 0
    for n in SMALL:
        size = like[n][0].size
        out[n] = packed[:like[n].shape[0], off:off + size].reshape(like[n].shape)
        off += size
    return out


def kernel(x, positions, ffn1_norm, ffn1_w_in, ffn1_w_out, mix_norm, w_in, gm_v_norm, gm_w_s, gm_b_s, mla_q_norm, mla_kv_norm, mla_w_uq, mla_w_ukv, mla_q_gain, mla_k_gain, ssd_conv_w, ssd_conv_b, ssd_dt_bias, ssd_a_log, ssd_d, ssd_norm, w_branch, w_out, ffn2_norm, ffn2_w_in, ffn2_w_out, loss_target, m_ffn1_norm, m_ffn1_w_in, m_ffn1_w_out, m_mix_norm, m_w_in, m_gm_v_norm, m_gm_w_s, m_gm_b_s, m_mla_q_norm, m_mla_kv_norm, m_mla_w_uq, m_mla_w_ukv, m_mla_q_gain, m_mla_k_gain, m_ssd_conv_w, m_ssd_conv_b, m_ssd_dt_bias, m_ssd_a_log, m_ssd_d, m_ssd_norm, m_w_branch, m_w_out, m_ffn2_norm, m_ffn2_w_in, m_ffn2_w_out, v_ffn1_norm, v_ffn1_w_in, v_ffn1_w_out, v_mix_norm, v_w_in, v_gm_v_norm, v_gm_w_s, v_gm_b_s, v_mla_q_norm, v_mla_kv_norm, v_mla_w_uq, v_mla_w_ukv, v_mla_q_gain, v_mla_k_gain, v_ssd_conv_w, v_ssd_conv_b, v_ssd_dt_bias, v_ssd_a_log, v_ssd_d, v_ssd_norm, v_w_branch, v_w_out, v_ffn2_norm, v_ffn2_w_in, v_ffn2_w_out):
    given = dict(locals())
    w = {n: given[n] for n in WEIGHTS}
    mom = {n: given["m_" + n] for n in WEIGHTS}
    var = {n: given["v_" + n] for n in WEIGHTS}
    groups = {"ffn_w_in": ("ffn1_w_in", "ffn2_w_in"), "ffn_w_out": ("ffn1_w_out", "ffn2_w_out"), "w_in": ("w_in",),
              "mla_w_uq": ("mla_w_uq",), "mla_w_ukv": ("mla_w_ukv",), "ssd_conv_w": ("ssd_conv_w",),
              "w_branch": ("w_branch",), "w_out": ("w_out",)}

    shards = [_layer_shards(w, l) for l in range(w_out.shape[0])]
    loss, dx, grads = _local_step(x[0], positions[0], loss_target[0], {n: w[n] for n in VECTORS}, shards=shards)

    outs = [{}, {}, {}, {}]
    for i, names in enumerate(groups.values()):
        parts = [layer[i] for layer in grads["reduced"]]
        at = 0
        for n in names:
            res = _adamw_layers(parts, w[n], mom[n], var[n], "adamw_" + n, at=at)
            at += w[n][0].size // w[n].shape[-1]
            for o, r in zip(outs, res):
                o[n] = r

    small_parts = _gather_rows(_side_by_side(grads), "small_grads_all_gather")
    small = _adamw(small_parts, _side_by_side(w), _side_by_side(mom), _side_by_side(var), "adamw_small", SMALL_ROWS)
    ws_parts = _gather_rows(_rows(grads["gm_w_s"]), "gm_w_s_grads_all_gather")
    ws = _adamw(ws_parts, _rows(w["gm_w_s"]), _rows(mom["gm_w_s"]), _rows(var["gm_w_s"]), "adamw_gm_w_s",
                _tile_rows(ws_parts.shape[1], LANES))
    for o, sm, r in zip(outs, small, ws):
        o.update(_apart(sm, w))
        o["gm_w_s"] = r.reshape(w["gm_w_s"].shape)

    loss = lax.psum(loss, ("x", "y", "c"))
    return (loss, dx[None], *[o[n] for o in outs for n in WEIGHTS])
```

```python
import functools

import jax
import jax.numpy as jnp
import numpy as np
from jax import lax
from jax.experimental import pallas as pl
from jax.experimental.pallas import tpu as pltpu

F32 = jnp.float32
BF16 = jnp.bfloat16

D_MODEL = 1024
DEPTH = 4
D_FF = 2816
FFN_RESID = 0.5
EPS = 1e-6
GM_WIDTH = 512
GM_GROUPS = 4
GM_CHUNK = 128
MLA_HEADS = 8
MLA_Q_RANK = 384
MLA_KV_RANK = 256
MLA_NOPE = 64
MLA_ROPE = 32
MLA_QK_DIM = 96
MLA_V = 64
ROPE_THETA = 10000.0
SSD_HEADS = 8
SSD_HEAD_DIM = 64
SSD_INNER = 512
SSD_GROUPS = 2
SSD_STATE = 128
SSD_CONV = 4
SSD_CHUNK = 128
SSD_CONV_DIM = 1024
N_BRANCH = 3
IN_WIDTHS = (1024, 384, 256, 32, 512, 1024, 8, 3072)
IN_OFFSETS = (0, 1024, 1408, 1664, 1696, 2208, 3232, 3240)
IN_COLS = 6312
LANES = 128
N_DEV = 8

ADAM_LR = 0.001
ADAM_B1 = 0.9
ADAM_B2 = 0.999
ADAM_EPS = 1e-08
ADAM_WD = 0.01
ADAM_STEP = 10

VMEM_LIMIT = 56 * 1024 * 1024

P_UV, P_XBC, P_Z, P_KR, P_CQ, P_GATES, P_CKV, P_DT = 0, 1024, 2048, 2560, 2688, 3072, 6144, 6400
P_COLS = 6528


def _cparams(sem):
    return pltpu.CompilerParams(dimension_semantics=sem, vmem_limit_bytes=VMEM_LIMIT)


def _bdot(a, b, dims):
    return lax.dot_general(a.astype(BF16), b.astype(BF16), (dims, ((), ())), preferred_element_type=F32)


@jax.custom_vjp
def _nn(a, b):
    return _bdot(a, b, ((1,), (0,)))


@jax.custom_vjp
def _nt(a, b):
    return _bdot(a, b, ((1,), (1,)))


@jax.custom_vjp
def _tn(a, b):
    return _bdot(a, b, ((0,), (0,)))


def _dot_fwd(dims):
    return lambda a, b: (_bdot(a, b, dims), (a, b))


_nn.defvjp(_dot_fwd(((1,), (0,))), lambda r, g: (_nt(g, r[1]).astype(r[0].dtype), _tn(r[0], g).astype(r[1].dtype)))
_nt.defvjp(_dot_fwd(((1,), (1,))), lambda r, g: (_nn(g, r[1]).astype(r[0].dtype), _tn(g, r[0]).astype(r[1].dtype)))
_tn.defvjp(_dot_fwd(((0,), (0,))), lambda r, g: (_nt(r[1], g).astype(r[0].dtype), _nn(r[0], g).astype(r[1].dtype)))


def _exact_nn(a, b):
    return lax.dot_general(a, b, (((1,), (0,)), ((), ())), precision=lax.Precision.HIGHEST, preferred_element_type=F32)


def _sigmoid(x):
    return 1.0 / (1.0 + jnp.exp(-x))


def _silu(x):
    return x * _sigmoid(x)


def _softplus(x):
    return jnp.maximum(x, 0.0) + jnp.log(1.0 + jnp.exp(-jnp.abs(x)))


def _gelu(x):
    return 0.5 * x * (1.0 + lax.erf(x * 0.7071067811865476))


def _pick(n, cands):
    for c in cands:
        if n % c == 0:
            return c
    return n


def _matmul(a, b, mode, out_dtype, name, alpha=1.0, tm=None, tn=None, tk=None):
    if mode == "nn":
        (m, k), (_, n) = a.shape, b.shape
    elif mode == "nt":
        (m, k), (n, _) = a.shape, b.shape
    else:
        (k, m), (_, n) = a.shape, b.shape
    tm = tm or _pick(m, (512, 384, 256, 128))
    tn = tn or _pick(n, (1024, 768, 512, 384, 256, 128))
    tk = tk or _pick(k, (1024, 512, 256, 128))
    nk = k // tk
    if mode == "nn":
        a_spec = pl.BlockSpec((tm, tk), lambda i, j, kk: (i, kk))
        b_spec = pl.BlockSpec((tk, tn), lambda i, j, kk: (kk, j))
        dot = _nn
    elif mode == "nt":
        a_spec = pl.BlockSpec((tm, tk), lambda i, j, kk: (i, kk))
        b_spec = pl.BlockSpec((tn, tk), lambda i, j, kk: (j, kk))
        dot = _nt
    else:
        a_spec = pl.BlockSpec((tk, tm), lambda i, j, kk: (kk, i))
        b_spec = pl.BlockSpec((tk, tn), lambda i, j, kk: (kk, j))
        dot = _tn

    def body(a_ref, b_ref, o_ref, acc_ref):
        kk = pl.program_id(2)

        @pl.when(kk == 0)
        def _():
            acc_ref[...] = jnp.zeros_like(acc_ref)

        acc_ref[...] += dot(a_ref[...], b_ref[...])

        @pl.when(kk == nk - 1)
        def _():
            o_ref[...] = (alpha * acc_ref[...]).astype(out_dtype)

    return pl.pallas_call(
        body,
        name=name,
        out_shape=jax.ShapeDtypeStruct((m, n), out_dtype),
        grid=(m // tm, n // tn, nk),
        in_specs=[a_spec, b_spec],
        out_specs=pl.BlockSpec((tm, tn), lambda i, j, kk: (i, j)),
        scratch_shapes=[pltpu.VMEM((tm, tn), F32)],
        compiler_params=_cparams(("parallel", "parallel", "arbitrary")),
    )(a, b)


def _rms_stats(x):
    r = lax.rsqrt(jnp.mean(x * x, axis=-1, keepdims=True) + EPS)
    return x * r, r


def _rms_bwd(xhat, r, gain, dy):
    dxhat = dy * gain
    return r * (dxhat - xhat * jnp.mean(dxhat * xhat, axis=-1, keepdims=True))


def _acc_rows(ref, val, first):
    s = jnp.sum(val, axis=0, keepdims=True)

    @pl.when(first)
    def _():
        ref[...] = s

    @pl.when(jnp.logical_not(first))
    def _():
        ref[...] += s


def _rmsnorm_fwd(x, gain, name, tm=512):
    s, d = x.shape

    def body(x_ref, g_ref, h_ref):
        xhat, _ = _rms_stats(x_ref[...])
        h_ref[...] = (xhat * g_ref[...]).astype(BF16)

    return pl.pallas_call(
        body,
        name=name,
        out_shape=jax.ShapeDtypeStruct((s, d), BF16),
        grid=(s // tm,),
        in_specs=[pl.BlockSpec((tm, d), lambda i: (i, 0)), pl.BlockSpec((1, d), lambda i: (0, 0))],
        out_specs=pl.BlockSpec((tm, d), lambda i: (i, 0)),
        compiler_params=_cparams(("parallel",)),
    )(x, gain)


def _rmsnorm_bwd(x, gain, dh, dres, name, tm=512):
    s, d = x.shape

    def body(x_ref, g_ref, dh_ref, dres_ref, dx_ref, dg_ref):
        xhat, r = _rms_stats(x_ref[...])
        dh = dh_ref[...].astype(F32)
        dx_ref[...] = dres_ref[...] + _rms_bwd(xhat, r, g_ref[...], dh)
        _acc_rows(dg_ref, dh * xhat, pl.program_id(0) == 0)

    return pl.pallas_call(
        body,
        name=name,
        out_shape=(jax.ShapeDtypeStruct((s, d), F32), jax.ShapeDtypeStruct((1, d), F32)),
        grid=(s // tm,),
        in_specs=[
            pl.BlockSpec((tm, d), lambda i: (i, 0)),
            pl.BlockSpec((1, d), lambda i: (0, 0)),
            pl.BlockSpec((tm, d), lambda i: (i, 0)),
            pl.BlockSpec((tm, d), lambda i: (i, 0)),
        ],
        out_specs=(pl.BlockSpec((tm, d), lambda i: (i, 0)), pl.BlockSpec((1, d), lambda i: (0, 0))),
        compiler_params=_cparams(("arbitrary",)),
    )(x, gain, dh, dres)


FF_BLOCK = 2 * D_FF // N_DEV
FF_BLOCKS = D_FF // FF_BLOCK
FF_ROWS = D_FF // N_DEV


def _ffn_weight_specs(layer):
    return [
        pl.BlockSpec((None, None, D_MODEL, FF_BLOCK), lambda i, j: (j, layer, 0, 0)),
        pl.BlockSpec((None, None, D_MODEL, FF_BLOCK), lambda i, j: (j + FF_BLOCKS, layer, 0, 0)),
        pl.BlockSpec((2, None, FF_ROWS, D_MODEL), lambda i, j: (j, layer, 0, 0)),
    ]


def _ffn_fwd(x, gain, w_in, w_out, layer, name, tm=512):
    s, d = x.shape

    def body(x_ref, gain_ref, wg_ref, wu_ref, wo_ref, y_ref, gu_ref, h_scr, acc_scr):
        j = pl.program_id(1)

        @pl.when(j == 0)
        def _():
            xhat, _ = _rms_stats(x_ref[...])
            h_scr[...] = (xhat * gain_ref[...]).astype(BF16)
            acc_scr[...] = jnp.zeros_like(acc_scr)

        h = h_scr[...]
        g = _nn(h, wg_ref[...])
        u = _nn(h, wu_ref[...])
        gu_ref[0] = g.astype(BF16)
        gu_ref[1] = u.astype(BF16)
        acc_scr[...] += _nn(_silu(g) * u, wo_ref[...].reshape(FF_BLOCK, d))

        @pl.when(j == FF_BLOCKS - 1)
        def _():
            y_ref[...] = x_ref[...] + FFN_RESID * acc_scr[...]

    return pl.pallas_call(
        body,
        name=name,
        out_shape=(
            jax.ShapeDtypeStruct((s, d), F32),
            jax.ShapeDtypeStruct((2, FF_BLOCKS, s, FF_BLOCK), BF16),
        ),
        grid=(s // tm, FF_BLOCKS),
        in_specs=[
            pl.BlockSpec((tm, d), lambda i, j: (i, 0)),
            pl.BlockSpec((1, d), lambda i, j: (0, 0)),
        ] + _ffn_weight_specs(layer),
        out_specs=(
            pl.BlockSpec((tm, d), lambda i, j: (i, 0)),
            pl.BlockSpec((2, None, tm, FF_BLOCK), lambda i, j: (0, j, i, 0)),
        ),
        scratch_shapes=[pltpu.VMEM((tm, d), BF16), pltpu.VMEM((tm, d), F32)],
        compiler_params=_cparams(("parallel", "arbitrary")),
    )(x, gain, w_in, w_in, w_out)


def _ffn_bwd(x, gain, w_in, w_out, layer, gu, dy, name, tm=512, tk=1024, pair=()):
    s, d = x.shape
    tk = min(tk, s)
    npair = len(pair)

    def body(x_ref, gain_ref, wg_ref, wu_ref, wo_ref, gu_ref, dy_ref,
             dx_ref, dgain_ref, h_ref, a_ref, dgu_ref, dyb_scr, acc_scr):
        i = pl.program_id(0)
        j = pl.program_id(1)

        @pl.when(j == 0)
        def _():
            xhat, _ = _rms_stats(x_ref[...])
            h_ref[...] = (xhat * gain_ref[...]).astype(BF16)
            dyb_scr[...] = (FFN_RESID * dy_ref[...]).astype(BF16)
            acc_scr[...] = jnp.zeros_like(acc_scr)

        da = _nt(dyb_scr[...], wo_ref[...].reshape(FF_BLOCK, d))
        gv = gu_ref[0].astype(F32)
        uv = gu_ref[1].astype(F32)
        sg = _sigmoid(gv)
        sl = gv * sg
        a_ref[...] = (sl * uv).astype(BF16)
        du = (da * sl).astype(BF16)
        dg = (da * uv * (sg * (1.0 + gv * (1.0 - sg)))).astype(BF16)
        dgu_ref[0] = dg
        dgu_ref[1] = du
        acc_scr[...] += _nt(dg, wg_ref[...]) + _nt(du, wu_ref[...])

        @pl.when(j == FF_BLOCKS - 1)
        def _():
            xhat, r = _rms_stats(x_ref[...])
            dh = acc_scr[...]
            dx_ref[...] = dy_ref[...] + _rms_bwd(xhat, r, gain_ref[...], dh)
            _acc_rows(dgain_ref, dh * xhat, i == 0)

    gu_spec = pl.BlockSpec((2, None, tm, FF_BLOCK), lambda i, j: (0, j, i, 0))
    dx, dgain, h, a, dgu, *theirs = pl.pallas_call(
        _host_exchange(body, 7, 5, pair, (s // tm - 1, FF_BLOCKS - 1), pair=True),
        name=name,
        out_shape=(
            jax.ShapeDtypeStruct((s, d), F32),
            jax.ShapeDtypeStruct((1, d), F32),
            jax.ShapeDtypeStruct((s, d), BF16),
            jax.ShapeDtypeStruct((FF_BLOCKS, s, FF_BLOCK), BF16),
            jax.ShapeDtypeStruct((2, FF_BLOCKS, s, FF_BLOCK), BF16),
        ) + _pair_shapes(pair),
        grid=(s // tm, FF_BLOCKS),
        in_specs=[
            pl.BlockSpec((tm, d), lambda i, j: (i, 0)),
            pl.BlockSpec((1, d), lambda i, j: (0, 0)),
        ] + _ffn_weight_specs(layer) + [gu_spec, pl.BlockSpec((tm, d), lambda i, j: (i, 0))] + [ANY] * npair,
        out_specs=(
            pl.BlockSpec((tm, d), lambda i, j: (i, 0)),
            pl.BlockSpec((1, d), lambda i, j: (0, 0)),
            pl.BlockSpec((tm, d), lambda i, j: (i, 0)),
            pl.BlockSpec((None, tm, FF_BLOCK), lambda i, j: (j, i, 0)),
            gu_spec,
        ) + (ANY,) * npair,
        scratch_shapes=[pltpu.VMEM((tm, d), BF16), pltpu.VMEM((tm, d), F32)] + (_pair_scratch(npair) if npair else []),
        compiler_params=_cparams(("arbitrary", "arbitrary")),
    )(x, gain, w_in, w_in, w_out, gu, dy, *pair)
    nk = s // tk

    def acc_matmul(first, last, acc_ref, o_ref, val, alpha):
        @pl.when(first)
        def _():
            acc_ref[...] = jnp.zeros_like(acc_ref)

        acc_ref[...] += val

        @pl.when(last)
        def _():
            o_ref[...] = (alpha * acc_ref[...]).astype(BF16)

    def dwin_body(h_ref, dgu_ref, o_ref, acc_ref):
        kk = pl.program_id(2)
        acc_matmul(kk == 0, kk == nk - 1, acc_ref, o_ref, _tn(h_ref[...], dgu_ref[...]), 1.0)

    tmw = 512
    dw_in = pl.pallas_call(
        dwin_body,
        name=name + "_dwin",
        out_shape=jax.ShapeDtypeStruct((N_DEV, d, FF_BLOCK), BF16),
        grid=(N_DEV, d // tmw, nk),
        in_specs=[
            pl.BlockSpec((tk, tmw), lambda n, i, kk: (kk, i)),
            pl.BlockSpec((None, tk, FF_BLOCK), lambda n, i, kk: (n, kk, 0)),
        ],
        out_specs=pl.BlockSpec((None, tmw, FF_BLOCK), lambda n, i, kk: (n, i, 0)),
        scratch_shapes=[pltpu.VMEM((tmw, FF_BLOCK), F32)],
        compiler_params=_cparams(("parallel", "parallel", "arbitrary")),
    )(h, dgu.reshape(N_DEV, s, FF_BLOCK))

    def dwout_body(a_ref, dy_ref, o_ref, acc_ref):
        kk = pl.program_id(1)
        acc_matmul(kk == 0, kk == nk - 1, acc_ref, o_ref, _tn(a_ref[...], dy_ref[...]), FFN_RESID)

    dw_out = pl.pallas_call(
        dwout_body,
        name=name + "_dwout",
        out_shape=jax.ShapeDtypeStruct((FF_BLOCKS, FF_BLOCK, d), BF16),
        grid=(FF_BLOCKS, nk),
        in_specs=[
            pl.BlockSpec((None, tk, FF_BLOCK), lambda j, kk: (j, kk, 0)),
            pl.BlockSpec((tk, d), lambda j, kk: (kk, 0)),
        ],
        out_specs=pl.BlockSpec((None, FF_BLOCK, d), lambda j, kk: (j, 0, 0)),
        scratch_shapes=[pltpu.VMEM((FF_BLOCK, d), F32)],
        compiler_params=_cparams(("parallel", "arbitrary")),
    )(a, dy)
    return dx, dgain, dw_in, dw_out.reshape(N_DEV, FF_ROWS, d), theirs


def _acc(ref, val, first):
    @pl.when(first)
    def _():
        ref[...] = val

    @pl.when(jnp.logical_not(first))
    def _():
        ref[...] += val


def _full(shape):
    nd = len(shape)
    return pl.BlockSpec(shape, lambda *_: (0,) * nd)


def _iota(shape, dim):
    return lax.broadcasted_iota(jnp.int32, shape, dim)


def _gmlp_chunk(u, v, gain, w_s, b_s):
    va = [_gelu(t) for t in v]
    ms = sum(jnp.sum(t * t, axis=-1, keepdims=True) for t in va) * (1.0 / GM_WIDTH)
    r = lax.rsqrt(ms + EPS)
    tri = _iota((GM_CHUNK, GM_CHUNK), 0) >= _iota((GM_CHUNK, GM_CHUNK), 1)
    out = []
    for g in range(GM_GROUPS):
        vn = va[g] * r * gain[g]
        sp = _nn(jnp.where(tri, w_s[g], 0.0), vn) + b_s[g]
        out.append(_gelu(u[g]) * sp)
    return out


def _gmlp_load(uv_ref, c):
    rows = pl.ds(c * GM_CHUNK, GM_CHUNK)
    u = [uv_ref[rows, pl.ds(g * LANES, LANES)] for g in range(GM_GROUPS)]
    v = [uv_ref[rows, pl.ds(GM_WIDTH + g * LANES, LANES)] for g in range(GM_GROUPS)]
    return u, v


def _gmlp_params(gain_ref, ws_ref, bs_ref):
    gain = [gain_ref[:, pl.ds(g * LANES, LANES)] for g in range(GM_GROUPS)]
    w_s = [ws_ref[g] for g in range(GM_GROUPS)]
    b_s = [bs_ref[g] for g in range(GM_GROUPS)]
    return gain, w_s, b_s


def _gmlp_fwd(proj, gain, w_s, b_s, name, tm=512):
    s = proj.shape[0]

    def body(uv_ref, gain_ref, ws_ref, bs_ref, y_ref):
        params = _gmlp_params(gain_ref, ws_ref, bs_ref)
        for c in range(tm // GM_CHUNK):
            u, v = _gmlp_load(uv_ref, c)
            y = _gmlp_chunk(u, v, *params)
            for g in range(GM_GROUPS):
                y_ref[pl.ds(c * GM_CHUNK, GM_CHUNK), pl.ds(g * LANES, LANES)] = y[g]

    return pl.pallas_call(
        body,
        name=name,
        out_shape=jax.ShapeDtypeStruct((s, GM_WIDTH), F32),
        grid=(s // tm,),
        in_specs=[
            pl.BlockSpec((tm, 2 * GM_WIDTH), lambda i: (i, P_UV // (2 * GM_WIDTH))),
            _full((1, GM_WIDTH)),
            _full((GM_GROUPS, GM_CHUNK, GM_CHUNK)),
            _full((GM_GROUPS, GM_CHUNK, 1)),
        ],
        out_specs=pl.BlockSpec((tm, GM_WIDTH), lambda i: (i, 0)),
        compiler_params=_cparams(("parallel",)),
    )(proj, gain, w_s, b_s)


def _gmlp_bwd(proj, gain, w_s, b_s, dy, name, tm=512):
    s = proj.shape[0]

    def body(uv_ref, gain_ref, ws_ref, bs_ref, dy_ref, duv_ref, dgain_ref, dws_ref, dbs_ref):
        params = _gmlp_params(gain_ref, ws_ref, bs_ref)
        dgain = dws = dbs = None
        for c in range(tm // GM_CHUNK):
            rows = pl.ds(c * GM_CHUNK, GM_CHUNK)
            u, v = _gmlp_load(uv_ref, c)
            _, vjp = jax.vjp(_gmlp_chunk, u, v, *params)
            du, dv, dg, dw, db = vjp([dy_ref[rows, pl.ds(g * LANES, LANES)] for g in range(GM_GROUPS)])
            for g in range(GM_GROUPS):
                duv_ref[rows, pl.ds(g * LANES, LANES)] = du[g].astype(BF16)
                duv_ref[rows, pl.ds(GM_WIDTH + g * LANES, LANES)] = dv[g].astype(BF16)
            if c == 0:
                dgain, dws, dbs = dg, dw, db
            else:
                dgain = [p + q for p, q in zip(dgain, dg)]
                dws = [p + q for p, q in zip(dws, dw)]
                dbs = [p + q for p, q in zip(dbs, db)]
        first = pl.program_id(0) == 0
        for g in range(GM_GROUPS):
            _acc(dgain_ref.at[:, pl.ds(g * LANES, LANES)], dgain[g], first)
            _acc(dws_ref.at[g], dws[g], first)
            _acc(dbs_ref.at[g], dbs[g], first)

    return pl.pallas_call(
        body,
        name=name,
        out_shape=(
            jax.ShapeDtypeStruct((s, 2 * GM_WIDTH), BF16),
            jax.ShapeDtypeStruct((1, GM_WIDTH), F32),
            jax.ShapeDtypeStruct((GM_GROUPS, GM_CHUNK, GM_CHUNK), F32),
            jax.ShapeDtypeStruct((GM_GROUPS, GM_CHUNK, 1), F32),
        ),
        grid=(s // tm,),
        in_specs=[
            pl.BlockSpec((tm, 2 * GM_WIDTH), lambda i: (i, P_UV // (2 * GM_WIDTH))),
            _full((1, GM_WIDTH)),
            _full((GM_GROUPS, GM_CHUNK, GM_CHUNK)),
            _full((GM_GROUPS, GM_CHUNK, 1)),
            pl.BlockSpec((tm, GM_WIDTH), lambda i: (i, 0)),
        ],
        out_specs=(
            pl.BlockSpec((tm, 2 * GM_WIDTH), lambda i: (i, 0)),
            _full((1, GM_WIDTH)),
            _full((GM_GROUPS, GM_CHUNK, GM_CHUNK)),
            _full((GM_GROUPS, GM_CHUNK, 1)),
        ),
        compiler_params=_cparams(("arbitrary",)),
    )(proj, gain, w_s, b_s, dy)


HEAD_BLOCK = MLA_HEADS * LANES


def _mla_heads(rope, q_all, k_all, kr, qg, kg):
    cosf, sinf, rot = rope

    def head_norm(t, gain):
        r = lax.rsqrt(jnp.sum(t * t, axis=-1, keepdims=True) * (1.0 / MLA_QK_DIM) + EPS)
        th = t * r * gain
        return th * cosf + _nn(th, rot) * sinf

    q = [head_norm(t, qg) * ATT_SCALE for t in q_all]
    k = [head_norm(t + kr, kg) for t in k_all]
    return q, k


def _mla_up(refs, w_scr, up_scr):
    cq_ref, ckv_ref, qn_ref, kvn_ref, wq_ref, wk_ref, wv_ref = refs
    wq_scr, wk_scr, wv_scr = w_scr
    qa_scr, ka_scr = up_scr

    @pl.when(pl.program_id(0) == 0)
    def _():
        for h in range(MLA_HEADS):
            cols = pl.ds(h * LANES, LANES)
            wq_scr[:, cols] = wq_ref[h]
            wk_scr[:, cols] = wk_ref[h]
            wv_scr[:, cols] = wv_ref[h]

    xq, rq = _rms_stats(cq_ref[...])
    xk, rk = _rms_stats(ckv_ref[...])
    qn = (xq * qn_ref[...]).astype(BF16)
    kvn = (xk * kvn_ref[...]).astype(BF16)
    qa_scr[...] = _nn(qn, wq_scr[...])
    ka_scr[...] = _nn(kvn, wk_scr[...])
    heads = lambda scr: [scr[:, pl.ds(h * LANES, LANES)] for h in range(MLA_HEADS)]
    return (xq, rq, qn), (xk, rk, kvn), heads(qa_scr), heads(ka_scr)


def _mla_scratch(tm):
    w = lambda rank: pltpu.VMEM((rank, HEAD_BLOCK), BF16)
    up = pltpu.VMEM((tm, HEAD_BLOCK), F32)
    return [w(MLA_Q_RANK), w(MLA_KV_RANK), w(MLA_KV_RANK), up, up]


def _mla_pre_specs(tm):
    row = lambda w, off: pl.BlockSpec((tm, w), lambda i: (i, off // w))
    return [
        row(MLA_Q_RANK, P_CQ),
        row(MLA_KV_RANK, P_CKV),
        row(LANES, P_KR),
        pl.BlockSpec((tm, LANES), lambda i: (i, 0)),
        pl.BlockSpec((tm, LANES), lambda i: (i, 0)),
        _full((LANES, LANES)),
        _full((1, MLA_Q_RANK)),
        _full((1, MLA_KV_RANK)),
        _full((MLA_HEADS, MLA_Q_RANK, LANES)),
        _full((MLA_HEADS, MLA_KV_RANK, LANES)),
        _full((MLA_HEADS, MLA_KV_RANK, LANES)),
        _full((1, LANES)),
        _full((1, LANES)),
    ]


def _mla_pre_fwd(proj, cosf, sinf, rot, qn_g, kvn_g, wq, wk, wv, qg, kg, name, tm=256):
    s = proj.shape[0]

    def body(cq_ref, ckv_ref, kr_ref, cos_ref, sin_ref, rot_ref, qn_ref, kvn_ref, wq_ref, wk_ref, wv_ref, qg_ref, kg_ref,
             q_ref, k_ref, v_ref, *scr):
        _, (_, _, kvn), q_all, k_all = _mla_up((cq_ref, ckv_ref, qn_ref, kvn_ref, wq_ref, wk_ref, wv_ref), scr[:3], scr[3:])
        v_ref[...] = _nn(kvn, scr[2][...]).astype(BF16)
        rope = (cos_ref[...], sin_ref[...], rot_ref[...])
        q, k = _mla_heads(rope, q_all, k_all, kr_ref[...], qg_ref[...], kg_ref[...])
        for h in range(MLA_HEADS):
            cols = pl.ds(h * LANES, LANES)
            q_ref[:, cols] = q[h].astype(BF16)
            k_ref[:, cols] = k[h].astype(BF16)

    out = jax.ShapeDtypeStruct((s, HEAD_BLOCK), BF16)
    blk = pl.BlockSpec((tm, HEAD_BLOCK), lambda i: (i, 0))
    return pl.pallas_call(
        body,
        name=name,
        out_shape=(out, out, out),
        grid=(s // tm,),
        in_specs=_mla_pre_specs(tm),
        out_specs=(blk, blk, blk),
        scratch_shapes=_mla_scratch(tm),
        compiler_params=_cparams(("arbitrary",)),
    )(proj, proj, proj, cosf, sinf, rot, qn_g, kvn_g, wq, wk, wv, qg, kg)


def _mla_pre_bwd(proj, cosf, sinf, rot, qn_g, kvn_g, wq, wk, wv, qg, kg, dq, dk, dv, name, tm=256):
    s = proj.shape[0]

    def body(cq_ref, ckv_ref, kr_ref, cos_ref, sin_ref, rot_ref, qn_ref, kvn_ref, wq_ref, wk_ref, wv_ref, qg_ref, kg_ref,
             dq_ref, dk_ref, dv_ref,
             dcq_ref, dckv_ref, dkr_ref, dqn_ref, dkvn_ref, dwq_ref, dwk_ref, dwv_ref, dqg_ref, dkg_ref, *scr):
        (xq, rq, qn), (xk, rk, kvn), q_all, k_all = _mla_up(
            (cq_ref, ckv_ref, qn_ref, kvn_ref, wq_ref, wk_ref, wv_ref), scr[:3], scr[3:])
        wq_scr, wk_scr, wv_scr, qa_scr, ka_scr = scr
        rope = (cos_ref[...], sin_ref[...], rot_ref[...])
        _, vjp = jax.vjp(functools.partial(_mla_heads, rope), q_all, k_all, kr_ref[...], qg_ref[...], kg_ref[...])
        heads = lambda ref: [ref[:, pl.ds(h * LANES, LANES)] for h in range(MLA_HEADS)]
        dq_all, dk_all, dkr, dqg, dkg = vjp((heads(dq_ref), heads(dk_ref)))
        for h in range(MLA_HEADS):
            cols = pl.ds(h * LANES, LANES)
            qa_scr[:, cols] = dq_all[h]
            ka_scr[:, cols] = dk_all[h]
        dqa, dka, dva = qa_scr[...], ka_scr[...], dv_ref[...]
        dqn = _nt(dqa, wq_scr[...])
        dkvn = _nt(dka, wk_scr[...]) + _nt(dva, wv_scr[...])
        dcq_ref[...] = _rms_bwd(xq, rq, qn_ref[...], dqn).astype(BF16)
        dckv_ref[...] = _rms_bwd(xk, rk, kvn_ref[...], dkvn).astype(BF16)
        dkr_ref[...] = dkr.astype(BF16)
        first = pl.program_id(0) == 0
        _acc_rows(dqn_ref, dqn * xq, first)
        _acc_rows(dkvn_ref, dkvn * xk, first)
        _acc(dqg_ref, dqg, first)
        _acc(dkg_ref, dkg, first)
        dwq, dwk, dwv = _tn(qn, dqa), _tn(kvn, dka), _tn(kvn, dva)
        for h in range(MLA_HEADS):
            cols = slice(h * LANES, (h + 1) * LANES)
            _acc(dwq_ref.at[h], dwq[:, cols], first)
            _acc(dwk_ref.at[h], dwk[:, cols], first)
            _acc(dwv_ref.at[h], dwv[:, cols], first)

    hb = pl.BlockSpec((tm, HEAD_BLOCK), lambda i: (i, 0))
    row = lambda w: pl.BlockSpec((tm, w), lambda i: (i, 0))
    sds = jax.ShapeDtypeStruct
    return pl.pallas_call(
        body,
        name=name,
        out_shape=(
            sds((s, MLA_Q_RANK), BF16), sds((s, MLA_KV_RANK), BF16), sds((s, LANES), BF16),
            sds((1, MLA_Q_RANK), F32), sds((1, MLA_KV_RANK), F32),
            sds((MLA_HEADS, MLA_Q_RANK, LANES), F32), sds((MLA_HEADS, MLA_KV_RANK, LANES), F32),
            sds((MLA_HEADS, MLA_KV_RANK, LANES), F32),
            sds((1, LANES), F32), sds((1, LANES), F32),
        ),
        grid=(s // tm,),
        in_specs=_mla_pre_specs(tm) + [hb, hb, hb],
        out_specs=(
            row(MLA_Q_RANK), row(MLA_KV_RANK), row(LANES),
            _full((1, MLA_Q_RANK)), _full((1, MLA_KV_RANK)),
            _full((MLA_HEADS, MLA_Q_RANK, LANES)), _full((MLA_HEADS, MLA_KV_RANK, LANES)),
            _full((MLA_HEADS, MLA_KV_RANK, LANES)),
            _full((1, LANES)), _full((1, LANES)),
        ),
        scratch_shapes=_mla_scratch(tm),
        compiler_params=_cparams(("arbitrary",)),
    )(proj, proj, proj, cosf, sinf, rot, qn_g, kvn_g, wq, wk, wv, qg, kg, dq, dk, dv)


ATT_SCALE = MLA_QK_DIM ** -0.5
NEG_BIG = -1e30


def _att_scores(q, k, diagonal, q_at=0):
    s = _nt(q, k)
    if diagonal:
        s = jnp.where(_iota(s.shape, 0) + q_at >= _iota(s.shape, 1), s, NEG_BIG)
    return s


def _att_scores_t(k, q, diagonal, q_at=0):
    s = _nt(k, q)
    if diagonal:
        s = jnp.where(_iota(s.shape, 0) <= _iota(s.shape, 1) + q_at, s, NEG_BIG)
    return s


ATT_SPLIT = 1


SUBLANES = 8


def _as_row(col_lanes):
    return jnp.transpose(col_lanes)[0:SUBLANES, :]


def _key_loop(lo, hi, t, step):
    rows = lambda i: pl.ds(pl.multiple_of(i * t, t), t)

    def body(i, carry):
        step(rows(lo + 2 * i))
        step(rows(lo + 2 * i + 1))
        return carry

    count = jnp.asarray(hi - lo, jnp.int32)
    lax.fori_loop(0, lax.div(count, 2), body, 0)

    @pl.when(lax.rem(count, 2) == 1)
    def _():
        step(rows(hi - 1))


def _attention_fwd(q, k, v, name, t=512, gather=()):
    s = q.shape[0]
    n = s // t
    ng = len(gather)

    def body(q_ref, k_ref, v_ref, *rest):
        shard_refs, rest = rest[:ng], rest[ng:]
        o_ref, lse_ref, lse_t_ref = rest[:3]
        out_refs, rest = rest[3:3 + ng], rest[3 + ng:]
        m_scr, acc_scr = rest[:2]
        qi = pl.program_id(1)
        if ng:
            @pl.when(jnp.logical_and(pl.program_id(0) == 0, qi == 0))
            def _():
                _gather_start(_gather_plan(shard_refs, out_refs, *rest[2:]))

        lane = _iota((1, LANES), 1)
        m_scr[...] = jnp.full_like(m_scr, NEG_BIG)
        acc_scr[...] = jnp.zeros_like(acc_scr)

        w = t // ATT_SPLIT

        def step(rows, diagonal=False):
            kb = k_ref[rows, :]
            vb = jnp.where(lane == MLA_V, 1.0, v_ref[rows, :].astype(F32)).astype(BF16)
            for c in range(ATT_SPLIT):
                cols = pl.ds(c * w, w)
                sc = _att_scores_t(kb, q_ref[cols, :], diagonal, c * w)
                m_old = m_scr[:, cols]
                m_new = jnp.maximum(m_old, jnp.max(sc, axis=0, keepdims=True))
                p = jnp.exp(sc - m_new)
                acc_scr[:, cols] = jnp.exp(m_old - m_new) * acc_scr[:, cols] + _tn(vb, p)
                m_scr[:, cols] = m_new

        _key_loop(0, qi, t, step)
        step(pl.ds(pl.multiple_of(qi * t, t), t), diagonal=True)
        acc = acc_scr[...]
        row = _iota((LANES, 1), 0)
        l = jnp.sum(jnp.where(row == MLA_V, acc, 0.0), axis=0, keepdims=True)
        o_ref[...] = jnp.transpose(jnp.where(row < MLA_V, acc / l, 0.0))
        lse = jnp.broadcast_to(m_scr[...] + jnp.log(l), (LANES, t))
        lse_ref[...] = jnp.transpose(lse)
        lse_t_ref[...] = lse[0:SUBLANES, :]
        if ng:
            @pl.when(jnp.logical_and(pl.program_id(0) == MLA_HEADS - 1, qi == n - 1))
            def _():
                _gather_finish(_gather_plan(shard_refs, out_refs, *rest[2:]))

    qspec = pl.BlockSpec((t, LANES), lambda h, qi: (qi, h))
    kspec = pl.BlockSpec((s, LANES), lambda h, qi: (0, h))
    out = jax.ShapeDtypeStruct((s, HEAD_BLOCK), F32)
    res = pl.pallas_call(
        body,
        name=name,
        out_shape=(out, out, jax.ShapeDtypeStruct((MLA_HEADS * SUBLANES, s), F32)) + _gathered_shapes(gather),
        grid=(MLA_HEADS, n),
        in_specs=[qspec, kspec, kspec] + [ANY] * ng,
        out_specs=(qspec, qspec, pl.BlockSpec((SUBLANES, t), lambda h, qi: (h, qi))) + (ANY,) * ng,
        scratch_shapes=[pltpu.VMEM((1, t), F32), pltpu.VMEM((LANES, t), F32)] + (_gather_scratch(ng) if ng else []),
        compiler_params=_cparams(("arbitrary", "arbitrary") if ng else ("parallel", "parallel")),
    )(q, k, v, *gather)
    return res[0], res[1], res[2], list(res[3:])


def _host_exchange(body, n_in, n_out, parts, last, pair=False):
    na = len(parts)
    if not na:
        return body
    plan, start, finish = (_pair_plan, _pair_start, _pair_finish) if pair else (
        _exchange_plan, _exchange_start, _exchange_finish)

    def hosted(*refs):
        ins, part_refs = refs[:n_in], refs[n_in:n_in + na]
        outs = refs[n_in + na:n_in + na + n_out]
        got_refs = refs[n_in + na + n_out:n_in + 2 * na + n_out]
        scratch, sems = refs[n_in + 2 * na + n_out:-3], refs[-3:]
        at = lambda step: jnp.logical_and(pl.program_id(0) == step[0], pl.program_id(1) == step[1])

        @pl.when(at((0, 0)))
        def _():
            start(plan(part_refs, got_refs, *sems))

        body(*ins, *outs, *scratch)

        @pl.when(at(last))
        def _():
            finish(plan(part_refs, got_refs, *sems))

    return hosted


def _attention_bwd(q, k, v, o, lse, lse_t, do, name, t=512, exchange_dq=(), exchange_dkv=()):
    s = q.shape[0]
    n = s // t
    last = (MLA_HEADS - 1, n - 1)
    sem = lambda parts: ("arbitrary", "arbitrary") if parts else ("parallel", "parallel")

    def dq_body(q_ref, k_ref, v_ref, o_ref, lse_ref, do_ref, dq_ref, delta_t_ref, acc_scr):
        qi = pl.program_id(1)
        do = do_ref[...]
        delta = jnp.sum(do * o_ref[...], axis=-1, keepdims=True)
        delta_t_ref[...] = _as_row(jnp.broadcast_to(delta, (t, LANES)))
        acc_scr[...] = jnp.zeros_like(acc_scr)

        w = t // ATT_SPLIT

        def step(rows, diagonal=False):
            kb, vb = k_ref[rows, :], v_ref[rows, :]
            for c in range(ATT_SPLIT):
                part = pl.ds(c * w, w)
                p = jnp.exp(_att_scores(q_ref[part, :], kb, diagonal, c * w) - lse_ref[part, 0:1])
                ds = p * (_nt(do_ref[part, :], vb) - delta[c * w:(c + 1) * w])
                acc_scr[part, :] += _nn(ds, kb)

        _key_loop(0, qi, t, step)
        step(pl.ds(pl.multiple_of(qi * t, t), t), diagonal=True)
        dq_ref[...] = acc_scr[...]

    def dkv_body(q_ref, k_ref, v_ref, lse_t_ref, delta_t_ref, do_ref, dk_ref, dv_ref, dk_scr, dv_scr):
        ki = pl.program_id(1)
        dk_scr[...] = jnp.zeros_like(dk_scr)
        dv_scr[...] = jnp.zeros_like(dv_scr)

        w = t // ATT_SPLIT

        def step(rows, diagonal=False):
            dv, dk = dv_scr[...], dk_scr[...]
            for c in range(ATT_SPLIT):
                part = pl.ds(pl.multiple_of(rows.start + c * w, w), w)
                qb = q_ref[part, :]
                dob = do_ref[part, :]
                p = jnp.exp(_att_scores_t(k_ref[...], qb, diagonal, c * w) - lse_t_ref[0:1, part])
                dv = dv + _nn(p, dob)
                ds = p * (_nt(v_ref[...], dob) - delta_t_ref[0:1, part])
                dk = dk + _nn(ds, qb)
            dv_scr[...] = dv
            dk_scr[...] = dk

        step(pl.ds(pl.multiple_of(ki * t, t), t), diagonal=True)
        _key_loop(ki + 1, n, t, step)
        dk_ref[...] = dk_scr[...]
        dv_ref[...] = dv_scr[...]

    out = jax.ShapeDtypeStruct((s, HEAD_BLOCK), F32)
    blk = pl.BlockSpec((t, LANES), lambda h, i: (i, h))
    head = pl.BlockSpec((s, LANES), lambda h, i: (0, h))
    row_blk = pl.BlockSpec((SUBLANES, t), lambda h, i: (h, i))
    row_head = pl.BlockSpec((SUBLANES, s), lambda h, i: (h, 0))
    na, nb = len(exchange_dq), len(exchange_dkv)
    dq, delta_t, *got_dq = pl.pallas_call(
        _host_exchange(dq_body, 6, 2, exchange_dq, last),
        name=name + "_dq",
        out_shape=(out, jax.ShapeDtypeStruct((MLA_HEADS * SUBLANES, s), F32)) + _same_shapes(exchange_dq),
        grid=(MLA_HEADS, n),
        in_specs=[blk, head, head, blk, blk, blk] + [ANY] * na,
        out_specs=(blk, row_blk) + (ANY,) * na,
        scratch_shapes=[pltpu.VMEM((t, LANES), F32)] + (_exchange_scratch(na) if na else []),
        compiler_params=_cparams(sem(exchange_dq)),
    )(q, k, v, o, lse, do, *exchange_dq)
    dk, dv, *got_dkv = pl.pallas_call(
        _host_exchange(dkv_body, 6, 2, exchange_dkv, last),
        name=name + "_dkv",
        out_shape=(out, out) + _same_shapes(exchange_dkv),
        grid=(MLA_HEADS, n),
        in_specs=[head, blk, blk, row_head, row_head, head] + [ANY] * nb,
        out_specs=(blk, blk) + (ANY,) * nb,
        scratch_shapes=[pltpu.VMEM((t, LANES), F32), pltpu.VMEM((t, LANES), F32)]
        + (_exchange_scratch(nb) if nb else []),
        compiler_params=_cparams(sem(exchange_dkv)),
    )(q, k, v, lse_t, delta_t, do, *exchange_dkv)
    return dq, dk, dv, got_dq, got_dkv


HALO = 8


def _conv_fwd(proj, w, b, name, tm=512):
    s = proj.shape[0]
    cb = P_XBC // SSD_CONV_DIM

    def body(x_ref, halo_ref, w_ref, b_ref, y_ref, cat_scr):
        i = pl.program_id(0)
        cat_scr[pl.ds(0, HALO), :] = jnp.where(i > 0, halo_ref[...], 0.0)
        cat_scr[pl.ds(HALO, tm), :] = x_ref[...]
        pre = b_ref[...]
        for j in range(SSD_CONV):
            pre = pre + w_ref[pl.ds(SSD_CONV - 1 - j, 1), :] * cat_scr[pl.ds(HALO - j, tm), :]
        y_ref[...] = _silu(pre)

    return pl.pallas_call(
        body,
        name=name,
        out_shape=jax.ShapeDtypeStruct((s, SSD_CONV_DIM), F32),
        grid=(s // tm,),
        in_specs=[
            pl.BlockSpec((tm, SSD_CONV_DIM), lambda i: (i, cb)),
            pl.BlockSpec((HALO, SSD_CONV_DIM), lambda i: (jnp.maximum(i * (tm // HALO) - 1, 0), cb)),
            _full((SSD_CONV, SSD_CONV_DIM)),
            _full((1, SSD_CONV_DIM)),
        ],
        out_specs=pl.BlockSpec((tm, SSD_CONV_DIM), lambda i: (i, 0)),
        scratch_shapes=[pltpu.VMEM((tm + HALO, SSD_CONV_DIM), F32)],
        compiler_params=_cparams(("parallel",)),
    )(proj, proj, w, b)


def _conv_bwd(proj, w, b, dact, name, tm=512):
    s = proj.shape[0]
    cb = P_XBC // SSD_CONV_DIM
    n = s // tm

    def pre_body(x_ref, halo_ref, w_ref, b_ref, dact_ref, dpre_ref, dw_ref, db_ref, cat_scr):
        i = pl.program_id(0)
        cat_scr[pl.ds(0, HALO), :] = jnp.where(i > 0, halo_ref[...], 0.0)
        cat_scr[pl.ds(HALO, tm), :] = x_ref[...]
        pre = b_ref[...]
        for j in range(SSD_CONV):
            pre = pre + w_ref[pl.ds(SSD_CONV - 1 - j, 1), :] * cat_scr[pl.ds(HALO - j, tm), :]
        sg = _sigmoid(pre)
        dpre = dact_ref[...] * (sg * (1.0 + pre * (1.0 - sg)))
        dpre_ref[...] = dpre
        first = i == 0
        _acc_rows(db_ref, dpre, first)
        for j in range(SSD_CONV):
            _acc_rows(dw_ref.at[pl.ds(SSD_CONV - 1 - j, 1), :], dpre * cat_scr[pl.ds(HALO - j, tm), :], first)

    dpre, dw, db = pl.pallas_call(
        pre_body,
        name=name + "_pre",
        out_shape=(
            jax.ShapeDtypeStruct((s, SSD_CONV_DIM), F32),
            jax.ShapeDtypeStruct((SSD_CONV, SSD_CONV_DIM), F32),
            jax.ShapeDtypeStruct((1, SSD_CONV_DIM), F32),
        ),
        grid=(n,),
        in_specs=[
            pl.BlockSpec((tm, SSD_CONV_DIM), lambda i: (i, cb)),
            pl.BlockSpec((HALO, SSD_CONV_DIM), lambda i: (jnp.maximum(i * (tm // HALO) - 1, 0), cb)),
            _full((SSD_CONV, SSD_CONV_DIM)),
            _full((1, SSD_CONV_DIM)),
            pl.BlockSpec((tm, SSD_CONV_DIM), lambda i: (i, 0)),
        ],
        out_specs=(
            pl.BlockSpec((tm, SSD_CONV_DIM), lambda i: (i, 0)),
            _full((SSD_CONV, SSD_CONV_DIM)),
            _full((1, SSD_CONV_DIM)),
        ),
        scratch_shapes=[pltpu.VMEM((tm + HALO, SSD_CONV_DIM), F32)],
        compiler_params=_cparams(("arbitrary",)),
    )(proj, proj, w, b, dact)

    def dx_body(d_ref, halo_ref, w_ref, dx_ref, cat_scr):
        i = pl.program_id(0)
        cat_scr[pl.ds(0, tm), :] = d_ref[...]
        cat_scr[pl.ds(tm, HALO), :] = jnp.where(i < n - 1, halo_ref[...], 0.0)
        dx = jnp.zeros((tm, SSD_CONV_DIM), F32)
        for j in range(SSD_CONV):
            dx = dx + w_ref[pl.ds(SSD_CONV - 1 - j, 1), :] * cat_scr[pl.ds(j, tm), :]
        dx_ref[...] = dx.astype(BF16)

    dx = pl.pallas_call(
        dx_body,
        name=name + "_dx",
        out_shape=jax.ShapeDtypeStruct((s, SSD_CONV_DIM), BF16),
        grid=(n,),
        in_specs=[
            pl.BlockSpec((tm, SSD_CONV_DIM), lambda i: (i, 0)),
            pl.BlockSpec((HALO, SSD_CONV_DIM), lambda i: (jnp.minimum((i + 1) * (tm // HALO), s // HALO - 1), 0)),
            _full((SSD_CONV, SSD_CONV_DIM)),
        ],
        out_specs=pl.BlockSpec((tm, SSD_CONV_DIM), lambda i: (i, 0)),
        scratch_shapes=[pltpu.VMEM((tm + HALO, SSD_CONV_DIM), F32)],
        compiler_params=_cparams(("parallel",)),
    )(dpre, dpre, w)
    return dx, dw, db


N_PAIR = SSD_HEADS // 2


def _ssd_chunk(xs, bm, cm, z, dtp, state, dtb, alog, dskip, ng):
    t = SSD_CHUNK
    lane = _iota((1, LANES), 1)
    row = _iota((LANES, 1), 0)
    dt_all = jnp.where(lane < SSD_HEADS, _softplus(dtp + dtb), 0.0)
    da = dt_all * (-jnp.exp(alog))
    causal = _iota((t, t), 0) >= _iota((t, t), 1)
    cs = _exact_nn(causal.astype(F32), da)
    cs_t = cs.T
    tot = jnp.sum(da, axis=0, keepdims=True)
    col = lambda m, h: jnp.sum(jnp.where(lane == h, m, 0.0), axis=1, keepdims=True)
    rowv = lambda m, h: jnp.sum(jnp.where(row == h, m, 0.0), axis=0, keepdims=True)
    low = lane < SSD_HEAD_DIM
    cb = [_nt(cm[g], bm[g]) for g in range(SSD_GROUPS)]
    gated, new_state = [], []
    for j in range(N_PAIR):
        g = j // (N_PAIR // SSD_GROUPS)
        h0, h1 = 2 * j, 2 * j + 1
        y = jnp.zeros((t, LANES), F32)
        for h, mask in ((h0, low), (h1, jnp.logical_not(low))):
            lmat = jnp.exp(jnp.where(causal, col(cs, h) - rowv(cs_t, h), NEG_BIG))
            y = y + _nn(cb[g] * lmat, jnp.where(mask, xs[j] * col(dt_all, h), 0.0))
        cs_p = jnp.where(low, col(cs, h0), col(cs, h1))
        dt_p = jnp.where(low, col(dt_all, h0), col(dt_all, h1))
        tot_p = jnp.where(low, col(tot, h0), col(tot, h1))
        tot_c = jnp.where(row < SSD_HEAD_DIM, col(tot, h0), col(tot, h1))
        d_p = jnp.where(low, col(dskip, h0), col(dskip, h1))
        xdt = xs[j] * dt_p
        y = y + _nt(cm[g], state[j]) * jnp.exp(cs_p) + xs[j] * d_p
        new_state.append(state[j] * jnp.exp(tot_c) + _tn(xdt * jnp.exp(tot_p - cs_p), bm[g]))
        gated.append(y * _silu(z[j]))
    out = []
    per_group = N_PAIR // SSD_GROUPS
    for g in range(SSD_GROUPS):
        blocks = gated[g * per_group:(g + 1) * per_group]
        ms = sum(jnp.sum(v * v, axis=-1, keepdims=True) for v in blocks) * (1.0 / (per_group * LANES))
        r = lax.rsqrt(ms + EPS)
        out += [v * r * ng[g * per_group + i] for i, v in enumerate(blocks)]
    return out, new_state


def _ssd_specs(rev, nc):
    idx = (lambda c: nc - 1 - c) if rev else (lambda c: c)
    t = SSD_CHUNK
    return [
        pl.BlockSpec((t, SSD_CONV_DIM), lambda c: (idx(c), 0)),
        pl.BlockSpec((t, SSD_INNER), lambda c: (idx(c), P_Z // SSD_INNER)),
        pl.BlockSpec((t, LANES), lambda c: (idx(c), P_DT // LANES)),
        _full((1, LANES)), _full((1, LANES)), _full((1, LANES)), _full((1, SSD_INNER)),
    ]


def _ssd_args(act_ref, z_ref, dt_ref, dtb_ref, alog_ref, dskip_ref, ng_ref):
    blk = lambda ref, off, n: [ref[:, pl.ds(off + i * LANES, LANES)] for i in range(n)]
    xs = blk(act_ref, 0, N_PAIR)
    bm = blk(act_ref, SSD_INNER, SSD_GROUPS)
    cm = blk(act_ref, SSD_INNER + SSD_GROUPS * SSD_STATE, SSD_GROUPS)
    return xs, bm, cm, blk(z_ref, 0, N_PAIR), dt_ref[...], dtb_ref[...], alog_ref[...], dskip_ref[...], blk(ng_ref, 0, N_PAIR)


def _ssd_fwd(act, proj, dtb, alog, dskip, ng, name):
    s = act.shape[0]
    nc = s // SSD_CHUNK

    def body(act_ref, z_ref, dt_ref, dtb_ref, alog_ref, dskip_ref, ng_ref, y_ref, st_ref, st_scr):
        @pl.when(pl.program_id(0) == 0)
        def _():
            st_scr[...] = jnp.zeros_like(st_scr)

        xs, bm, cm, z, dtp, dtb_v, alog_v, dskip_v, ng_v = _ssd_args(act_ref, z_ref, dt_ref, dtb_ref, alog_ref, dskip_ref, ng_ref)
        state = [st_scr[j] for j in range(N_PAIR)]
        st_ref[0] = st_scr[...]
        y, new_state = _ssd_chunk(xs, bm, cm, z, dtp, state, dtb_v, alog_v, dskip_v, ng_v)
        for j in range(N_PAIR):
            y_ref[:, pl.ds(j * LANES, LANES)] = y[j]
            st_scr[j] = new_state[j]

    return pl.pallas_call(
        body,
        name=name,
        out_shape=(
            jax.ShapeDtypeStruct((s, SSD_INNER), F32),
            jax.ShapeDtypeStruct((nc, N_PAIR, LANES, SSD_STATE), F32),
        ),
        grid=(nc,),
        in_specs=_ssd_specs(False, nc),
        out_specs=(
            pl.BlockSpec((SSD_CHUNK, SSD_INNER), lambda c: (c, 0)),
            pl.BlockSpec((1, N_PAIR, LANES, SSD_STATE), lambda c: (c, 0, 0, 0)),
        ),
        scratch_shapes=[pltpu.VMEM((N_PAIR, LANES, SSD_STATE), F32)],
        compiler_params=_cparams(("arbitrary",)),
    )(act, proj, proj, dtb, alog, dskip, ng)


def _ssd_bwd(act, proj, dtb, alog, dskip, ng, states, dy, name):
    s = act.shape[0]
    nc = s // SSD_CHUNK

    def body(act_ref, z_ref, dt_ref, dtb_ref, alog_ref, dskip_ref, ng_ref, st_ref, dy_ref,
             dact_ref, dz_ref, ddt_ref, ddtb_ref, dalog_ref, ddskip_ref, dng_ref, dst_scr):
        first = pl.program_id(0) == 0

        @pl.when(first)
        def _():
            dst_scr[...] = jnp.zeros_like(dst_scr)

        xs, bm, cm, z, dtp, dtb_v, alog_v, dskip_v, ng_v = _ssd_args(act_ref, z_ref, dt_ref, dtb_ref, alog_ref, dskip_ref, ng_ref)
        state = [st_ref[0, j] for j in range(N_PAIR)]
        _, vjp = jax.vjp(_ssd_chunk, xs, bm, cm, z, dtp, state, dtb_v, alog_v, dskip_v, ng_v)
        dy_v = [dy_ref[:, pl.ds(j * LANES, LANES)] for j in range(N_PAIR)]
        dxs, dbm, dcm, dz, ddtp, dstate, ddtb, dalog, ddskip, dng = vjp((dy_v, [dst_scr[j] for j in range(N_PAIR)]))
        for i, v in enumerate(dxs + dbm + dcm):
            dact_ref[:, pl.ds(i * LANES, LANES)] = v
        for j in range(N_PAIR):
            dz_ref[:, pl.ds(j * LANES, LANES)] = dz[j].astype(BF16)
            dst_scr[j] = dstate[j]
            _acc(dng_ref.at[:, pl.ds(j * LANES, LANES)], dng[j], first)
        ddt_ref[...] = ddtp.astype(BF16)
        _acc(ddtb_ref, ddtb, first)
        _acc(dalog_ref, dalog, first)
        _acc(ddskip_ref, ddskip, first)

    rv = lambda c: nc - 1 - c
    sds = jax.ShapeDtypeStruct
    return pl.pallas_call(
        body,
        name=name,
        out_shape=(
            sds((s, SSD_CONV_DIM), F32), sds((s, SSD_INNER), BF16), sds((s, LANES), BF16),
            sds((1, LANES), F32), sds((1, LANES), F32), sds((1, LANES), F32), sds((1, SSD_INNER), F32),
        ),
        grid=(nc,),
        in_specs=_ssd_specs(True, nc) + [
            pl.BlockSpec((1, N_PAIR, LANES, SSD_STATE), lambda c: (rv(c), 0, 0, 0)),
            pl.BlockSpec((SSD_CHUNK, SSD_INNER), lambda c: (rv(c), 0)),
        ],
        out_specs=(
            pl.BlockSpec((SSD_CHUNK, SSD_CONV_DIM), lambda c: (rv(c), 0)),
            pl.BlockSpec((SSD_CHUNK, SSD_INNER), lambda c: (rv(c), 0)),
            pl.BlockSpec((SSD_CHUNK, LANES), lambda c: (rv(c), 0)),
            _full((1, LANES)), _full((1, LANES)), _full((1, LANES)), _full((1, SSD_INNER)),
        ),
        scratch_shapes=[pltpu.VMEM((N_PAIR, LANES, SSD_STATE), F32)],
        compiler_params=_cparams(("arbitrary",)),
    )(act, proj, proj, dtb, alog, dskip, ng, states, dy)


def _merge_specs(tm):
    row = lambda w: pl.BlockSpec((tm, w), lambda i: (i, 0))
    return [
        row(GM_WIDTH), row(HEAD_BLOCK), row(SSD_INNER),
        pl.BlockSpec((tm, N_BRANCH * D_MODEL), lambda i: (i, P_GATES // (N_BRANCH * D_MODEL))),
        row(D_MODEL),
        _full((GM_WIDTH, D_MODEL)), _full((HEAD_BLOCK, D_MODEL)), _full((SSD_INNER, D_MODEL)), _full((D_MODEL, D_MODEL)),
    ]


def _merge_fwd(ya, yb, yc, proj, x1, pa, pb, pc, wo, name, tm=256):
    s = x1.shape[0]

    def body(ya_ref, yb_ref, yc_ref, gates_ref, x1_ref, pa_ref, pb_ref, pc_ref, wo_ref, x2_ref, mg_ref):
        merged = jnp.zeros((tm, D_MODEL), F32)
        for i, (y_ref, p_ref) in enumerate(((ya_ref, pa_ref), (yb_ref, pb_ref), (yc_ref, pc_ref))):
            gate = _sigmoid(gates_ref[:, pl.ds(i * D_MODEL, D_MODEL)])
            merged = merged + gate * _nn(y_ref[...], p_ref[...])
        mg_ref[...] = merged.astype(BF16)
        x2_ref[...] = x1_ref[...] + _nn(merged, wo_ref[...])

    row = lambda w: pl.BlockSpec((tm, w), lambda i: (i, 0))
    return pl.pallas_call(
        body,
        name=name,
        out_shape=(jax.ShapeDtypeStruct((s, D_MODEL), F32), jax.ShapeDtypeStruct((s, D_MODEL), BF16)),
        grid=(s // tm,),
        in_specs=_merge_specs(tm),
        out_specs=(row(D_MODEL), row(D_MODEL)),
        compiler_params=_cparams(("parallel",)),
    )(ya, yb, yc, proj, x1, pa, pb, pc, wo)


def _merge_bwd(ya, yb, yc, proj, dx2, pa, pb, pc, wo, name, tm=256):
    s = dx2.shape[0]

    def body(ya_ref, yb_ref, yc_ref, gates_ref, dx2_ref, pa_ref, pb_ref, pc_ref, wo_ref,
             dya_ref, dyb_ref, dyc_ref, dgates_ref, ta_ref, tb_ref, tc_ref):
        dmerged = _nt(dx2_ref[...], wo_ref[...])
        branches = ((ya_ref, pa_ref, dya_ref, ta_ref), (yb_ref, pb_ref, dyb_ref, tb_ref), (yc_ref, pc_ref, dyc_ref, tc_ref))
        for i, (y_ref, p_ref, dy_ref, t_ref) in enumerate(branches):
            cols = pl.ds(i * D_MODEL, D_MODEL)
            gate = _sigmoid(gates_ref[:, cols])
            dgates_ref[:, cols] = (dmerged * _nn(y_ref[...], p_ref[...]) * gate * (1.0 - gate)).astype(BF16)
            dt = (dmerged * gate).astype(BF16)
            t_ref[...] = dt
            dy_ref[...] = _nt(dt, p_ref[...])

    row = lambda w: pl.BlockSpec((tm, w), lambda i: (i, 0))
    sds = jax.ShapeDtypeStruct
    return pl.pallas_call(
        body,
        name=name,
        out_shape=(
            sds((s, GM_WIDTH), F32), sds((s, HEAD_BLOCK), F32), sds((s, SSD_INNER), F32),
            sds((s, N_BRANCH * D_MODEL), BF16),
            sds((s, D_MODEL), BF16), sds((s, D_MODEL), BF16), sds((s, D_MODEL), BF16),
        ),
        grid=(s // tm,),
        in_specs=_merge_specs(tm),
        out_specs=(row(GM_WIDTH), row(HEAD_BLOCK), row(SSD_INNER), row(N_BRANCH * D_MODEL),
                   row(D_MODEL), row(D_MODEL), row(D_MODEL)),
        compiler_params=_cparams(("parallel",)),
    )(ya, yb, yc, proj, dx2, pa, pb, pc, wo)


def _loss_head(y, target, name, tm=512):
    s, d = y.shape

    def body(y_ref, t_ref, dy_ref, loss_ref):
        err = y_ref[...] - t_ref[...]
        dy_ref[...] = err * (1.0 / d)
        part = jnp.sum(jnp.sum(err * err, axis=1, keepdims=True), axis=0, keepdims=True) * (0.5 / d)
        _acc(loss_ref, jnp.broadcast_to(part, (1, LANES)), pl.program_id(0) == 0)

    return pl.pallas_call(
        body,
        name=name,
        out_shape=(jax.ShapeDtypeStruct((s, d), F32), jax.ShapeDtypeStruct((1, LANES), F32)),
        grid=(s // tm,),
        in_specs=[pl.BlockSpec((tm, d), lambda i: (i, 0)), pl.BlockSpec((tm, d), lambda i: (i, 0))],
        out_specs=(pl.BlockSpec((tm, d), lambda i: (i, 0)), _full((1, LANES))),
        compiler_params=_cparams(("arbitrary",)),
    )(y, target)


IN_SHARD = IN_COLS // N_DEV
P_OF_PIECE = (P_UV, P_CQ, P_CKV, P_KR + MLA_NOPE, P_Z, P_XBC, P_DT, P_GATES)


def _pad_lanes(w, n=LANES):
    return jnp.pad(w, [(0, 0)] * (w.ndim - 1) + [(0, n - w.shape[-1])])


def _in_proj_layout(blocks):
    def cols(i):
        a, b, out = IN_OFFSETS[i], IN_OFFSETS[i] + IN_WIDTHS[i], []
        while a < b:
            k, lo = divmod(a, IN_SHARD)
            hi = min(IN_SHARD, lo + b - a)
            out.append(blocks[k, :, lo:hi])
            a += hi - lo
        return out

    zeros = lambda n: jnp.zeros((D_MODEL, n), blocks.dtype)
    uv, cq, ckv, kr, z, xbc, dt, gates = (cols(i) for i in range(8))
    return jnp.concatenate(uv + xbc + z + [zeros(MLA_NOPE)] + kr + [zeros(LANES - MLA_QK_DIM)] + cq + gates + ckv
                           + dt + [zeros(LANES - SSD_HEADS)], axis=1)


def _in_proj_unlayout(dw):
    out = []
    for k in range(N_DEV):
        a, b, parts = k * IN_SHARD, (k + 1) * IN_SHARD, []
        for i in range(8):
            lo, hi = max(a, IN_OFFSETS[i]), min(b, IN_OFFSETS[i] + IN_WIDTHS[i])
            if lo < hi:
                at = P_OF_PIECE[i] + lo - IN_OFFSETS[i]
                parts.append(dw[:, at:at + hi - lo])
        out.append(jnp.concatenate(parts, axis=1))
    return jnp.stack(out)


def _row(v, n=None):
    v = v.reshape(1, -1)
    return v if n is None else jnp.pad(v, ((0, 0), (0, n - v.shape[1])))


GATHERED = ("ffn_w_in", "ffn_w_out", "w_in", "mla_w_uq", "mla_w_ukv", "ssd_conv_w", "w_branch", "w_out")


def _layer_shards(w, l):
    pair = lambda a, b: jnp.stack([w[a][l], w[b][l]]).astype(BF16)
    one = lambda n: w[n][l].astype(BF16)
    return [pair("ffn1_w_in", "ffn2_w_in"), pair("ffn1_w_out", "ffn2_w_out"), one("w_in"), one("mla_w_uq"),
            one("mla_w_ukv"), one("ssd_conv_w"), one("w_branch"), one("w_out")]


def _layer_weights(gathered, vec, l):
    gw = dict(zip(GATHERED, gathered))
    kv = gw["mla_w_ukv"]
    branch = jnp.moveaxis(gw["w_branch"], 0, 2).reshape(N_BRANCH, GM_WIDTH, D_MODEL)
    return dict(
        ffn_w_in=gw["ffn_w_in"], ffn_w_out=gw["ffn_w_out"], ffn1_at=0, ffn2_at=1,
        ffn1_norm=_row(vec["ffn1_norm"][l]), ffn2_norm=_row(vec["ffn2_norm"][l]),
        mix_norm=_row(vec["mix_norm"][l]), w_in=_in_proj_layout(gw["w_in"]),
        gm_v_norm=_row(vec["gm_v_norm"][l]), gm_w_s=vec["gm_w_s"][l], gm_b_s=vec["gm_b_s"][l][..., None],
        q_norm=_row(vec["mla_q_norm"][l]), kv_norm=_row(vec["mla_kv_norm"][l]),
        wq=_pad_lanes(gw["mla_w_uq"]), wk=_pad_lanes(kv[:, :, :MLA_NOPE]), wv=_pad_lanes(kv[:, :, MLA_NOPE:]),
        q_gain=_row(vec["mla_q_gain"][l], LANES), k_gain=_row(vec["mla_k_gain"][l], LANES),
        conv_w=jnp.moveaxis(gw["ssd_conv_w"], 0, 1).reshape(SSD_CONV, SSD_CONV_DIM).astype(F32),
        conv_b=_row(vec["ssd_conv_b"][l]),
        dt_bias=_row(vec["ssd_dt_bias"][l], LANES), a_log=_row(vec["ssd_a_log"][l], LANES),
        d_skip=_row(vec["ssd_d"][l], LANES), ssd_norm=_row(vec["ssd_norm"][l]),
        pa=branch[0],
        pb=jnp.pad(branch[1].reshape(MLA_HEADS, MLA_V, D_MODEL), ((0, 0), (0, LANES - MLA_V), (0, 0))).reshape(HEAD_BLOCK, D_MODEL),
        pc=branch[2], wo=gw["w_out"].reshape(D_MODEL, D_MODEL),
    )


def _layer_fwd(x, k, rope, next_shards=()):
    cosf, sinf, rot = rope
    x1, gu1 = _ffn_fwd(x, k["ffn1_norm"], k["ffn_w_in"], k["ffn_w_out"], k["ffn1_at"], "ffn1_fwd")
    h = _rmsnorm_fwd(x1, k["mix_norm"], "mix_norm_fwd")
    proj = _matmul(h, k["w_in"], "nn", F32, "in_proj_fwd", tn=2176)
    ya = _gmlp_fwd(proj, k["gm_v_norm"], k["gm_w_s"], k["gm_b_s"], "gmlp_fwd")
    q, kk, v = _mla_pre_fwd(proj, cosf, sinf, rot, k["q_norm"], k["kv_norm"], k["wq"], k["wk"], k["wv"],
                            k["q_gain"], k["k_gain"], "mla_pre_fwd")
    yb, lse, lse_t, next_gathered = _attention_fwd(q, kk, v, "attention_fwd", gather=next_shards)
    act = _conv_fwd(proj, k["conv_w"], k["conv_b"], "conv_fwd")
    yc, states = _ssd_fwd(act, proj, k["dt_bias"], k["a_log"], k["d_skip"], k["ssd_norm"], "ssd_fwd")
    x2, merged = _merge_fwd(ya, yb, yc, proj, x1, k["pa"], k["pb"], k["pc"], k["wo"], "merge_fwd")
    x3, gu2 = _ffn_fwd(x2, k["ffn2_norm"], k["ffn_w_in"], k["ffn_w_out"], k["ffn2_at"], "ffn2_fwd")
    saved = dict(x=x, x1=x1, x2=x2, gu1=gu1, gu2=gu2, h=h, proj=proj, ya=ya, yb=yb, yc=yc, q=q, k=kk, v=v,
                 lse=lse, lse_t=lse_t, act=act, states=states, merged=merged)
    return x3, saved, next_gathered


def _layer_bwd(dx3, k, sv, rope, above=()):
    cosf, sinf, rot = rope
    g = {}
    dx2, g["ffn2_norm"], g["ffn2_w_in"], g["ffn2_w_out"], theirs = _ffn_bwd(
        sv["x2"], k["ffn2_norm"], k["ffn_w_in"], k["ffn_w_out"], k["ffn2_at"], sv["gu2"], dx3, "ffn2_bwd",
        pair=tuple(above))
    exchange = _pair_sums(above, theirs, "grads")
    proj = sv["proj"]
    dya, dyb, dyc, dgates, ta, tb, tc = _merge_bwd(sv["ya"], sv["yb"], sv["yc"], proj, dx2, k["pa"], k["pb"], k["pc"],
                                                   k["wo"], "merge_bwd")
    g["w_out"] = _matmul(sv["merged"], dx2, "tn", BF16, "w_out_grad").reshape(N_DEV, D_MODEL // N_DEV, D_MODEL)
    dpa = _matmul(sv["ya"], ta, "tn", BF16, "branch_a_grad")
    dpb = _matmul(sv["yb"], tb, "tn", BF16, "branch_b_grad")
    dpc = _matmul(sv["yc"], tc, "tn", BF16, "branch_c_grad")
    branch = jnp.stack([dpa, dpb.reshape(MLA_HEADS, LANES, D_MODEL)[:, :MLA_V].reshape(GM_WIDTH, D_MODEL), dpc])
    g["w_branch"] = jnp.moveaxis(branch.reshape(N_BRANCH, GM_WIDTH, N_DEV, LANES), 2, 0)
    duv, g["gm_v_norm"], g["gm_w_s"], dbs = _gmlp_bwd(proj, k["gm_v_norm"], k["gm_w_s"], k["gm_b_s"], dya, "gmlp_bwd")
    g["gm_b_s"] = dbs[..., 0]
    dq, dk, dv, got_a, got_b = _attention_bwd(sv["q"], sv["k"], sv["v"], sv["yb"], sv["lse"], sv["lse_t"], dyb,
                                               "attention_bwd", exchange_dq=tuple(exchange[:1]),
                                               exchange_dkv=tuple(exchange[1:]))
    dcq, dckv, dkr, dqn, dkvn, dwq, dwk, dwv, dqg, dkg = _mla_pre_bwd(
        proj, cosf, sinf, rot, k["q_norm"], k["kv_norm"], k["wq"], k["wk"], k["wv"], k["q_gain"], k["k_gain"],
        dq, dk, dv, "mla_pre_bwd")
    g["mla_q_norm"], g["mla_kv_norm"] = dqn, dkvn
    g["mla_w_uq"] = dwq[:, :, :MLA_QK_DIM].astype(BF16)
    g["mla_w_ukv"] = jnp.concatenate([dwk[:, :, :MLA_NOPE], dwv[:, :, :MLA_V]], axis=-1).astype(BF16)
    g["mla_q_gain"], g["mla_k_gain"] = dqg[:, :MLA_QK_DIM], dkg[:, :MLA_QK_DIM]
    dact, dz, ddt, ddtb, dalog, ddsk, g["ssd_norm"] = _ssd_bwd(
        sv["act"], proj, k["dt_bias"], k["a_log"], k["d_skip"], k["ssd_norm"], sv["states"], dyc, "ssd_bwd")
    g["ssd_dt_bias"], g["ssd_a_log"], g["ssd_d"] = ddtb[:, :SSD_HEADS], dalog[:, :SSD_HEADS], ddsk[:, :SSD_HEADS]
    dxbc, dcw, g["ssd_conv_b"] = _conv_bwd(proj, k["conv_w"], k["conv_b"], dact, "conv_bwd")
    g["ssd_conv_w"] = jnp.moveaxis(dcw.reshape(SSD_CONV, N_DEV, LANES), 1, 0).astype(BF16)
    dproj = jnp.concatenate([duv, dxbc, dz, dkr, dcq, dgates, dckv, ddt], axis=1)
    dh = _matmul(dproj, k["w_in"], "nt", BF16, "in_proj_dh", tk=2176)
    g["w_in"] = _in_proj_unlayout(_matmul(sv["h"], dproj, "tn", BF16, "in_proj_grad", tn=2176))
    dx1, g["mix_norm"] = _rmsnorm_bwd(sv["x1"], k["mix_norm"], dh, dx2, "mix_norm_bwd")
    dx, g["ffn1_norm"], g["ffn1_w_in"], g["ffn1_w_out"], _ = _ffn_bwd(
        sv["x"], k["ffn1_norm"], k["ffn_w_in"], k["ffn_w_out"], k["ffn1_at"], sv["gu1"], dx1, "ffn1_bwd")
    return dx, g, got_a + got_b


def _layer_contribs(g):
    pair = lambda a, b: jnp.stack([g[a], g[b]], axis=1)
    arrays = [pair("ffn1_w_in", "ffn2_w_in"), pair("ffn1_w_out", "ffn2_w_out"), g["w_in"], g["mla_w_uq"],
              g["mla_w_ukv"], g["ssd_conv_w"], g["w_branch"], g["w_out"]]
    return [a.reshape(N_DEV, -1, a.shape[-1]) for a in arrays]


def _pair_sums(contribs, theirs, name):
    return [_pair_sum(c, t, "%s_pair_sum_%s" % (name, n), _tile_rows(c.shape[1], c.shape[2]))
            for n, c, t in zip(GATHERED, contribs, theirs)]


def _rope_tables(positions):
    s = positions.shape[0]
    inv_freq = 1.0 / (ROPE_THETA ** (jnp.arange(0, MLA_ROPE, 2, dtype=F32) / MLA_ROPE))
    ang = positions.astype(F32)[:, None] * inv_freq
    cos, sin = jnp.cos(ang), jnp.sin(ang)
    tail = LANES - MLA_QK_DIM
    cosf = jnp.concatenate([jnp.ones((s, MLA_NOPE), F32), cos, cos, jnp.ones((s, tail), F32)], axis=1)
    sinf = jnp.concatenate([jnp.zeros((s, MLA_NOPE), F32), sin, sin, jnp.zeros((s, tail), F32)], axis=1)
    half = MLA_ROPE // 2
    rot = np.zeros((LANES, LANES), np.float32)
    for i in range(half):
        rot[MLA_NOPE + half + i, MLA_NOPE + i] = -1.0
        rot[MLA_NOPE + i, MLA_NOPE + half + i] = 1.0
    return cosf, sinf, jnp.asarray(rot)


MATRICES = ("ffn1_w_in", "ffn1_w_out", "w_in", "mla_w_uq", "mla_w_ukv", "ssd_conv_w", "w_branch", "w_out", "ffn2_w_in",
            "ffn2_w_out")
VECTORS = ("ffn1_norm", "mix_norm", "gm_v_norm", "gm_w_s", "gm_b_s", "mla_q_norm", "mla_kv_norm", "mla_q_gain",
           "mla_k_gain", "ssd_conv_b", "ssd_dt_bias", "ssd_a_log", "ssd_d", "ssd_norm", "ffn2_norm")
WEIGHTS = ("ffn1_norm", "ffn1_w_in", "ffn1_w_out", "mix_norm", "w_in", "gm_v_norm", "gm_w_s", "gm_b_s", "mla_q_norm",
           "mla_kv_norm", "mla_w_uq", "mla_w_ukv", "mla_q_gain", "mla_k_gain", "ssd_conv_w", "ssd_conv_b", "ssd_dt_bias",
           "ssd_a_log", "ssd_d", "ssd_norm", "w_branch", "w_out", "ffn2_norm", "ffn2_w_in", "ffn2_w_out")


def _local_step(x, positions, target, vec, shards=None, gathered=None):
    rope = _rope_tables(positions)
    depth = vec["ffn1_norm"].shape[0]
    saved = []
    here = _gather_layer(shards[0], "layer0_all_gather") if gathered is None else gathered[0]
    for l in range(depth):
        k = _layer_weights(here, vec, l)
        ahead = shards[l + 1] if gathered is None and l + 1 < depth else ()
        x, sv, here = _layer_fwd(x, k, rope, ahead)
        if gathered is not None and l + 1 < depth:
            here = gathered[l + 1]
        saved.append((k, sv))
    dy, loss = _loss_head(x, target, "loss_head")
    grads, reduced, above = [], [], ()
    for k, sv in reversed(saved):
        dy, g, got = _layer_bwd(dy, k, sv, rope, above=above)
        grads.append(g)
        if gathered is None:
            if above:
                reduced.append(got)
            above = _layer_contribs(g)
    if gathered is None:
        sums = _pair_sums(above, _pair_exchange(above, "layer0_grads_pair_exchange"), "layer0_grads")
        reduced.append(_chip_exchange(sums, "layer0_grads_chip_exchange"))
    grads.reverse()
    reduced.reverse()
    out = {n: jnp.stack([g[n].reshape(vec[n].shape[1:]) for g in grads]) for n in VECTORS}
    if gathered is None:
        out["reduced"] = reduced
    else:
        out.update({n: [g[n] for g in grads] for n in MATRICES})
    return loss[0, 0], dy, out


MESH = pl.DeviceIdType.MESH
N_CHIP = 4
ANY = pl.BlockSpec(memory_space=pl.ANY)


def _place():
    return lax.axis_index("x"), lax.axis_index("y"), lax.axis_index("c")


GATHER_COPIES = 7


def _gather_plan(shard_refs, out_refs, send_sems, recv_sems, local_sems):
    x, y, c = _place()
    me, sibling = (x, y, c), (x, y, 1 - c)
    chips = [(1 - x, y), (x, 1 - y), (1 - x, 1 - y)]
    slot = lambda px, py, pc: 4 * px + 2 * py + pc
    plans = []
    for i, (src, out) in enumerate(zip(shard_refs, out_refs)):
        def copy(k, block, to, from_shard=False, i=i, src=src, out=out):
            return pltpu.make_async_remote_copy(
                src_ref=src if from_shard else out.at[slot(*block)], dst_ref=out.at[slot(*block)],
                send_sem=send_sems.at[GATHER_COPIES * i + k], recv_sem=recv_sems.at[GATHER_COPIES * i + k],
                device_id=to, device_id_type=MESH)

        plans.append(dict(
            mine=lambda i=i, src=src, out=out: pltpu.make_async_copy(src, out.at[slot(*me)], local_sems.at[i]),
            first=lambda copy=copy: [copy(0, me, sibling, True)] + [copy(1 + j, me, (*chip, c), True)
                                                                    for j, chip in enumerate(chips)],
            arrived=lambda j, copy=copy: copy(1 + j, (*chips[j], c), me),
            passed=lambda j, copy=copy: copy(4 + j, (*chips[j], c), sibling),
            from_sibling=lambda copy=copy: [copy(0, sibling, me)] + [copy(4 + j, (*chip, 1 - c), me)
                                                                     for j, chip in enumerate(chips)],
        ))
    return plans


def _gather_start(plans):
    for p in plans:
        p["mine"]().start()
        for cp in p["first"]():
            cp.start()


def _gather_finish(plans):
    for j in range(N_CHIP - 1):
        for p in plans:
            p["arrived"](j).wait_recv()
            p["passed"](j).start()
    for p in plans:
        for cp in p["from_sibling"]():
            cp.wait_recv()
        for cp in p["first"]() + [p["passed"](j) for j in range(N_CHIP - 1)]:
            cp.wait_send()
        p["mine"]().wait()


def _gather_scratch(n):
    return [pltpu.SemaphoreType.DMA((GATHER_COPIES * n,)), pltpu.SemaphoreType.DMA((GATHER_COPIES * n,)),
            pltpu.SemaphoreType.DMA((n,))]


def _gathered_shapes(shards):
    return tuple(jax.ShapeDtypeStruct((N_DEV, *a.shape), a.dtype) for a in shards)


def _gather_layer(shards, name):
    n = len(shards)

    def body(*refs):
        plans = _gather_plan(refs[:n], refs[n:2 * n], *refs[2 * n:])
        _gather_start(plans)
        _gather_finish(plans)

    return pl.pallas_call(
        body,
        name=name,
        out_shape=_gathered_shapes(shards),
        in_specs=[ANY] * n,
        out_specs=(ANY,) * n,
        scratch_shapes=_gather_scratch(n),
    )(*shards)


def _all_gather(shard, name):
    m, n = shard.shape

    def body(x_ref, out_ref, send_sems, recv_sems, local_sem):
        x, y, c = _place()
        me, sibling = (x, y, c), (x, y, 1 - c)
        chips = [(1 - x, y), (x, 1 - y), (1 - x, 1 - y)]

        def rows(px, py, pc):
            return out_ref.at[pl.ds((4 * px + 2 * py + pc) * m, m), :]

        def copy(k, block, to, src=None):
            return pltpu.make_async_remote_copy(
                src_ref=rows(*block) if src is None else src, dst_ref=rows(*block),
                send_sem=send_sems.at[k], recv_sem=recv_sems.at[k], device_id=to, device_id_type=MESH)

        mine = pltpu.make_async_copy(x_ref, rows(*me), local_sem)
        mine.start()
        first = [copy(0, me, sibling, src=x_ref)]
        first += [copy(1 + j, me, (*chip, c), src=x_ref) for j, chip in enumerate(chips)]
        for cp in first:
            cp.start()
        passed = [copy(4 + j, (*chip, c), sibling) for j, chip in enumerate(chips)]
        for j, chip in enumerate(chips):
            copy(1 + j, (*chip, c), me).wait_recv()
            passed[j].start()
        copy(0, sibling, me).wait_recv()
        for j, chip in enumerate(chips):
            copy(4 + j, (*chip, 1 - c), me).wait_recv()
        for cp in first + passed:
            cp.wait_send()
        mine.wait()

    return pl.pallas_call(
        body,
        name=name,
        out_shape=jax.ShapeDtypeStruct((N_DEV * m, n), shard.dtype),
        in_specs=[ANY],
        out_specs=ANY,
        scratch_shapes=[pltpu.SemaphoreType.DMA((7,)), pltpu.SemaphoreType.DMA((7,)), pltpu.SemaphoreType.DMA],
    )(shard)


def _pair_exchange(contribs, name):
    na = len(contribs)

    def body(*refs):
        plan = _pair_plan(refs[:na], refs[na:2 * na], *refs[2 * na:])
        _pair_start(plan)
        _pair_finish(plan)

    return pl.pallas_call(
        body,
        name=name,
        out_shape=_pair_shapes(contribs),
        in_specs=[ANY] * na,
        out_specs=(ANY,) * na,
        scratch_shapes=_pair_scratch(na),
    )(*contribs)


def _pair_plan(g_refs, got_refs, send_sems, recv_sems, unused_sems):
    x, y, c = _place()
    return [lambda i=i, j=j, g=g, got=got: pltpu.make_async_remote_copy(
        src_ref=g.at[2 * j + (1 - c)], dst_ref=got.at[j], send_sem=send_sems.at[N_CHIP * i + j],
        recv_sem=recv_sems.at[N_CHIP * i + j], device_id=(x, y, 1 - c), device_id_type=MESH)
        for i, (g, got) in enumerate(zip(g_refs, got_refs)) for j in range(N_CHIP)]


def _pair_start(plan):
    for cp in plan:
        cp().start()


def _pair_finish(plan):
    for cp in plan:
        cp().wait()


def _pair_scratch(n):
    return [pltpu.SemaphoreType.DMA((N_CHIP * n,)), pltpu.SemaphoreType.DMA((N_CHIP * n,)), pltpu.SemaphoreType.DMA((1,))]


def _pair_shapes(contribs):
    return tuple(jax.ShapeDtypeStruct((N_CHIP, *a.shape[1:]), a.dtype) for a in contribs)


def _pair_sum(contrib, theirs, name, tr):
    _, r, n = contrib.shape
    side = lax.axis_index("c").astype(jnp.int32).reshape(1)

    def body(c_ref, a_ref, b_ref, o_ref):
        o_ref[...] = (a_ref[...].astype(F32) + b_ref[...].astype(F32)).astype(BF16)

    spec = pl.BlockSpec((1, tr, n), lambda j, i, c_ref: (j, i, 0))
    return pl.pallas_call(
        body,
        name=name,
        out_shape=jax.ShapeDtypeStruct(theirs.shape, BF16),
        grid_spec=pltpu.PrefetchScalarGridSpec(
            num_scalar_prefetch=1,
            grid=(N_CHIP, r // tr),
            in_specs=[pl.BlockSpec((1, tr, n), lambda j, i, c_ref: (2 * j + c_ref[0], i, 0)), spec],
            out_specs=spec,
        ),
        compiler_params=_cparams(("parallel", "parallel")),
    )(side, contrib, theirs)


OTHER_CHIPS = N_CHIP - 1


def _exchange_plan(part_refs, got_refs, send_sems, recv_sems, local_sems):
    x, y, c = _place()
    mine = 2 * x + y
    chips = [(1 - x, y), (x, 1 - y), (1 - x, 1 - y)]
    plans = []
    for i, (p, got) in enumerate(zip(part_refs, got_refs)):
        def copy(k, outbound, i=i, p=p, got=got):
            cx, cy = chips[k]
            return pltpu.make_async_remote_copy(
                src_ref=p.at[2 * cx + cy] if outbound else p.at[mine],
                dst_ref=got.at[mine] if outbound else got.at[2 * cx + cy],
                send_sem=send_sems.at[OTHER_CHIPS * i + k], recv_sem=recv_sems.at[OTHER_CHIPS * i + k],
                device_id=(cx, cy, c), device_id_type=MESH)

        plans.append(dict(copy=copy, local=lambda i=i, p=p, got=got: pltpu.make_async_copy(
            p.at[mine], got.at[mine], local_sems.at[i])))
    return plans


def _exchange_start(plans):
    for p in plans:
        p["local"]().start()
        for k in range(OTHER_CHIPS):
            p["copy"](k, True).start()


def _exchange_finish(plans):
    for p in plans:
        for k in range(OTHER_CHIPS):
            p["copy"](k, False).wait_recv()
        for k in range(OTHER_CHIPS):
            p["copy"](k, True).wait_send()
        p["local"]().wait()


def _exchange_scratch(n):
    return [pltpu.SemaphoreType.DMA((OTHER_CHIPS * n,)), pltpu.SemaphoreType.DMA((OTHER_CHIPS * n,)),
            pltpu.SemaphoreType.DMA((n,))]


def _same_shapes(arrays):
    return tuple(jax.ShapeDtypeStruct(a.shape, a.dtype) for a in arrays)


def _chip_exchange(parts, name):
    na = len(parts)

    def body(*refs):
        plans = _exchange_plan(refs[:na], refs[na:2 * na], *refs[2 * na:])
        _exchange_start(plans)
        _exchange_finish(plans)

    return pl.pallas_call(
        body,
        name=name,
        out_shape=_same_shapes(parts),
        in_specs=[ANY] * na,
        out_specs=(ANY,) * na,
        scratch_shapes=_exchange_scratch(na),
    )(*parts)


def _adamw(parts, w, m, v, name, tr, at=0):
    k = parts.shape[0]
    r, n = w.shape
    first = at // tr

    def body(p_ref, w_ref, m_ref, v_ref, g_ref, d_ref, nm_ref, nv_ref):
        g = p_ref[0].astype(F32)
        for i in range(1, k):
            g = g + p_ref[i].astype(F32)
        m_new = ADAM_B1 * m_ref[...] + (1.0 - ADAM_B1) * g
        v_new = ADAM_B2 * v_ref[...] + (1.0 - ADAM_B2) * (g * g)
        m_hat = m_new / (1.0 - ADAM_B1 ** ADAM_STEP)
        v_hat = v_new / (1.0 - ADAM_B2 ** ADAM_STEP)
        g_ref[...] = g
        d_ref[...] = -ADAM_LR * (m_hat / (jnp.sqrt(v_hat) + ADAM_EPS) + ADAM_WD * w_ref[...])
        nm_ref[...] = m_new
        nv_ref[...] = v_new

    spec = pl.BlockSpec((tr, n), lambda i: (i, 0))
    out = jax.ShapeDtypeStruct((r, n), F32)
    return pl.pallas_call(
        body,
        name=name,
        out_shape=(out, out, out, out),
        grid=(r // tr,),
        in_specs=[pl.BlockSpec((k, tr, n), lambda i: (0, i + first, 0)), spec, spec, spec],
        out_specs=(spec, spec, spec, spec),
        compiler_params=_cparams(("parallel",)),
    )(parts, w, m, v)


def _adamw_layers(parts, w, m, v, name, at=0):
    depth = len(parts)
    k = parts[0].shape[0]
    w3, m3, v3 = (a.reshape(depth, -1, a.shape[-1]) for a in (w, m, v))
    _, r, n = w3.shape
    tr = _tile_rows(r, n * depth)
    first = at // tr

    def body(*refs):
        p_refs = refs[:depth]
        w_ref, m_ref, v_ref, g_ref, d_ref, nm_ref, nv_ref = refs[depth:]
        for l in range(depth):
            @pl.when(pl.program_id(0) == l)
            def _(p_ref=p_refs[l]):
                g = p_ref[0].astype(F32)
                for i in range(1, k):
                    g = g + p_ref[i].astype(F32)
                m_new = ADAM_B1 * m_ref[...] + (1.0 - ADAM_B1) * g
                v_new = ADAM_B2 * v_ref[...] + (1.0 - ADAM_B2) * (g * g)
                m_hat = m_new / (1.0 - ADAM_B1 ** ADAM_STEP)
                v_hat = v_new / (1.0 - ADAM_B2 ** ADAM_STEP)
                g_ref[...] = g
                d_ref[...] = -ADAM_LR * (m_hat / (jnp.sqrt(v_hat) + ADAM_EPS) + ADAM_WD * w_ref[...])
                nm_ref[...] = m_new
                nv_ref[...] = v_new

    part_spec = lambda l: pl.BlockSpec((k, tr, n), lambda j, i: (0, jnp.where(j == l, i + first, first), 0))
    spec = pl.BlockSpec((None, tr, n), lambda j, i: (j, i, 0))
    out = jax.ShapeDtypeStruct(w3.shape, F32)
    res = pl.pallas_call(
        body,
        name=name,
        out_shape=(out, out, out, out),
        grid=(depth, r // tr),
        in_specs=[part_spec(l) for l in range(depth)] + [spec, spec, spec],
        out_specs=(spec, spec, spec, spec),
        compiler_params=_cparams(("parallel", "parallel")),
    )(*parts, w3, m3, v3)
    return [a.reshape(w.shape) for a in res]


TILE_BYTES = 2 * 1024 * 1024
SUBLANES_16BIT = 16


def _tile_rows(r, n):
    best = None
    for t in range(SUBLANES_16BIT, r, SUBLANES_16BIT):
        if r % t == 0 and t * n * 4 <= TILE_BYTES:
            best = t
    return best or r


def _rows(a):
    return a.reshape(-1, a.shape[-1])


def _gather_rows(shard, name):
    return _all_gather(_rows(shard), name).reshape(N_DEV, *shard.shape)


SMALL = ("ffn1_norm", "mix_norm", "gm_v_norm", "gm_b_s", "mla_q_norm", "mla_kv_norm", "mla_q_gain", "mla_k_gain",
         "ssd_conv_b", "ssd_dt_bias", "ssd_a_log", "ssd_d", "ssd_norm", "ffn2_norm")
SMALL_ROWS = 8
SMALL_COLS = 7040


def _side_by_side(d):
    cols = jnp.concatenate([d[n].reshape(d[n].shape[0], -1) for n in SMALL], axis=1)
    return jnp.pad(cols, ((0, SMALL_ROWS - cols.shape[0]), (0, SMALL_COLS - cols.shape[1])))


def _apart(packed, like):
    out, off = {}, 0
    for n in SMALL:
        size = like[n][0].size
        out[n] = packed[:like[n].shape[0], off:off + size].reshape(like[n].shape)
        off += size
    return out


def kernel(x, positions, ffn1_norm, ffn1_w_in, ffn1_w_out, mix_norm, w_in, gm_v_norm, gm_w_s, gm_b_s, mla_q_norm, mla_kv_norm, mla_w_uq, mla_w_ukv, mla_q_gain, mla_k_gain, ssd_conv_w, ssd_conv_b, ssd_dt_bias, ssd_a_log, ssd_d, ssd_norm, w_branch, w_out, ffn2_norm, ffn2_w_in, ffn2_w_out, loss_target, m_ffn1_norm, m_ffn1_w_in, m_ffn1_w_out, m_mix_norm, m_w_in, m_gm_v_norm, m_gm_w_s, m_gm_b_s, m_mla_q_norm, m_mla_kv_norm, m_mla_w_uq, m_mla_w_ukv, m_mla_q_gain, m_mla_k_gain, m_ssd_conv_w, m_ssd_conv_b, m_ssd_dt_bias, m_ssd_a_log, m_ssd_d, m_ssd_norm, m_w_branch, m_w_out, m_ffn2_norm, m_ffn2_w_in, m_ffn2_w_out, v_ffn1_norm, v_ffn1_w_in, v_ffn1_w_out, v_mix_norm, v_w_in, v_gm_v_norm, v_gm_w_s, v_gm_b_s, v_mla_q_norm, v_mla_kv_norm, v_mla_w_uq, v_mla_w_ukv, v_mla_q_gain, v_mla_k_gain, v_ssd_conv_w, v_ssd_conv_b, v_ssd_dt_bias, v_ssd_a_log, v_ssd_d, v_ssd_norm, v_w_branch, v_w_out, v_ffn2_norm, v_ffn2_w_in, v_ffn2_w_out):
    given = dict(locals())
    w = {n: given[n] for n in WEIGHTS}
    mom = {n: given["m_" + n] for n in WEIGHTS}
    var = {n: given["v_" + n] for n in WEIGHTS}
    groups = {"ffn_w_in": ("ffn1_w_in", "ffn2_w_in"), "ffn_w_out": ("ffn1_w_out", "ffn2_w_out"), "w_in": ("w_in",),
              "mla_w_uq": ("mla_w_uq",), "mla_w_ukv": ("mla_w_ukv",), "ssd_conv_w": ("ssd_conv_w",),
              "w_branch": ("w_branch",), "w_out": ("w_out",)}

    shards = [_layer_shards(w, l) for l in range(w_out.shape[0])]
    loss, dx, grads = _local_step(x[0], positions[0], loss_target[0], {n: w[n] for n in VECTORS}, shards=shards)

    outs = [{}, {}, {}, {}]
    for i, names in enumerate(groups.values()):
        parts = [layer[i] for layer in grads["reduced"]]
        at = 0
        for n in names:
            res = _adamw_layers(parts, w[n], mom[n], var[n], "adamw_" + n, at=at)
            at += w[n][0].size // w[n].shape[-1]
            for o, r in zip(outs, res):
                o[n] = r

    small_parts = _gather_rows(_side_by_side(grads), "small_grads_all_gather")
    small = _adamw(small_parts, _side_by_side(w), _side_by_side(mom), _side_by_side(var), "adamw_small", SMALL_ROWS)
    ws_parts = _gather_rows(_rows(grads["gm_w_s"]), "gm_w_s_grads_all_gather")
    ws = _adamw(ws_parts, _rows(w["gm_w_s"]), _rows(mom["gm_w_s"]), _rows(var["gm_w_s"]), "adamw_gm_w_s",
                _tile_rows(ws_parts.shape[1], LANES))
    for o, sm, r in zip(outs, small, ws):
        o.update(_apart(sm, w))
        o["gm_w_s"] = r.reshape(w["gm_w_s"].shape)

    loss = lax.psum(loss, ("x", "y", "c"))
    return (loss, dx[None], *[o[n] for o in outs for n in WEIGHTS])
```

```python
import functools

import jax
import jax.numpy as jnp
import numpy as np
from jax import lax
from jax.experimental import pallas as pl
from jax.experimental.pallas import tpu as pltpu

F32 = jnp.float32
BF16 = jnp.bfloat16

D_MODEL = 1024
DEPTH = 4
D_FF = 2816
FFN_RESID = 0.5
EPS = 1e-6
GM_WIDTH = 512
GM_GROUPS = 4
GM_CHUNK = 128
MLA_HEADS = 8
MLA_Q_RANK = 384
MLA_KV_RANK = 256
MLA_NOPE = 64
MLA_ROPE = 32
MLA_QK_DIM = 96
MLA_V = 64
ROPE_THETA = 10000.0
SSD_HEADS = 8
SSD_HEAD_DIM = 64
SSD_INNER = 512
SSD_GROUPS = 2
SSD_STATE = 128
SSD_CONV = 4
SSD_CHUNK = 128
SSD_CONV_DIM = 1024
N_BRANCH = 3
IN_WIDTHS = (1024, 384, 256, 32, 512, 1024, 8, 3072)
IN_OFFSETS = (0, 1024, 1408, 1664, 1696, 2208, 3232, 3240)
IN_COLS = 6312
LANES = 128
N_DEV = 8

ADAM_LR = 0.001
ADAM_B1 = 0.9
ADAM_B2 = 0.999
ADAM_EPS = 1e-08
ADAM_WD = 0.01
ADAM_STEP = 10

VMEM_LIMIT = 56 * 1024 * 1024

P_UV, P_XBC, P_Z, P_KR, P_CQ, P_GATES, P_CKV, P_DT = 0, 1024, 2048, 2560, 2688, 3072, 6144, 6400
P_COLS = 6528


def _cparams(sem):
    return pltpu.CompilerParams(dimension_semantics=sem, vmem_limit_bytes=VMEM_LIMIT)


def _bdot(a, b, dims):
    return lax.dot_general(a.astype(BF16), b.astype(BF16), (dims, ((), ())), preferred_element_type=F32)


@jax.custom_vjp
def _nn(a, b):
    return _bdot(a, b, ((1,), (0,)))


@jax.custom_vjp
def _nt(a, b):
    return _bdot(a, b, ((1,), (1,)))


@jax.custom_vjp
def _tn(a, b):
    return _bdot(a, b, ((0,), (0,)))


def _dot_fwd(dims):
    return lambda a, b: (_bdot(a, b, dims), (a, b))


_nn.defvjp(_dot_fwd(((1,), (0,))), lambda r, g: (_nt(g, r[1]).astype(r[0].dtype), _tn(r[0], g).astype(r[1].dtype)))
_nt.defvjp(_dot_fwd(((1,), (1,))), lambda r, g: (_nn(g, r[1]).astype(r[0].dtype), _tn(g, r[0]).astype(r[1].dtype)))
_tn.defvjp(_dot_fwd(((0,), (0,))), lambda r, g: (_nt(r[1], g).astype(r[0].dtype), _nn(r[0], g).astype(r[1].dtype)))


def _exact_nn(a, b):
    return lax.dot_general(a, b, (((1,), (0,)), ((), ())), precision=lax.Precision.HIGHEST, preferred_element_type=F32)


def _sigmoid(x):
    return 1.0 / (1.0 + jnp.exp(-x))


def _silu(x):
    return x * _sigmoid(x)


def _softplus(x):
    return jnp.maximum(x, 0.0) + jnp.log(1.0 + jnp.exp(-jnp.abs(x)))


def _gelu(x):
    return 0.5 * x * (1.0 + lax.erf(x * 0.7071067811865476))


def _pick(n, cands):
    for c in cands:
        if n % c == 0:
            return c
    return n


def _matmul(a, b, mode, out_dtype, name, alpha=1.0, tm=None, tn=None, tk=None):
    if mode == "nn":
        (m, k), (_, n) = a.shape, b.shape
    elif mode == "nt":
        (m, k), (n, _) = a.shape, b.shape
    else:
        (k, m), (_, n) = a.shape, b.shape
    tm = tm or _pick(m, (512, 384, 256, 128))
    tn = tn or _pick(n, (1024, 768, 512, 384, 256, 128))
    tk = tk or _pick(k, (1024, 512, 256, 128))
    nk = k // tk
    if mode == "nn":
        a_spec = pl.BlockSpec((tm, tk), lambda i, j, kk: (i, kk))
        b_spec = pl.BlockSpec((tk, tn), lambda i, j, kk: (kk, j))
        dot = _nn
    elif mode == "nt":
        a_spec = pl.BlockSpec((tm, tk), lambda i, j, kk: (i, kk))
        b_spec = pl.BlockSpec((tn, tk), lambda i, j, kk: (j, kk))
        dot = _nt
    else:
        a_spec = pl.BlockSpec((tk, tm), lambda i, j, kk: (kk, i))
        b_spec = pl.BlockSpec((tk, tn), lambda i, j, kk: (kk, j))
        dot = _tn

    def body(a_ref, b_ref, o_ref, acc_ref):
        kk = pl.program_id(2)

        @pl.when(kk == 0)
        def _():
            acc_ref[...] = jnp.zeros_like(acc_ref)

        acc_ref[...] += dot(a_ref[...], b_ref[...])

        @pl.when(kk == nk - 1)
        def _():
            o_ref[...] = (alpha * acc_ref[...]).astype(out_dtype)

    return pl.pallas_call(
        body,
        name=name,
        out_shape=jax.ShapeDtypeStruct((m, n), out_dtype),
        grid=(m // tm, n // tn, nk),
        in_specs=[a_spec, b_spec],
        out_specs=pl.BlockSpec((tm, tn), lambda i, j, kk: (i, j)),
        scratch_shapes=[pltpu.VMEM((tm, tn), F32)],
        compiler_params=_cparams(("parallel", "parallel", "arbitrary")),
    )(a, b)


def _rms_stats(x):
    r = lax.rsqrt(jnp.mean(x * x, axis=-1, keepdims=True) + EPS)
    return x * r, r


def _rms_bwd(xhat, r, gain, dy):
    dxhat = dy * gain
    return r * (dxhat - xhat * jnp.mean(dxhat * xhat, axis=-1, keepdims=True))


def _acc_rows(ref, val, first):
    s = jnp.sum(val, axis=0, keepdims=True)

    @pl.when(first)
    def _():
        ref[...] = s

    @pl.when(jnp.logical_not(first))
    def _():
        ref[...] += s


def _rmsnorm_fwd(x, gain, name, tm=512):
    s, d = x.shape

    def body(x_ref, g_ref, h_ref):
        xhat, _ = _rms_stats(x_ref[...])
        h_ref[...] = (xhat * g_ref[...]).astype(BF16)

    return pl.pallas_call(
        body,
        name=name,
        out_shape=jax.ShapeDtypeStruct((s, d), BF16),
        grid=(s // tm,),
        in_specs=[pl.BlockSpec((tm, d), lambda i: (i, 0)), pl.BlockSpec((1, d), lambda i: (0, 0))],
        out_specs=pl.BlockSpec((tm, d), lambda i: (i, 0)),
        compiler_params=_cparams(("parallel",)),
    )(x, gain)


def _rmsnorm_bwd(x, gain, dh, dres, name, tm=512):
    s, d = x.shape

    def body(x_ref, g_ref, dh_ref, dres_ref, dx_ref, dg_ref):
        xhat, r = _rms_stats(x_ref[...])
        dh = dh_ref[...].astype(F32)
        dx_ref[...] = dres_ref[...] + _rms_bwd(xhat, r, g_ref[...], dh)
        _acc_rows(dg_ref, dh * xhat, pl.program_id(0) == 0)

    return pl.pallas_call(
        body,
        name=name,
        out_shape=(jax.ShapeDtypeStruct((s, d), F32), jax.ShapeDtypeStruct((1, d), F32)),
        grid=(s // tm,),
        in_specs=[
            pl.BlockSpec((tm, d), lambda i: (i, 0)),
            pl.BlockSpec((1, d), lambda i: (0, 0)),
            pl.BlockSpec((tm, d), lambda i: (i, 0)),
            pl.BlockSpec((tm, d), lambda i: (i, 0)),
        ],
        out_specs=(pl.BlockSpec((tm, d), lambda i: (i, 0)), pl.BlockSpec((1, d), lambda i: (0, 0))),
        compiler_params=_cparams(("arbitrary",)),
    )(x, gain, dh, dres)


FF_BLOCK = 2 * D_FF // N_DEV
FF_BLOCKS = D_FF // FF_BLOCK
FF_ROWS = D_FF // N_DEV


def _ffn_weight_specs(layer):
    return [
        pl.BlockSpec((None, None, D_MODEL, FF_BLOCK), lambda i, j: (j, layer, 0, 0)),
        pl.BlockSpec((None, None, D_MODEL, FF_BLOCK), lambda i, j: (j + FF_BLOCKS, layer, 0, 0)),
        pl.BlockSpec((2, None, FF_ROWS, D_MODEL), lambda i, j: (j, layer, 0, 0)),
    ]


def _ffn_fwd(x, gain, w_in, w_out, layer, name, tm=512):
    s, d = x.shape

    def body(x_ref, gain_ref, wg_ref, wu_ref, wo_ref, y_ref, gu_ref, h_scr, acc_scr):
        j = pl.program_id(1)

        @pl.when(j == 0)
        def _():
            xhat, _ = _rms_stats(x_ref[...])
            h_scr[...] = (xhat * gain_ref[...]).astype(BF16)
            acc_scr[...] = jnp.zeros_like(acc_scr)

        h = h_scr[...]
        g = _nn(h, wg_ref[...])
        u = _nn(h, wu_ref[...])
        gu_ref[0] = g.astype(BF16)
        gu_ref[1] = u.astype(BF16)
        acc_scr[...] += _nn(_silu(g) * u, wo_ref[...].reshape(FF_BLOCK, d))

        @pl.when(j == FF_BLOCKS - 1)
        def _():
            y_ref[...] = x_ref[...] + FFN_RESID * acc_scr[...]

    return pl.pallas_call(
        body,
        name=name,
        out_shape=(
            jax.ShapeDtypeStruct((s, d), F32),
            jax.ShapeDtypeStruct((2, FF_BLOCKS, s, FF_BLOCK), BF16),
        ),
        grid=(s // tm, FF_BLOCKS),
        in_specs=[
            pl.BlockSpec((tm, d), lambda i, j: (i, 0)),
            pl.BlockSpec((1, d), lambda i, j: (0, 0)),
        ] + _ffn_weight_specs(layer),
        out_specs=(
            pl.BlockSpec((tm, d), lambda i, j: (i, 0)),
            pl.BlockSpec((2, None, tm, FF_BLOCK), lambda i, j: (0, j, i, 0)),
        ),
        scratch_shapes=[pltpu.VMEM((tm, d), BF16), pltpu.VMEM((tm, d), F32)],
        compiler_params=_cparams(("parallel", "arbitrary")),
    )(x, gain, w_in, w_in, w_out)


def _ffn_bwd(x, gain, w_in, w_out, layer, gu, dy, name, tm=512, tk=1024, pair=()):
    s, d = x.shape
    tk = min(tk, s)
    npair = len(pair)

    def body(x_ref, gain_ref, wg_ref, wu_ref, wo_ref, gu_ref, dy_ref,
             dx_ref, dgain_ref, h_ref, a_ref, dgu_ref, dyb_scr, acc_scr):
        i = pl.program_id(0)
        j = pl.program_id(1)

        @pl.when(j == 0)
        def _():
            xhat, _ = _rms_stats(x_ref[...])
            h_ref[...] = (xhat * gain_ref[...]).astype(BF16)
            dyb_scr[...] = (FFN_RESID * dy_ref[...]).astype(BF16)
            acc_scr[...] = jnp.zeros_like(acc_scr)

        da = _nt(dyb_scr[...], wo_ref[...].reshape(FF_BLOCK, d))
        gv = gu_ref[0].astype(F32)
        uv = gu_ref[1].astype(F32)
        sg = _sigmoid(gv)
        sl = gv * sg
        a_ref[...] = (sl * uv).astype(BF16)
        du = (da * sl).astype(BF16)
        dg = (da * uv * (sg * (1.0 + gv * (1.0 - sg)))).astype(BF16)
        dgu_ref[0] = dg
        dgu_ref[1] = du
        acc_scr[...] += _nt(dg, wg_ref[...]) + _nt(du, wu_ref[...])

        @pl.when(j == FF_BLOCKS - 1)
        def _():
            xhat, r = _rms_stats(x_ref[...])
            dh = acc_scr[...]
            dx_ref[...] = dy_ref[...] + _rms_bwd(xhat, r, gain_ref[...], dh)
            _acc_rows(dgain_ref, dh * xhat, i == 0)

    gu_spec = pl.BlockSpec((2, None, tm, FF_BLOCK), lambda i, j: (0, j, i, 0))
    dx, dgain, h, a, dgu, *theirs = pl.pallas_call(
        _host_exchange(body, 7, 5, pair, (s // tm - 1, FF_BLOCKS - 1), pair=True),
        name=name,
        out_shape=(
            jax.ShapeDtypeStruct((s, d), F32),
            jax.ShapeDtypeStruct((1, d), F32),
            jax.ShapeDtypeStruct((s, d), BF16),
            jax.ShapeDtypeStruct((FF_BLOCKS, s, FF_BLOCK), BF16),
            jax.ShapeDtypeStruct((2, FF_BLOCKS, s, FF_BLOCK), BF16),
        ) + _pair_shapes(pair),
        grid=(s // tm, FF_BLOCKS),
        in_specs=[
            pl.BlockSpec((tm, d), lambda i, j: (i, 0)),
            pl.BlockSpec((1, d), lambda i, j: (0, 0)),
        ] + _ffn_weight_specs(layer) + [gu_spec, pl.BlockSpec((tm, d), lambda i, j: (i, 0))] + [ANY] * npair,
        out_specs=(
            pl.BlockSpec((tm, d), lambda i, j: (i, 0)),
            pl.BlockSpec((1, d), lambda i, j: (0, 0)),
            pl.BlockSpec((tm, d), lambda i, j: (i, 0)),
            pl.BlockSpec((None, tm, FF_BLOCK), lambda i, j: (j, i, 0)),
            gu_spec,
        ) + (ANY,) * npair,
        scratch_shapes=[pltpu.VMEM((tm, d), BF16), pltpu.VMEM((tm, d), F32)] + (_pair_scratch(npair) if npair else []),
        compiler_params=_cparams(("arbitrary", "arbitrary")),
    )(x, gain, w_in, w_in, w_out, gu, dy, *pair)
    nk = s // tk

    def acc_matmul(first, last, acc_ref, o_ref, val, alpha):
        @pl.when(first)
        def _():
            acc_ref[...] = jnp.zeros_like(acc_ref)

        acc_ref[...] += val

        @pl.when(last)
        def _():
            o_ref[...] = (alpha * acc_ref[...]).astype(BF16)

    def dwin_body(h_ref, dgu_ref, o_ref, acc_ref):
        kk = pl.program_id(2)
        acc_matmul(kk == 0, kk == nk - 1, acc_ref, o_ref, _tn(h_ref[...], dgu_ref[...]), 1.0)

    tmw = d
    dw_in = pl.pallas_call(
        dwin_body,
        name=name + "_dwin",
        out_shape=jax.ShapeDtypeStruct((N_DEV, d, FF_BLOCK), BF16),
        grid=(N_DEV, d // tmw, nk),
        in_specs=[
            pl.BlockSpec((tk, tmw), lambda n, i, kk: (kk, i)),
            pl.BlockSpec((None, tk, FF_BLOCK), lambda n, i, kk: (n, kk, 0)),
        ],
        out_specs=pl.BlockSpec((None, tmw, FF_BLOCK), lambda n, i, kk: (n, i, 0)),
        scratch_shapes=[pltpu.VMEM((tmw, FF_BLOCK), F32)],
        compiler_params=_cparams(("parallel", "parallel", "arbitrary")),
    )(h, dgu.reshape(N_DEV, s, FF_BLOCK))

    def dwout_body(a_ref, dy_ref, o_ref, acc_ref):
        kk = pl.program_id(1)
        acc_matmul(kk == 0, kk == nk - 1, acc_ref, o_ref, _tn(a_ref[...], dy_ref[...]), FFN_RESID)

    dw_out = pl.pallas_call(
        dwout_body,
        name=name + "_dwout",
        out_shape=jax.ShapeDtypeStruct((FF_BLOCKS, FF_BLOCK, d), BF16),
        grid=(FF_BLOCKS, nk),
        in_specs=[
            pl.BlockSpec((None, tk, FF_BLOCK), lambda j, kk: (j, kk, 0)),
            pl.BlockSpec((tk, d), lambda j, kk: (kk, 0)),
        ],
        out_specs=pl.BlockSpec((None, FF_BLOCK, d), lambda j, kk: (j, 0, 0)),
        scratch_shapes=[pltpu.VMEM((FF_BLOCK, d), F32)],
        compiler_params=_cparams(("parallel", "arbitrary")),
    )(a, dy)
    return dx, dgain, dw_in, dw_out.reshape(N_DEV, FF_ROWS, d), theirs


def _acc(ref, val, first):
    @pl.when(first)
    def _():
        ref[...] = val

    @pl.when(jnp.logical_not(first))
    def _():
        ref[...] += val


def _full(shape):
    nd = len(shape)
    return pl.BlockSpec(shape, lambda *_: (0,) * nd)


def _iota(shape, dim):
    return lax.broadcasted_iota(jnp.int32, shape, dim)


def _gmlp_chunk(u, v, gain, w_s, b_s):
    va = [_gelu(t) for t in v]
    ms = sum(jnp.sum(t * t, axis=-1, keepdims=True) for t in va) * (1.0 / GM_WIDTH)
    r = lax.rsqrt(ms + EPS)
    tri = _iota((GM_CHUNK, GM_CHUNK), 0) >= _iota((GM_CHUNK, GM_CHUNK), 1)
    out = []
    for g in range(GM_GROUPS):
        vn = va[g] * r * gain[g]
        sp = _nn(jnp.where(tri, w_s[g], 0.0), vn) + b_s[g]
        out.append(_gelu(u[g]) * sp)
    return out


def _gmlp_load(uv_ref, c):
    rows = pl.ds(c * GM_CHUNK, GM_CHUNK)
    u = [uv_ref[rows, pl.ds(g * LANES, LANES)] for g in range(GM_GROUPS)]
    v = [uv_ref[rows, pl.ds(GM_WIDTH + g * LANES, LANES)] for g in range(GM_GROUPS)]
    return u, v


def _gmlp_params(gain_ref, ws_ref, bs_ref):
    gain = [gain_ref[:, pl.ds(g * LANES, LANES)] for g in range(GM_GROUPS)]
    w_s = [ws_ref[g] for g in range(GM_GROUPS)]
    b_s = [bs_ref[g] for g in range(GM_GROUPS)]
    return gain, w_s, b_s


def _gmlp_fwd(proj, gain, w_s, b_s, name, tm=512):
    s = proj.shape[0]

    def body(uv_ref, gain_ref, ws_ref, bs_ref, y_ref):
        params = _gmlp_params(gain_ref, ws_ref, bs_ref)
        for c in range(tm // GM_CHUNK):
            u, v = _gmlp_load(uv_ref, c)
            y = _gmlp_chunk(u, v, *params)
            for g in range(GM_GROUPS):
                y_ref[pl.ds(c * GM_CHUNK, GM_CHUNK), pl.ds(g * LANES, LANES)] = y[g]

    return pl.pallas_call(
        body,
        name=name,
        out_shape=jax.ShapeDtypeStruct((s, GM_WIDTH), F32),
        grid=(s // tm,),
        in_specs=[
            pl.BlockSpec((tm, 2 * GM_WIDTH), lambda i: (i, P_UV // (2 * GM_WIDTH))),
            _full((1, GM_WIDTH)),
            _full((GM_GROUPS, GM_CHUNK, GM_CHUNK)),
            _full((GM_GROUPS, GM_CHUNK, 1)),
        ],
        out_specs=pl.BlockSpec((tm, GM_WIDTH), lambda i: (i, 0)),
        compiler_params=_cparams(("parallel",)),
    )(proj, gain, w_s, b_s)


def _gmlp_bwd(proj, gain, w_s, b_s, dy, name, tm=512):
    s = proj.shape[0]

    def body(uv_ref, gain_ref, ws_ref, bs_ref, dy_ref, duv_ref, dgain_ref, dws_ref, dbs_ref):
        params = _gmlp_params(gain_ref, ws_ref, bs_ref)
        dgain = dws = dbs = None
        for c in range(tm // GM_CHUNK):
            rows = pl.ds(c * GM_CHUNK, GM_CHUNK)
            u, v = _gmlp_load(uv_ref, c)
            _, vjp = jax.vjp(_gmlp_chunk, u, v, *params)
            du, dv, dg, dw, db = vjp([dy_ref[rows, pl.ds(g * LANES, LANES)] for g in range(GM_GROUPS)])
            for g in range(GM_GROUPS):
                duv_ref[rows, pl.ds(g * LANES, LANES)] = du[g].astype(BF16)
                duv_ref[rows, pl.ds(GM_WIDTH + g * LANES, LANES)] = dv[g].astype(BF16)
            if c == 0:
                dgain, dws, dbs = dg, dw, db
            else:
                dgain = [p + q for p, q in zip(dgain, dg)]
                dws = [p + q for p, q in zip(dws, dw)]
                dbs = [p + q for p, q in zip(dbs, db)]
        first = pl.program_id(0) == 0
        for g in range(GM_GROUPS):
            _acc(dgain_ref.at[:, pl.ds(g * LANES, LANES)], dgain[g], first)
            _acc(dws_ref.at[g], dws[g], first)
            _acc(dbs_ref.at[g], dbs[g], first)

    return pl.pallas_call(
        body,
        name=name,
        out_shape=(
            jax.ShapeDtypeStruct((s, 2 * GM_WIDTH), BF16),
            jax.ShapeDtypeStruct((1, GM_WIDTH), F32),
            jax.ShapeDtypeStruct((GM_GROUPS, GM_CHUNK, GM_CHUNK), F32),
            jax.ShapeDtypeStruct((GM_GROUPS, GM_CHUNK, 1), F32),
        ),
        grid=(s // tm,),
        in_specs=[
            pl.BlockSpec((tm, 2 * GM_WIDTH), lambda i: (i, P_UV // (2 * GM_WIDTH))),
            _full((1, GM_WIDTH)),
            _full((GM_GROUPS, GM_CHUNK, GM_CHUNK)),
            _full((GM_GROUPS, GM_CHUNK, 1)),
            pl.BlockSpec((tm, GM_WIDTH), lambda i: (i, 0)),
        ],
        out_specs=(
            pl.BlockSpec((tm, 2 * GM_WIDTH), lambda i: (i, 0)),
            _full((1, GM_WIDTH)),
            _full((GM_GROUPS, GM_CHUNK, GM_CHUNK)),
            _full((GM_GROUPS, GM_CHUNK, 1)),
        ),
        compiler_params=_cparams(("arbitrary",)),
    )(proj, gain, w_s, b_s, dy)


HEAD_BLOCK = MLA_HEADS * LANES


def _mla_heads(rope, q_all, k_all, kr, qg, kg):
    cosf, sinf, rot = rope

    def head_norm(t, gain):
        r = lax.rsqrt(jnp.sum(t * t, axis=-1, keepdims=True) * (1.0 / MLA_QK_DIM) + EPS)
        th = t * r * gain
        return th * cosf + _nn(th, rot) * sinf

    q = [head_norm(t, qg) * ATT_SCALE for t in q_all]
    k = [head_norm(t + kr, kg) for t in k_all]
    return q, k


def _mla_up(refs, w_scr, up_scr):
    cq_ref, ckv_ref, qn_ref, kvn_ref, wq_ref, wk_ref, wv_ref = refs
    wq_scr, wk_scr, wv_scr = w_scr
    qa_scr, ka_scr = up_scr

    @pl.when(pl.program_id(0) == 0)
    def _():
        for h in range(MLA_HEADS):
            cols = pl.ds(h * LANES, LANES)
            wq_scr[:, cols] = wq_ref[h]
            wk_scr[:, cols] = wk_ref[h]
            wv_scr[:, cols] = wv_ref[h]

    xq, rq = _rms_stats(cq_ref[...])
    xk, rk = _rms_stats(ckv_ref[...])
    qn = (xq * qn_ref[...]).astype(BF16)
    kvn = (xk * kvn_ref[...]).astype(BF16)
    qa_scr[...] = _nn(qn, wq_scr[...])
    ka_scr[...] = _nn(kvn, wk_scr[...])
    heads = lambda scr: [scr[:, pl.ds(h * LANES, LANES)] for h in range(MLA_HEADS)]
    return (xq, rq, qn), (xk, rk, kvn), heads(qa_scr), heads(ka_scr)


def _mla_scratch(tm):
    w = lambda rank: pltpu.VMEM((rank, HEAD_BLOCK), BF16)
    up = pltpu.VMEM((tm, HEAD_BLOCK), F32)
    return [w(MLA_Q_RANK), w(MLA_KV_RANK), w(MLA_KV_RANK), up, up]


def _mla_pre_specs(tm):
    row = lambda w, off: pl.BlockSpec((tm, w), lambda i: (i, off // w))
    return [
        row(MLA_Q_RANK, P_CQ),
        row(MLA_KV_RANK, P_CKV),
        row(LANES, P_KR),
        pl.BlockSpec((tm, LANES), lambda i: (i, 0)),
        pl.BlockSpec((tm, LANES), lambda i: (i, 0)),
        _full((LANES, LANES)),
        _full((1, MLA_Q_RANK)),
        _full((1, MLA_KV_RANK)),
        _full((MLA_HEADS, MLA_Q_RANK, LANES)),
        _full((MLA_HEADS, MLA_KV_RANK, LANES)),
        _full((MLA_HEADS, MLA_KV_RANK, LANES)),
        _full((1, LANES)),
        _full((1, LANES)),
    ]


def _mla_pre_fwd(proj, cosf, sinf, rot, qn_g, kvn_g, wq, wk, wv, qg, kg, name, tm=256):
    s = proj.shape[0]

    def body(cq_ref, ckv_ref, kr_ref, cos_ref, sin_ref, rot_ref, qn_ref, kvn_ref, wq_ref, wk_ref, wv_ref, qg_ref, kg_ref,
             q_ref, k_ref, v_ref, *scr):
        _, (_, _, kvn), q_all, k_all = _mla_up((cq_ref, ckv_ref, qn_ref, kvn_ref, wq_ref, wk_ref, wv_ref), scr[:3], scr[3:])
        v_ref[...] = _nn(kvn, scr[2][...]).astype(BF16)
        rope = (cos_ref[...], sin_ref[...], rot_ref[...])
        q, k = _mla_heads(rope, q_all, k_all, kr_ref[...], qg_ref[...], kg_ref[...])
        for h in range(MLA_HEADS):
            cols = pl.ds(h * LANES, LANES)
            q_ref[:, cols] = q[h].astype(BF16)
            k_ref[:, cols] = k[h].astype(BF16)

    out = jax.ShapeDtypeStruct((s, HEAD_BLOCK), BF16)
    blk = pl.BlockSpec((tm, HEAD_BLOCK), lambda i: (i, 0))
    return pl.pallas_call(
        body,
        name=name,
        out_shape=(out, out, out),
        grid=(s // tm,),
        in_specs=_mla_pre_specs(tm),
        out_specs=(blk, blk, blk),
        scratch_shapes=_mla_scratch(tm),
        compiler_params=_cparams(("arbitrary",)),
    )(proj, proj, proj, cosf, sinf, rot, qn_g, kvn_g, wq, wk, wv, qg, kg)


def _mla_pre_bwd(proj, cosf, sinf, rot, qn_g, kvn_g, wq, wk, wv, qg, kg, dq, dk, dv, name, tm=256):
    s = proj.shape[0]

    def body(cq_ref, ckv_ref, kr_ref, cos_ref, sin_ref, rot_ref, qn_ref, kvn_ref, wq_ref, wk_ref, wv_ref, qg_ref, kg_ref,
             dq_ref, dk_ref, dv_ref,
             dcq_ref, dckv_ref, dkr_ref, dqn_ref, dkvn_ref, dwq_ref, dwk_ref, dwv_ref, dqg_ref, dkg_ref, *scr):
        (xq, rq, qn), (xk, rk, kvn), q_all, k_all = _mla_up(
            (cq_ref, ckv_ref, qn_ref, kvn_ref, wq_ref, wk_ref, wv_ref), scr[:3], scr[3:])
        wq_scr, wk_scr, wv_scr, qa_scr, ka_scr = scr
        rope = (cos_ref[...], sin_ref[...], rot_ref[...])
        _, vjp = jax.vjp(functools.partial(_mla_heads, rope), q_all, k_all, kr_ref[...], qg_ref[...], kg_ref[...])
        heads = lambda ref: [ref[:, pl.ds(h * LANES, LANES)] for h in range(MLA_HEADS)]
        dq_all, dk_all, dkr, dqg, dkg = vjp((heads(dq_ref), heads(dk_ref)))
        for h in range(MLA_HEADS):
            cols = pl.ds(h * LANES, LANES)
            qa_scr[:, cols] = dq_all[h]
            ka_scr[:, cols] = dk_all[h]
        dqa, dka, dva = qa_scr[...], ka_scr[...], dv_ref[...]
        dqn = _nt(dqa, wq_scr[...])
        dkvn = _nt(dka, wk_scr[...]) + _nt(dva, wv_scr[...])
        dcq_ref[...] = _rms_bwd(xq, rq, qn_ref[...], dqn).astype(BF16)
        dckv_ref[...] = _rms_bwd(xk, rk, kvn_ref[...], dkvn).astype(BF16)
        dkr_ref[...] = dkr.astype(BF16)
        first = pl.program_id(0) == 0
        _acc_rows(dqn_ref, dqn * xq, first)
        _acc_rows(dkvn_ref, dkvn * xk, first)
        _acc(dqg_ref, dqg, first)
        _acc(dkg_ref, dkg, first)
        dwq, dwk, dwv = _tn(qn, dqa), _tn(kvn, dka), _tn(kvn, dva)
        for h in range(MLA_HEADS):
            cols = slice(h * LANES, (h + 1) * LANES)
            _acc(dwq_ref.at[h], dwq[:, cols], first)
            _acc(dwk_ref.at[h], dwk[:, cols], first)
            _acc(dwv_ref.at[h], dwv[:, cols], first)

    hb = pl.BlockSpec((tm, HEAD_BLOCK), lambda i: (i, 0))
    row = lambda w: pl.BlockSpec((tm, w), lambda i: (i, 0))
    sds = jax.ShapeDtypeStruct
    return pl.pallas_call(
        body,
        name=name,
        out_shape=(
            sds((s, MLA_Q_RANK), BF16), sds((s, MLA_KV_RANK), BF16), sds((s, LANES), BF16),
            sds((1, MLA_Q_RANK), F32), sds((1, MLA_KV_RANK), F32),
            sds((MLA_HEADS, MLA_Q_RANK, LANES), F32), sds((MLA_HEADS, MLA_KV_RANK, LANES), F32),
            sds((MLA_HEADS, MLA_KV_RANK, LANES), F32),
            sds((1, LANES), F32), sds((1, LANES), F32),
        ),
        grid=(s // tm,),
        in_specs=_mla_pre_specs(tm) + [hb, hb, hb],
        out_specs=(
            row(MLA_Q_RANK), row(MLA_KV_RANK), row(LANES),
            _full((1, MLA_Q_RANK)), _full((1, MLA_KV_RANK)),
            _full((MLA_HEADS, MLA_Q_RANK, LANES)), _full((MLA_HEADS, MLA_KV_RANK, LANES)),
            _full((MLA_HEADS, MLA_KV_RANK, LANES)),
            _full((1, LANES)), _full((1, LANES)),
        ),
        scratch_shapes=_mla_scratch(tm),
        compiler_params=_cparams(("arbitrary",)),
    )(proj, proj, proj, cosf, sinf, rot, qn_g, kvn_g, wq, wk, wv, qg, kg, dq, dk, dv)


ATT_SCALE = MLA_QK_DIM ** -0.5
NEG_BIG = -1e30


def _att_scores(q, k, diagonal, q_at=0):
    s = _nt(q, k)
    if diagonal:
        s = jnp.where(_iota(s.shape, 0) + q_at >= _iota(s.shape, 1), s, NEG_BIG)
    return s


def _att_scores_t(k, q, diagonal, q_at=0):
    s = _nt(k, q)
    if diagonal:
        s = jnp.where(_iota(s.shape, 0) <= _iota(s.shape, 1) + q_at, s, NEG_BIG)
    return s


ATT_SPLIT = 1


SUBLANES = 8


def _as_row(col_lanes):
    return jnp.transpose(col_lanes)[0:SUBLANES, :]


def _key_loop(lo, hi, t, step):
    rows = lambda i: pl.ds(pl.multiple_of(i * t, t), t)

    def body(i, carry):
        step(rows(lo + 2 * i))
        step(rows(lo + 2 * i + 1))
        return carry

    count = jnp.asarray(hi - lo, jnp.int32)
    lax.fori_loop(0, lax.div(count, 2), body, 0)

    @pl.when(lax.rem(count, 2) == 1)
    def _():
        step(rows(hi - 1))


def _attention_fwd(q, k, v, name, t=512, gather=()):
    s = q.shape[0]
    n = s // t
    ng = len(gather)

    def body(q_ref, k_ref, v_ref, *rest):
        shard_refs, rest = rest[:ng], rest[ng:]
        o_ref, lse_ref, lse_t_ref = rest[:3]
        out_refs, rest = rest[3:3 + ng], rest[3 + ng:]
        m_scr, acc_scr = rest[:2]
        qi = pl.program_id(1)
        if ng:
            @pl.when(jnp.logical_and(pl.program_id(0) == 0, qi == 0))
            def _():
                _gather_start(_gather_plan(shard_refs, out_refs, *rest[2:]))

        lane = _iota((1, LANES), 1)
        m_scr[...] = jnp.full_like(m_scr, NEG_BIG)
        acc_scr[...] = jnp.zeros_like(acc_scr)

        w = t // ATT_SPLIT

        def step(rows, diagonal=False):
            kb = k_ref[rows, :]
            vb = jnp.where(lane == MLA_V, 1.0, v_ref[rows, :].astype(F32)).astype(BF16)
            for c in range(ATT_SPLIT):
                cols = pl.ds(c * w, w)
                sc = _att_scores_t(kb, q_ref[cols, :], diagonal, c * w)
                m_old = m_scr[:, cols]
                m_new = jnp.maximum(m_old, jnp.max(sc, axis=0, keepdims=True))
                p = jnp.exp(sc - m_new)
                acc_scr[:, cols] = jnp.exp(m_old - m_new) * acc_scr[:, cols] + _tn(vb, p)
                m_scr[:, cols] = m_new

        _key_loop(0, qi, t, step)
        step(pl.ds(pl.multiple_of(qi * t, t), t), diagonal=True)
        acc = acc_scr[...]
        row = _iota((LANES, 1), 0)
        l = jnp.sum(jnp.where(row == MLA_V, acc, 0.0), axis=0, keepdims=True)
        o_ref[...] = jnp.transpose(jnp.where(row < MLA_V, acc / l, 0.0))
        lse = jnp.broadcast_to(m_scr[...] + jnp.log(l), (LANES, t))
        lse_ref[...] = jnp.transpose(lse)
        lse_t_ref[...] = lse[0:SUBLANES, :]
        if ng:
            @pl.when(jnp.logical_and(pl.program_id(0) == MLA_HEADS - 1, qi == n - 1))
            def _():
                _gather_finish(_gather_plan(shard_refs, out_refs, *rest[2:]))

    qspec = pl.BlockSpec((t, LANES), lambda h, qi: (qi, h))
    kspec = pl.BlockSpec((s, LANES), lambda h, qi: (0, h))
    out = jax.ShapeDtypeStruct((s, HEAD_BLOCK), F32)
    res = pl.pallas_call(
        body,
        name=name,
        out_shape=(out, out, jax.ShapeDtypeStruct((MLA_HEADS * SUBLANES, s), F32)) + _gathered_shapes(gather),
        grid=(MLA_HEADS, n),
        in_specs=[qspec, kspec, kspec] + [ANY] * ng,
        out_specs=(qspec, qspec, pl.BlockSpec((SUBLANES, t), lambda h, qi: (h, qi))) + (ANY,) * ng,
        scratch_shapes=[pltpu.VMEM((1, t), F32), pltpu.VMEM((LANES, t), F32)] + (_gather_scratch(ng) if ng else []),
        compiler_params=_cparams(("arbitrary", "arbitrary") if ng else ("parallel", "parallel")),
    )(q, k, v, *gather)
    return res[0], res[1], res[2], list(res[3:])


def _host_exchange(body, n_in, n_out, parts, last, pair=False):
    na = len(parts)
    if not na:
        return body
    plan, start, finish = (_pair_plan, _pair_start, _pair_finish) if pair else (
        _exchange_plan, _exchange_start, _exchange_finish)

    def hosted(*refs):
        ins, part_refs = refs[:n_in], refs[n_in:n_in + na]
        outs = refs[n_in + na:n_in + na + n_out]
        got_refs = refs[n_in + na + n_out:n_in + 2 * na + n_out]
        scratch, sems = refs[n_in + 2 * na + n_out:-3], refs[-3:]
        at = lambda step: jnp.logical_and(pl.program_id(0) == step[0], pl.program_id(1) == step[1])

        @pl.when(at((0, 0)))
        def _():
            start(plan(part_refs, got_refs, *sems))

        body(*ins, *outs, *scratch)

        @pl.when(at(last))
        def _():
            finish(plan(part_refs, got_refs, *sems))

    return hosted


def _attention_bwd(q, k, v, o, lse, lse_t, do, name, t=512, exchange_dq=(), exchange_dkv=()):
    s = q.shape[0]
    n = s // t
    last = (MLA_HEADS - 1, n - 1)
    sem = lambda parts: ("arbitrary", "arbitrary") if parts else ("parallel", "parallel")

    def dq_body(q_ref, k_ref, v_ref, o_ref, lse_ref, do_ref, dq_ref, delta_t_ref, acc_scr):
        qi = pl.program_id(1)
        do = do_ref[...]
        delta = jnp.sum(do * o_ref[...], axis=-1, keepdims=True)
        delta_t_ref[...] = _as_row(jnp.broadcast_to(delta, (t, LANES)))
        acc_scr[...] = jnp.zeros_like(acc_scr)

        w = t // ATT_SPLIT

        def step(rows, diagonal=False):
            kb, vb = k_ref[rows, :], v_ref[rows, :]
            for c in range(ATT_SPLIT):
                part = pl.ds(c * w, w)
                p = jnp.exp(_att_scores(q_ref[part, :], kb, diagonal, c * w) - lse_ref[part, 0:1])
                ds = p * (_nt(do_ref[part, :], vb) - delta[c * w:(c + 1) * w])
                acc_scr[part, :] += _nn(ds, kb)

        _key_loop(0, qi, t, step)
        step(pl.ds(pl.multiple_of(qi * t, t), t), diagonal=True)
        dq_ref[...] = acc_scr[...]

    def dkv_body(q_ref, k_ref, v_ref, lse_t_ref, delta_t_ref, do_ref, dk_ref, dv_ref, dk_scr, dv_scr):
        ki = pl.program_id(1)
        dk_scr[...] = jnp.zeros_like(dk_scr)
        dv_scr[...] = jnp.zeros_like(dv_scr)

        w = t // ATT_SPLIT

        def step(rows, diagonal=False):
            dv, dk = dv_scr[...], dk_scr[...]
            for c in range(ATT_SPLIT):
                part = pl.ds(pl.multiple_of(rows.start + c * w, w), w)
                qb = q_ref[part, :]
                dob = do_ref[part, :]
                p = jnp.exp(_att_scores_t(k_ref[...], qb, diagonal, c * w) - lse_t_ref[0:1, part])
                dv = dv + _nn(p, dob)
                ds = p * (_nt(v_ref[...], dob) - delta_t_ref[0:1, part])
                dk = dk + _nn(ds, qb)
            dv_scr[...] = dv
            dk_scr[...] = dk

        step(pl.ds(pl.multiple_of(ki * t, t), t), diagonal=True)
        _key_loop(ki + 1, n, t, step)
        dk_ref[...] = dk_scr[...]
        dv_ref[...] = dv_scr[...]

    out = jax.ShapeDtypeStruct((s, HEAD_BLOCK), F32)
    blk = pl.BlockSpec((t, LANES), lambda h, i: (i, h))
    head = pl.BlockSpec((s, LANES), lambda h, i: (0, h))
    row_blk = pl.BlockSpec((SUBLANES, t), lambda h, i: (h, i))
    row_head = pl.BlockSpec((SUBLANES, s), lambda h, i: (h, 0))
    na, nb = len(exchange_dq), len(exchange_dkv)
    dq, delta_t, *got_dq = pl.pallas_call(
        _host_exchange(dq_body, 6, 2, exchange_dq, last),
        name=name + "_dq",
        out_shape=(out, jax.ShapeDtypeStruct((MLA_HEADS * SUBLANES, s), F32)) + _same_shapes(exchange_dq),
        grid=(MLA_HEADS, n),
        in_specs=[blk, head, head, blk, blk, blk] + [ANY] * na,
        out_specs=(blk, row_blk) + (ANY,) * na,
        scratch_shapes=[pltpu.VMEM((t, LANES), F32)] + (_exchange_scratch(na) if na else []),
        compiler_params=_cparams(sem(exchange_dq)),
    )(q, k, v, o, lse, do, *exchange_dq)
    dk, dv, *got_dkv = pl.pallas_call(
        _host_exchange(dkv_body, 6, 2, exchange_dkv, last),
        name=name + "_dkv",
        out_shape=(out, out) + _same_shapes(exchange_dkv),
        grid=(MLA_HEADS, n),
        in_specs=[head, blk, blk, row_head, row_head, head] + [ANY] * nb,
        out_specs=(blk, blk) + (ANY,) * nb,
        scratch_shapes=[pltpu.VMEM((t, LANES), F32), pltpu.VMEM((t, LANES), F32)]
        + (_exchange_scratch(nb) if nb else []),
        compiler_params=_cparams(sem(exchange_dkv)),
    )(q, k, v, lse_t, delta_t, do, *exchange_dkv)
    return dq, dk, dv, got_dq, got_dkv


HALO = 8


def _conv_fwd(proj, w, b, name, tm=512):
    s = proj.shape[0]
    cb = P_XBC // SSD_CONV_DIM

    def body(x_ref, halo_ref, w_ref, b_ref, y_ref, cat_scr):
        i = pl.program_id(0)
        cat_scr[pl.ds(0, HALO), :] = jnp.where(i > 0, halo_ref[...], 0.0)
        cat_scr[pl.ds(HALO, tm), :] = x_ref[...]
        pre = b_ref[...]
        for j in range(SSD_CONV):
            pre = pre + w_ref[pl.ds(SSD_CONV - 1 - j, 1), :] * cat_scr[pl.ds(HALO - j, tm), :]
        y_ref[...] = _silu(pre)

    return pl.pallas_call(
        body,
        name=name,
        out_shape=jax.ShapeDtypeStruct((s, SSD_CONV_DIM), F32),
        grid=(s // tm,),
        in_specs=[
            pl.BlockSpec((tm, SSD_CONV_DIM), lambda i: (i, cb)),
            pl.BlockSpec((HALO, SSD_CONV_DIM), lambda i: (jnp.maximum(i * (tm // HALO) - 1, 0), cb)),
            _full((SSD_CONV, SSD_CONV_DIM)),
            _full((1, SSD_CONV_DIM)),
        ],
        out_specs=pl.BlockSpec((tm, SSD_CONV_DIM), lambda i: (i, 0)),
        scratch_shapes=[pltpu.VMEM((tm + HALO, SSD_CONV_DIM), F32)],
        compiler_params=_cparams(("parallel",)),
    )(proj, proj, w, b)


def _conv_bwd(proj, w, b, dact, name, tm=512):
    s = proj.shape[0]
    cb = P_XBC // SSD_CONV_DIM
    n = s // tm

    def pre_body(x_ref, halo_ref, w_ref, b_ref, dact_ref, dpre_ref, dw_ref, db_ref, cat_scr):
        i = pl.program_id(0)
        cat_scr[pl.ds(0, HALO), :] = jnp.where(i > 0, halo_ref[...], 0.0)
        cat_scr[pl.ds(HALO, tm), :] = x_ref[...]
        pre = b_ref[...]
        for j in range(SSD_CONV):
            pre = pre + w_ref[pl.ds(SSD_CONV - 1 - j, 1), :] * cat_scr[pl.ds(HALO - j, tm), :]
        sg = _sigmoid(pre)
        dpre = dact_ref[...] * (sg * (1.0 + pre * (1.0 - sg)))
        dpre_ref[...] = dpre
        first = i == 0
        _acc_rows(db_ref, dpre, first)
        for j in range(SSD_CONV):
            _acc_rows(dw_ref.at[pl.ds(SSD_CONV - 1 - j, 1), :], dpre * cat_scr[pl.ds(HALO - j, tm), :], first)

    dpre, dw, db = pl.pallas_call(
        pre_body,
        name=name + "_pre",
        out_shape=(
            jax.ShapeDtypeStruct((s, SSD_CONV_DIM), F32),
            jax.ShapeDtypeStruct((SSD_CONV, SSD_CONV_DIM), F32),
            jax.ShapeDtypeStruct((1, SSD_CONV_DIM), F32),
        ),
        grid=(n,),
        in_specs=[
            pl.BlockSpec((tm, SSD_CONV_DIM), lambda i: (i, cb)),
            pl.BlockSpec((HALO, SSD_CONV_DIM), lambda i: (jnp.maximum(i * (tm // HALO) - 1, 0), cb)),
            _full((SSD_CONV, SSD_CONV_DIM)),
            _full((1, SSD_CONV_DIM)),
            pl.BlockSpec((tm, SSD_CONV_DIM), lambda i: (i, 0)),
        ],
        out_specs=(
            pl.BlockSpec((tm, SSD_CONV_DIM), lambda i: (i, 0)),
            _full((SSD_CONV, SSD_CONV_DIM)),
            _full((1, SSD_CONV_DIM)),
        ),
        scratch_shapes=[pltpu.VMEM((tm + HALO, SSD_CONV_DIM), F32)],
        compiler_params=_cparams(("arbitrary",)),
    )(proj, proj, w, b, dact)

    def dx_body(d_ref, halo_ref, w_ref, dx_ref, cat_scr):
        i = pl.program_id(0)
        cat_scr[pl.ds(0, tm), :] = d_ref[...]
        cat_scr[pl.ds(tm, HALO), :] = jnp.where(i < n - 1, halo_ref[...], 0.0)
        dx = jnp.zeros((tm, SSD_CONV_DIM), F32)
        for j in range(SSD_CONV):
            dx = dx + w_ref[pl.ds(SSD_CONV - 1 - j, 1), :] * cat_scr[pl.ds(j, tm), :]
        dx_ref[...] = dx.astype(BF16)

    dx = pl.pallas_call(
        dx_body,
        name=name + "_dx",
        out_shape=jax.ShapeDtypeStruct((s, SSD_CONV_DIM), BF16),
        grid=(n,),
        in_specs=[
            pl.BlockSpec((tm, SSD_CONV_DIM), lambda i: (i, 0)),
            pl.BlockSpec((HALO, SSD_CONV_DIM), lambda i: (jnp.minimum((i + 1) * (tm // HALO), s // HALO - 1), 0)),
            _full((SSD_CONV, SSD_CONV_DIM)),
        ],
        out_specs=pl.BlockSpec((tm, SSD_CONV_DIM), lambda i: (i, 0)),
        scratch_shapes=[pltpu.VMEM((tm + HALO, SSD_CONV_DIM), F32)],
        compiler_params=_cparams(("parallel",)),
    )(dpre, dpre, w)
    return dx, dw, db


N_PAIR = SSD_HEADS // 2


def _ssd_chunk(xs, bm, cm, z, dtp, state, dtb, alog, dskip, ng):
    t = SSD_CHUNK
    lane = _iota((1, LANES), 1)
    row = _iota((LANES, 1), 0)
    dt_all = jnp.where(lane < SSD_HEADS, _softplus(dtp + dtb), 0.0)
    da = dt_all * (-jnp.exp(alog))
    causal = _iota((t, t), 0) >= _iota((t, t), 1)
    cs = _exact_nn(causal.astype(F32), da)
    cs_t = cs.T
    tot = jnp.sum(da, axis=0, keepdims=True)
    col = lambda m, h: jnp.sum(jnp.where(lane == h, m, 0.0), axis=1, keepdims=True)
    rowv = lambda m, h: jnp.sum(jnp.where(row == h, m, 0.0), axis=0, keepdims=True)
    low = lane < SSD_HEAD_DIM
    cb = [_nt(cm[g], bm[g]) for g in range(SSD_GROUPS)]
    gated, new_state = [], []
    for j in range(N_PAIR):
        g = j // (N_PAIR // SSD_GROUPS)
        h0, h1 = 2 * j, 2 * j + 1
        y = jnp.zeros((t, LANES), F32)
        for h, mask in ((h0, low), (h1, jnp.logical_not(low))):
            lmat = jnp.exp(jnp.where(causal, col(cs, h) - rowv(cs_t, h), NEG_BIG))
            y = y + _nn(cb[g] * lmat, jnp.where(mask, xs[j] * col(dt_all, h), 0.0))
        cs_p = jnp.where(low, col(cs, h0), col(cs, h1))
        dt_p = jnp.where(low, col(dt_all, h0), col(dt_all, h1))
        tot_p = jnp.where(low, col(tot, h0), col(tot, h1))
        tot_c = jnp.where(row < SSD_HEAD_DIM, col(tot, h0), col(tot, h1))
        d_p = jnp.where(low, col(dskip, h0), col(dskip, h1))
        xdt = xs[j] * dt_p
        y = y + _nt(cm[g], state[j]) * jnp.exp(cs_p) + xs[j] * d_p
        new_state.append(state[j] * jnp.exp(tot_c) + _tn(xdt * jnp.exp(tot_p - cs_p), bm[g]))
        gated.append(y * _silu(z[j]))
    out = []
    per_group = N_PAIR // SSD_GROUPS
    for g in range(SSD_GROUPS):
        blocks = gated[g * per_group:(g + 1) * per_group]
        ms = sum(jnp.sum(v * v, axis=-1, keepdims=True) for v in blocks) * (1.0 / (per_group * LANES))
        r = lax.rsqrt(ms + EPS)
        out += [v * r * ng[g * per_group + i] for i, v in enumerate(blocks)]
    return out, new_state


def _ssd_specs(rev, nc):
    idx = (lambda c: nc - 1 - c) if rev else (lambda c: c)
    t = SSD_CHUNK
    return [
        pl.BlockSpec((t, SSD_CONV_DIM), lambda c: (idx(c), 0)),
        pl.BlockSpec((t, SSD_INNER), lambda c: (idx(c), P_Z // SSD_INNER)),
        pl.BlockSpec((t, LANES), lambda c: (idx(c), P_DT // LANES)),
        _full((1, LANES)), _full((1, LANES)), _full((1, LANES)), _full((1, SSD_INNER)),
    ]


def _ssd_args(act_ref, z_ref, dt_ref, dtb_ref, alog_ref, dskip_ref, ng_ref):
    blk = lambda ref, off, n: [ref[:, pl.ds(off + i * LANES, LANES)] for i in range(n)]
    xs = blk(act_ref, 0, N_PAIR)
    bm = blk(act_ref, SSD_INNER, SSD_GROUPS)
    cm = blk(act_ref, SSD_INNER + SSD_GROUPS * SSD_STATE, SSD_GROUPS)
    return xs, bm, cm, blk(z_ref, 0, N_PAIR), dt_ref[...], dtb_ref[...], alog_ref[...], dskip_ref[...], blk(ng_ref, 0, N_PAIR)


def _ssd_fwd(act, proj, dtb, alog, dskip, ng, name):
    s = act.shape[0]
    nc = s // SSD_CHUNK

    def body(act_ref, z_ref, dt_ref, dtb_ref, alog_ref, dskip_ref, ng_ref, y_ref, st_ref, st_scr):
        @pl.when(pl.program_id(0) == 0)
        def _():
            st_scr[...] = jnp.zeros_like(st_scr)

        xs, bm, cm, z, dtp, dtb_v, alog_v, dskip_v, ng_v = _ssd_args(act_ref, z_ref, dt_ref, dtb_ref, alog_ref, dskip_ref, ng_ref)
        state = [st_scr[j] for j in range(N_PAIR)]
        st_ref[0] = st_scr[...]
        y, new_state = _ssd_chunk(xs, bm, cm, z, dtp, state, dtb_v, alog_v, dskip_v, ng_v)
        for j in range(N_PAIR):
            y_ref[:, pl.ds(j * LANES, LANES)] = y[j]
            st_scr[j] = new_state[j]

    return pl.pallas_call(
        body,
        name=name,
        out_shape=(
            jax.ShapeDtypeStruct((s, SSD_INNER), F32),
            jax.ShapeDtypeStruct((nc, N_PAIR, LANES, SSD_STATE), F32),
        ),
        grid=(nc,),
        in_specs=_ssd_specs(False, nc),
        out_specs=(
            pl.BlockSpec((SSD_CHUNK, SSD_INNER), lambda c: (c, 0)),
            pl.BlockSpec((1, N_PAIR, LANES, SSD_STATE), lambda c: (c, 0, 0, 0)),
        ),
        scratch_shapes=[pltpu.VMEM((N_PAIR, LANES, SSD_STATE), F32)],
        compiler_params=_cparams(("arbitrary",)),
    )(act, proj, proj, dtb, alog, dskip, ng)


def _ssd_bwd(act, proj, dtb, alog, dskip, ng, states, dy, name):
    s = act.shape[0]
    nc = s // SSD_CHUNK

    def body(act_ref, z_ref, dt_ref, dtb_ref, alog_ref, dskip_ref, ng_ref, st_ref, dy_ref,
             dact_ref, dz_ref, ddt_ref, ddtb_ref, dalog_ref, ddskip_ref, dng_ref, dst_scr):
        first = pl.program_id(0) == 0

        @pl.when(first)
        def _():
            dst_scr[...] = jnp.zeros_like(dst_scr)

        xs, bm, cm, z, dtp, dtb_v, alog_v, dskip_v, ng_v = _ssd_args(act_ref, z_ref, dt_ref, dtb_ref, alog_ref, dskip_ref, ng_ref)
        state = [st_ref[0, j] for j in range(N_PAIR)]
        _, vjp = jax.vjp(_ssd_chunk, xs, bm, cm, z, dtp, state, dtb_v, alog_v, dskip_v, ng_v)
        dy_v = [dy_ref[:, pl.ds(j * LANES, LANES)] for j in range(N_PAIR)]
        dxs, dbm, dcm, dz, ddtp, dstate, ddtb, dalog, ddskip, dng = vjp((dy_v, [dst_scr[j] for j in range(N_PAIR)]))
        for i, v in enumerate(dxs + dbm + dcm):
            dact_ref[:, pl.ds(i * LANES, LANES)] = v
        for j in range(N_PAIR):
            dz_ref[:, pl.ds(j * LANES, LANES)] = dz[j].astype(BF16)
            dst_scr[j] = dstate[j]
            _acc(dng_ref.at[:, pl.ds(j * LANES, LANES)], dng[j], first)
        ddt_ref[...] = ddtp.astype(BF16)
        _acc(ddtb_ref, ddtb, first)
        _acc(dalog_ref, dalog, first)
        _acc(ddskip_ref, ddskip, first)

    rv = lambda c: nc - 1 - c
    sds = jax.ShapeDtypeStruct
    return pl.pallas_call(
        body,
        name=name,
        out_shape=(
            sds((s, SSD_CONV_DIM), F32), sds((s, SSD_INNER), BF16), sds((s, LANES), BF16),
            sds((1, LANES), F32), sds((1, LANES), F32), sds((1, LANES), F32), sds((1, SSD_INNER), F32),
        ),
        grid=(nc,),
        in_specs=_ssd_specs(True, nc) + [
            pl.BlockSpec((1, N_PAIR, LANES, SSD_STATE), lambda c: (rv(c), 0, 0, 0)),
            pl.BlockSpec((SSD_CHUNK, SSD_INNER), lambda c: (rv(c), 0)),
        ],
        out_specs=(
            pl.BlockSpec((SSD_CHUNK, SSD_CONV_DIM), lambda c: (rv(c), 0)),
            pl.BlockSpec((SSD_CHUNK, SSD_INNER), lambda c: (rv(c), 0)),
            pl.BlockSpec((SSD_CHUNK, LANES), lambda c: (rv(c), 0)),
            _full((1, LANES)), _full((1, LANES)), _full((1, LANES)), _full((1, SSD_INNER)),
        ),
        scratch_shapes=[pltpu.VMEM((N_PAIR, LANES, SSD_STATE), F32)],
        compiler_params=_cparams(("arbitrary",)),
    )(act, proj, proj, dtb, alog, dskip, ng, states, dy)


def _merge_specs(tm):
    row = lambda w: pl.BlockSpec((tm, w), lambda i: (i, 0))
    return [
        row(GM_WIDTH), row(HEAD_BLOCK), row(SSD_INNER),
        pl.BlockSpec((tm, N_BRANCH * D_MODEL), lambda i: (i, P_GATES // (N_BRANCH * D_MODEL))),
        row(D_MODEL),
        _full((GM_WIDTH, D_MODEL)), _full((HEAD_BLOCK, D_MODEL)), _full((SSD_INNER, D_MODEL)), _full((D_MODEL, D_MODEL)),
    ]


def _merge_fwd(ya, yb, yc, proj, x1, pa, pb, pc, wo, name, tm=256):
    s = x1.shape[0]

    def body(ya_ref, yb_ref, yc_ref, gates_ref, x1_ref, pa_ref, pb_ref, pc_ref, wo_ref, x2_ref, mg_ref):
        merged = jnp.zeros((tm, D_MODEL), F32)
        for i, (y_ref, p_ref) in enumerate(((ya_ref, pa_ref), (yb_ref, pb_ref), (yc_ref, pc_ref))):
            gate = _sigmoid(gates_ref[:, pl.ds(i * D_MODEL, D_MODEL)])
            merged = merged + gate * _nn(y_ref[...], p_ref[...])
        mg_ref[...] = merged.astype(BF16)
        x2_ref[...] = x1_ref[...] + _nn(merged, wo_ref[...])

    row = lambda w: pl.BlockSpec((tm, w), lambda i: (i, 0))
    return pl.pallas_call(
        body,
        name=name,
        out_shape=(jax.ShapeDtypeStruct((s, D_MODEL), F32), jax.ShapeDtypeStruct((s, D_MODEL), BF16)),
        grid=(s // tm,),
        in_specs=_merge_specs(tm),
        out_specs=(row(D_MODEL), row(D_MODEL)),
        compiler_params=_cparams(("parallel",)),
    )(ya, yb, yc, proj, x1, pa, pb, pc, wo)


def _merge_bwd(ya, yb, yc, proj, dx2, pa, pb, pc, wo, name, tm=256):
    s = dx2.shape[0]

    def body(ya_ref, yb_ref, yc_ref, gates_ref, dx2_ref, pa_ref, pb_ref, pc_ref, wo_ref,
             dya_ref, dyb_ref, dyc_ref, dgates_ref, ta_ref, tb_ref, tc_ref):
        dmerged = _nt(dx2_ref[...], wo_ref[...])
        branches = ((ya_ref, pa_ref, dya_ref, ta_ref), (yb_ref, pb_ref, dyb_ref, tb_ref), (yc_ref, pc_ref, dyc_ref, tc_ref))
        for i, (y_ref, p_ref, dy_ref, t_ref) in enumerate(branches):
            cols = pl.ds(i * D_MODEL, D_MODEL)
            gate = _sigmoid(gates_ref[:, cols])
            dgates_ref[:, cols] = (dmerged * _nn(y_ref[...], p_ref[...]) * gate * (1.0 - gate)).astype(BF16)
            dt = (dmerged * gate).astype(BF16)
            t_ref[...] = dt
            dy_ref[...] = _nt(dt, p_ref[...])

    row = lambda w: pl.BlockSpec((tm, w), lambda i: (i, 0))
    sds = jax.ShapeDtypeStruct
    return pl.pallas_call(
        body,
        name=name,
        out_shape=(
            sds((s, GM_WIDTH), F32), sds((s, HEAD_BLOCK), F32), sds((s, SSD_INNER), F32),
            sds((s, N_BRANCH * D_MODEL), BF16),
            sds((s, D_MODEL), BF16), sds((s, D_MODEL), BF16), sds((s, D_MODEL), BF16),
        ),
        grid=(s // tm,),
        in_specs=_merge_specs(tm),
        out_specs=(row(GM_WIDTH), row(HEAD_BLOCK), row(SSD_INNER), row(N_BRANCH * D_MODEL),
                   row(D_MODEL), row(D_MODEL), row(D_MODEL)),
        compiler_params=_cparams(("parallel",)),
    )(ya, yb, yc, proj, dx2, pa, pb, pc, wo)


def _loss_head(y, target, name, tm=512):
    s, d = y.shape

    def body(y_ref, t_ref, dy_ref, loss_ref):
        err = y_ref[...] - t_ref[...]
        dy_ref[...] = err * (1.0 / d)
        part = jnp.sum(jnp.sum(err * err, axis=1, keepdims=True), axis=0, keepdims=True) * (0.5 / d)
        _acc(loss_ref, jnp.broadcast_to(part, (1, LANES)), pl.program_id(0) == 0)

    return pl.pallas_call(
        body,
        name=name,
        out_shape=(jax.ShapeDtypeStruct((s, d), F32), jax.ShapeDtypeStruct((1, LANES), F32)),
        grid=(s // tm,),
        in_specs=[pl.BlockSpec((tm, d), lambda i: (i, 0)), pl.BlockSpec((tm, d), lambda i: (i, 0))],
        out_specs=(pl.BlockSpec((tm, d), lambda i: (i, 0)), _full((1, LANES))),
        compiler_params=_cparams(("arbitrary",)),
    )(y, target)


IN_SHARD = IN_COLS // N_DEV
P_OF_PIECE = (P_UV, P_CQ, P_CKV, P_KR + MLA_NOPE, P_Z, P_XBC, P_DT, P_GATES)


def _pad_lanes(w, n=LANES):
    return jnp.pad(w, [(0, 0)] * (w.ndim - 1) + [(0, n - w.shape[-1])])


def _in_proj_layout(blocks):
    def cols(i):
        a, b, out = IN_OFFSETS[i], IN_OFFSETS[i] + IN_WIDTHS[i], []
        while a < b:
            k, lo = divmod(a, IN_SHARD)
            hi = min(IN_SHARD, lo + b - a)
            out.append(blocks[k, :, lo:hi])
            a += hi - lo
        return out

    zeros = lambda n: jnp.zeros((D_MODEL, n), blocks.dtype)
    uv, cq, ckv, kr, z, xbc, dt, gates = (cols(i) for i in range(8))
    return jnp.concatenate(uv + xbc + z + [zeros(MLA_NOPE)] + kr + [zeros(LANES - MLA_QK_DIM)] + cq + gates + ckv
                           + dt + [zeros(LANES - SSD_HEADS)], axis=1)


def _in_proj_unlayout(dw):
    out = []
    for k in range(N_DEV):
        a, b, parts = k * IN_SHARD, (k + 1) * IN_SHARD, []
        for i in range(8):
            lo, hi = max(a, IN_OFFSETS[i]), min(b, IN_OFFSETS[i] + IN_WIDTHS[i])
            if lo < hi:
                at = P_OF_PIECE[i] + lo - IN_OFFSETS[i]
                parts.append(dw[:, at:at + hi - lo])
        out.append(jnp.concatenate(parts, axis=1))
    return jnp.stack(out)


def _row(v, n=None):
    v = v.reshape(1, -1)
    return v if n is None else jnp.pad(v, ((0, 0), (0, n - v.shape[1])))


GATHERED = ("ffn_w_in", "ffn_w_out", "w_in", "mla_w_uq", "mla_w_ukv", "ssd_conv_w", "w_branch", "w_out")


def _layer_shards(w, l):
    pair = lambda a, b: jnp.stack([w[a][l], w[b][l]]).astype(BF16)
    one = lambda n: w[n][l].astype(BF16)
    return [pair("ffn1_w_in", "ffn2_w_in"), pair("ffn1_w_out", "ffn2_w_out"), one("w_in"), one("mla_w_uq"),
            one("mla_w_ukv"), one("ssd_conv_w"), one("w_branch"), one("w_out")]


def _layer_weights(gathered, vec, l):
    gw = dict(zip(GATHERED, gathered))
    kv = gw["mla_w_ukv"]
    branch = jnp.moveaxis(gw["w_branch"], 0, 2).reshape(N_BRANCH, GM_WIDTH, D_MODEL)
    return dict(
        ffn_w_in=gw["ffn_w_in"], ffn_w_out=gw["ffn_w_out"], ffn1_at=0, ffn2_at=1,
        ffn1_norm=_row(vec["ffn1_norm"][l]), ffn2_norm=_row(vec["ffn2_norm"][l]),
        mix_norm=_row(vec["mix_norm"][l]), w_in=_in_proj_layout(gw["w_in"]),
        gm_v_norm=_row(vec["gm_v_norm"][l]), gm_w_s=vec["gm_w_s"][l], gm_b_s=vec["gm_b_s"][l][..., None],
        q_norm=_row(vec["mla_q_norm"][l]), kv_norm=_row(vec["mla_kv_norm"][l]),
        wq=_pad_lanes(gw["mla_w_uq"]), wk=_pad_lanes(kv[:, :, :MLA_NOPE]), wv=_pad_lanes(kv[:, :, MLA_NOPE:]),
        q_gain=_row(vec["mla_q_gain"][l], LANES), k_gain=_row(vec["mla_k_gain"][l], LANES),
        conv_w=jnp.moveaxis(gw["ssd_conv_w"], 0, 1).reshape(SSD_CONV, SSD_CONV_DIM).astype(F32),
        conv_b=_row(vec["ssd_conv_b"][l]),
        dt_bias=_row(vec["ssd_dt_bias"][l], LANES), a_log=_row(vec["ssd_a_log"][l], LANES),
        d_skip=_row(vec["ssd_d"][l], LANES), ssd_norm=_row(vec["ssd_norm"][l]),
        pa=branch[0],
        pb=jnp.pad(branch[1].reshape(MLA_HEADS, MLA_V, D_MODEL), ((0, 0), (0, LANES - MLA_V), (0, 0))).reshape(HEAD_BLOCK, D_MODEL),
        pc=branch[2], wo=gw["w_out"].reshape(D_MODEL, D_MODEL),
    )


def _layer_fwd(x, k, rope, next_shards=()):
    cosf, sinf, rot = rope
    x1, gu1 = _ffn_fwd(x, k["ffn1_norm"], k["ffn_w_in"], k["ffn_w_out"], k["ffn1_at"], "ffn1_fwd")
    h = _rmsnorm_fwd(x1, k["mix_norm"], "mix_norm_fwd")
    proj = _matmul(h, k["w_in"], "nn", F32, "in_proj_fwd", tn=2176)
    ya = _gmlp_fwd(proj, k["gm_v_norm"], k["gm_w_s"], k["gm_b_s"], "gmlp_fwd")
    q, kk, v = _mla_pre_fwd(proj, cosf, sinf, rot, k["q_norm"], k["kv_norm"], k["wq"], k["wk"], k["wv"],
                            k["q_gain"], k["k_gain"], "mla_pre_fwd")
    yb, lse, lse_t, next_gathered = _attention_fwd(q, kk, v, "attention_fwd", gather=next_shards)
    act = _conv_fwd(proj, k["conv_w"], k["conv_b"], "conv_fwd")
    yc, states = _ssd_fwd(act, proj, k["dt_bias"], k["a_log"], k["d_skip"], k["ssd_norm"], "ssd_fwd")
    x2, merged = _merge_fwd(ya, yb, yc, proj, x1, k["pa"], k["pb"], k["pc"], k["wo"], "merge_fwd")
    x3, gu2 = _ffn_fwd(x2, k["ffn2_norm"], k["ffn_w_in"], k["ffn_w_out"], k["ffn2_at"], "ffn2_fwd")
    saved = dict(x=x, x1=x1, x2=x2, gu1=gu1, gu2=gu2, h=h, proj=proj, ya=ya, yb=yb, yc=yc, q=q, k=kk, v=v,
                 lse=lse, lse_t=lse_t, act=act, states=states, merged=merged)
    return x3, saved, next_gathered


def _layer_bwd(dx3, k, sv, rope, above=()):
    cosf, sinf, rot = rope
    g = {}
    dx2, g["ffn2_norm"], g["ffn2_w_in"], g["ffn2_w_out"], theirs = _ffn_bwd(
        sv["x2"], k["ffn2_norm"], k["ffn_w_in"], k["ffn_w_out"], k["ffn2_at"], sv["gu2"], dx3, "ffn2_bwd",
        pair=tuple(above))
    exchange = _pair_sums(above, theirs, "grads")
    proj = sv["proj"]
    dya, dyb, dyc, dgates, ta, tb, tc = _merge_bwd(sv["ya"], sv["yb"], sv["yc"], proj, dx2, k["pa"], k["pb"], k["pc"],
                                                   k["wo"], "merge_bwd")
    g["w_out"] = _matmul(sv["merged"], dx2, "tn", BF16, "w_out_grad").reshape(N_DEV, D_MODEL // N_DEV, D_MODEL)
    dpa = _matmul(sv["ya"], ta, "tn", BF16, "branch_a_grad")
    dpb = _matmul(sv["yb"], tb, "tn", BF16, "branch_b_grad")
    dpc = _matmul(sv["yc"], tc, "tn", BF16, "branch_c_grad")
    branch = jnp.stack([dpa, dpb.reshape(MLA_HEADS, LANES, D_MODEL)[:, :MLA_V].reshape(GM_WIDTH, D_MODEL), dpc])
    g["w_branch"] = jnp.moveaxis(branch.reshape(N_BRANCH, GM_WIDTH, N_DEV, LANES), 2, 0)
    duv, g["gm_v_norm"], g["gm_w_s"], dbs = _gmlp_bwd(proj, k["gm_v_norm"], k["gm_w_s"], k["gm_b_s"], dya, "gmlp_bwd")
    g["gm_b_s"] = dbs[..., 0]
    dq, dk, dv, got_a, got_b = _attention_bwd(sv["q"], sv["k"], sv["v"], sv["yb"], sv["lse"], sv["lse_t"], dyb,
                                               "attention_bwd", exchange_dq=tuple(exchange[:1]),
                                               exchange_dkv=tuple(exchange[1:]))
    dcq, dckv, dkr, dqn, dkvn, dwq, dwk, dwv, dqg, dkg = _mla_pre_bwd(
        proj, cosf, sinf, rot, k["q_norm"], k["kv_norm"], k["wq"], k["wk"], k["wv"], k["q_gain"], k["k_gain"],
        dq, dk, dv, "mla_pre_bwd")
    g["mla_q_norm"], g["mla_kv_norm"] = dqn, dkvn
    g["mla_w_uq"] = dwq[:, :, :MLA_QK_DIM].astype(BF16)
    g["mla_w_ukv"] = jnp.concatenate([dwk[:, :, :MLA_NOPE], dwv[:, :, :MLA_V]], axis=-1).astype(BF16)
    g["mla_q_gain"], g["mla_k_gain"] = dqg[:, :MLA_QK_DIM], dkg[:, :MLA_QK_DIM]
    dact, dz, ddt, ddtb, dalog, ddsk, g["ssd_norm"] = _ssd_bwd(
        sv["act"], proj, k["dt_bias"], k["a_log"], k["d_skip"], k["ssd_norm"], sv["states"], dyc, "ssd_bwd")
    g["ssd_dt_bias"], g["ssd_a_log"], g["ssd_d"] = ddtb[:, :SSD_HEADS], dalog[:, :SSD_HEADS], ddsk[:, :SSD_HEADS]
    dxbc, dcw, g["ssd_conv_b"] = _conv_bwd(proj, k["conv_w"], k["conv_b"], dact, "conv_bwd")
    g["ssd_conv_w"] = jnp.moveaxis(dcw.reshape(SSD_CONV, N_DEV, LANES), 1, 0).astype(BF16)
    dproj = jnp.concatenate([duv, dxbc, dz, dkr, dcq, dgates, dckv, ddt], axis=1)
    dh = _matmul(dproj, k["w_in"], "nt", BF16, "in_proj_dh", tk=2176)
    g["w_in"] = _in_proj_unlayout(_matmul(sv["h"], dproj, "tn", BF16, "in_proj_grad", tm=D_MODEL, tn=2176))
    dx1, g["mix_norm"] = _rmsnorm_bwd(sv["x1"], k["mix_norm"], dh, dx2, "mix_norm_bwd")
    dx, g["ffn1_norm"], g["ffn1_w_in"], g["ffn1_w_out"], _ = _ffn_bwd(
        sv["x"], k["ffn1_norm"], k["ffn_w_in"], k["ffn_w_out"], k["ffn1_at"], sv["gu1"], dx1, "ffn1_bwd")
    return dx, g, got_a + got_b


def _layer_contribs(g):
    pair = lambda a, b: jnp.stack([g[a], g[b]], axis=1)
    arrays = [pair("ffn1_w_in", "ffn2_w_in"), pair("ffn1_w_out", "ffn2_w_out"), g["w_in"], g["mla_w_uq"],
              g["mla_w_ukv"], g["ssd_conv_w"], g["w_branch"], g["w_out"]]
    return [a.reshape(N_DEV, -1, a.shape[-1]) for a in arrays]


def _pair_sums(contribs, theirs, name):
    return [_pair_sum(c, t, "%s_pair_sum_%s" % (name, n), _tile_rows(c.shape[1], c.shape[2]))
            for n, c, t in zip(GATHERED, contribs, theirs)]


def _rope_tables(positions):
    s = positions.shape[0]
    inv_freq = 1.0 / (ROPE_THETA ** (jnp.arange(0, MLA_ROPE, 2, dtype=F32) / MLA_ROPE))
    ang = positions.astype(F32)[:, None] * inv_freq
    cos, sin = jnp.cos(ang), jnp.sin(ang)
    tail = LANES - MLA_QK_DIM
    cosf = jnp.concatenate([jnp.ones((s, MLA_NOPE), F32), cos, cos, jnp.ones((s, tail), F32)], axis=1)
    sinf = jnp.concatenate([jnp.zeros((s, MLA_NOPE), F32), sin, sin, jnp.zeros((s, tail), F32)], axis=1)
    half = MLA_ROPE // 2
    rot = np.zeros((LANES, LANES), np.float32)
    for i in range(half):
        rot[MLA_NOPE + half + i, MLA_NOPE + i] = -1.0
        rot[MLA_NOPE + i, MLA_NOPE + half + i] = 1.0
    return cosf, sinf, jnp.asarray(rot)


MATRICES = ("ffn1_w_in", "ffn1_w_out", "w_in", "mla_w_uq", "mla_w_ukv", "ssd_conv_w", "w_branch", "w_out", "ffn2_w_in",
            "ffn2_w_out")
VECTORS = ("ffn1_norm", "mix_norm", "gm_v_norm", "gm_w_s", "gm_b_s", "mla_q_norm", "mla_kv_norm", "mla_q_gain",
           "mla_k_gain", "ssd_conv_b", "ssd_dt_bias", "ssd_a_log", "ssd_d", "ssd_norm", "ffn2_norm")
WEIGHTS = ("ffn1_norm", "ffn1_w_in", "ffn1_w_out", "mix_norm", "w_in", "gm_v_norm", "gm_w_s", "gm_b_s", "mla_q_norm",
           "mla_kv_norm", "mla_w_uq", "mla_w_ukv", "mla_q_gain", "mla_k_gain", "ssd_conv_w", "ssd_conv_b", "ssd_dt_bias",
           "ssd_a_log", "ssd_d", "ssd_norm", "w_branch", "w_out", "ffn2_norm", "ffn2_w_in", "ffn2_w_out")


def _local_step(x, positions, target, vec, shards=None, gathered=None):
    rope = _rope_tables(positions)
    depth = vec["ffn1_norm"].shape[0]
    saved = []
    here = _gather_layer(shards[0], "layer0_all_gather") if gathered is None else gathered[0]
    for l in range(depth):
        k = _layer_weights(here, vec, l)
        ahead = shards[l + 1] if gathered is None and l + 1 < depth else ()
        x, sv, here = _layer_fwd(x, k, rope, ahead)
        if gathered is not None and l + 1 < depth:
            here = gathered[l + 1]
        saved.append((k, sv))
    dy, loss = _loss_head(x, target, "loss_head")
    grads, reduced, above = [], [], ()
    for k, sv in reversed(saved):
        dy, g, got = _layer_bwd(dy, k, sv, rope, above=above)
        grads.append(g)
        if gathered is None:
            if above:
                reduced.append(got)
            above = _layer_contribs(g)
    if gathered is None:
        sums = _pair_sums(above, _pair_exchange(above, "layer0_grads_pair_exchange"), "layer0_grads")
        reduced.append(_chip_exchange(sums, "layer0_grads_chip_exchange"))
    grads.reverse()
    reduced.reverse()
    out = {n: jnp.stack([g[n].reshape(vec[n].shape[1:]) for g in grads]) for n in VECTORS}
    if gathered is None:
        out["reduced"] = reduced
    else:
        out.update({n: [g[n] for g in grads] for n in MATRICES})
    return loss[0, 0], dy, out


MESH = pl.DeviceIdType.MESH
N_CHIP = 4
ANY = pl.BlockSpec(memory_space=pl.ANY)


def _place():
    return lax.axis_index("x"), lax.axis_index("y"), lax.axis_index("c")


GATHER_COPIES = 7


def _gather_plan(shard_refs, out_refs, send_sems, recv_sems, local_sems):
    x, y, c = _place()
    me, sibling = (x, y, c), (x, y, 1 - c)
    chips = [(1 - x, y), (x, 1 - y), (1 - x, 1 - y)]
    slot = lambda px, py, pc: 4 * px + 2 * py + pc
    plans = []
    for i, (src, out) in enumerate(zip(shard_refs, out_refs)):
        def copy(k, block, to, from_shard=False, i=i, src=src, out=out):
            return pltpu.make_async_remote_copy(
                src_ref=src if from_shard else out.at[slot(*block)], dst_ref=out.at[slot(*block)],
                send_sem=send_sems.at[GATHER_COPIES * i + k], recv_sem=recv_sems.at[GATHER_COPIES * i + k],
                device_id=to, device_id_type=MESH)

        plans.append(dict(
            mine=lambda i=i, src=src, out=out: pltpu.make_async_copy(src, out.at[slot(*me)], local_sems.at[i]),
            first=lambda copy=copy: [copy(0, me, sibling, True)] + [copy(1 + j, me, (*chip, c), True)
                                                                    for j, chip in enumerate(chips)],
            arrived=lambda j, copy=copy: copy(1 + j, (*chips[j], c), me),
            passed=lambda j, copy=copy: copy(4 + j, (*chips[j], c), sibling),
            from_sibling=lambda copy=copy: [copy(0, sibling, me)] + [copy(4 + j, (*chip, 1 - c), me)
                                                                     for j, chip in enumerate(chips)],
        ))
    return plans


def _gather_start(plans):
    for p in plans:
        p["mine"]().start()
        for cp in p["first"]():
            cp.start()


def _gather_finish(plans):
    for j in range(N_CHIP - 1):
        for p in plans:
            p["arrived"](j).wait_recv()
            p["passed"](j).start()
    for p in plans:
        for cp in p["from_sibling"]():
            cp.wait_recv()
        for cp in p["first"]() + [p["passed"](j) for j in range(N_CHIP - 1)]:
            cp.wait_send()
        p["mine"]().wait()


def _gather_scratch(n):
    return [pltpu.SemaphoreType.DMA((GATHER_COPIES * n,)), pltpu.SemaphoreType.DMA((GATHER_COPIES * n,)),
            pltpu.SemaphoreType.DMA((n,))]


def _gathered_shapes(shards):
    return tuple(jax.ShapeDtypeStruct((N_DEV, *a.shape), a.dtype) for a in shards)


def _gather_layer(shards, name):
    n = len(shards)

    def body(*refs):
        plans = _gather_plan(refs[:n], refs[n:2 * n], *refs[2 * n:])
        _gather_start(plans)
        _gather_finish(plans)

    return pl.pallas_call(
        body,
        name=name,
        out_shape=_gathered_shapes(shards),
        in_specs=[ANY] * n,
        out_specs=(ANY,) * n,
        scratch_shapes=_gather_scratch(n),
    )(*shards)


def _all_gather(shard, name):
    m, n = shard.shape

    def body(x_ref, out_ref, send_sems, recv_sems, local_sem):
        x, y, c = _place()
        me, sibling = (x, y, c), (x, y, 1 - c)
        chips = [(1 - x, y), (x, 1 - y), (1 - x, 1 - y)]

        def rows(px, py, pc):
            return out_ref.at[pl.ds((4 * px + 2 * py + pc) * m, m), :]

        def copy(k, block, to, src=None):
            return pltpu.make_async_remote_copy(
                src_ref=rows(*block) if src is None else src, dst_ref=rows(*block),
                send_sem=send_sems.at[k], recv_sem=recv_sems.at[k], device_id=to, device_id_type=MESH)

        mine = pltpu.make_async_copy(x_ref, rows(*me), local_sem)
        mine.start()
        first = [copy(0, me, sibling, src=x_ref)]
        first += [copy(1 + j, me, (*chip, c), src=x_ref) for j, chip in enumerate(chips)]
        for cp in first:
            cp.start()
        passed = [copy(4 + j, (*chip, c), sibling) for j, chip in enumerate(chips)]
        for j, chip in enumerate(chips):
            copy(1 + j, (*chip, c), me).wait_recv()
            passed[j].start()
        copy(0, sibling, me).wait_recv()
        for j, chip in enumerate(chips):
            copy(4 + j, (*chip, 1 - c), me).wait_recv()
        for cp in first + passed:
            cp.wait_send()
        mine.wait()

    return pl.pallas_call(
        body,
        name=name,
        out_shape=jax.ShapeDtypeStruct((N_DEV * m, n), shard.dtype),
        in_specs=[ANY],
        out_specs=ANY,
        scratch_shapes=[pltpu.SemaphoreType.DMA((7,)), pltpu.SemaphoreType.DMA((7,)), pltpu.SemaphoreType.DMA],
    )(shard)


def _pair_exchange(contribs, name):
    na = len(contribs)

    def body(*refs):
        plan = _pair_plan(refs[:na], refs[na:2 * na], *refs[2 * na:])
        _pair_start(plan)
        _pair_finish(plan)

    return pl.pallas_call(
        body,
        name=name,
        out_shape=_pair_shapes(contribs),
        in_specs=[ANY] * na,
        out_specs=(ANY,) * na,
        scratch_shapes=_pair_scratch(na),
    )(*contribs)


def _pair_plan(g_refs, got_refs, send_sems, recv_sems, unused_sems):
    x, y, c = _place()
    return [lambda i=i, j=j, g=g, got=got: pltpu.make_async_remote_copy(
        src_ref=g.at[2 * j + (1 - c)], dst_ref=got.at[j], send_sem=send_sems.at[N_CHIP * i + j],
        recv_sem=recv_sems.at[N_CHIP * i + j], device_id=(x, y, 1 - c), device_id_type=MESH)
        for i, (g, got) in enumerate(zip(g_refs, got_refs)) for j in range(N_CHIP)]


def _pair_start(plan):
    for cp in plan:
        cp().start()


def _pair_finish(plan):
    for cp in plan:
        cp().wait()


def _pair_scratch(n):
    return [pltpu.SemaphoreType.DMA((N_CHIP * n,)), pltpu.SemaphoreType.DMA((N_CHIP * n,)), pltpu.SemaphoreType.DMA((1,))]


def _pair_shapes(contribs):
    return tuple(jax.ShapeDtypeStruct((N_CHIP, *a.shape[1:]), a.dtype) for a in contribs)


def _pair_sum(contrib, theirs, name, tr):
    _, r, n = contrib.shape
    side = lax.axis_index("c").astype(jnp.int32).reshape(1)

    def body(c_ref, a_ref, b_ref, o_ref):
        o_ref[...] = (a_ref[...].astype(F32) + b_ref[...].astype(F32)).astype(BF16)

    spec = pl.BlockSpec((1, tr, n), lambda j, i, c_ref: (j, i, 0))
    return pl.pallas_call(
        body,
        name=name,
        out_shape=jax.ShapeDtypeStruct(theirs.shape, BF16),
        grid_spec=pltpu.PrefetchScalarGridSpec(
            num_scalar_prefetch=1,
            grid=(N_CHIP, r // tr),
            in_specs=[pl.BlockSpec((1, tr, n), lambda j, i, c_ref: (2 * j + c_ref[0], i, 0)), spec],
            out_specs=spec,
        ),
        compiler_params=_cparams(("parallel", "parallel")),
    )(side, contrib, theirs)


OTHER_CHIPS = N_CHIP - 1


def _exchange_plan(part_refs, got_refs, send_sems, recv_sems, local_sems):
    x, y, c = _place()
    mine = 2 * x + y
    chips = [(1 - x, y), (x, 1 - y), (1 - x, 1 - y)]
    plans = []
    for i, (p, got) in enumerate(zip(part_refs, got_refs)):
        def copy(k, outbound, i=i, p=p, got=got):
            cx, cy = chips[k]
            return pltpu.make_async_remote_copy(
                src_ref=p.at[2 * cx + cy] if outbound else p.at[mine],
                dst_ref=got.at[mine] if outbound else got.at[2 * cx + cy],
                send_sem=send_sems.at[OTHER_CHIPS * i + k], recv_sem=recv_sems.at[OTHER_CHIPS * i + k],
                device_id=(cx, cy, c), device_id_type=MESH)

        plans.append(dict(copy=copy, local=lambda i=i, p=p, got=got: pltpu.make_async_copy(
            p.at[mine], got.at[mine], local_sems.at[i])))
    return plans


def _exchange_start(plans):
    for p in plans:
        p["local"]().start()
        for k in range(OTHER_CHIPS):
            p["copy"](k, True).start()


def _exchange_finish(plans):
    for p in plans:
        for k in range(OTHER_CHIPS):
            p["copy"](k, False).wait_recv()
        for k in range(OTHER_CHIPS):
            p["copy"](k, True).wait_send()
        p["local"]().wait()


def _exchange_scratch(n):
    return [pltpu.SemaphoreType.DMA((OTHER_CHIPS * n,)), pltpu.SemaphoreType.DMA((OTHER_CHIPS * n,)),
            pltpu.SemaphoreType.DMA((n,))]


def _same_shapes(arrays):
    return tuple(jax.ShapeDtypeStruct(a.shape, a.dtype) for a in arrays)


def _chip_exchange(parts, name):
    na = len(parts)

    def body(*refs):
        plans = _exchange_plan(refs[:na], refs[na:2 * na], *refs[2 * na:])
        _exchange_start(plans)
        _exchange_finish(plans)

    return pl.pallas_call(
        body,
        name=name,
        out_shape=_same_shapes(parts),
        in_specs=[ANY] * na,
        out_specs=(ANY,) * na,
        scratch_shapes=_exchange_scratch(na),
    )(*parts)


def _adamw(parts, w, m, v, name, tr, at=0):
    k = parts.shape[0]
    r, n = w.shape
    first = at // tr

    def body(p_ref, w_ref, m_ref, v_ref, g_ref, d_ref, nm_ref, nv_ref):
        g = p_ref[0].astype(F32)
        for i in range(1, k):
            g = g + p_ref[i].astype(F32)
        m_new = ADAM_B1 * m_ref[...] + (1.0 - ADAM_B1) * g
        v_new = ADAM_B2 * v_ref[...] + (1.0 - ADAM_B2) * (g * g)
        m_hat = m_new / (1.0 - ADAM_B1 ** ADAM_STEP)
        v_hat = v_new / (1.0 - ADAM_B2 ** ADAM_STEP)
        g_ref[...] = g
        d_ref[...] = -ADAM_LR * (m_hat / (jnp.sqrt(v_hat) + ADAM_EPS) + ADAM_WD * w_ref[...])
        nm_ref[...] = m_new
        nv_ref[...] = v_new

    spec = pl.BlockSpec((tr, n), lambda i: (i, 0))
    out = jax.ShapeDtypeStruct((r, n), F32)
    return pl.pallas_call(
        body,
        name=name,
        out_shape=(out, out, out, out),
        grid=(r // tr,),
        in_specs=[pl.BlockSpec((k, tr, n), lambda i: (0, i + first, 0)), spec, spec, spec],
        out_specs=(spec, spec, spec, spec),
        compiler_params=_cparams(("parallel",)),
    )(parts, w, m, v)


def _adamw_layers(parts, w, m, v, name, at=0):
    depth = len(parts)
    k = parts[0].shape[0]
    w3, m3, v3 = (a.reshape(depth, -1, a.shape[-1]) for a in (w, m, v))
    _, r, n = w3.shape
    tr = _tile_rows(r, n * depth)
    first = at // tr

    def body(*refs):
        p_refs = refs[:depth]
        w_ref, m_ref, v_ref, g_ref, d_ref, nm_ref, nv_ref = refs[depth:]
        for l in range(depth):
            @pl.when(pl.program_id(0) == l)
            def _(p_ref=p_refs[l]):
                g = p_ref[0].astype(F32)
                for i in range(1, k):
                    g = g + p_ref[i].astype(F32)
                m_new = ADAM_B1 * m_ref[...] + (1.0 - ADAM_B1) * g
                v_new = ADAM_B2 * v_ref[...] + (1.0 - ADAM_B2) * (g * g)
                m_hat = m_new / (1.0 - ADAM_B1 ** ADAM_STEP)
                v_hat = v_new / (1.0 - ADAM_B2 ** ADAM_STEP)
                g_ref[...] = g
                d_ref[...] = -ADAM_LR * (m_hat / (jnp.sqrt(v_hat) + ADAM_EPS) + ADAM_WD * w_ref[...])
                nm_ref[...] = m_new
                nv_ref[...] = v_new

    part_spec = lambda l: pl.BlockSpec((k, tr, n), lambda j, i: (0, jnp.where(j == l, i + first, first), 0))
    spec = pl.BlockSpec((None, tr, n), lambda j, i: (j, i, 0))
    out = jax.ShapeDtypeStruct(w3.shape, F32)
    res = pl.pallas_call(
        body,
        name=name,
        out_shape=(out, out, out, out),
        grid=(depth, r // tr),
        in_specs=[part_spec(l) for l in range(depth)] + [spec, spec, spec],
        out_specs=(spec, spec, spec, spec),
        compiler_params=_cparams(("parallel", "parallel")),
    )(*parts, w3, m3, v3)
    return [a.reshape(w.shape) for a in res]


TILE_BYTES = 2 * 1024 * 1024
SUBLANES_16BIT = 16


def _tile_rows(r, n):
    best = None
    for t in range(SUBLANES_16BIT, r, SUBLANES_16BIT):
        if r % t == 0 and t * n * 4 <= TILE_BYTES:
            best = t
    return best or r


def _rows(a):
    return a.reshape(-1, a.shape[-1])


def _gather_rows(shard, name):
    return _all_gather(_rows(shard), name).reshape(N_DEV, *shard.shape)


SMALL = ("ffn1_norm", "mix_norm", "gm_v_norm", "gm_b_s", "mla_q_norm", "mla_kv_norm", "mla_q_gain", "mla_k_gain",
         "ssd_conv_b", "ssd_dt_bias", "ssd_a_log", "ssd_d", "ssd_norm", "ffn2_norm")
SMALL_ROWS = 8
SMALL_COLS = 7040


def _side_by_side(d):
    cols = jnp.concatenate([d[n].reshape(d[n].shape[0], -1) for n in SMALL], axis=1)
    return jnp.pad(cols, ((0, SMALL_ROWS - cols.shape[0]), (0, SMALL_COLS - cols.shape[1])))


def _apart(packed, like):
    out, off = {}, 0
    for n in SMALL:
        size = like[n][0].size
        out[n] = packed[:like[n].shape[0], off:off + size].reshape(like[n].shape)
        off += size
    return out


def kernel(x, positions, ffn1_norm, ffn1_w_in, ffn1_w_out, mix_norm, w_in, gm_v_norm, gm_w_s, gm_b_s, mla_q_norm, mla_kv_norm, mla_w_uq, mla_w_ukv, mla_q_gain, mla_k_gain, ssd_conv_w, ssd_conv_b, ssd_dt_bias, ssd_a_log, ssd_d, ssd_norm, w_branch, w_out, ffn2_norm, ffn2_w_in, ffn2_w_out, loss_target, m_ffn1_norm, m_ffn1_w_in, m_ffn1_w_out, m_mix_norm, m_w_in, m_gm_v_norm, m_gm_w_s, m_gm_b_s, m_mla_q_norm, m_mla_kv_norm, m_mla_w_uq, m_mla_w_ukv, m_mla_q_gain, m_mla_k_gain, m_ssd_conv_w, m_ssd_conv_b, m_ssd_dt_bias, m_ssd_a_log, m_ssd_d, m_ssd_norm, m_w_branch, m_w_out, m_ffn2_norm, m_ffn2_w_in, m_ffn2_w_out, v_ffn1_norm, v_ffn1_w_in, v_ffn1_w_out, v_mix_norm, v_w_in, v_gm_v_norm, v_gm_w_s, v_gm_b_s, v_mla_q_norm, v_mla_kv_norm, v_mla_w_uq, v_mla_w_ukv, v_mla_q_gain, v_mla_k_gain, v_ssd_conv_w, v_ssd_conv_b, v_ssd_dt_bias, v_ssd_a_log, v_ssd_d, v_ssd_norm, v_w_branch, v_w_out, v_ffn2_norm, v_ffn2_w_in, v_ffn2_w_out):
    given = dict(locals())
    w = {n: given[n] for n in WEIGHTS}
    mom = {n: given["m_" + n] for n in WEIGHTS}
    var = {n: given["v_" + n] for n in WEIGHTS}
    groups = {"ffn_w_in": ("ffn1_w_in", "ffn2_w_in"), "ffn_w_out": ("ffn1_w_out", "ffn2_w_out"), "w_in": ("w_in",),
              "mla_w_uq": ("mla_w_uq",), "mla_w_ukv": ("mla_w_ukv",), "ssd_conv_w": ("ssd_conv_w",),
              "w_branch": ("w_branch",), "w_out": ("w_out",)}

    shards = [_layer_shards(w, l) for l in range(w_out.shape[0])]
    loss, dx, grads = _local_step(x[0], positions[0], loss_target[0], {n: w[n] for n in VECTORS}, shards=shards)

    outs = [{}, {}, {}, {}]
    for i, names in enumerate(groups.values()):
        parts = [layer[i] for layer in grads["reduced"]]
        at = 0
        for n in names:
            res = _adamw_layers(parts, w[n], mom[n], var[n], "adamw_" + n, at=at)
            at += w[n][0].size // w[n].shape[-1]
            for o, r in zip(outs, res):
                o[n] = r

    small_parts = _gather_rows(_side_by_side(grads), "small_grads_all_gather")
    small = _adamw(small_parts, _side_by_side(w), _side_by_side(mom), _side_by_side(var), "adamw_small", SMALL_ROWS)
    ws_parts = _gather_rows(_rows(grads["gm_w_s"]), "gm_w_s_grads_all_gather")
    ws = _adamw(ws_parts, _rows(w["gm_w_s"]), _rows(mom["gm_w_s"]), _rows(var["gm_w_s"]), "adamw_gm_w_s",
                _tile_rows(ws_parts.shape[1], LANES))
    for o, sm, r in zip(outs, small, ws):
        o.update(_apart(sm, w))
        o["gm_w_s"] = r.reshape(w["gm_w_s"].shape)

    loss = lax.psum(loss, ("x", "y", "c"))
    return (loss, dx[None], *[o[n] for o in outs for n in WEIGHTS])
```

```python
import functools

import jax
import jax.numpy as jnp
import numpy as np
from jax import lax
from jax.experimental import pallas as pl
from jax.experimental.pallas import tpu as pltpu

F32 = jnp.float32
BF16 = jnp.bfloat16

D_MODEL = 1024
DEPTH = 4
D_FF = 2816
FFN_RESID = 0.5
EPS = 1e-6
GM_WIDTH = 512
GM_GROUPS = 4
GM_CHUNK = 128
MLA_HEADS = 8
MLA_Q_RANK = 384
MLA_KV_RANK = 256
MLA_NOPE = 64
MLA_ROPE = 32
MLA_QK_DIM = 96
MLA_V = 64
ROPE_THETA = 10000.0
SSD_HEADS = 8
SSD_HEAD_DIM = 64
SSD_INNER = 512
SSD_GROUPS = 2
SSD_STATE = 128
SSD_CONV = 4
SSD_CHUNK = 128
SSD_CONV_DIM = 1024
N_BRANCH = 3
IN_WIDTHS = (1024, 384, 256, 32, 512, 1024, 8, 3072)
IN_OFFSETS = (0, 1024, 1408, 1664, 1696, 2208, 3232, 3240)
IN_COLS = 6312
LANES = 128
N_DEV = 8

ADAM_LR = 0.001
ADAM_B1 = 0.9
ADAM_B2 = 0.999
ADAM_EPS = 1e-08
ADAM_WD = 0.01
ADAM_STEP = 10

VMEM_LIMIT = 56 * 1024 * 1024

P_UV, P_XBC, P_Z, P_KR, P_CQ, P_GATES, P_CKV, P_DT = 0, 1024, 2048, 2560, 2688, 3072, 6144, 6400
P_COLS = 6528


def _cparams(sem):
    return pltpu.CompilerParams(dimension_semantics=sem, vmem_limit_bytes=VMEM_LIMIT)


def _bdot(a, b, dims):
    return lax.dot_general(a.astype(BF16), b.astype(BF16), (dims, ((), ())), preferred_element_type=F32)


@jax.custom_vjp
def _nn(a, b):
    return _bdot(a, b, ((1,), (0,)))


@jax.custom_vjp
def _nt(a, b):
    return _bdot(a, b, ((1,), (1,)))


@jax.custom_vjp
def _tn(a, b):
    return _bdot(a, b, ((0,), (0,)))


def _dot_fwd(dims):
    return lambda a, b: (_bdot(a, b, dims), (a, b))


_nn.defvjp(_dot_fwd(((1,), (0,))), lambda r, g: (_nt(g, r[1]).astype(r[0].dtype), _tn(r[0], g).astype(r[1].dtype)))
_nt.defvjp(_dot_fwd(((1,), (1,))), lambda r, g: (_nn(g, r[1]).astype(r[0].dtype), _tn(g, r[0]).astype(r[1].dtype)))
_tn.defvjp(_dot_fwd(((0,), (0,))), lambda r, g: (_nt(r[1], g).astype(r[0].dtype), _nn(r[0], g).astype(r[1].dtype)))


def _exact_nn(a, b):
    return lax.dot_general(a, b, (((1,), (0,)), ((), ())), precision=lax.Precision.HIGHEST, preferred_element_type=F32)


def _sigmoid(x):
    return 1.0 / (1.0 + jnp.exp(-x))


def _silu(x):
    return x * _sigmoid(x)


def _softplus(x):
    return jnp.maximum(x, 0.0) + jnp.log(1.0 + jnp.exp(-jnp.abs(x)))


def _gelu(x):
    return 0.5 * x * (1.0 + lax.erf(x * 0.7071067811865476))


def _pick(n, cands):
    for c in cands:
        if n % c == 0:
            return c
    return n


def _matmul(a, b, mode, out_dtype, name, alpha=1.0, tm=None, tn=None, tk=None):
    if mode == "nn":
        (m, k), (_, n) = a.shape, b.shape
    elif mode == "nt":
        (m, k), (n, _) = a.shape, b.shape
    else:
        (k, m), (_, n) = a.shape, b.shape
    tm = tm or _pick(m, (512, 384, 256, 128))
    tn = tn or _pick(n, (1024, 768, 512, 384, 256, 128))
    tk = tk or _pick(k, (1024, 512, 256, 128))
    nk = k // tk
    if mode == "nn":
        a_spec = pl.BlockSpec((tm, tk), lambda i, j, kk: (i, kk))
        b_spec = pl.BlockSpec((tk, tn), lambda i, j, kk: (kk, j))
        dot = _nn
    elif mode == "nt":
        a_spec = pl.BlockSpec((tm, tk), lambda i, j, kk: (i, kk))
        b_spec = pl.BlockSpec((tn, tk), lambda i, j, kk: (j, kk))
        dot = _nt
    else:
        a_spec = pl.BlockSpec((tk, tm), lambda i, j, kk: (kk, i))
        b_spec = pl.BlockSpec((tk, tn), lambda i, j, kk: (kk, j))
        dot = _tn

    def body(a_ref, b_ref, o_ref, acc_ref):
        kk = pl.program_id(2)

        @pl.when(kk == 0)
        def _():
            acc_ref[...] = jnp.zeros_like(acc_ref)

        acc_ref[...] += dot(a_ref[...], b_ref[...])

        @pl.when(kk == nk - 1)
        def _():
            o_ref[...] = (alpha * acc_ref[...]).astype(out_dtype)

    return pl.pallas_call(
        body,
        name=name,
        out_shape=jax.ShapeDtypeStruct((m, n), out_dtype),
        grid=(m // tm, n // tn, nk),
        in_specs=[a_spec, b_spec],
        out_specs=pl.BlockSpec((tm, tn), lambda i, j, kk: (i, j)),
        scratch_shapes=[pltpu.VMEM((tm, tn), F32)],
        compiler_params=_cparams(("parallel", "parallel", "arbitrary")),
    )(a, b)


def _rms_stats(x):
    r = lax.rsqrt(jnp.mean(x * x, axis=-1, keepdims=True) + EPS)
    return x * r, r


def _rms_bwd(xhat, r, gain, dy):
    dxhat = dy * gain
    return r * (dxhat - xhat * jnp.mean(dxhat * xhat, axis=-1, keepdims=True))


def _acc_rows(ref, val, first):
    s = jnp.sum(val, axis=0, keepdims=True)

    @pl.when(first)
    def _():
        ref[...] = s

    @pl.when(jnp.logical_not(first))
    def _():
        ref[...] += s


def _rmsnorm_fwd(x, gain, name, tm=512):
    s, d = x.shape

    def body(x_ref, g_ref, h_ref):
        xhat, _ = _rms_stats(x_ref[...])
        h_ref[...] = (xhat * g_ref[...]).astype(BF16)

    return pl.pallas_call(
        body,
        name=name,
        out_shape=jax.ShapeDtypeStruct((s, d), BF16),
        grid=(s // tm,),
        in_specs=[pl.BlockSpec((tm, d), lambda i: (i, 0)), pl.BlockSpec((1, d), lambda i: (0, 0))],
        out_specs=pl.BlockSpec((tm, d), lambda i: (i, 0)),
        compiler_params=_cparams(("parallel",)),
    )(x, gain)


def _rmsnorm_bwd(x, gain, dh, dres, name, tm=512):
    s, d = x.shape

    def body(x_ref, g_ref, dh_ref, dres_ref, dx_ref, dg_ref):
        xhat, r = _rms_stats(x_ref[...])
        dh = dh_ref[...].astype(F32)
        dx_ref[...] = dres_ref[...] + _rms_bwd(xhat, r, g_ref[...], dh)
        _acc_rows(dg_ref, dh * xhat, pl.program_id(0) == 0)

    return pl.pallas_call(
        body,
        name=name,
        out_shape=(jax.ShapeDtypeStruct((s, d), F32), jax.ShapeDtypeStruct((1, d), F32)),
        grid=(s // tm,),
        in_specs=[
            pl.BlockSpec((tm, d), lambda i: (i, 0)),
            pl.BlockSpec((1, d), lambda i: (0, 0)),
            pl.BlockSpec((tm, d), lambda i: (i, 0)),
            pl.BlockSpec((tm, d), lambda i: (i, 0)),
        ],
        out_specs=(pl.BlockSpec((tm, d), lambda i: (i, 0)), pl.BlockSpec((1, d), lambda i: (0, 0))),
        compiler_params=_cparams(("arbitrary",)),
    )(x, gain, dh, dres)


FF_BLOCK = 2 * D_FF // N_DEV
FF_BLOCKS = D_FF // FF_BLOCK
FF_ROWS = D_FF // N_DEV


def _ffn_weight_specs(layer):
    return [
        pl.BlockSpec((None, None, D_MODEL, FF_BLOCK), lambda i, j: (j, layer, 0, 0)),
        pl.BlockSpec((None, None, D_MODEL, FF_BLOCK), lambda i, j: (j + FF_BLOCKS, layer, 0, 0)),
        pl.BlockSpec((2, None, FF_ROWS, D_MODEL), lambda i, j: (j, layer, 0, 0)),
    ]


def _ffn_fwd(x, gain, w_in, w_out, layer, name, tm=512):
    s, d = x.shape

    def body(x_ref, gain_ref, wg_ref, wu_ref, wo_ref, y_ref, gu_ref, h_scr, acc_scr):
        j = pl.program_id(1)

        @pl.when(j == 0)
        def _():
            xhat, _ = _rms_stats(x_ref[...])
            h_scr[...] = (xhat * gain_ref[...]).astype(BF16)
            acc_scr[...] = jnp.zeros_like(acc_scr)

        h = h_scr[...]
        g = _nn(h, wg_ref[...])
        u = _nn(h, wu_ref[...])
        gu_ref[0] = g.astype(BF16)
        gu_ref[1] = u.astype(BF16)
        acc_scr[...] += _nn(_silu(g) * u, wo_ref[...].reshape(FF_BLOCK, d))

        @pl.when(j == FF_BLOCKS - 1)
        def _():
            y_ref[...] = x_ref[...] + FFN_RESID * acc_scr[...]

    return pl.pallas_call(
        body,
        name=name,
        out_shape=(
            jax.ShapeDtypeStruct((s, d), F32),
            jax.ShapeDtypeStruct((2, FF_BLOCKS, s, FF_BLOCK), BF16),
        ),
        grid=(s // tm, FF_BLOCKS),
        in_specs=[
            pl.BlockSpec((tm, d), lambda i, j: (i, 0)),
            pl.BlockSpec((1, d), lambda i, j: (0, 0)),
        ] + _ffn_weight_specs(layer),
        out_specs=(
            pl.BlockSpec((tm, d), lambda i, j: (i, 0)),
            pl.BlockSpec((2, None, tm, FF_BLOCK), lambda i, j: (0, j, i, 0)),
        ),
        scratch_shapes=[pltpu.VMEM((tm, d), BF16), pltpu.VMEM((tm, d), F32)],
        compiler_params=_cparams(("parallel", "arbitrary")),
    )(x, gain, w_in, w_in, w_out)


def _ffn_bwd(x, gain, w_in, w_out, layer, gu, dy, name, tm=512, tk=1024, pair=()):
    s, d = x.shape
    tk = min(tk, s)
    npair = len(pair)

    def body(x_ref, gain_ref, wg_ref, wu_ref, wo_ref, gu_ref, dy_ref,
             dx_ref, dgain_ref, h_ref, a_ref, dgu_ref, dyb_scr, acc_scr):
        i = pl.program_id(0)
        j = pl.program_id(1)

        @pl.when(j == 0)
        def _():
            xhat, _ = _rms_stats(x_ref[...])
            h_ref[...] = (xhat * gain_ref[...]).astype(BF16)
            dyb_scr[...] = (FFN_RESID * dy_ref[...]).astype(BF16)
            acc_scr[...] = jnp.zeros_like(acc_scr)

        da = _nt(dyb_scr[...], wo_ref[...].reshape(FF_BLOCK, d))
        gv = gu_ref[0].astype(F32)
        uv = gu_ref[1].astype(F32)
        sg = _sigmoid(gv)
        sl = gv * sg
        a_ref[...] = (sl * uv).astype(BF16)
        du = (da * sl).astype(BF16)
        dg = (da * uv * (sg * (1.0 + gv * (1.0 - sg)))).astype(BF16)
        dgu_ref[0] = dg
        dgu_ref[1] = du
        acc_scr[...] += _nt(dg, wg_ref[...]) + _nt(du, wu_ref[...])

        @pl.when(j == FF_BLOCKS - 1)
        def _():
            xhat, r = _rms_stats(x_ref[...])
            dh = acc_scr[...]
            dx_ref[...] = dy_ref[...] + _rms_bwd(xhat, r, gain_ref[...], dh)
            _acc_rows(dgain_ref, dh * xhat, i == 0)

    gu_spec = pl.BlockSpec((2, None, tm, FF_BLOCK), lambda i, j: (0, j, i, 0))
    dx, dgain, h, a, dgu, *theirs = pl.pallas_call(
        _host_exchange(body, 7, 5, pair, (s // tm - 1, FF_BLOCKS - 1), pair=True),
        name=name,
        out_shape=(
            jax.ShapeDtypeStruct((s, d), F32),
            jax.ShapeDtypeStruct((1, d), F32),
            jax.ShapeDtypeStruct((s, d), BF16),
            jax.ShapeDtypeStruct((FF_BLOCKS, s, FF_BLOCK), BF16),
            jax.ShapeDtypeStruct((2, FF_BLOCKS, s, FF_BLOCK), BF16),
        ) + _pair_shapes(pair),
        grid=(s // tm, FF_BLOCKS),
        in_specs=[
            pl.BlockSpec((tm, d), lambda i, j: (i, 0)),
            pl.BlockSpec((1, d), lambda i, j: (0, 0)),
        ] + _ffn_weight_specs(layer) + [gu_spec, pl.BlockSpec((tm, d), lambda i, j: (i, 0))] + [ANY] * npair,
        out_specs=(
            pl.BlockSpec((tm, d), lambda i, j: (i, 0)),
            pl.BlockSpec((1, d), lambda i, j: (0, 0)),
            pl.BlockSpec((tm, d), lambda i, j: (i, 0)),
            pl.BlockSpec((None, tm, FF_BLOCK), lambda i, j: (j, i, 0)),
            gu_spec,
        ) + (ANY,) * npair,
        scratch_shapes=[pltpu.VMEM((tm, d), BF16), pltpu.VMEM((tm, d), F32)] + (_pair_scratch(npair) if npair else []),
        compiler_params=_cparams(("arbitrary", "arbitrary")),
    )(x, gain, w_in, w_in, w_out, gu, dy, *pair)
    nk = s // tk

    def acc_matmul(first, last, acc_ref, o_ref, val, alpha):
        @pl.when(first)
        def _():
            acc_ref[...] = jnp.zeros_like(acc_ref)

        acc_ref[...] += val

        @pl.when(last)
        def _():
            o_ref[...] = (alpha * acc_ref[...]).astype(BF16)

    def dwin_body(h_ref, dgu_ref, o_ref, acc_ref):
        kk = pl.program_id(2)
        acc_matmul(kk == 0, kk == nk - 1, acc_ref, o_ref, _tn(h_ref[...], dgu_ref[...]), 1.0)

    tmw = d
    dw_in = pl.pallas_call(
        dwin_body,
        name=name + "_dwin",
        out_shape=jax.ShapeDtypeStruct((N_DEV, d, FF_BLOCK), BF16),
        grid=(N_DEV, d // tmw, nk),
        in_specs=[
            pl.BlockSpec((tk, tmw), lambda n, i, kk: (kk, i)),
            pl.BlockSpec((None, tk, FF_BLOCK), lambda n, i, kk: (n, kk, 0)),
        ],
        out_specs=pl.BlockSpec((None, tmw, FF_BLOCK), lambda n, i, kk: (n, i, 0)),
        scratch_shapes=[pltpu.VMEM((tmw, FF_BLOCK), F32)],
        compiler_params=_cparams(("parallel", "parallel", "arbitrary")),
    )(h, dgu.reshape(N_DEV, s, FF_BLOCK))

    def dwout_body(a_ref, dy_ref, o_ref, acc_ref):
        kk = pl.program_id(1)
        acc_matmul(kk == 0, kk == nk - 1, acc_ref, o_ref, _tn(a_ref[...], dy_ref[...]), FFN_RESID)

    dw_out = pl.pallas_call(
        dwout_body,
        name=name + "_dwout",
        out_shape=jax.ShapeDtypeStruct((FF_BLOCKS, FF_BLOCK, d), BF16),
        grid=(FF_BLOCKS, nk),
        in_specs=[
            pl.BlockSpec((None, tk, FF_BLOCK), lambda j, kk: (j, kk, 0)),
            pl.BlockSpec((tk, d), lambda j, kk: (kk, 0)),
        ],
        out_specs=pl.BlockSpec((None, FF_BLOCK, d), lambda j, kk: (j, 0, 0)),
        scratch_shapes=[pltpu.VMEM((FF_BLOCK, d), F32)],
        compiler_params=_cparams(("parallel", "arbitrary")),
    )(a, dy)
    return dx, dgain, dw_in, dw_out.reshape(N_DEV, FF_ROWS, d), theirs


def _acc(ref, val, first):
    @pl.when(first)
    def _():
        ref[...] = val

    @pl.when(jnp.logical_not(first))
    def _():
        ref[...] += val


def _full(shape):
    nd = len(shape)
    return pl.BlockSpec(shape, lambda *_: (0,) * nd)


def _iota(shape, dim):
    return lax.broadcasted_iota(jnp.int32, shape, dim)


def _gmlp_chunk(u, v, gain, w_s, b_s):
    va = [_gelu(t) for t in v]
    ms = sum(jnp.sum(t * t, axis=-1, keepdims=True) for t in va) * (1.0 / GM_WIDTH)
    r = lax.rsqrt(ms + EPS)
    tri = _iota((GM_CHUNK, GM_CHUNK), 0) >= _iota((GM_CHUNK, GM_CHUNK), 1)
    out = []
    for g in range(GM_GROUPS):
        vn = va[g] * r * gain[g]
        sp = _nn(jnp.where(tri, w_s[g], 0.0), vn) + b_s[g]
        out.append(_gelu(u[g]) * sp)
    return out


def _gmlp_load(uv_ref, c):
    rows = pl.ds(c * GM_CHUNK, GM_CHUNK)
    u = [uv_ref[rows, pl.ds(g * LANES, LANES)] for g in range(GM_GROUPS)]
    v = [uv_ref[rows, pl.ds(GM_WIDTH + g * LANES, LANES)] for g in range(GM_GROUPS)]
    return u, v


def _gmlp_params(gain_ref, ws_ref, bs_ref):
    gain = [gain_ref[:, pl.ds(g * LANES, LANES)] for g in range(GM_GROUPS)]
    w_s = [ws_ref[g] for g in range(GM_GROUPS)]
    b_s = [bs_ref[g] for g in range(GM_GROUPS)]
    return gain, w_s, b_s


def _gmlp_fwd(proj, gain, w_s, b_s, name, tm=512):
    s = proj.shape[0]

    def body(uv_ref, gain_ref, ws_ref, bs_ref, y_ref):
        params = _gmlp_params(gain_ref, ws_ref, bs_ref)
        for c in range(tm // GM_CHUNK):
            u, v = _gmlp_load(uv_ref, c)
            y = _gmlp_chunk(u, v, *params)
            for g in range(GM_GROUPS):
                y_ref[pl.ds(c * GM_CHUNK, GM_CHUNK), pl.ds(g * LANES, LANES)] = y[g]

    return pl.pallas_call(
        body,
        name=name,
        out_shape=jax.ShapeDtypeStruct((s, GM_WIDTH), F32),
        grid=(s // tm,),
        in_specs=[
            pl.BlockSpec((tm, 2 * GM_WIDTH), lambda i: (i, P_UV // (2 * GM_WIDTH))),
            _full((1, GM_WIDTH)),
            _full((GM_GROUPS, GM_CHUNK, GM_CHUNK)),
            _full((GM_GROUPS, GM_CHUNK, 1)),
        ],
        out_specs=pl.BlockSpec((tm, GM_WIDTH), lambda i: (i, 0)),
        compiler_params=_cparams(("parallel",)),
    )(proj, gain, w_s, b_s)


def _gmlp_bwd(proj, gain, w_s, b_s, dy, name, tm=512):
    s = proj.shape[0]

    def body(uv_ref, gain_ref, ws_ref, bs_ref, dy_ref, duv_ref, dgain_ref, dws_ref, dbs_ref):
        params = _gmlp_params(gain_ref, ws_ref, bs_ref)
        dgain = dws = dbs = None
        for c in range(tm // GM_CHUNK):
            rows = pl.ds(c * GM_CHUNK, GM_CHUNK)
            u, v = _gmlp_load(uv_ref, c)
            _, vjp = jax.vjp(_gmlp_chunk, u, v, *params)
            du, dv, dg, dw, db = vjp([dy_ref[rows, pl.ds(g * LANES, LANES)] for g in range(GM_GROUPS)])
            for g in range(GM_GROUPS):
                duv_ref[rows, pl.ds(g * LANES, LANES)] = du[g].astype(BF16)
                duv_ref[rows, pl.ds(GM_WIDTH + g * LANES, LANES)] = dv[g].astype(BF16)
            if c == 0:
                dgain, dws, dbs = dg, dw, db
            else:
                dgain = [p + q for p, q in zip(dgain, dg)]
                dws = [p + q for p, q in zip(dws, dw)]
                dbs = [p + q for p, q in zip(dbs, db)]
        first = pl.program_id(0) == 0
        for g in range(GM_GROUPS):
            _acc(dgain_ref.at[:, pl.ds(g * LANES, LANES)], dgain[g], first)
            _acc(dws_ref.at[g], dws[g], first)
            _acc(dbs_ref.at[g], dbs[g], first)

    return pl.pallas_call(
        body,
        name=name,
        out_shape=(
            jax.ShapeDtypeStruct((s, 2 * GM_WIDTH), BF16),
            jax.ShapeDtypeStruct((1, GM_WIDTH), F32),
            jax.ShapeDtypeStruct((GM_GROUPS, GM_CHUNK, GM_CHUNK), F32),
            jax.ShapeDtypeStruct((GM_GROUPS, GM_CHUNK, 1), F32),
        ),
        grid=(s // tm,),
        in_specs=[
            pl.BlockSpec((tm, 2 * GM_WIDTH), lambda i: (i, P_UV // (2 * GM_WIDTH))),
            _full((1, GM_WIDTH)),
            _full((GM_GROUPS, GM_CHUNK, GM_CHUNK)),
            _full((GM_GROUPS, GM_CHUNK, 1)),
            pl.BlockSpec((tm, GM_WIDTH), lambda i: (i, 0)),
        ],
        out_specs=(
            pl.BlockSpec((tm, 2 * GM_WIDTH), lambda i: (i, 0)),
            _full((1, GM_WIDTH)),
            _full((GM_GROUPS, GM_CHUNK, GM_CHUNK)),
            _full((GM_GROUPS, GM_CHUNK, 1)),
        ),
        compiler_params=_cparams(("arbitrary",)),
    )(proj, gain, w_s, b_s, dy)


HEAD_BLOCK = MLA_HEADS * LANES


def _mla_heads(rope, q_all, k_all, kr, qg, kg):
    cosf, sinf, rot = rope

    def head_norm(t, gain):
        r = lax.rsqrt(jnp.sum(t * t, axis=-1, keepdims=True) * (1.0 / MLA_QK_DIM) + EPS)
        th = t * r * gain
        return th * cosf + _nn(th, rot) * sinf

    q = [head_norm(t, qg) * ATT_SCALE for t in q_all]
    k = [head_norm(t + kr, kg) for t in k_all]
    return q, k


def _mla_up(refs, w_scr, up_scr):
    cq_ref, ckv_ref, qn_ref, kvn_ref, wq_ref, wk_ref, wv_ref = refs
    wq_scr, wk_scr, wv_scr = w_scr
    qa_scr, ka_scr = up_scr

    @pl.when(pl.program_id(0) == 0)
    def _():
        for h in range(MLA_HEADS):
            cols = pl.ds(h * LANES, LANES)
            wq_scr[:, cols] = wq_ref[h]
            wk_scr[:, cols] = wk_ref[h]
            wv_scr[:, cols] = wv_ref[h]

    xq, rq = _rms_stats(cq_ref[...])
    xk, rk = _rms_stats(ckv_ref[...])
    qn = (xq * qn_ref[...]).astype(BF16)
    kvn = (xk * kvn_ref[...]).astype(BF16)
    qa_scr[...] = _nn(qn, wq_scr[...])
    ka_scr[...] = _nn(kvn, wk_scr[...])
    heads = lambda scr: [scr[:, pl.ds(h * LANES, LANES)] for h in range(MLA_HEADS)]
    return (xq, rq, qn), (xk, rk, kvn), heads(qa_scr), heads(ka_scr)


def _mla_scratch(tm):
    w = lambda rank: pltpu.VMEM((rank, HEAD_BLOCK), BF16)
    up = pltpu.VMEM((tm, HEAD_BLOCK), F32)
    return [w(MLA_Q_RANK), w(MLA_KV_RANK), w(MLA_KV_RANK), up, up]


def _mla_pre_specs(tm):
    row = lambda w, off: pl.BlockSpec((tm, w), lambda i: (i, off // w))
    return [
        row(MLA_Q_RANK, P_CQ),
        row(MLA_KV_RANK, P_CKV),
        row(LANES, P_KR),
        pl.BlockSpec((tm, LANES), lambda i: (i, 0)),
        pl.BlockSpec((tm, LANES), lambda i: (i, 0)),
        _full((LANES, LANES)),
        _full((1, MLA_Q_RANK)),
        _full((1, MLA_KV_RANK)),
        _full((MLA_HEADS, MLA_Q_RANK, LANES)),
        _full((MLA_HEADS, MLA_KV_RANK, LANES)),
        _full((MLA_HEADS, MLA_KV_RANK, LANES)),
        _full((1, LANES)),
        _full((1, LANES)),
    ]


def _mla_pre_fwd(proj, cosf, sinf, rot, qn_g, kvn_g, wq, wk, wv, qg, kg, name, tm=256):
    s = proj.shape[0]

    def body(cq_ref, ckv_ref, kr_ref, cos_ref, sin_ref, rot_ref, qn_ref, kvn_ref, wq_ref, wk_ref, wv_ref, qg_ref, kg_ref,
             q_ref, k_ref, v_ref, *scr):
        _, (_, _, kvn), q_all, k_all = _mla_up((cq_ref, ckv_ref, qn_ref, kvn_ref, wq_ref, wk_ref, wv_ref), scr[:3], scr[3:])
        v_ref[...] = _nn(kvn, scr[2][...]).astype(BF16)
        rope = (cos_ref[...], sin_ref[...], rot_ref[...])
        q, k = _mla_heads(rope, q_all, k_all, kr_ref[...], qg_ref[...], kg_ref[...])
        for h in range(MLA_HEADS):
            cols = pl.ds(h * LANES, LANES)
            q_ref[:, cols] = q[h].astype(BF16)
            k_ref[:, cols] = k[h].astype(BF16)

    out = jax.ShapeDtypeStruct((s, HEAD_BLOCK), BF16)
    blk = pl.BlockSpec((tm, HEAD_BLOCK), lambda i: (i, 0))
    return pl.pallas_call(
        body,
        name=name,
        out_shape=(out, out, out),
        grid=(s // tm,),
        in_specs=_mla_pre_specs(tm),
        out_specs=(blk, blk, blk),
        scratch_shapes=_mla_scratch(tm),
        compiler_params=_cparams(("arbitrary",)),
    )(proj, proj, proj, cosf, sinf, rot, qn_g, kvn_g, wq, wk, wv, qg, kg)


def _mla_pre_bwd(proj, cosf, sinf, rot, qn_g, kvn_g, wq, wk, wv, qg, kg, dq, dk, dv, name, tm=256):
    s = proj.shape[0]

    def body(cq_ref, ckv_ref, kr_ref, cos_ref, sin_ref, rot_ref, qn_ref, kvn_ref, wq_ref, wk_ref, wv_ref, qg_ref, kg_ref,
             dq_ref, dk_ref, dv_ref,
             dcq_ref, dckv_ref, dkr_ref, dqn_ref, dkvn_ref, dwq_ref, dwk_ref, dwv_ref, dqg_ref, dkg_ref, *scr):
        (xq, rq, qn), (xk, rk, kvn), q_all, k_all = _mla_up(
            (cq_ref, ckv_ref, qn_ref, kvn_ref, wq_ref, wk_ref, wv_ref), scr[:3], scr[3:])
        wq_scr, wk_scr, wv_scr, qa_scr, ka_scr = scr
        rope = (cos_ref[...], sin_ref[...], rot_ref[...])
        _, vjp = jax.vjp(functools.partial(_mla_heads, rope), q_all, k_all, kr_ref[...], qg_ref[...], kg_ref[...])
        heads = lambda ref: [ref[:, pl.ds(h * LANES, LANES)] for h in range(MLA_HEADS)]
        dq_all, dk_all, dkr, dqg, dkg = vjp((heads(dq_ref), heads(dk_ref)))
        for h in range(MLA_HEADS):
            cols = pl.ds(h * LANES, LANES)
            qa_scr[:, cols] = dq_all[h]
            ka_scr[:, cols] = dk_all[h]
        dqa, dka, dva = qa_scr[...], ka_scr[...], dv_ref[...]
        dqn = _nt(dqa, wq_scr[...])
        dkvn = _nt(dka, wk_scr[...]) + _nt(dva, wv_scr[...])
        dcq_ref[...] = _rms_bwd(xq, rq, qn_ref[...], dqn).astype(BF16)
        dckv_ref[...] = _rms_bwd(xk, rk, kvn_ref[...], dkvn).astype(BF16)
        dkr_ref[...] = dkr.astype(BF16)
        first = pl.program_id(0) == 0
        _acc_rows(dqn_ref, dqn * xq, first)
        _acc_rows(dkvn_ref, dkvn * xk, first)
        _acc(dqg_ref, dqg, first)
        _acc(dkg_ref, dkg, first)
        dwq, dwk, dwv = _tn(qn, dqa), _tn(kvn, dka), _tn(kvn, dva)
        for h in range(MLA_HEADS):
            cols = slice(h * LANES, (h + 1) * LANES)
            _acc(dwq_ref.at[h], dwq[:, cols], first)
            _acc(dwk_ref.at[h], dwk[:, cols], first)
            _acc(dwv_ref.at[h], dwv[:, cols], first)

    hb = pl.BlockSpec((tm, HEAD_BLOCK), lambda i: (i, 0))
    row = lambda w: pl.BlockSpec((tm, w), lambda i: (i, 0))
    sds = jax.ShapeDtypeStruct
    return pl.pallas_call(
        body,
        name=name,
        out_shape=(
            sds((s, MLA_Q_RANK), BF16), sds((s, MLA_KV_RANK), BF16), sds((s, LANES), BF16),
            sds((1, MLA_Q_RANK), F32), sds((1, MLA_KV_RANK), F32),
            sds((MLA_HEADS, MLA_Q_RANK, LANES), F32), sds((MLA_HEADS, MLA_KV_RANK, LANES), F32),
            sds((MLA_HEADS, MLA_KV_RANK, LANES), F32),
            sds((1, LANES), F32), sds((1, LANES), F32),
        ),
        grid=(s // tm,),
        in_specs=_mla_pre_specs(tm) + [hb, hb, hb],
        out_specs=(
            row(MLA_Q_RANK), row(MLA_KV_RANK), row(LANES),
            _full((1, MLA_Q_RANK)), _full((1, MLA_KV_RANK)),
            _full((MLA_HEADS, MLA_Q_RANK, LANES)), _full((MLA_HEADS, MLA_KV_RANK, LANES)),
            _full((MLA_HEADS, MLA_KV_RANK, LANES)),
            _full((1, LANES)), _full((1, LANES)),
        ),
        scratch_shapes=_mla_scratch(tm),
        compiler_params=_cparams(("arbitrary",)),
    )(proj, proj, proj, cosf, sinf, rot, qn_g, kvn_g, wq, wk, wv, qg, kg, dq, dk, dv)


ATT_SCALE = MLA_QK_DIM ** -0.5
NEG_BIG = -1e30


def _att_scores(q, k, diagonal, q_at=0):
    s = _nt(q, k)
    if diagonal:
        s = jnp.where(_iota(s.shape, 0) + q_at >= _iota(s.shape, 1), s, NEG_BIG)
    return s


def _att_scores_t(k, q, diagonal, q_at=0):
    s = _nt(k, q)
    if diagonal:
        s = jnp.where(_iota(s.shape, 0) <= _iota(s.shape, 1) + q_at, s, NEG_BIG)
    return s


ATT_SPLIT = 1


SUBLANES = 8


def _as_row(col_lanes):
    return jnp.transpose(col_lanes)[0:SUBLANES, :]


def _key_loop(lo, hi, t, step):
    rows = lambda i: pl.ds(pl.multiple_of(i * t, t), t)

    def body(i, carry):
        step(rows(lo + 2 * i))
        step(rows(lo + 2 * i + 1))
        return carry

    count = jnp.asarray(hi - lo, jnp.int32)
    lax.fori_loop(0, lax.div(count, 2), body, 0)

    @pl.when(lax.rem(count, 2) == 1)
    def _():
        step(rows(hi - 1))


def _attention_fwd(q, k, v, name, t=512, gather=()):
    s = q.shape[0]
    n = s // t
    ng = len(gather)

    def body(q_ref, k_ref, v_ref, *rest):
        shard_refs, rest = rest[:ng], rest[ng:]
        o_ref, lse_ref, lse_t_ref = rest[:3]
        out_refs, rest = rest[3:3 + ng], rest[3 + ng:]
        m_scr, acc_scr = rest[:2]
        qi = pl.program_id(1)
        if ng:
            @pl.when(jnp.logical_and(pl.program_id(0) == 0, qi == 0))
            def _():
                _gather_start(_gather_plan(shard_refs, out_refs, *rest[2:]))

        lane = _iota((1, LANES), 1)
        m_scr[...] = jnp.full_like(m_scr, NEG_BIG)
        acc_scr[...] = jnp.zeros_like(acc_scr)

        w = t // ATT_SPLIT

        def step(rows, diagonal=False):
            kb = k_ref[rows, :]
            vb = jnp.where(lane == MLA_V, 1.0, v_ref[rows, :].astype(F32)).astype(BF16)
            for c in range(ATT_SPLIT):
                cols = pl.ds(c * w, w)
                sc = _att_scores_t(kb, q_ref[cols, :], diagonal, c * w)
                m_old = m_scr[:, cols]
                m_new = jnp.maximum(m_old, jnp.max(sc, axis=0, keepdims=True))
                p = jnp.exp(sc - m_new)
                acc_scr[:, cols] = jnp.exp(m_old - m_new) * acc_scr[:, cols] + _tn(vb, p)
                m_scr[:, cols] = m_new

        _key_loop(0, qi, t, step)
        step(pl.ds(pl.multiple_of(qi * t, t), t), diagonal=True)
        acc = acc_scr[...]
        row = _iota((LANES, 1), 0)
        l = jnp.sum(jnp.where(row == MLA_V, acc, 0.0), axis=0, keepdims=True)
        o_ref[...] = jnp.transpose(jnp.where(row < MLA_V, acc / l, 0.0))
        lse = jnp.broadcast_to(m_scr[...] + jnp.log(l), (LANES, t))
        lse_ref[...] = jnp.transpose(lse)
        lse_t_ref[...] = lse[0:SUBLANES, :]
        if ng:
            @pl.when(jnp.logical_and(pl.program_id(0) == MLA_HEADS - 1, qi == n - 1))
            def _():
                _gather_finish(_gather_plan(shard_refs, out_refs, *rest[2:]))

    qspec = pl.BlockSpec((t, LANES), lambda h, qi: (qi, h))
    kspec = pl.BlockSpec((s, LANES), lambda h, qi: (0, h))
    out = jax.ShapeDtypeStruct((s, HEAD_BLOCK), F32)
    res = pl.pallas_call(
        body,
        name=name,
        out_shape=(out, out, jax.ShapeDtypeStruct((MLA_HEADS * SUBLANES, s), F32)) + _gathered_shapes(gather),
        grid=(MLA_HEADS, n),
        in_specs=[qspec, kspec, kspec] + [ANY] * ng,
        out_specs=(qspec, qspec, pl.BlockSpec((SUBLANES, t), lambda h, qi: (h, qi))) + (ANY,) * ng,
        scratch_shapes=[pltpu.VMEM((1, t), F32), pltpu.VMEM((LANES, t), F32)] + (_gather_scratch(ng) if ng else []),
        compiler_params=_cparams(("arbitrary", "arbitrary") if ng else ("parallel", "parallel")),
    )(q, k, v, *gather)
    return res[0], res[1], res[2], list(res[3:])


def _host_exchange(body, n_in, n_out, parts, last, pair=False):
    na = len(parts)
    if not na:
        return body
    plan, start, finish = (_pair_plan, _pair_start, _pair_finish) if pair else (
        _exchange_plan, _exchange_start, _exchange_finish)

    def hosted(*refs):
        ins, part_refs = refs[:n_in], refs[n_in:n_in + na]
        outs = refs[n_in + na:n_in + na + n_out]
        got_refs = refs[n_in + na + n_out:n_in + 2 * na + n_out]
        scratch, sems = refs[n_in + 2 * na + n_out:-3], refs[-3:]
        at = lambda step: jnp.logical_and(pl.program_id(0) == step[0], pl.program_id(1) == step[1])

        @pl.when(at((0, 0)))
        def _():
            start(plan(part_refs, got_refs, *sems))

        body(*ins, *outs, *scratch)

        @pl.when(at(last))
        def _():
            finish(plan(part_refs, got_refs, *sems))

    return hosted


def _attention_bwd(q, k, v, o, lse, lse_t, do, name, t=512, exchange_dq=(), exchange_dkv=()):
    s = q.shape[0]
    n = s // t
    last = (MLA_HEADS - 1, n - 1)
    sem = lambda parts: ("arbitrary", "arbitrary") if parts else ("parallel", "parallel")

    def dq_body(q_ref, k_ref, v_ref, o_ref, lse_ref, do_ref, dq_ref, delta_t_ref, acc_scr):
        qi = pl.program_id(1)
        do = do_ref[...]
        delta = jnp.sum(do * o_ref[...], axis=-1, keepdims=True)
        delta_t_ref[...] = _as_row(jnp.broadcast_to(delta, (t, LANES)))
        acc_scr[...] = jnp.zeros_like(acc_scr)

        w = t // ATT_SPLIT

        def step(rows, diagonal=False):
            kb, vb = k_ref[rows, :], v_ref[rows, :]
            for c in range(ATT_SPLIT):
                part = pl.ds(c * w, w)
                p = jnp.exp(_att_scores(q_ref[part, :], kb, diagonal, c * w) - lse_ref[part, 0:1])
                ds = p * (_nt(do_ref[part, :], vb) - delta[c * w:(c + 1) * w])
                acc_scr[part, :] += _nn(ds, kb)

        _key_loop(0, qi, t, step)
        step(pl.ds(pl.multiple_of(qi * t, t), t), diagonal=True)
        dq_ref[...] = acc_scr[...]

    def dkv_body(q_ref, k_ref, v_ref, lse_t_ref, delta_t_ref, do_ref, dk_ref, dv_ref, dk_scr, dv_scr):
        ki = pl.program_id(1)
        dk_scr[...] = jnp.zeros_like(dk_scr)
        dv_scr[...] = jnp.zeros_like(dv_scr)

        w = t // ATT_SPLIT

        def step(rows, diagonal=False):
            dv, dk = dv_scr[...], dk_scr[...]
            for c in range(ATT_SPLIT):
                part = pl.ds(pl.multiple_of(rows.start + c * w, w), w)
                qb = q_ref[part, :]
                dob = do_ref[part, :]
                p = jnp.exp(_att_scores_t(k_ref[...], qb, diagonal, c * w) - lse_t_ref[0:1, part])
                dv = dv + _nn(p, dob)
                ds = p * (_nt(v_ref[...], dob) - delta_t_ref[0:1, part])
                dk = dk + _nn(ds, qb)
            dv_scr[...] = dv
            dk_scr[...] = dk

        step(pl.ds(pl.multiple_of(ki * t, t), t), diagonal=True)
        _key_loop(ki + 1, n, t, step)
        dk_ref[...] = dk_scr[...]
        dv_ref[...] = dv_scr[...]

    out = jax.ShapeDtypeStruct((s, HEAD_BLOCK), F32)
    blk = pl.BlockSpec((t, LANES), lambda h, i: (i, h))
    head = pl.BlockSpec((s, LANES), lambda h, i: (0, h))
    row_blk = pl.BlockSpec((SUBLANES, t), lambda h, i: (h, i))
    row_head = pl.BlockSpec((SUBLANES, s), lambda h, i: (h, 0))
    na, nb = len(exchange_dq), len(exchange_dkv)
    dq, delta_t, *got_dq = pl.pallas_call(
        _host_exchange(dq_body, 6, 2, exchange_dq, last),
        name=name + "_dq",
        out_shape=(out, jax.ShapeDtypeStruct((MLA_HEADS * SUBLANES, s), F32)) + _same_shapes(exchange_dq),
        grid=(MLA_HEADS, n),
        in_specs=[blk, head, head, blk, blk, blk] + [ANY] * na,
        out_specs=(blk, row_blk) + (ANY,) * na,
        scratch_shapes=[pltpu.VMEM((t, LANES), F32)] + (_exchange_scratch(na) if na else []),
        compiler_params=_cparams(sem(exchange_dq)),
    )(q, k, v, o, lse, do, *exchange_dq)
    dk, dv, *got_dkv = pl.pallas_call(
        _host_exchange(dkv_body, 6, 2, exchange_dkv, last),
        name=name + "_dkv",
        out_shape=(out, out) + _same_shapes(exchange_dkv),
        grid=(MLA_HEADS, n),
        in_specs=[head, blk, blk, row_head, row_head, head] + [ANY] * nb,
        out_specs=(blk, blk) + (ANY,) * nb,
        scratch_shapes=[pltpu.VMEM((t, LANES), F32), pltpu.VMEM((t, LANES), F32)]
        + (_exchange_scratch(nb) if nb else []),
        compiler_params=_cparams(sem(exchange_dkv)),
    )(q, k, v, lse_t, delta_t, do, *exchange_dkv)
    return dq, dk, dv, got_dq, got_dkv


HALO = 8


def _conv_fwd(proj, w, b, name, tm=512):
    s = proj.shape[0]
    cb = P_XBC // SSD_CONV_DIM

    def body(x_ref, halo_ref, w_ref, b_ref, y_ref, cat_scr):
        i = pl.program_id(0)
        cat_scr[pl.ds(0, HALO), :] = jnp.where(i > 0, halo_ref[...], 0.0)
        cat_scr[pl.ds(HALO, tm), :] = x_ref[...]
        pre = b_ref[...]
        for j in range(SSD_CONV):
            pre = pre + w_ref[pl.ds(SSD_CONV - 1 - j, 1), :] * cat_scr[pl.ds(HALO - j, tm), :]
        y_ref[...] = _silu(pre)

    return pl.pallas_call(
        body,
        name=name,
        out_shape=jax.ShapeDtypeStruct((s, SSD_CONV_DIM), F32),
        grid=(s // tm,),
        in_specs=[
            pl.BlockSpec((tm, SSD_CONV_DIM), lambda i: (i, cb)),
            pl.BlockSpec((HALO, SSD_CONV_DIM), lambda i: (jnp.maximum(i * (tm // HALO) - 1, 0), cb)),
            _full((SSD_CONV, SSD_CONV_DIM)),
            _full((1, SSD_CONV_DIM)),
        ],
        out_specs=pl.BlockSpec((tm, SSD_CONV_DIM), lambda i: (i, 0)),
        scratch_shapes=[pltpu.VMEM((tm + HALO, SSD_CONV_DIM), F32)],
        compiler_params=_cparams(("parallel",)),
    )(proj, proj, w, b)


def _conv_bwd(proj, w, b, dact, name, tm=512):
    s = proj.shape[0]
    cb = P_XBC // SSD_CONV_DIM
    n = s // tm

    def pre_body(x_ref, halo_ref, w_ref, b_ref, dact_ref, dpre_ref, dw_ref, db_ref, cat_scr):
        i = pl.program_id(0)
        cat_scr[pl.ds(0, HALO), :] = jnp.where(i > 0, halo_ref[...], 0.0)
        cat_scr[pl.ds(HALO, tm), :] = x_ref[...]
        pre = b_ref[...]
        for j in range(SSD_CONV):
            pre = pre + w_ref[pl.ds(SSD_CONV - 1 - j, 1), :] * cat_scr[pl.ds(HALO - j, tm), :]
        sg = _sigmoid(pre)
        dpre = dact_ref[...] * (sg * (1.0 + pre * (1.0 - sg)))
        dpre_ref[...] = dpre
        first = i == 0
        _acc_rows(db_ref, dpre, first)
        for j in range(SSD_CONV):
            _acc_rows(dw_ref.at[pl.ds(SSD_CONV - 1 - j, 1), :], dpre * cat_scr[pl.ds(HALO - j, tm), :], first)

    dpre, dw, db = pl.pallas_call(
        pre_body,
        name=name + "_pre",
        out_shape=(
            jax.ShapeDtypeStruct((s, SSD_CONV_DIM), F32),
            jax.ShapeDtypeStruct((SSD_CONV, SSD_CONV_DIM), F32),
            jax.ShapeDtypeStruct((1, SSD_CONV_DIM), F32),
        ),
        grid=(n,),
        in_specs=[
            pl.BlockSpec((tm, SSD_CONV_DIM), lambda i: (i, cb)),
            pl.BlockSpec((HALO, SSD_CONV_DIM), lambda i: (jnp.maximum(i * (tm // HALO) - 1, 0), cb)),
            _full((SSD_CONV, SSD_CONV_DIM)),
            _full((1, SSD_CONV_DIM)),
            pl.BlockSpec((tm, SSD_CONV_DIM), lambda i: (i, 0)),
        ],
        out_specs=(
            pl.BlockSpec((tm, SSD_CONV_DIM), lambda i: (i, 0)),
            _full((SSD_CONV, SSD_CONV_DIM)),
            _full((1, SSD_CONV_DIM)),
        ),
        scratch_shapes=[pltpu.VMEM((tm + HALO, SSD_CONV_DIM), F32)],
        compiler_params=_cparams(("arbitrary",)),
    )(proj, proj, w, b, dact)

    def dx_body(d_ref, halo_ref, w_ref, dx_ref, cat_scr):
        i = pl.program_id(0)
        cat_scr[pl.ds(0, tm), :] = d_ref[...]
        cat_scr[pl.ds(tm, HALO), :] = jnp.where(i < n - 1, halo_ref[...], 0.0)
        dx = jnp.zeros((tm, SSD_CONV_DIM), F32)
        for j in range(SSD_CONV):
            dx = dx + w_ref[pl.ds(SSD_CONV - 1 - j, 1), :] * cat_scr[pl.ds(j, tm), :]
        dx_ref[...] = dx.astype(BF16)

    dx = pl.pallas_call(
        dx_body,
        name=name + "_dx",
        out_shape=jax.ShapeDtypeStruct((s, SSD_CONV_DIM), BF16),
        grid=(n,),
        in_specs=[
            pl.BlockSpec((tm, SSD_CONV_DIM), lambda i: (i, 0)),
            pl.BlockSpec((HALO, SSD_CONV_DIM), lambda i: (jnp.minimum((i + 1) * (tm // HALO), s // HALO - 1), 0)),
            _full((SSD_CONV, SSD_CONV_DIM)),
        ],
        out_specs=pl.BlockSpec((tm, SSD_CONV_DIM), lambda i: (i, 0)),
        scratch_shapes=[pltpu.VMEM((tm + HALO, SSD_CONV_DIM), F32)],
        compiler_params=_cparams(("parallel",)),
    )(dpre, dpre, w)
    return dx, dw, db


N_PAIR = SSD_HEADS // 2


def _ssd_chunk(xs, bm, cm, z, dtp, state, dtb, alog, dskip, ng):
    t = SSD_CHUNK
    lane = _iota((1, LANES), 1)
    row = _iota((LANES, 1), 0)
    dt_all = jnp.where(lane < SSD_HEADS, _softplus(dtp + dtb), 0.0)
    da = dt_all * (-jnp.exp(alog))
    causal = _iota((t, t), 0) >= _iota((t, t), 1)
    cs = _exact_nn(causal.astype(F32), da)
    cs_t = cs.T
    tot = jnp.sum(da, axis=0, keepdims=True)
    col = lambda m, h: jnp.sum(jnp.where(lane == h, m, 0.0), axis=1, keepdims=True)
    rowv = lambda m, h: jnp.sum(jnp.where(row == h, m, 0.0), axis=0, keepdims=True)
    low = lane < SSD_HEAD_DIM
    cb = [_nt(cm[g], bm[g]) for g in range(SSD_GROUPS)]
    gated, new_state = [], []
    for j in range(N_PAIR):
        g = j // (N_PAIR // SSD_GROUPS)
        h0, h1 = 2 * j, 2 * j + 1
        y = jnp.zeros((t, LANES), F32)
        for h, mask in ((h0, low), (h1, jnp.logical_not(low))):
            lmat = jnp.exp(jnp.where(causal, col(cs, h) - rowv(cs_t, h), NEG_BIG))
            y = y + _nn(cb[g] * lmat, jnp.where(mask, xs[j] * col(dt_all, h), 0.0))
        cs_p = jnp.where(low, col(cs, h0), col(cs, h1))
        dt_p = jnp.where(low, col(dt_all, h0), col(dt_all, h1))
        tot_p = jnp.where(low, col(tot, h0), col(tot, h1))
        tot_c = jnp.where(row < SSD_HEAD_DIM, col(tot, h0), col(tot, h1))
        d_p = jnp.where(low, col(dskip, h0), col(dskip, h1))
        xdt = xs[j] * dt_p
        y = y + _nt(cm[g], state[j]) * jnp.exp(cs_p) + xs[j] * d_p
        new_state.append(state[j] * jnp.exp(tot_c) + _tn(xdt * jnp.exp(tot_p - cs_p), bm[g]))
        gated.append(y * _silu(z[j]))
    out = []
    per_group = N_PAIR // SSD_GROUPS
    for g in range(SSD_GROUPS):
        blocks = gated[g * per_group:(g + 1) * per_group]
        ms = sum(jnp.sum(v * v, axis=-1, keepdims=True) for v in blocks) * (1.0 / (per_group * LANES))
        r = lax.rsqrt(ms + EPS)
        out += [v * r * ng[g * per_group + i] for i, v in enumerate(blocks)]
    return out, new_state


def _ssd_specs(rev, nc):
    idx = (lambda c: nc - 1 - c) if rev else (lambda c: c)
    t = SSD_CHUNK
    return [
        pl.BlockSpec((t, SSD_CONV_DIM), lambda c: (idx(c), 0)),
        pl.BlockSpec((t, SSD_INNER), lambda c: (idx(c), P_Z // SSD_INNER)),
        pl.BlockSpec((t, LANES), lambda c: (idx(c), P_DT // LANES)),
        _full((1, LANES)), _full((1, LANES)), _full((1, LANES)), _full((1, SSD_INNER)),
    ]


def _ssd_args(act_ref, z_ref, dt_ref, dtb_ref, alog_ref, dskip_ref, ng_ref):
    blk = lambda ref, off, n: [ref[:, pl.ds(off + i * LANES, LANES)] for i in range(n)]
    xs = blk(act_ref, 0, N_PAIR)
    bm = blk(act_ref, SSD_INNER, SSD_GROUPS)
    cm = blk(act_ref, SSD_INNER + SSD_GROUPS * SSD_STATE, SSD_GROUPS)
    return xs, bm, cm, blk(z_ref, 0, N_PAIR), dt_ref[...], dtb_ref[...], alog_ref[...], dskip_ref[...], blk(ng_ref, 0, N_PAIR)


def _ssd_fwd(act, proj, dtb, alog, dskip, ng, name):
    s = act.shape[0]
    nc = s // SSD_CHUNK

    def body(act_ref, z_ref, dt_ref, dtb_ref, alog_ref, dskip_ref, ng_ref, y_ref, st_ref, st_scr):
        @pl.when(pl.program_id(0) == 0)
        def _():
            st_scr[...] = jnp.zeros_like(st_scr)

        xs, bm, cm, z, dtp, dtb_v, alog_v, dskip_v, ng_v = _ssd_args(act_ref, z_ref, dt_ref, dtb_ref, alog_ref, dskip_ref, ng_ref)
        state = [st_scr[j] for j in range(N_PAIR)]
        st_ref[0] = st_scr[...]
        y, new_state = _ssd_chunk(xs, bm, cm, z, dtp, state, dtb_v, alog_v, dskip_v, ng_v)
        for j in range(N_PAIR):
            y_ref[:, pl.ds(j * LANES, LANES)] = y[j]
            st_scr[j] = new_state[j]

    return pl.pallas_call(
        body,
        name=name,
        out_shape=(
            jax.ShapeDtypeStruct((s, SSD_INNER), F32),
            jax.ShapeDtypeStruct((nc, N_PAIR, LANES, SSD_STATE), F32),
        ),
        grid=(nc,),
        in_specs=_ssd_specs(False, nc),
        out_specs=(
            pl.BlockSpec((SSD_CHUNK, SSD_INNER), lambda c: (c, 0)),
            pl.BlockSpec((1, N_PAIR, LANES, SSD_STATE), lambda c: (c, 0, 0, 0)),
        ),
        scratch_shapes=[pltpu.VMEM((N_PAIR, LANES, SSD_STATE), F32)],
        compiler_params=_cparams(("arbitrary",)),
    )(act, proj, proj, dtb, alog, dskip, ng)


def _ssd_bwd(act, proj, dtb, alog, dskip, ng, states, dy, name):
    s = act.shape[0]
    nc = s // SSD_CHUNK

    def body(act_ref, z_ref, dt_ref, dtb_ref, alog_ref, dskip_ref, ng_ref, st_ref, dy_ref,
             dact_ref, dz_ref, ddt_ref, ddtb_ref, dalog_ref, ddskip_ref, dng_ref, dst_scr):
        first = pl.program_id(0) == 0

        @pl.when(first)
        def _():
            dst_scr[...] = jnp.zeros_like(dst_scr)

        xs, bm, cm, z, dtp, dtb_v, alog_v, dskip_v, ng_v = _ssd_args(act_ref, z_ref, dt_ref, dtb_ref, alog_ref, dskip_ref, ng_ref)
        state = [st_ref[0, j] for j in range(N_PAIR)]
        _, vjp = jax.vjp(_ssd_chunk, xs, bm, cm, z, dtp, state, dtb_v, alog_v, dskip_v, ng_v)
        dy_v = [dy_ref[:, pl.ds(j * LANES, LANES)] for j in range(N_PAIR)]
        dxs, dbm, dcm, dz, ddtp, dstate, ddtb, dalog, ddskip, dng = vjp((dy_v, [dst_scr[j] for j in range(N_PAIR)]))
        for i, v in enumerate(dxs + dbm + dcm):
            dact_ref[:, pl.ds(i * LANES, LANES)] = v
        for j in range(N_PAIR):
            dz_ref[:, pl.ds(j * LANES, LANES)] = dz[j].astype(BF16)
            dst_scr[j] = dstate[j]
            _acc(dng_ref.at[:, pl.ds(j * LANES, LANES)], dng[j], first)
        ddt_ref[...] = ddtp.astype(BF16)
        _acc(ddtb_ref, ddtb, first)
        _acc(dalog_ref, dalog, first)
        _acc(ddskip_ref, ddskip, first)

    rv = lambda c: nc - 1 - c
    sds = jax.ShapeDtypeStruct
    return pl.pallas_call(
        body,
        name=name,
        out_shape=(
            sds((s, SSD_CONV_DIM), F32), sds((s, SSD_INNER), BF16), sds((s, LANES), BF16),
            sds((1, LANES), F32), sds((1, LANES), F32), sds((1, LANES), F32), sds((1, SSD_INNER), F32),
        ),
        grid=(nc,),
        in_specs=_ssd_specs(True, nc) + [
            pl.BlockSpec((1, N_PAIR, LANES, SSD_STATE), lambda c: (rv(c), 0, 0, 0)),
            pl.BlockSpec((SSD_CHUNK, SSD_INNER), lambda c: (rv(c), 0)),
        ],
        out_specs=(
            pl.BlockSpec((SSD_CHUNK, SSD_CONV_DIM), lambda c: (rv(c), 0)),
            pl.BlockSpec((SSD_CHUNK, SSD_INNER), lambda c: (rv(c), 0)),
            pl.BlockSpec((SSD_CHUNK, LANES), lambda c: (rv(c), 0)),
            _full((1, LANES)), _full((1, LANES)), _full((1, LANES)), _full((1, SSD_INNER)),
        ),
        scratch_shapes=[pltpu.VMEM((N_PAIR, LANES, SSD_STATE), F32)],
        compiler_params=_cparams(("arbitrary",)),
    )(act, proj, proj, dtb, alog, dskip, ng, states, dy)


def _merge_specs(tm):
    row = lambda w: pl.BlockSpec((tm, w), lambda i: (i, 0))
    return [
        row(GM_WIDTH), row(HEAD_BLOCK), row(SSD_INNER),
        pl.BlockSpec((tm, N_BRANCH * D_MODEL), lambda i: (i, P_GATES // (N_BRANCH * D_MODEL))),
        row(D_MODEL),
        _full((GM_WIDTH, D_MODEL)), _full((HEAD_BLOCK, D_MODEL)), _full((SSD_INNER, D_MODEL)), _full((D_MODEL, D_MODEL)),
    ]


def _merge_fwd(ya, yb, yc, proj, x1, pa, pb, pc, wo, name, tm=256):
    s = x1.shape[0]

    def body(ya_ref, yb_ref, yc_ref, gates_ref, x1_ref, pa_ref, pb_ref, pc_ref, wo_ref, x2_ref, mg_ref):
        merged = jnp.zeros((tm, D_MODEL), F32)
        for i, (y_ref, p_ref) in enumerate(((ya_ref, pa_ref), (yb_ref, pb_ref), (yc_ref, pc_ref))):
            gate = _sigmoid(gates_ref[:, pl.ds(i * D_MODEL, D_MODEL)])
            merged = merged + gate * _nn(y_ref[...], p_ref[...])
        mg_ref[...] = merged.astype(BF16)
        x2_ref[...] = x1_ref[...] + _nn(merged, wo_ref[...])

    row = lambda w: pl.BlockSpec((tm, w), lambda i: (i, 0))
    return pl.pallas_call(
        body,
        name=name,
        out_shape=(jax.ShapeDtypeStruct((s, D_MODEL), F32), jax.ShapeDtypeStruct((s, D_MODEL), BF16)),
        grid=(s // tm,),
        in_specs=_merge_specs(tm),
        out_specs=(row(D_MODEL), row(D_MODEL)),
        compiler_params=_cparams(("parallel",)),
    )(ya, yb, yc, proj, x1, pa, pb, pc, wo)


def _merge_bwd(ya, yb, yc, proj, dx2, pa, pb, pc, wo, name, tm=256):
    s = dx2.shape[0]

    def body(ya_ref, yb_ref, yc_ref, gates_ref, dx2_ref, pa_ref, pb_ref, pc_ref, wo_ref,
             dya_ref, dyb_ref, dyc_ref, dgates_ref, ta_ref, tb_ref, tc_ref):
        dmerged = _nt(dx2_ref[...], wo_ref[...])
        branches = ((ya_ref, pa_ref, dya_ref, ta_ref), (yb_ref, pb_ref, dyb_ref, tb_ref), (yc_ref, pc_ref, dyc_ref, tc_ref))
        for i, (y_ref, p_ref, dy_ref, t_ref) in enumerate(branches):
            cols = pl.ds(i * D_MODEL, D_MODEL)
            gate = _sigmoid(gates_ref[:, cols])
            dgates_ref[:, cols] = (dmerged * _nn(y_ref[...], p_ref[...]) * gate * (1.0 - gate)).astype(BF16)
            dt = (dmerged * gate).astype(BF16)
            t_ref[...] = dt
            dy_ref[...] = _nt(dt, p_ref[...])

    row = lambda w: pl.BlockSpec((tm, w), lambda i: (i, 0))
    sds = jax.ShapeDtypeStruct
    return pl.pallas_call(
        body,
        name=name,
        out_shape=(
            sds((s, GM_WIDTH), F32), sds((s, HEAD_BLOCK), F32), sds((s, SSD_INNER), F32),
            sds((s, N_BRANCH * D_MODEL), BF16),
            sds((s, D_MODEL), BF16), sds((s, D_MODEL), BF16), sds((s, D_MODEL), BF16),
        ),
        grid=(s // tm,),
        in_specs=_merge_specs(tm),
        out_specs=(row(GM_WIDTH), row(HEAD_BLOCK), row(SSD_INNER), row(N_BRANCH * D_MODEL),
                   row(D_MODEL), row(D_MODEL), row(D_MODEL)),
        compiler_params=_cparams(("parallel",)),
    )(ya, yb, yc, proj, dx2, pa, pb, pc, wo)


def _loss_head(y, target, name, tm=512):
    s, d = y.shape

    def body(y_ref, t_ref, dy_ref, loss_ref):
        err = y_ref[...] - t_ref[...]
        dy_ref[...] = err * (1.0 / d)
        part = jnp.sum(jnp.sum(err * err, axis=1, keepdims=True), axis=0, keepdims=True) * (0.5 / d)
        _acc(loss_ref, jnp.broadcast_to(part, (1, LANES)), pl.program_id(0) == 0)

    return pl.pallas_call(
        body,
        name=name,
        out_shape=(jax.ShapeDtypeStruct((s, d), F32), jax.ShapeDtypeStruct((1, LANES), F32)),
        grid=(s // tm,),
        in_specs=[pl.BlockSpec((tm, d), lambda i: (i, 0)), pl.BlockSpec((tm, d), lambda i: (i, 0))],
        out_specs=(pl.BlockSpec((tm, d), lambda i: (i, 0)), _full((1, LANES))),
        compiler_params=_cparams(("arbitrary",)),
    )(y, target)


IN_SHARD = IN_COLS // N_DEV
P_OF_PIECE = (P_UV, P_CQ, P_CKV, P_KR + MLA_NOPE, P_Z, P_XBC, P_DT, P_GATES)


def _pad_lanes(w, n=LANES):
    return jnp.pad(w, [(0, 0)] * (w.ndim - 1) + [(0, n - w.shape[-1])])


def _in_proj_layout(blocks):
    def cols(i):
        a, b, out = IN_OFFSETS[i], IN_OFFSETS[i] + IN_WIDTHS[i], []
        while a < b:
            k, lo = divmod(a, IN_SHARD)
            hi = min(IN_SHARD, lo + b - a)
            out.append(blocks[k, :, lo:hi])
            a += hi - lo
        return out

    zeros = lambda n: jnp.zeros((D_MODEL, n), blocks.dtype)
    uv, cq, ckv, kr, z, xbc, dt, gates = (cols(i) for i in range(8))
    return jnp.concatenate(uv + xbc + z + [zeros(MLA_NOPE)] + kr + [zeros(LANES - MLA_QK_DIM)] + cq + gates + ckv
                           + dt + [zeros(LANES - SSD_HEADS)], axis=1)


def _in_proj_unlayout(dw):
    out = []
    for k in range(N_DEV):
        a, b, parts = k * IN_SHARD, (k + 1) * IN_SHARD, []
        for i in range(8):
            lo, hi = max(a, IN_OFFSETS[i]), min(b, IN_OFFSETS[i] + IN_WIDTHS[i])
            if lo < hi:
                at = P_OF_PIECE[i] + lo - IN_OFFSETS[i]
                parts.append(dw[:, at:at + hi - lo])
        out.append(jnp.concatenate(parts, axis=1))
    return jnp.stack(out)


def _row(v, n=None):
    v = v.reshape(1, -1)
    return v if n is None else jnp.pad(v, ((0, 0), (0, n - v.shape[1])))


GATHERED = ("ffn_w_in", "ffn_w_out", "w_in", "mla_w_uq", "mla_w_ukv", "ssd_conv_w", "w_branch", "w_out")


def _layer_shards(w, l):
    pair = lambda a, b: jnp.stack([w[a][l], w[b][l]]).astype(BF16)
    one = lambda n: w[n][l].astype(BF16)
    return [pair("ffn1_w_in", "ffn2_w_in"), pair("ffn1_w_out", "ffn2_w_out"), one("w_in"), one("mla_w_uq"),
            one("mla_w_ukv"), one("ssd_conv_w"), one("w_branch"), one("w_out")]


def _layer_weights(gathered, vec, l):
    gw = dict(zip(GATHERED, gathered))
    kv = gw["mla_w_ukv"]
    branch = jnp.moveaxis(gw["w_branch"], 0, 2).reshape(N_BRANCH, GM_WIDTH, D_MODEL)
    return dict(
        ffn_w_in=gw["ffn_w_in"], ffn_w_out=gw["ffn_w_out"], ffn1_at=0, ffn2_at=1,
        ffn1_norm=_row(vec["ffn1_norm"][l]), ffn2_norm=_row(vec["ffn2_norm"][l]),
        mix_norm=_row(vec["mix_norm"][l]), w_in=_in_proj_layout(gw["w_in"]),
        gm_v_norm=_row(vec["gm_v_norm"][l]), gm_w_s=vec["gm_w_s"][l], gm_b_s=vec["gm_b_s"][l][..., None],
        q_norm=_row(vec["mla_q_norm"][l]), kv_norm=_row(vec["mla_kv_norm"][l]),
        wq=_pad_lanes(gw["mla_w_uq"]), wk=_pad_lanes(kv[:, :, :MLA_NOPE]), wv=_pad_lanes(kv[:, :, MLA_NOPE:]),
        q_gain=_row(vec["mla_q_gain"][l], LANES), k_gain=_row(vec["mla_k_gain"][l], LANES),
        conv_w=jnp.moveaxis(gw["ssd_conv_w"], 0, 1).reshape(SSD_CONV, SSD_CONV_DIM).astype(F32),
        conv_b=_row(vec["ssd_conv_b"][l]),
        dt_bias=_row(vec["ssd_dt_bias"][l], LANES), a_log=_row(vec["ssd_a_log"][l], LANES),
        d_skip=_row(vec["ssd_d"][l], LANES), ssd_norm=_row(vec["ssd_norm"][l]),
        pa=branch[0],
        pb=jnp.pad(branch[1].reshape(MLA_HEADS, MLA_V, D_MODEL), ((0, 0), (0, LANES - MLA_V), (0, 0))).reshape(HEAD_BLOCK, D_MODEL),
        pc=branch[2], wo=gw["w_out"].reshape(D_MODEL, D_MODEL),
    )


def _layer_fwd(x, k, rope, next_shards=()):
    cosf, sinf, rot = rope
    x1, gu1 = _ffn_fwd(x, k["ffn1_norm"], k["ffn_w_in"], k["ffn_w_out"], k["ffn1_at"], "ffn1_fwd")
    h = _rmsnorm_fwd(x1, k["mix_norm"], "mix_norm_fwd")
    proj = _matmul(h, k["w_in"], "nn", F32, "in_proj_fwd", tn=2176)
    ya = _gmlp_fwd(proj, k["gm_v_norm"], k["gm_w_s"], k["gm_b_s"], "gmlp_fwd")
    q, kk, v = _mla_pre_fwd(proj, cosf, sinf, rot, k["q_norm"], k["kv_norm"], k["wq"], k["wk"], k["wv"],
                            k["q_gain"], k["k_gain"], "mla_pre_fwd")
    yb, lse, lse_t, next_gathered = _attention_fwd(q, kk, v, "attention_fwd", gather=next_shards)
    act = _conv_fwd(proj, k["conv_w"], k["conv_b"], "conv_fwd")
    yc, states = _ssd_fwd(act, proj, k["dt_bias"], k["a_log"], k["d_skip"], k["ssd_norm"], "ssd_fwd")
    x2, merged = _merge_fwd(ya, yb, yc, proj, x1, k["pa"], k["pb"], k["pc"], k["wo"], "merge_fwd")
    x3, gu2 = _ffn_fwd(x2, k["ffn2_norm"], k["ffn_w_in"], k["ffn_w_out"], k["ffn2_at"], "ffn2_fwd")
    saved = dict(x=x, x1=x1, x2=x2, gu1=gu1, gu2=gu2, h=h, proj=proj, ya=ya, yb=yb, yc=yc, q=q, k=kk, v=v,
                 lse=lse, lse_t=lse_t, act=act, states=states, merged=merged)
    return x3, saved, next_gathered


def _layer_bwd(dx3, k, sv, rope, above=()):
    cosf, sinf, rot = rope
    g = {}
    dx2, g["ffn2_norm"], g["ffn2_w_in"], g["ffn2_w_out"], theirs = _ffn_bwd(
        sv["x2"], k["ffn2_norm"], k["ffn_w_in"], k["ffn_w_out"], k["ffn2_at"], sv["gu2"], dx3, "ffn2_bwd",
        pair=tuple(above))
    exchange = _pair_sums(above, theirs, "grads")
    proj = sv["proj"]
    dya, dyb, dyc, dgates, ta, tb, tc = _merge_bwd(sv["ya"], sv["yb"], sv["yc"], proj, dx2, k["pa"], k["pb"], k["pc"],
                                                   k["wo"], "merge_bwd")
    g["w_out"] = _matmul(sv["merged"], dx2, "tn", BF16, "w_out_grad").reshape(N_DEV, D_MODEL // N_DEV, D_MODEL)
    dpa = _matmul(sv["ya"], ta, "tn", BF16, "branch_a_grad")
    dpb = _matmul(sv["yb"], tb, "tn", BF16, "branch_b_grad")
    dpc = _matmul(sv["yc"], tc, "tn", BF16, "branch_c_grad")
    branch = jnp.stack([dpa, dpb.reshape(MLA_HEADS, LANES, D_MODEL)[:, :MLA_V].reshape(GM_WIDTH, D_MODEL), dpc])
    g["w_branch"] = jnp.moveaxis(branch.reshape(N_BRANCH, GM_WIDTH, N_DEV, LANES), 2, 0)
    duv, g["gm_v_norm"], g["gm_w_s"], dbs = _gmlp_bwd(proj, k["gm_v_norm"], k["gm_w_s"], k["gm_b_s"], dya, "gmlp_bwd")
    g["gm_b_s"] = dbs[..., 0]
    dq, dk, dv, got_a, got_b = _attention_bwd(sv["q"], sv["k"], sv["v"], sv["yb"], sv["lse"], sv["lse_t"], dyb,
                                               "attention_bwd", exchange_dq=tuple(exchange[:1]),
                                               exchange_dkv=tuple(exchange[1:]))
    dcq, dckv, dkr, dqn, dkvn, dwq, dwk, dwv, dqg, dkg = _mla_pre_bwd(
        proj, cosf, sinf, rot, k["q_norm"], k["kv_norm"], k["wq"], k["wk"], k["wv"], k["q_gain"], k["k_gain"],
        dq, dk, dv, "mla_pre_bwd")
    g["mla_q_norm"], g["mla_kv_norm"] = dqn, dkvn
    g["mla_w_uq"] = dwq[:, :, :MLA_QK_DIM].astype(BF16)
    g["mla_w_ukv"] = jnp.concatenate([dwk[:, :, :MLA_NOPE], dwv[:, :, :MLA_V]], axis=-1).astype(BF16)
    g["mla_q_gain"], g["mla_k_gain"] = dqg[:, :MLA_QK_DIM], dkg[:, :MLA_QK_DIM]
    dact, dz, ddt, ddtb, dalog, ddsk, g["ssd_norm"] = _ssd_bwd(
        sv["act"], proj, k["dt_bias"], k["a_log"], k["d_skip"], k["ssd_norm"], sv["states"], dyc, "ssd_bwd")
    g["ssd_dt_bias"], g["ssd_a_log"], g["ssd_d"] = ddtb[:, :SSD_HEADS], dalog[:, :SSD_HEADS], ddsk[:, :SSD_HEADS]
    dxbc, dcw, g["ssd_conv_b"] = _conv_bwd(proj, k["conv_w"], k["conv_b"], dact, "conv_bwd")
    g["ssd_conv_w"] = jnp.moveaxis(dcw.reshape(SSD_CONV, N_DEV, LANES), 1, 0).astype(BF16)
    dproj = jnp.concatenate([duv, dxbc, dz, dkr, dcq, dgates, dckv, ddt], axis=1)
    dh = _matmul(dproj, k["w_in"], "nt", BF16, "in_proj_dh", tm=1024, tk=2176)
    g["w_in"] = _in_proj_unlayout(_matmul(sv["h"], dproj, "tn", BF16, "in_proj_grad", tm=D_MODEL, tn=2176))
    dx1, g["mix_norm"] = _rmsnorm_bwd(sv["x1"], k["mix_norm"], dh, dx2, "mix_norm_bwd")
    dx, g["ffn1_norm"], g["ffn1_w_in"], g["ffn1_w_out"], _ = _ffn_bwd(
        sv["x"], k["ffn1_norm"], k["ffn_w_in"], k["ffn_w_out"], k["ffn1_at"], sv["gu1"], dx1, "ffn1_bwd")
    return dx, g, got_a + got_b


def _layer_contribs(g):
    pair = lambda a, b: jnp.stack([g[a], g[b]], axis=1)
    arrays = [pair("ffn1_w_in", "ffn2_w_in"), pair("ffn1_w_out", "ffn2_w_out"), g["w_in"], g["mla_w_uq"],
              g["mla_w_ukv"], g["ssd_conv_w"], g["w_branch"], g["w_out"]]
    return [a.reshape(N_DEV, -1, a.shape[-1]) for a in arrays]


def _pair_sums(contribs, theirs, name):
    return [_pair_sum(c, t, "%s_pair_sum_%s" % (name, n), _tile_rows(c.shape[1], c.shape[2]))
            for n, c, t in zip(GATHERED, contribs, theirs)]


def _rope_tables(positions):
    s = positions.shape[0]
    inv_freq = 1.0 / (ROPE_THETA ** (jnp.arange(0, MLA_ROPE, 2, dtype=F32) / MLA_ROPE))
    ang = positions.astype(F32)[:, None] * inv_freq
    cos, sin = jnp.cos(ang), jnp.sin(ang)
    tail = LANES - MLA_QK_DIM
    cosf = jnp.concatenate([jnp.ones((s, MLA_NOPE), F32), cos, cos, jnp.ones((s, tail), F32)], axis=1)
    sinf = jnp.concatenate([jnp.zeros((s, MLA_NOPE), F32), sin, sin, jnp.zeros((s, tail), F32)], axis=1)
    half = MLA_ROPE // 2
    rot = np.zeros((LANES, LANES), np.float32)
    for i in range(half):
        rot[MLA_NOPE + half + i, MLA_NOPE + i] = -1.0
        rot[MLA_NOPE + i, MLA_NOPE + half + i] = 1.0
    return cosf, sinf, jnp.asarray(rot)


MATRICES = ("ffn1_w_in", "ffn1_w_out", "w_in", "mla_w_uq", "mla_w_ukv", "ssd_conv_w", "w_branch", "w_out", "ffn2_w_in",
            "ffn2_w_out")
VECTORS = ("ffn1_norm", "mix_norm", "gm_v_norm", "gm_w_s", "gm_b_s", "mla_q_norm", "mla_kv_norm", "mla_q_gain",
           "mla_k_gain", "ssd_conv_b", "ssd_dt_bias", "ssd_a_log", "ssd_d", "ssd_norm", "ffn2_norm")
WEIGHTS = ("ffn1_norm", "ffn1_w_in", "ffn1_w_out", "mix_norm", "w_in", "gm_v_norm", "gm_w_s", "gm_b_s", "mla_q_norm",
           "mla_kv_norm", "mla_w_uq", "mla_w_ukv", "mla_q_gain", "mla_k_gain", "ssd_conv_w", "ssd_conv_b", "ssd_dt_bias",
           "ssd_a_log", "ssd_d", "ssd_norm", "w_branch", "w_out", "ffn2_norm", "ffn2_w_in", "ffn2_w_out")


def _local_step(x, positions, target, vec, shards=None, gathered=None):
    rope = _rope_tables(positions)
    depth = vec["ffn1_norm"].shape[0]
    saved = []
    here = _gather_layer(shards[0], "layer0_all_gather") if gathered is None else gathered[0]
    for l in range(depth):
        k = _layer_weights(here, vec, l)
        ahead = shards[l + 1] if gathered is None and l + 1 < depth else ()
        x, sv, here = _layer_fwd(x, k, rope, ahead)
        if gathered is not None and l + 1 < depth:
            here = gathered[l + 1]
        saved.append((k, sv))
    dy, loss = _loss_head(x, target, "loss_head")
    grads, reduced, above = [], [], ()
    for k, sv in reversed(saved):
        dy, g, got = _layer_bwd(dy, k, sv, rope, above=above)
        grads.append(g)
        if gathered is None:
            if above:
                reduced.append(got)
            above = _layer_contribs(g)
    if gathered is None:
        sums = _pair_sums(above, _pair_exchange(above, "layer0_grads_pair_exchange"), "layer0_grads")
        reduced.append(_chip_exchange(sums, "layer0_grads_chip_exchange"))
    grads.reverse()
    reduced.reverse()
    out = {n: jnp.stack([g[n].reshape(vec[n].shape[1:]) for g in grads]) for n in VECTORS}
    if gathered is None:
        out["reduced"] = reduced
    else:
        out.update({n: [g[n] for g in grads] for n in MATRICES})
    return loss[0, 0], dy, out


MESH = pl.DeviceIdType.MESH
N_CHIP = 4
ANY = pl.BlockSpec(memory_space=pl.ANY)


def _place():
    return lax.axis_index("x"), lax.axis_index("y"), lax.axis_index("c")


GATHER_COPIES = 7


def _gather_plan(shard_refs, out_refs, send_sems, recv_sems, local_sems):
    x, y, c = _place()
    me, sibling = (x, y, c), (x, y, 1 - c)
    chips = [(1 - x, y), (x, 1 - y), (1 - x, 1 - y)]
    slot = lambda px, py, pc: 4 * px + 2 * py + pc
    plans = []
    for i, (src, out) in enumerate(zip(shard_refs, out_refs)):
        def copy(k, block, to, from_shard=False, i=i, src=src, out=out):
            return pltpu.make_async_remote_copy(
                src_ref=src if from_shard else out.at[slot(*block)], dst_ref=out.at[slot(*block)],
                send_sem=send_sems.at[GATHER_COPIES * i + k], recv_sem=recv_sems.at[GATHER_COPIES * i + k],
                device_id=to, device_id_type=MESH)

        plans.append(dict(
            mine=lambda i=i, src=src, out=out: pltpu.make_async_copy(src, out.at[slot(*me)], local_sems.at[i]),
            first=lambda copy=copy: [copy(0, me, sibling, True)] + [copy(1 + j, me, (*chip, c), True)
                                                                    for j, chip in enumerate(chips)],
            arrived=lambda j, copy=copy: copy(1 + j, (*chips[j], c), me),
            passed=lambda j, copy=copy: copy(4 + j, (*chips[j], c), sibling),
            from_sibling=lambda copy=copy: [copy(0, sibling, me)] + [copy(4 + j, (*chip, 1 - c), me)
                                                                     for j, chip in enumerate(chips)],
        ))
    return plans


def _gather_start(plans):
    for p in plans:
        p["mine"]().start()
        for cp in p["first"]():
            cp.start()


def _gather_finish(plans):
    for j in range(N_CHIP - 1):
        for p in plans:
            p["arrived"](j).wait_recv()
            p["passed"](j).start()
    for p in plans:
        for cp in p["from_sibling"]():
            cp.wait_recv()
        for cp in p["first"]() + [p["passed"](j) for j in range(N_CHIP - 1)]:
            cp.wait_send()
        p["mine"]().wait()


def _gather_scratch(n):
    return [pltpu.SemaphoreType.DMA((GATHER_COPIES * n,)), pltpu.SemaphoreType.DMA((GATHER_COPIES * n,)),
            pltpu.SemaphoreType.DMA((n,))]


def _gathered_shapes(shards):
    return tuple(jax.ShapeDtypeStruct((N_DEV, *a.shape), a.dtype) for a in shards)


def _gather_layer(shards, name):
    n = len(shards)

    def body(*refs):
        plans = _gather_plan(refs[:n], refs[n:2 * n], *refs[2 * n:])
        _gather_start(plans)
        _gather_finish(plans)

    return pl.pallas_call(
        body,
        name=name,
        out_shape=_gathered_shapes(shards),
        in_specs=[ANY] * n,
        out_specs=(ANY,) * n,
        scratch_shapes=_gather_scratch(n),
    )(*shards)


def _all_gather(shard, name):
    m, n = shard.shape

    def body(x_ref, out_ref, send_sems, recv_sems, local_sem):
        x, y, c = _place()
        me, sibling = (x, y, c), (x, y, 1 - c)
        chips = [(1 - x, y), (x, 1 - y), (1 - x, 1 - y)]

        def rows(px, py, pc):
            return out_ref.at[pl.ds((4 * px + 2 * py + pc) * m, m), :]

        def copy(k, block, to, src=None):
            return pltpu.make_async_remote_copy(
                src_ref=rows(*block) if src is None else src, dst_ref=rows(*block),
                send_sem=send_sems.at[k], recv_sem=recv_sems.at[k], device_id=to, device_id_type=MESH)

        mine = pltpu.make_async_copy(x_ref, rows(*me), local_sem)
        mine.start()
        first = [copy(0, me, sibling, src=x_ref)]
        first += [copy(1 + j, me, (*chip, c), src=x_ref) for j, chip in enumerate(chips)]
        for cp in first:
            cp.start()
        passed = [copy(4 + j, (*chip, c), sibling) for j, chip in enumerate(chips)]
        for j, chip in enumerate(chips):
            copy(1 + j, (*chip, c), me).wait_recv()
            passed[j].start()
        copy(0, sibling, me).wait_recv()
        for j, chip in enumerate(chips):
            copy(4 + j, (*chip, 1 - c), me).wait_recv()
        for cp in first + passed:
            cp.wait_send()
        mine.wait()

    return pl.pallas_call(
        body,
        name=name,
        out_shape=jax.ShapeDtypeStruct((N_DEV * m, n), shard.dtype),
        in_specs=[ANY],
        out_specs=ANY,
        scratch_shapes=[pltpu.SemaphoreType.DMA((7,)), pltpu.SemaphoreType.DMA((7,)), pltpu.SemaphoreType.DMA],
    )(shard)


def _pair_exchange(contribs, name):
    na = len(contribs)

    def body(*refs):
        plan = _pair_plan(refs[:na], refs[na:2 * na], *refs[2 * na:])
        _pair_start(plan)
        _pair_finish(plan)

    return pl.pallas_call(
        body,
        name=name,
        out_shape=_pair_shapes(contribs),
        in_specs=[ANY] * na,
        out_specs=(ANY,) * na,
        scratch_shapes=_pair_scratch(na),
    )(*contribs)


def _pair_plan(g_refs, got_refs, send_sems, recv_sems, unused_sems):
    x, y, c = _place()
    return [lambda i=i, j=j, g=g, got=got: pltpu.make_async_remote_copy(
        src_ref=g.at[2 * j + (1 - c)], dst_ref=got.at[j], send_sem=send_sems.at[N_CHIP * i + j],
        recv_sem=recv_sems.at[N_CHIP * i + j], device_id=(x, y, 1 - c), device_id_type=MESH)
        for i, (g, got) in enumerate(zip(g_refs, got_refs)) for j in range(N_CHIP)]


def _pair_start(plan):
    for cp in plan:
        cp().start()


def _pair_finish(plan):
    for cp in plan:
        cp().wait()


def _pair_scratch(n):
    return [pltpu.SemaphoreType.DMA((N_CHIP * n,)), pltpu.SemaphoreType.DMA((N_CHIP * n,)), pltpu.SemaphoreType.DMA((1,))]


def _pair_shapes(contribs):
    return tuple(jax.ShapeDtypeStruct((N_CHIP, *a.shape[1:]), a.dtype) for a in contribs)


def _pair_sum(contrib, theirs, name, tr):
    _, r, n = contrib.shape
    side = lax.axis_index("c").astype(jnp.int32).reshape(1)

    def body(c_ref, a_ref, b_ref, o_ref):
        o_ref[...] = (a_ref[...].astype(F32) + b_ref[...].astype(F32)).astype(BF16)

    spec = pl.BlockSpec((1, tr, n), lambda j, i, c_ref: (j, i, 0))
    return pl.pallas_call(
        body,
        name=name,
        out_shape=jax.ShapeDtypeStruct(theirs.shape, BF16),
        grid_spec=pltpu.PrefetchScalarGridSpec(
            num_scalar_prefetch=1,
            grid=(N_CHIP, r // tr),
            in_specs=[pl.BlockSpec((1, tr, n), lambda j, i, c_ref: (2 * j + c_ref[0], i, 0)), spec],
            out_specs=spec,
        ),
        compiler_params=_cparams(("parallel", "parallel")),
    )(side, contrib, theirs)


OTHER_CHIPS = N_CHIP - 1


def _exchange_plan(part_refs, got_refs, send_sems, recv_sems, local_sems):
    x, y, c = _place()
    mine = 2 * x + y
    chips = [(1 - x, y), (x, 1 - y), (1 - x, 1 - y)]
    plans = []
    for i, (p, got) in enumerate(zip(part_refs, got_refs)):
        def copy(k, outbound, i=i, p=p, got=got):
            cx, cy = chips[k]
            return pltpu.make_async_remote_copy(
                src_ref=p.at[2 * cx + cy] if outbound else p.at[mine],
                dst_ref=got.at[mine] if outbound else got.at[2 * cx + cy],
                send_sem=send_sems.at[OTHER_CHIPS * i + k], recv_sem=recv_sems.at[OTHER_CHIPS * i + k],
                device_id=(cx, cy, c), device_id_type=MESH)

        plans.append(dict(copy=copy, local=lambda i=i, p=p, got=got: pltpu.make_async_copy(
            p.at[mine], got.at[mine], local_sems.at[i])))
    return plans


def _exchange_start(plans):
    for p in plans:
        p["local"]().start()
        for k in range(OTHER_CHIPS):
            p["copy"](k, True).start()


def _exchange_finish(plans):
    for p in plans:
        for k in range(OTHER_CHIPS):
            p["copy"](k, False).wait_recv()
        for k in range(OTHER_CHIPS):
            p["copy"](k, True).wait_send()
        p["local"]().wait()


def _exchange_scratch(n):
    return [pltpu.SemaphoreType.DMA((OTHER_CHIPS * n,)), pltpu.SemaphoreType.DMA((OTHER_CHIPS * n,)),
            pltpu.SemaphoreType.DMA((n,))]


def _same_shapes(arrays):
    return tuple(jax.ShapeDtypeStruct(a.shape, a.dtype) for a in arrays)


def _chip_exchange(parts, name):
    na = len(parts)

    def body(*refs):
        plans = _exchange_plan(refs[:na], refs[na:2 * na], *refs[2 * na:])
        _exchange_start(plans)
        _exchange_finish(plans)

    return pl.pallas_call(
        body,
        name=name,
        out_shape=_same_shapes(parts),
        in_specs=[ANY] * na,
        out_specs=(ANY,) * na,
        scratch_shapes=_exchange_scratch(na),
    )(*parts)


def _adamw(parts, w, m, v, name, tr, at=0):
    k = parts.shape[0]
    r, n = w.shape
    first = at // tr

    def body(p_ref, w_ref, m_ref, v_ref, g_ref, d_ref, nm_ref, nv_ref):
        g = p_ref[0].astype(F32)
        for i in range(1, k):
            g = g + p_ref[i].astype(F32)
        m_new = ADAM_B1 * m_ref[...] + (1.0 - ADAM_B1) * g
        v_new = ADAM_B2 * v_ref[...] + (1.0 - ADAM_B2) * (g * g)
        m_hat = m_new / (1.0 - ADAM_B1 ** ADAM_STEP)
        v_hat = v_new / (1.0 - ADAM_B2 ** ADAM_STEP)
        g_ref[...] = g
        d_ref[...] = -ADAM_LR * (m_hat / (jnp.sqrt(v_hat) + ADAM_EPS) + ADAM_WD * w_ref[...])
        nm_ref[...] = m_new
        nv_ref[...] = v_new

    spec = pl.BlockSpec((tr, n), lambda i: (i, 0))
    out = jax.ShapeDtypeStruct((r, n), F32)
    return pl.pallas_call(
        body,
        name=name,
        out_shape=(out, out, out, out),
        grid=(r // tr,),
        in_specs=[pl.BlockSpec((k, tr, n), lambda i: (0, i + first, 0)), spec, spec, spec],
        out_specs=(spec, spec, spec, spec),
        compiler_params=_cparams(("parallel",)),
    )(parts, w, m, v)


def _adamw_layers(parts, w, m, v, name, at=0):
    depth = len(parts)
    k = parts[0].shape[0]
    w3, m3, v3 = (a.reshape(depth, -1, a.shape[-1]) for a in (w, m, v))
    _, r, n = w3.shape
    tr = _tile_rows(r, n * depth)
    first = at // tr

    def body(*refs):
        p_refs = refs[:depth]
        w_ref, m_ref, v_ref, g_ref, d_ref, nm_ref, nv_ref = refs[depth:]
        for l in range(depth):
            @pl.when(pl.program_id(0) == l)
            def _(p_ref=p_refs[l]):
                g = p_ref[0].astype(F32)
                for i in range(1, k):
                    g = g + p_ref[i].astype(F32)
                m_new = ADAM_B1 * m_ref[...] + (1.0 - ADAM_B1) * g
                v_new = ADAM_B2 * v_ref[...] + (1.0 - ADAM_B2) * (g * g)
                m_hat = m_new / (1.0 - ADAM_B1 ** ADAM_STEP)
                v_hat = v_new / (1.0 - ADAM_B2 ** ADAM_STEP)
                g_ref[...] = g
                d_ref[...] = -ADAM_LR * (m_hat / (jnp.sqrt(v_hat) + ADAM_EPS) + ADAM_WD * w_ref[...])
                nm_ref[...] = m_new
                nv_ref[...] = v_new

    part_spec = lambda l: pl.BlockSpec((k, tr, n), lambda j, i: (0, jnp.where(j == l, i + first, first), 0))
    spec = pl.BlockSpec((None, tr, n), lambda j, i: (j, i, 0))
    out = jax.ShapeDtypeStruct(w3.shape, F32)
    res = pl.pallas_call(
        body,
        name=name,
        out_shape=(out, out, out, out),
        grid=(depth, r // tr),
        in_specs=[part_spec(l) for l in range(depth)] + [spec, spec, spec],
        out_specs=(spec, spec, spec, spec),
        compiler_params=_cparams(("parallel", "parallel")),
    )(*parts, w3, m3, v3)
    return [a.reshape(w.shape) for a in res]


TILE_BYTES = 2 * 1024 * 1024
SUBLANES_16BIT = 16


def _tile_rows(r, n):
    best = None
    for t in range(SUBLANES_16BIT, r, SUBLANES_16BIT):
        if r % t == 0 and t * n * 4 <= TILE_BYTES:
            best = t
    return best or r


def _rows(a):
    return a.reshape(-1, a.shape[-1])


def _gather_rows(shard, name):
    return _all_gather(_rows(shard), name).reshape(N_DEV, *shard.shape)


SMALL = ("ffn1_norm", "mix_norm", "gm_v_norm", "gm_b_s", "mla_q_norm", "mla_kv_norm", "mla_q_gain", "mla_k_gain",
         "ssd_conv_b", "ssd_dt_bias", "ssd_a_log", "ssd_d", "ssd_norm", "ffn2_norm")
SMALL_ROWS = 8
SMALL_COLS = 7040


def _side_by_side(d):
    cols = jnp.concatenate([d[n].reshape(d[n].shape[0], -1) for n in SMALL], axis=1)
    return jnp.pad(cols, ((0, SMALL_ROWS - cols.shape[0]), (0, SMALL_COLS - cols.shape[1])))


def _apart(packed, like):
    out, off = {}, 0
    for n in SMALL:
        size = like[n][0].size
        out[n] = packed[:like[n].shape[0], off:off + size].reshape(like[n].shape)
        off += size
    return out


def kernel(x, positions, ffn1_norm, ffn1_w_in, ffn1_w_out, mix_norm, w_in, gm_v_norm, gm_w_s, gm_b_s, mla_q_norm, mla_kv_norm, mla_w_uq, mla_w_ukv, mla_q_gain, mla_k_gain, ssd_conv_w, ssd_conv_b, ssd_dt_bias, ssd_a_log, ssd_d, ssd_norm, w_branch, w_out, ffn2_norm, ffn2_w_in, ffn2_w_out, loss_target, m_ffn1_norm, m_ffn1_w_in, m_ffn1_w_out, m_mix_norm, m_w_in, m_gm_v_norm, m_gm_w_s, m_gm_b_s, m_mla_q_norm, m_mla_kv_norm, m_mla_w_uq, m_mla_w_ukv, m_mla_q_gain, m_mla_k_gain, m_ssd_conv_w, m_ssd_conv_b, m_ssd_dt_bias, m_ssd_a_log, m_ssd_d, m_ssd_norm, m_w_branch, m_w_out, m_ffn2_norm, m_ffn2_w_in, m_ffn2_w_out, v_ffn1_norm, v_ffn1_w_in, v_ffn1_w_out, v_mix_norm, v_w_in, v_gm_v_norm, v_gm_w_s, v_gm_b_s, v_mla_q_norm, v_mla_kv_norm, v_mla_w_uq, v_mla_w_ukv, v_mla_q_gain, v_mla_k_gain, v_ssd_conv_w, v_ssd_conv_b, v_ssd_dt_bias, v_ssd_a_log, v_ssd_d, v_ssd_norm, v_w_branch, v_w_out, v_ffn2_norm, v_ffn2_w_in, v_ffn2_w_out):
    given = dict(locals())
    w = {n: given[n] for n in WEIGHTS}
    mom = {n: given["m_" + n] for n in WEIGHTS}
    var = {n: given["v_" + n] for n in WEIGHTS}
    groups = {"ffn_w_in": ("ffn1_w_in", "ffn2_w_in"), "ffn_w_out": ("ffn1_w_out", "ffn2_w_out"), "w_in": ("w_in",),
              "mla_w_uq": ("mla_w_uq",), "mla_w_ukv": ("mla_w_ukv",), "ssd_conv_w": ("ssd_conv_w",),
              "w_branch": ("w_branch",), "w_out": ("w_out",)}

    shards = [_layer_shards(w, l) for l in range(w_out.shape[0])]
    loss, dx, grads = _local_step(x[0], positions[0], loss_target[0], {n: w[n] for n in VECTORS}, shards=shards)

    outs = [{}, {}, {}, {}]
    for i, names in enumerate(groups.values()):
        parts = [layer[i] for layer in grads["reduced"]]
        at = 0
        for n in names:
            res = _adamw_layers(parts, w[n], mom[n], var[n], "adamw_" + n, at=at)
            at += w[n][0].size // w[n].shape[-1]
            for o, r in zip(outs, res):
                o[n] = r

    small_parts = _gather_rows(_side_by_side(grads), "small_grads_all_gather")
    small = _adamw(small_parts, _side_by_side(w), _side_by_side(mom), _side_by_side(var), "adamw_small", SMALL_ROWS)
    ws_parts = _gather_rows(_rows(grads["gm_w_s"]), "gm_w_s_grads_all_gather")
    ws = _adamw(ws_parts, _rows(w["gm_w_s"]), _rows(mom["gm_w_s"]), _rows(var["gm_w_s"]), "adamw_gm_w_s",
                _tile_rows(ws_parts.shape[1], LANES))
    for o, sm, r in zip(outs, small, ws):
        o.update(_apart(sm, w))
        o["gm_w_s"] = r.reshape(w["gm_w_s"].shape)

    loss = lax.psum(loss, ("x", "y", "c"))
    return (loss, dx[None], *[o[n] for o in outs for n in WEIGHTS])
```
